```python
import jax, jax.numpy as jnp
from jax import lax
import numpy as np

D_MODEL = 1024
BATCH = 8
SEQ = 2048
DEPTH = 1

HEAD_DIM = 64
N_HEADS = D_MODEL // HEAD_DIM
N_MOBA_HEADS = N_HEADS // 2
N_FOX_HEADS = N_HEADS - N_MOBA_HEADS
MOBA_WIDTH = N_MOBA_HEADS * HEAD_DIM
FOX_WIDTH = N_FOX_HEADS * HEAD_DIM
MOBA_BLOCK = 256
MOBA_TOPK = 3
MOBA_Q_CHUNK = 16
FOX_Q_BLOCK = 128
ROPE_THETA = 500000.0
ROPE_DIM = HEAD_DIM // 4
D_FF = ((8 * D_MODEL // 3 + 127) // 128) * 128
CONV_WIDTH = 3
NORM_EPS = 1e-6
NEG_INF = -1e30
IN_COLS = 3 * MOBA_WIDTH + 3 * FOX_WIDTH + N_FOX_HEADS

kernel_name = 'hybrid_moba_fox_convffn_block'


def rms_norm(x, gain):
    xf = x.astype(jnp.float32)
    y = xf * lax.rsqrt(jnp.mean(xf * xf, axis=-1, keepdims=True) + NORM_EPS)
    return (y * gain.astype(jnp.float32)).astype(x.dtype)


def partial_rope(x, positions):
    half = ROPE_DIM // 2
    inv_freq = jnp.power(ROPE_THETA, -2.0 * jnp.arange(half, dtype=jnp.float32) / ROPE_DIM)
    ang = positions.astype(jnp.float32)[:, None] * inv_freq[None, :]
    cos = jnp.cos(ang)[None, :, None, :]
    sin = jnp.sin(ang)[None, :, None, :]
    xf = x.astype(jnp.float32)
    x1 = xf[..., :half]
    x2 = xf[..., half:ROPE_DIM]
    out = jnp.concatenate([x1 * cos - x2 * sin, x2 * cos + x1 * sin, xf[..., ROPE_DIM:]], axis=-1)
    return out.astype(x.dtype)


def moba_attention(q, k, v):
    b, h, t, dh = q.shape
    n_blocks = -(-t // MOBA_BLOCK)
    t_pad = n_blocks * MOBA_BLOCK
    pad = t_pad - t
    if pad:
        q, k, v = [jnp.pad(a, ((0, 0), (0, 0), (0, pad), (0, 0))) for a in (q, k, v)]
    scale = dh ** -0.5
    kb = k.reshape(b, h, n_blocks, MOBA_BLOCK, dh)
    vb = v.reshape(b, h, n_blocks, MOBA_BLOCK, dh)
    own_blk = jnp.arange(t_pad) // MOBA_BLOCK
    n_sel = min(MOBA_TOPK, n_blocks - 1)
    if n_sel > 0:
        k_mean = jnp.mean(kb.astype(jnp.float32), axis=3)
        gate = jnp.einsum('bhtd,bhnd->bhtn', q.astype(jnp.float32), k_mean)
        is_past = jnp.arange(n_blocks)[None, :] < own_blk[:, None]
        gate = jnp.where(is_past, gate, NEG_INF)
        _, sel_idx = lax.top_k(gate, n_sel)
        sel_valid = sel_idx < own_blk[None, None, :, None]
    bi = jnp.arange(b)[:, None, None, None]
    hi = jnp.arange(h)[None, :, None, None]

    def chunk(start):
        qc = lax.dynamic_slice_in_dim(q, start, MOBA_Q_CHUNK, axis=2).astype(jnp.float32)
        q_pos = start + jnp.arange(MOBA_Q_CHUNK)
        blk = start // MOBA_BLOCK
        k_own = lax.dynamic_index_in_dim(kb, blk, axis=2, keepdims=False).astype(jnp.float32)
        v_own = lax.dynamic_index_in_dim(vb, blk, axis=2, keepdims=False).astype(jnp.float32)
        s_own = jnp.einsum('bhqd,bhsd->bhqs', qc, k_own) * scale
        k_pos = blk * MOBA_BLOCK + jnp.arange(MOBA_BLOCK)
        s_own = jnp.where(k_pos[None, :] <= q_pos[:, None], s_own, NEG_INF)
        if n_sel > 0:
            idx = lax.dynamic_slice_in_dim(sel_idx, start, MOBA_Q_CHUNK, axis=2)
            valid = lax.dynamic_slice_in_dim(sel_valid, start, MOBA_Q_CHUNK, axis=2)
            k_sel = kb[bi, hi, idx].astype(jnp.float32)
            v_sel = vb[bi, hi, idx].astype(jnp.float32)
            s_sel = jnp.einsum('bhqd,bhqnsd->bhqns', qc, k_sel) * scale
            s_sel = jnp.where(valid[..., None], s_sel, NEG_INF)
            s_sel = s_sel.reshape(b, h, MOBA_Q_CHUNK, n_sel * MOBA_BLOCK)
            p = jax.nn.softmax(jnp.concatenate([s_sel, s_own], axis=-1), axis=-1)
            p_sel = p[..., :n_sel * MOBA_BLOCK].reshape(b, h, MOBA_Q_CHUNK, n_sel, MOBA_BLOCK)
            p_own = p[..., n_sel * MOBA_BLOCK:]
            out = (jnp.einsum('bhqns,bhqnsd->bhqd', p_sel, v_sel)
                   + jnp.einsum('bhqs,bhsd->bhqd', p_own, v_own))
        else:
            p = jax.nn.softmax(s_own, axis=-1)
            out = jnp.einsum('bhqs,bhsd->bhqd', p, v_own)
        return out.astype(q.dtype)

    starts = jnp.arange(0, t_pad, MOBA_Q_CHUNK)
    out = lax.map(chunk, starts)
    out = jnp.moveaxis(out, 0, 2).reshape(b, h, t_pad, dh)
    return out[:, :, :t]


def forgetting_attention(q, k, v, log_f):
    b, h, t, dh = q.shape
    scale = dh ** -0.5
    cum = jnp.cumsum(log_f, axis=-1)
    k_pos = jnp.arange(t)
    kf = k.astype(jnp.float32)
    vf = v.astype(jnp.float32)

    def block(start):
        qc = lax.dynamic_slice_in_dim(q, start, FOX_Q_BLOCK, axis=2).astype(jnp.float32)
        cq = lax.dynamic_slice_in_dim(cum, start, FOX_Q_BLOCK, axis=2)
        s = (jnp.einsum('bhqd,bhsd->bhqs', qc, kf) * scale
             + cq[..., :, None] - cum[..., None, :])
        q_pos = start + jnp.arange(FOX_Q_BLOCK)
        s = jnp.where(k_pos[None, :] <= q_pos[:, None], s, NEG_INF)
        p = jax.nn.softmax(s, axis=-1)
        return jnp.einsum('bhqs,bhsd->bhqd', p, vf).astype(q.dtype)

    out = lax.map(block, jnp.arange(0, t, FOX_Q_BLOCK))
    return jnp.moveaxis(out, 0, 2).reshape(b, h, t, dh)


def hybrid_layer(x, c, w_ada, b_ada, g_mix, w_in, b_forget, moba_q_gain, moba_k_gain,
                 fox_q_gain, fox_k_gain, w_out, g_ffn, w_up, conv_w, conv_b, w_down):
    b, t, _ = x.shape
    mod = (jax.nn.silu(c) @ w_ada + b_ada)[:, None, :]
    sh1, sc1, gt1, sh2, sc2, gt2 = jnp.split(mod, 6, axis=-1)

    hn = rms_norm(x, g_mix) * (1.0 + sc1) + sh1
    proj = hn @ w_in
    offs = np.cumsum([MOBA_WIDTH, MOBA_WIDTH, MOBA_WIDTH, FOX_WIDTH, FOX_WIDTH, FOX_WIDTH]).tolist()
    mq, mk, mv, fq, fk, fv, f_logit = jnp.split(proj, offs, axis=-1)
    positions = jnp.arange(t)
    mq = partial_rope(rms_norm(mq.reshape(b, t, N_MOBA_HEADS, HEAD_DIM), moba_q_gain), positions)
    mk = partial_rope(rms_norm(mk.reshape(b, t, N_MOBA_HEADS, HEAD_DIM), moba_k_gain), positions)
    mv = mv.reshape(b, t, N_MOBA_HEADS, HEAD_DIM)
    fq = rms_norm(fq.reshape(b, t, N_FOX_HEADS, HEAD_DIM), fox_q_gain)
    fk = rms_norm(fk.reshape(b, t, N_FOX_HEADS, HEAD_DIM), fox_k_gain)
    fv = fv.reshape(b, t, N_FOX_HEADS, HEAD_DIM)
    to_bhtd = lambda a: a.transpose(0, 2, 1, 3)
    log_f = jax.nn.log_sigmoid((f_logit + b_forget).astype(jnp.float32)).transpose(0, 2, 1)
    o_moba = moba_attention(to_bhtd(mq), to_bhtd(mk), to_bhtd(mv))
    o_fox = forgetting_attention(to_bhtd(fq), to_bhtd(fk), to_bhtd(fv), log_f)
    o = jnp.concatenate([to_bhtd(o_moba).reshape(b, t, MOBA_WIDTH),
                         to_bhtd(o_fox).reshape(b, t, FOX_WIDTH)], axis=-1)
    x = x + gt1 * (o @ w_out)

    hn = rms_norm(x, g_ffn) * (1.0 + sc2) + sh2
    u = hn @ w_up
    u_pad = jnp.pad(u, ((0, 0), (CONV_WIDTH - 1, 0), (0, 0)))
    u = sum(conv_w[i] * u_pad[:, i:i + t] for i in range(CONV_WIDTH)) + conv_b
    a, val = jnp.split(u, 2, axis=-1)
    x = x + gt2 * ((jax.nn.silu(a) * val) @ w_down)
    return x


def setup_inputs(seed: int = 0) -> dict:
    key = jax.random.key(seed)
    ks = jax.random.split(key, 19)
    f32 = jnp.float32
    nrm = lambda k, s: jax.random.normal(k, s, dtype=f32)
    gain = lambda k, n: 1.0 + 0.02 * nrm(k, (DEPTH, n))
    return {
        'x': nrm(ks[0], (BATCH, SEQ, D_MODEL)),
        'c': nrm(ks[1], (BATCH, D_MODEL)),
        'w_ada': nrm(ks[2], (DEPTH, D_MODEL, 6 * D_MODEL)) * (0.5 * D_MODEL ** -0.5),
        'b_ada': 0.02 * nrm(ks[3], (DEPTH, 6 * D_MODEL)),
        'g_mix': gain(ks[4], D_MODEL),
        'w_in': nrm(ks[5], (DEPTH, D_MODEL, IN_COLS)) * D_MODEL ** -0.5,
        'b_forget': jax.random.uniform(ks[6], (DEPTH, N_FOX_HEADS), dtype=f32, minval=1.0, maxval=4.0),
        'moba_q_gain': gain(ks[7], HEAD_DIM),
        'moba_k_gain': gain(ks[8], HEAD_DIM),
        'fox_q_gain': gain(ks[9], HEAD_DIM),
        'fox_k_gain': gain(ks[10], HEAD_DIM),
        'w_out': nrm(ks[11], (DEPTH, D_MODEL, D_MODEL)) * D_MODEL ** -0.5,
        'g_ffn': gain(ks[12], D_MODEL),
        'w_up': nrm(ks[13], (DEPTH, D_MODEL, 2 * D_FF)) * D_MODEL ** -0.5,
        'conv_w': nrm(ks[14], (DEPTH, CONV_WIDTH, 2 * D_FF)) * CONV_WIDTH ** -0.5,
        'conv_b': 0.02 * nrm(ks[15], (DEPTH, 2 * D_FF)),
        'w_down': nrm(ks[16], (DEPTH, D_FF, D_MODEL)) * D_FF ** -0.5,
    }


def reference(x, c, w_ada, b_ada, g_mix, w_in, b_forget, moba_q_gain, moba_k_gain,
              fox_q_gain, fox_k_gain, w_out, g_ffn, w_up, conv_w, conv_b, w_down):
    for l in range(DEPTH):
        x = hybrid_layer(x, c, w_ada[l], b_ada[l], g_mix[l], w_in[l], b_forget[l],
                         moba_q_gain[l], moba_k_gain[l], fox_q_gain[l], fox_k_gain[l],
                         w_out[l], g_ffn[l], w_up[l], conv_w[l], conv_b[l], w_down[l])
    return x
```

```python
import functools

import jax
import jax.numpy as jnp
import numpy as np
from jax import lax
from jax.experimental import pallas as pl
from jax.experimental.pallas import tpu as pltpu

D_MODEL = 1024
HEAD_DIM = 64
N_MOBA_HEADS = 8
N_FOX_HEADS = 8
MOBA_WIDTH = N_MOBA_HEADS * HEAD_DIM
FOX_WIDTH = N_FOX_HEADS * HEAD_DIM
MOBA_BLOCK = 256
MOBA_TOPK = 3
ROPE_THETA = 500000.0
ROPE_DIM = HEAD_DIM // 4
D_FF = 2816
CONV_WIDTH = 3
NORM_EPS = 1e-6
NEG_INF = -1e30
QKV_COLS = 3 * MOBA_WIDTH + 3 * FOX_WIDTH
LANES = 128
IN_COLS_PAD = QKV_COLS + LANES
SEQ_TILE = 256
FF_CHUNK = 256
N_FF_CHUNKS = D_FF // FF_CHUNK
HALO = 8
VMEM_LIMIT = 56 * 1024 * 1024

_NT = (((1,), (1,)), ((), ()))


def _bf16(a):
    return a.astype(jnp.bfloat16)


def _dot(a, b):
    return jnp.dot(a, b, preferred_element_type=jnp.float32)


def _dot_nt(a, b):
    return lax.dot_general(a, b, _NT, preferred_element_type=jnp.float32)


def _sigmoid(a):
    return 1.0 / (1.0 + jnp.exp(-a))


def _adaln_kernel(c_ref, w_ref, b_ref, o_ref):
    c = c_ref[...]
    s = c * _sigmoid(c)
    o_ref[...] = _dot(_bf16(s), _bf16(w_ref[...])) + b_ref[...]


def _adaln(c, w_ada, b_ada):
    b, d = c.shape
    n = w_ada.shape[1]
    tn = 1536
    return pl.pallas_call(
        _adaln_kernel,
        grid=(n // tn,),
        in_specs=[pl.BlockSpec((b, d), lambda j: (0, 0)),
                  pl.BlockSpec((d, tn), lambda j: (0, j)),
                  pl.BlockSpec((1, tn), lambda j: (0, j))],
        out_specs=pl.BlockSpec((b, tn), lambda j: (0, j)),
        out_shape=jax.ShapeDtypeStruct((b, n), jnp.float32),
        compiler_params=pltpu.CompilerParams(vmem_limit_bytes=VMEM_LIMIT),
        name="adaln",
    )(c, w_ada, b_ada.reshape(1, n))


def _rms_mod(x, g, sc, sh):
    ms = jnp.mean(x * x, axis=-1, keepdims=True)
    y = x * lax.rsqrt(ms + NORM_EPS)
    return (y * g) * (1.0 + sc) + sh


def _head_norm(p, gain, bd):
    sq = _bf16(p * p)
    half = 2 * LANES
    ss = jnp.concatenate([_dot(sq[:, :half], bd), _dot(sq[:, half:], bd)], axis=1)
    return (p * lax.rsqrt(ss * (1.0 / HEAD_DIM) + NORM_EPS)) * gain


def _rope(y, c, sa, sb):
    half = ROPE_DIM // 2
    outs = []
    for i in range(y.shape[1] // LANES):
        yc = y[:, i * LANES:(i + 1) * LANES]
        up = pltpu.roll(yc, LANES - half, 1)
        dn = pltpu.roll(yc, half, 1)
        outs.append(yc * c + up * sa + dn * sb)
    return jnp.concatenate(outs, axis=1)


def _inproj_kernel(x_ref, sc_ref, sh_ref, g_ref, w_ref, gains_ref, rc_ref, rsa_ref, rsb_ref, bd_ref,
                   mq_ref, mk_ref, mv_ref, fq_ref, fk_ref, fv_ref, fl_ref):
    scale = HEAD_DIM ** -0.5
    hn = _rms_mod(x_ref[0], g_ref[...], sc_ref[0], sh_ref[0])
    proj = _dot(_bf16(hn), w_ref[...])
    bd = bd_ref[...]
    c, sa, sb = rc_ref[...], rsa_ref[...], rsb_ref[...]
    w = MOBA_WIDTH
    mq = _rope(_head_norm(proj[:, 0:w], gains_ref[0:1, :], bd), c, sa, sb)
    mq_ref[0] = _bf16(mq * scale)
    mk = _rope(_head_norm(proj[:, w:2 * w], gains_ref[1:2, :], bd), c, sa, sb)
    mk_ref[0] = _bf16(mk)
    mv_ref[0] = _bf16(proj[:, 2 * w:3 * w])
    fq = _head_norm(proj[:, 3 * w:4 * w], gains_ref[2:3, :], bd)
    fq_ref[0] = _bf16(fq * scale)
    fk = _head_norm(proj[:, 4 * w:5 * w], gains_ref[3:4, :], bd)
    fk_ref[0] = _bf16(fk)
    fv_ref[0] = _bf16(proj[:, 5 * w:6 * w])
    fl_ref[0] = proj[:, QKV_COLS:IN_COLS_PAD]


def _rope_tables(t):
    half = ROPE_DIM // 2
    inv_freq = jnp.power(ROPE_THETA, -2.0 * jnp.arange(half, dtype=jnp.float32) / ROPE_DIM)
    ang = jnp.arange(t, dtype=jnp.float32)[:, None] * inv_freq[None, :]
    cos, sin = jnp.cos(ang), jnp.sin(ang)
    d = np.arange(LANES) % HEAD_DIM
    first = jnp.asarray(d < half)[None, :]
    second = jnp.asarray((d >= half) & (d < ROPE_DIM))[None, :]
    idx = np.where(d < ROPE_DIM, d % half, 0)
    cos_l, sin_l = cos[:, idx], sin[:, idx]
    c = jnp.where(first | second, cos_l, 1.0)
    sa = jnp.where(first, -sin_l, 0.0)
    sb = jnp.where(second, sin_l, 0.0)
    return c, sa, sb


def _inproj(x, sc1, sh1, g_mix, w_pad, gains, tables, bd):
    b, t, d = x.shape
    tm = SEQ_TILE
    row = lambda i, j: (i, 0, 0)
    tile = lambda i, j: (i, j, 0)
    const = lambda i, j: (0, 0)
    wide = jax.ShapeDtypeStruct((b, t, MOBA_WIDTH), jnp.bfloat16)
    out_specs = [pl.BlockSpec((1, tm, MOBA_WIDTH), tile)] * 6 + [pl.BlockSpec((1, tm, LANES), tile)]
    return pl.pallas_call(
        _inproj_kernel,
        grid=(b, t // tm),
        in_specs=[pl.BlockSpec((1, tm, d), tile),
                  pl.BlockSpec((1, 1, d), row),
                  pl.BlockSpec((1, 1, d), row),
                  pl.BlockSpec((1, d), const),
                  pl.BlockSpec((d, IN_COLS_PAD), const),
                  pl.BlockSpec((4, MOBA_WIDTH), const),
                  pl.BlockSpec((tm, LANES), lambda i, j: (j, 0)),
                  pl.BlockSpec((tm, LANES), lambda i, j: (j, 0)),
                  pl.BlockSpec((tm, LANES), lambda i, j: (j, 0)),
                  pl.BlockSpec((2 * LANES, 2 * LANES), const)],
        out_specs=out_specs,
        out_shape=[wide] * 6 + [jax.ShapeDtypeStruct((b, t, LANES), jnp.float32)],
        compiler_params=pltpu.CompilerParams(vmem_limit_bytes=VMEM_LIMIT),
        name="inproj",
    )(x, sc1, sh1, g_mix, w_pad, gains, *tables, bd)


def _split3(a):
    hi = _bf16(a)
    r1 = a - hi.astype(jnp.float32)
    mid = _bf16(r1)
    lo = _bf16(r1 - mid.astype(jnp.float32))
    return hi, mid, lo


def _routing_kernel(mq_ref, mk_ref, fl_ref, bf_ref, selb_ref, cum_ref):
    t = mq_ref.shape[1]
    nb = t // MOBA_BLOCK
    q = mq_ref[0]
    k = mk_ref[0]
    blk_of_col = lax.broadcasted_iota(jnp.int32, (nb, t), 1) // MOBA_BLOCK
    blk_row = lax.broadcasted_iota(jnp.int32, (nb, t), 0)
    ind = _bf16(jnp.where(blk_of_col == blk_row, 1.0, 0.0))
    kmean = _dot(ind, k) * (1.0 / MOBA_BLOCK)
    reps = LANES // nb
    kmt = jnp.concatenate([kmean] * reps, axis=0)
    r_head = lax.broadcasted_iota(jnp.int32, kmt.shape, 0) // nb
    c_head = lax.broadcasted_iota(jnp.int32, kmt.shape, 1) // HEAD_DIM
    kmt = jnp.where(r_head == c_head, kmt, 0.0)
    k_hi = _bf16(kmt)
    k_lo = _bf16(kmt - k_hi.astype(jnp.float32))
    gate = (_dot_nt(q, k_hi) + _dot_nt(q, k_lo)) * (HEAD_DIM ** 0.5)

    lane = lax.broadcasted_iota(jnp.int32, gate.shape, 1)
    j = lane % nb
    own = lax.broadcasted_iota(jnp.int32, gate.shape, 0) // MOBA_BLOCK
    rank = jnp.zeros(gate.shape, jnp.int32)
    for d in range(1, nb):
        up = pltpu.roll(gate, LANES - d, 1)
        dn = pltpu.roll(gate, nb - d, 1)
        wrap = (j + d) >= nb
        other = jnp.where(wrap, dn, up)
        jo = jnp.where(wrap, j + d - nb, j + d)
        beats = (other > gate) | ((other == gate) & (jo < j))
        rank = rank + jnp.where(beats & (jo < own), 1, 0)
    chosen = (j < own) & (rank < MOBA_TOPK)
    selb_ref[0] = jnp.where(chosen, 0.0, NEG_INF)

    z = fl_ref[0] + bf_ref[...]
    logf = -(jnp.maximum(-z, 0.0) + jnp.log1p(jnp.exp(-jnp.abs(z))))
    rr = lax.broadcasted_iota(jnp.int32, (MOBA_BLOCK, MOBA_BLOCK), 0)
    cc = lax.broadcasted_iota(jnp.int32, (MOBA_BLOCK, MOBA_BLOCK), 1)
    tri = _bf16(jnp.where(cc <= rr, 1.0, 0.0))
    carry = jnp.zeros((1, LANES), jnp.float32)
    for i in range(nb):
        blk = logf[i * MOBA_BLOCK:(i + 1) * MOBA_BLOCK, :]
        hi, mid, lo = _split3(blk)
        pre = (_dot(tri, hi) + _dot(tri, mid)) + _dot(tri, lo) + carry
        cum_ref[0, i * MOBA_BLOCK:(i + 1) * MOBA_BLOCK, :] = pre
        carry = pre[MOBA_BLOCK - 1:MOBA_BLOCK, :]


def _routing(mq, mk, fl, bf_pad):
    b, t, w = mq.shape
    full = lambda i: (i, 0, 0)
    return pl.pallas_call(
        _routing_kernel,
        grid=(b,),
        in_specs=[pl.BlockSpec((1, t, w), full),
                  pl.BlockSpec((1, t, w), full),
                  pl.BlockSpec((1, t, LANES), full),
                  pl.BlockSpec((1, LANES), lambda i: (0, 0))],
        out_specs=[pl.BlockSpec((1, t, LANES), full), pl.BlockSpec((1, t, LANES), full)],
        out_shape=[jax.ShapeDtypeStruct((b, t, LANES), jnp.float32)] * 2,
        compiler_params=pltpu.CompilerParams(vmem_limit_bytes=VMEM_LIMIT),
        name="routing",
    )(mq, mk, fl, bf_pad)


def _attn_init(m_ref, l_ref, acc_ref):
    m_ref[...] = jnp.full(m_ref.shape, -jnp.inf, jnp.float32)
    l_ref[...] = jnp.zeros(l_ref.shape, jnp.float32)
    acc_ref[...] = jnp.zeros(acc_ref.shape, jnp.float32)


def _attn_step(h, s, v, m_ref, l_ref, acc_ref, shift=None):
    m_old = m_ref[h]
    row_max = jnp.max(s, axis=1, keepdims=True)
    if shift is not None:
        row_max = row_max + shift
    m_new = jnp.maximum(m_old, row_max)
    m_sub = m_new if shift is None else m_new - shift
    p = jnp.exp(s - m_sub)
    alpha = jnp.exp(m_old - m_new)
    l_ref[h] = alpha * l_ref[h] + jnp.sum(p, axis=1, keepdims=True)
    acc_ref[h] = alpha * acc_ref[h] + _dot(_bf16(p), v)
    m_ref[h] = m_new


def _attn_finish(o_ref, l_ref, acc_ref):
    lane = lax.broadcasted_iota(jnp.int32, (1, LANES), 1)
    o0 = acc_ref[0] / l_ref[0]
    o1 = acc_ref[1] / l_ref[1]
    o_ref[0] = _bf16(jnp.where(lane < HEAD_DIM, o0, o1))


def _head_masked(q2):
    lane = lax.broadcasted_iota(jnp.int32, (1, LANES), 1)
    zero = jnp.zeros_like(q2)
    return [jnp.where(lane < HEAD_DIM, q2, zero), jnp.where(lane >= HEAD_DIM, q2, zero)]


def _causal(tq):
    r = lax.broadcasted_iota(jnp.int32, (tq, tq), 0)
    c = lax.broadcasted_iota(jnp.int32, (tq, tq), 1)
    return c <= r


def _moba_kernel(q_ref, k_ref, v_ref, selb_ref, o_ref, m_ref, l_ref, acc_ref):
    tq = q_ref.shape[1]
    nb = k_ref.shape[1] // tq
    qi = pl.program_id(2)
    qm = _head_masked(q_ref[0])
    _attn_init(m_ref, l_ref, acc_ref)

    for j in range(nb - 1):
        @pl.when(j < qi)
        def _(j=j):
            k = k_ref[0, j * tq:(j + 1) * tq, :]
            v = v_ref[0, j * tq:(j + 1) * tq, :]
            for h in range(2):
                col = h * nb + j
                s = _dot_nt(qm[h], k) + selb_ref[0, 0, :, col:col + 1]
                _attn_step(h, s, v, m_ref, l_ref, acc_ref)

    start = pl.multiple_of(qi * tq, tq)
    k = k_ref[0, pl.ds(start, tq), :]
    v = v_ref[0, pl.ds(start, tq), :]
    keep = _causal(tq)
    for h in range(2):
        s = jnp.where(keep, _dot_nt(qm[h], k), NEG_INF)
        _attn_step(h, s, v, m_ref, l_ref, acc_ref)
    _attn_finish(o_ref, l_ref, acc_ref)


def _fox_kernel(q_ref, k_ref, v_ref, cq_ref, ck_ref, o_ref, m_ref, l_ref, acc_ref):
    tq = q_ref.shape[1]
    nb = k_ref.shape[1] // tq
    qi = pl.program_id(2)
    qm = _head_masked(q_ref[0])
    _attn_init(m_ref, l_ref, acc_ref)
    cq = [cq_ref[0, 0, :, h:h + 1] for h in range(2)]

    for j in range(nb - 1):
        @pl.when(j < qi)
        def _(j=j):
            k = k_ref[0, j * tq:(j + 1) * tq, :]
            v = v_ref[0, j * tq:(j + 1) * tq, :]
            for h in range(2):
                s = _dot_nt(qm[h], k) - ck_ref[0, 0, h:h + 1, j * tq:(j + 1) * tq]
                _attn_step(h, s, v, m_ref, l_ref, acc_ref, shift=cq[h])

    start = pl.multiple_of(qi * tq, tq)
    k = k_ref[0, pl.ds(start, tq), :]
    v = v_ref[0, pl.ds(start, tq), :]
    keep = _causal(tq)
    for h in range(2):
        ck = ck_ref[0, 0, h:h + 1, pl.ds(start, tq)]
        s = jnp.where(keep, _dot_nt(qm[h], k) - ck, NEG_INF)
        _attn_step(h, s, v, m_ref, l_ref, acc_ref, shift=cq[h])
    _attn_finish(o_ref, l_ref, acc_ref)


def _attention(kernel_fn, name, q, k, v, aux, aux_specs):
    b, t, w = q.shape
    tq = SEQ_TILE
    n_pairs = w // LANES
    scratch = [pltpu.VMEM((2, tq, 1), jnp.float32),
               pltpu.VMEM((2, tq, 1), jnp.float32),
               pltpu.VMEM((2, tq, LANES), jnp.float32)]
    return pl.pallas_call(
        kernel_fn,
        grid=(b, n_pairs, t // tq),
        in_specs=[pl.BlockSpec((1, tq, LANES), lambda i, p, j: (i, j, p)),
                  pl.BlockSpec((1, t, LANES), lambda i, p, j: (i, 0, p)),
                  pl.BlockSpec((1, t, LANES), lambda i, p, j: (i, 0, p))] + aux_specs,
        out_specs=pl.BlockSpec((1, tq, LANES), lambda i, p, j: (i, j, p)),
        out_shape=jax.ShapeDtypeStruct((b, t, w), jnp.bfloat16),
        scratch_shapes=scratch,
        compiler_params=pltpu.CompilerParams(vmem_limit_bytes=VMEM_LIMIT),
        name=name,
    )(q, k, v, *aux)


def _ffn_kernel(x_ref, om_ref, of_ref, wo_ref, gt1_ref, g_ref, sc_ref, sh_ref, wup_ref, cv_ref,
                wdn_ref, gt2_ref, out_ref, x1_ref, hn_ref, acc_ref, halo_ref, ubuf_ref):
    tm = x_ref.shape[1]

    @pl.when(pl.program_id(1) == 0)
    def _():
        halo_ref[...] = jnp.zeros(halo_ref.shape, jnp.float32)

    o = jnp.concatenate([om_ref[0], of_ref[0]], axis=1)
    x1 = x_ref[0] + gt1_ref[0] * _dot(o, wo_ref[...])
    x1_ref[...] = x1
    hn_ref[...] = _bf16(_rms_mod(x1, g_ref[...], sc_ref[0], sh_ref[0]))
    acc_ref[...] = jnp.zeros(acc_ref.shape, jnp.float32)

    def chunk(c, carry):
        u = _dot(hn_ref[...], wup_ref[c])
        ubuf_ref[0:HALO, :] = halo_ref[c]
        ubuf_ref[HALO:HALO + tm, :] = u
        halo_ref[c] = u[tm - HALO:tm, :]
        cv = cv_ref[c]
        u1 = ubuf_ref[HALO - 1:HALO - 1 + tm, :]
        u2 = ubuf_ref[HALO - 2:HALO - 2 + tm, :]
        uc = cv[0:1, :] * u2 + cv[1:2, :] * u1 + cv[2:3, :] * u + cv[3:4, :]
        a = uc[:, :FF_CHUNK]
        val = uc[:, FF_CHUNK:]
        hmid = _bf16((a * _sigmoid(a)) * val)
        acc_ref[...] += _dot(hmid, wdn_ref[c])
        return carry

    lax.fori_loop(0, N_FF_CHUNKS, chunk, 0)
    out_ref[0] = x1_ref[...] + gt2_ref[0] * acc_ref[...]


def _ffn(x, om, of, wo, gt1, g_ffn, sc2, sh2, wup_c, cv_c, wdn_c, gt2):
    b, t, d = x.shape
    tm = SEQ_TILE
    row = lambda i, j: (i, 0, 0)
    tile = lambda i, j: (i, j, 0)
    const2 = lambda i, j: (0, 0)
    const3 = lambda i, j: (0, 0, 0)
    once = dict(pipeline_mode=pl.Buffered(1))
    return pl.pallas_call(
        _ffn_kernel,
        grid=(b, t // tm),
        in_specs=[pl.BlockSpec((1, tm, d), tile),
                  pl.BlockSpec((1, tm, MOBA_WIDTH), tile),
                  pl.BlockSpec((1, tm, FOX_WIDTH), tile),
                  pl.BlockSpec((d, d), const2, **once),
                  pl.BlockSpec((1, 1, d), row),
                  pl.BlockSpec((1, d), const2),
                  pl.BlockSpec((1, 1, d), row),
                  pl.BlockSpec((1, 1, d), row),
                  pl.BlockSpec((N_FF_CHUNKS, d, 2 * FF_CHUNK), const3, **once),
                  pl.BlockSpec((N_FF_CHUNKS, HALO, 2 * FF_CHUNK), const3, **once),
                  pl.BlockSpec((N_FF_CHUNKS, FF_CHUNK, d), const3, **once),
                  pl.BlockSpec((1, 1, d), row)],
        out_specs=pl.BlockSpec((1, tm, d), tile),
        out_shape=jax.ShapeDtypeStruct((b, t, d), jnp.float32),
        scratch_shapes=[pltpu.VMEM((tm, d), jnp.float32),
                        pltpu.VMEM((tm, d), jnp.bfloat16),
                        pltpu.VMEM((tm, d), jnp.float32),
                        pltpu.VMEM((N_FF_CHUNKS, HALO, 2 * FF_CHUNK), jnp.float32),
                        pltpu.VMEM((HALO + tm, 2 * FF_CHUNK), jnp.float32)],
        compiler_params=pltpu.CompilerParams(vmem_limit_bytes=VMEM_LIMIT),
        name="ffn",
    )(x, om, of, wo, gt1, g_ffn, sc2, sh2, wup_c, cv_c, wdn_c, gt2)


def _pair_cols(a, width):
    b, t, _ = a.shape
    n_pairs = N_MOBA_HEADS // 2
    return a[:, :, :n_pairs * 2 * width].reshape(b, t, n_pairs, 2 * width).transpose(0, 2, 1, 3)


def _layer(x, c, w_ada, b_ada, g_mix, w_in, b_forget, moba_q_gain, moba_k_gain, fox_q_gain,
           fox_k_gain, w_out, g_ffn, w_up, conv_w, conv_b, w_down):
    b, t, d = x.shape
    nb = t // MOBA_BLOCK
    assert d == D_MODEL and t % SEQ_TILE == 0 and nb * N_MOBA_HEADS <= LANES

    mod = _adaln(c, w_ada, b_ada).reshape(b, 6, 1, d)
    sh1, sc1, gt1, sh2, sc2, gt2 = [mod[:, i] for i in range(6)]

    w_pad = _bf16(jnp.pad(w_in, ((0, 0), (0, IN_COLS_PAD - w_in.shape[1]))))
    gains = jnp.stack([jnp.tile(g, N_MOBA_HEADS) for g in
                       (moba_q_gain, moba_k_gain, fox_q_gain, fox_k_gain)])
    r = np.arange(2 * LANES) // HEAD_DIM
    bd = jnp.asarray(r[:, None] == r[None, :], jnp.bfloat16)
    mq, mk, mv, fq, fk, fv, fl = _inproj(x, sc1, sh1, g_mix.reshape(1, d), w_pad, gains,
                                         _rope_tables(t), bd)

    bf_pad = jnp.pad(b_forget, (0, LANES - N_FOX_HEADS)).reshape(1, LANES)
    selb, cum = _routing(mq, mk, fl, bf_pad)

    tq = SEQ_TILE
    selb_p = _pair_cols(selb, nb)
    cq_p = _pair_cols(cum, 1)
    ck_p = cq_p.transpose(0, 1, 3, 2)
    o_moba = _attention(_moba_kernel, "moba", mq, mk, mv, [selb_p],
                        [pl.BlockSpec((1, 1, tq, 2 * nb), lambda i, p, j: (i, p, j, 0))])
    o_fox = _attention(_fox_kernel, "fox", fq, fk, fv, [cq_p, ck_p],
                       [pl.BlockSpec((1, 1, tq, 2), lambda i, p, j: (i, p, j, 0)),
                        pl.BlockSpec((1, 1, 2, t), lambda i, p, j: (i, p, 0, 0))])

    wu = _bf16(w_up).reshape(d, 2, N_FF_CHUNKS, FF_CHUNK).transpose(2, 0, 1, 3)
    wu = wu.reshape(N_FF_CHUNKS, d, 2 * FF_CHUNK)
    cv = jnp.concatenate([conv_w, conv_b[None, :],
                          jnp.zeros((HALO - CONV_WIDTH - 1, 2 * D_FF), jnp.float32)], axis=0)
    cv = cv.reshape(HALO, 2, N_FF_CHUNKS, FF_CHUNK).transpose(2, 0, 1, 3)
    cv = cv.reshape(N_FF_CHUNKS, HALO, 2 * FF_CHUNK)
    wd = _bf16(w_down).reshape(N_FF_CHUNKS, FF_CHUNK, d)
    return _ffn(x, o_moba, o_fox, _bf16(w_out), gt1, g_ffn.reshape(1, d), sc2, sh2, wu, cv, wd, gt2)


def kernel(x, c, w_ada, b_ada, g_mix, w_in, b_forget, moba_q_gain, moba_k_gain, fox_q_gain,
           fox_k_gain, w_out, g_ffn, w_up, conv_w, conv_b, w_down):
    for l in range(w_ada.shape[0]):
        x = _layer(x, c, w_ada[l], b_ada[l], g_mix[l], w_in[l], b_forget[l], moba_q_gain[l],
                   moba_k_gain[l], fox_q_gain[l], fox_k_gain[l], w_out[l], g_ffn[l], w_up[l],
                   conv_w[l], conv_b[l], w_down[l])
    return x
```

```python
import math

import jax
import jax.numpy as jnp
import numpy as np
from jax import lax
from jax.experimental import pallas as pl
from jax.experimental.pallas import tpu as pltpu

D_MODEL = 1024
HEAD_DIM = 64
N_MOBA_HEADS = 8
N_FOX_HEADS = 8
MOBA_WIDTH = N_MOBA_HEADS * HEAD_DIM
FOX_WIDTH = N_FOX_HEADS * HEAD_DIM
MOBA_BLOCK = 256
MOBA_TOPK = 3
ROPE_THETA = 500000.0
ROPE_DIM = HEAD_DIM // 4
D_FF = 2816
CONV_WIDTH = 3
NORM_EPS = 1e-6
NEG_INF = -1e30
LOG2E = math.log2(math.e)
Q_SCALE = HEAD_DIM ** -0.5 * LOG2E
QKV_COLS = 3 * MOBA_WIDTH + 3 * FOX_WIDTH
LANES = 128
IN_COLS_PAD = QKV_COLS + LANES
SEQ_TILE = 256
FF_CHUNK = 256
N_FF_CHUNKS = D_FF // FF_CHUNK
HALO = 8
N_SPLIT = 3
VMEM_LIMIT = 56 * 1024 * 1024

_NT = (((1,), (1,)), ((), ()))


def _bf16(a):
    return a.astype(jnp.bfloat16)


def _dot(a, b):
    return jnp.dot(a, b, preferred_element_type=jnp.float32)


def _dot_nt(a, b):
    return lax.dot_general(a, b, _NT, preferred_element_type=jnp.float32)


def _sigmoid(a):
    return 1.0 / (1.0 + jnp.exp(-a))


def _adaln_kernel(c_ref, w_ref, b_ref, o_ref):
    c = c_ref[...]
    s = c * _sigmoid(c)
    o_ref[...] = _dot(_bf16(s), _bf16(w_ref[...])) + b_ref[...]


def _adaln(c, w_ada, b_ada):
    b, d = c.shape
    n = w_ada.shape[1]
    tn = 1536
    return pl.pallas_call(
        _adaln_kernel,
        grid=(n // tn,),
        in_specs=[pl.BlockSpec((b, d), lambda j: (0, 0)),
                  pl.BlockSpec((d, tn), lambda j: (0, j)),
                  pl.BlockSpec((1, tn), lambda j: (0, j))],
        out_specs=pl.BlockSpec((b, tn), lambda j: (0, j)),
        out_shape=jax.ShapeDtypeStruct((b, n), jnp.float32),
        compiler_params=pltpu.CompilerParams(vmem_limit_bytes=VMEM_LIMIT),
        name="adaln",
    )(c, w_ada, b_ada.reshape(1, n))


def _rms_mod(x, g, sc, sh):
    ms = jnp.mean(x * x, axis=-1, keepdims=True)
    y = x * lax.rsqrt(ms + NORM_EPS)
    return (y * g) * (1.0 + sc) + sh


def _head_norm(p, gain, bd):
    sq = _bf16(p * p)
    half = 2 * LANES
    ss = jnp.concatenate([_dot(sq[:, :half], bd), _dot(sq[:, half:], bd)], axis=1)
    return (p * lax.rsqrt(ss * (1.0 / HEAD_DIM) + NORM_EPS)) * gain


def _rope(y, c, sa, sb):
    half = ROPE_DIM // 2
    outs = []
    for i in range(y.shape[1] // LANES):
        yc = y[:, i * LANES:(i + 1) * LANES]
        up = pltpu.roll(yc, LANES - half, 1)
        dn = pltpu.roll(yc, half, 1)
        outs.append(yc * c + up * sa + dn * sb)
    return jnp.concatenate(outs, axis=1)


def _inproj_kernel(x_ref, sc_ref, sh_ref, g_ref, w_ref, gains_ref, rc_ref, rsa_ref, rsb_ref, bd_ref,
                   mq_ref, mk_ref, mv_ref, fq_ref, fk_ref, fv_ref, fl_ref):
    hn = _rms_mod(x_ref[0], g_ref[...], sc_ref[0], sh_ref[0])
    proj = _dot(_bf16(hn), w_ref[...])
    bd = bd_ref[...]
    c, sa, sb = rc_ref[...], rsa_ref[...], rsb_ref[...]
    w = MOBA_WIDTH
    mq = _rope(_head_norm(proj[:, 0:w], gains_ref[0:1, :], bd), c, sa, sb)
    mq_ref[0] = _bf16(mq * Q_SCALE)
    mk = _rope(_head_norm(proj[:, w:2 * w], gains_ref[1:2, :], bd), c, sa, sb)
    mk_ref[0] = _bf16(mk)
    mv_ref[0] = _bf16(proj[:, 2 * w:3 * w])
    fq = _head_norm(proj[:, 3 * w:4 * w], gains_ref[2:3, :], bd)
    fq_ref[0] = _bf16(fq * Q_SCALE)
    fk = _head_norm(proj[:, 4 * w:5 * w], gains_ref[3:4, :], bd)
    fk_ref[0] = _bf16(fk)
    fv_ref[0] = _bf16(proj[:, 5 * w:6 * w])
    fl_ref[0] = proj[:, QKV_COLS:IN_COLS_PAD]


def _rope_tables(t):
    half = ROPE_DIM // 2
    inv_freq = jnp.power(ROPE_THETA, -2.0 * jnp.arange(half, dtype=jnp.float32) / ROPE_DIM)
    ang = jnp.arange(t, dtype=jnp.float32)[:, None] * inv_freq[None, :]
    cos, sin = jnp.cos(ang), jnp.sin(ang)
    d = np.arange(LANES) % HEAD_DIM
    first = jnp.asarray(d < half)[None, :]
    second = jnp.asarray((d >= half) & (d < ROPE_DIM))[None, :]
    idx = np.where(d < ROPE_DIM, d % half, 0)
    cos_l, sin_l = cos[:, idx], sin[:, idx]
    c = jnp.where(first | second, cos_l, 1.0)
    sa = jnp.where(first, -sin_l, 0.0)
    sb = jnp.where(second, sin_l, 0.0)
    return c, sa, sb


def _inproj(x, sc1, sh1, g_mix, w_pad, gains, tables, bd):
    b, t, d = x.shape
    tm = SEQ_TILE
    row = lambda i, j: (i, 0, 0)
    tile = lambda i, j: (i, j, 0)
    const = lambda i, j: (0, 0)
    wide = jax.ShapeDtypeStruct((b, t, MOBA_WIDTH), jnp.bfloat16)
    out_specs = [pl.BlockSpec((1, tm, MOBA_WIDTH), tile)] * 6 + [pl.BlockSpec((1, tm, LANES), tile)]
    return pl.pallas_call(
        _inproj_kernel,
        grid=(b, t // tm),
        in_specs=[pl.BlockSpec((1, tm, d), tile),
                  pl.BlockSpec((1, 1, d), row),
                  pl.BlockSpec((1, 1, d), row),
                  pl.BlockSpec((1, d), const),
                  pl.BlockSpec((d, IN_COLS_PAD), const),
                  pl.BlockSpec((4, MOBA_WIDTH), const),
                  pl.BlockSpec((tm, LANES), lambda i, j: (j, 0)),
                  pl.BlockSpec((tm, LANES), lambda i, j: (j, 0)),
                  pl.BlockSpec((tm, LANES), lambda i, j: (j, 0)),
                  pl.BlockSpec((2 * LANES, 2 * LANES), const)],
        out_specs=out_specs,
        out_shape=[wide] * 6 + [jax.ShapeDtypeStruct((b, t, LANES), jnp.float32)],
        compiler_params=pltpu.CompilerParams(vmem_limit_bytes=VMEM_LIMIT),
        name="inproj",
    )(x, sc1, sh1, g_mix, w_pad, gains, *tables, bd)


def _split3(a):
    hi = _bf16(a)
    r1 = a - hi.astype(jnp.float32)
    mid = _bf16(r1)
    lo = _bf16(r1 - mid.astype(jnp.float32))
    return hi, mid, lo


def _aux_base(head):
    pair, h = divmod(head, 2)
    return pair * LANES + (HEAD_DIM if h == 0 else 0)


def _placements(nb):
    pm = np.zeros((LANES, MOBA_WIDTH), np.float32)
    pq = np.zeros((N_SPLIT, LANES, FOX_WIDTH), np.float32)
    pk = np.zeros((N_SPLIT, LANES, FOX_WIDTH), np.float32)
    ones = np.zeros((2, FOX_WIDTH), np.float32)
    for head in range(N_MOBA_HEADS):
        base = _aux_base(head)
        for j in range(nb):
            pm[head * nb + j, base + j] = 1.0
        for s in range(N_SPLIT):
            pq[s, head, base + s] = 1.0
            pk[s, head, base + N_SPLIT + s] = 1.0
            ones[0, base + N_SPLIT + s] = 1.0
            ones[1, base + s] = 1.0
    return (jnp.asarray(pm, jnp.bfloat16), jnp.asarray(pq, jnp.bfloat16),
            jnp.asarray(pk, jnp.bfloat16), jnp.asarray(ones))


def _routing_kernel(mq_ref, mk_ref, fl_ref, bf_ref, pm_ref, pq_ref, pk_ref, ones_ref,
                    mqa_ref, fqa_ref, fka_ref):
    t = mq_ref.shape[1]
    nb = t // MOBA_BLOCK
    q = mq_ref[0]
    k = mk_ref[0]
    blk_of_col = lax.broadcasted_iota(jnp.int32, (nb, t), 1) // MOBA_BLOCK
    blk_row = lax.broadcasted_iota(jnp.int32, (nb, t), 0)
    ind = _bf16(jnp.where(blk_of_col == blk_row, 1.0, 0.0))
    kmean = _dot(ind, k) * (1.0 / MOBA_BLOCK)
    reps = LANES // nb
    kmt = jnp.concatenate([kmean] * reps, axis=0)
    r_head = lax.broadcasted_iota(jnp.int32, kmt.shape, 0) // nb
    c_head = lax.broadcasted_iota(jnp.int32, kmt.shape, 1) // HEAD_DIM
    kmt = jnp.where(r_head == c_head, kmt, 0.0)
    k_hi = _bf16(kmt)
    k_lo = _bf16(kmt - k_hi.astype(jnp.float32))
    gate = (_dot_nt(q, k_hi) + _dot_nt(q, k_lo)) * (1.0 / Q_SCALE)

    lane = lax.broadcasted_iota(jnp.int32, gate.shape, 1)
    j = lane % nb
    own = lax.broadcasted_iota(jnp.int32, gate.shape, 0) // MOBA_BLOCK
    rank = jnp.zeros(gate.shape, jnp.int32)
    for d in range(1, nb):
        up = pltpu.roll(gate, LANES - d, 1)
        dn = pltpu.roll(gate, nb - d, 1)
        wrap = (j + d) >= nb
        other = jnp.where(wrap, dn, up)
        jo = jnp.where(wrap, j + d - nb, j + d)
        beats = (other > gate) | ((other == gate) & (jo < j))
        rank = rank + jnp.where(beats & (jo < own), 1, 0)
    attend = ((j < own) & (rank < MOBA_TOPK)) | (j == own)
    selb = _bf16(jnp.where(attend, 0.0, NEG_INF))
    mqa_ref[0] = _bf16(_dot(selb, pm_ref[...]))

    z = fl_ref[0] + bf_ref[...]
    logf = -(jnp.maximum(-z, 0.0) + jnp.log1p(jnp.exp(-jnp.abs(z))))
    rr = lax.broadcasted_iota(jnp.int32, (MOBA_BLOCK, MOBA_BLOCK), 0)
    cc = lax.broadcasted_iota(jnp.int32, (MOBA_BLOCK, MOBA_BLOCK), 1)
    tri = _bf16(jnp.where(cc <= rr, 1.0, 0.0))
    carry = jnp.zeros((1, LANES), jnp.float32)
    for i in range(nb):
        rows = slice(i * MOBA_BLOCK, (i + 1) * MOBA_BLOCK)
        hi, mid, lo = _split3(logf[rows, :])
        cum = (_dot(tri, hi) + _dot(tri, mid)) + _dot(tri, lo) + carry
        carry = cum[MOBA_BLOCK - 1:MOBA_BLOCK, :]
        parts = _split3(cum * LOG2E)
        cq = sum(_dot(parts[s], pq_ref[s]) for s in range(N_SPLIT))
        ck = sum(_dot(parts[s], pk_ref[s]) for s in range(N_SPLIT))
        fqa_ref[0, rows, :] = _bf16(cq + ones_ref[0:1, :])
        fka_ref[0, rows, :] = _bf16(ones_ref[1:2, :] - ck)


def _routing(mq, mk, fl, bf_pad, placements):
    b, t, w = mq.shape
    pm, pq, pk, ones = placements
    full = lambda i: (i, 0, 0)
    c2 = lambda i: (0, 0)
    c3 = lambda i: (0, 0, 0)
    wide = jax.ShapeDtypeStruct((b, t, w), jnp.bfloat16)
    return pl.pallas_call(
        _routing_kernel,
        grid=(b,),
        in_specs=[pl.BlockSpec((1, t, w), full),
                  pl.BlockSpec((1, t, w), full),
                  pl.BlockSpec((1, t, LANES), full),
                  pl.BlockSpec((1, LANES), c2),
                  pl.BlockSpec(pm.shape, c2),
                  pl.BlockSpec(pq.shape, c3),
                  pl.BlockSpec(pk.shape, c3),
                  pl.BlockSpec(ones.shape, c2)],
        out_specs=[pl.BlockSpec((1, t, w), full)] * 3,
        out_shape=[wide] * 3,
        compiler_params=pltpu.CompilerParams(vmem_limit_bytes=VMEM_LIMIT),
        name="routing",
    )(mq, mk, fl, bf_pad, pm, pq, pk, ones)


def _attn_kernel(*refs, moba):
    if moba:
        q_ref, qa_ref, k_ref, v_ref, o_ref, kx_ref, vx_ref = refs
    else:
        q_ref, qa_ref, k_ref, v_ref, ka_ref, o_ref, kx_ref, vx_ref = refs
    tq = q_ref.shape[1]
    t = k_ref.shape[1]
    nb = t // tq
    qi = pl.program_id(2)
    low = lax.broadcasted_iota(jnp.int32, (1, LANES), 1) < HEAD_DIM

    @pl.when(qi == 0)
    def _():
        k2 = k_ref[0]
        v2 = v_ref[0]
        if moba:
            blk = lax.broadcasted_iota(jnp.int32, (t, LANES), 0) // MOBA_BLOCK
            ln = lax.broadcasted_iota(jnp.int32, (t, LANES), 1) % HEAD_DIM
            ka = _bf16(jnp.where(ln == blk, 1.0, 0.0))
        else:
            ka = ka_ref[0]
        one = jnp.ones_like(v2)
        kx_ref[0] = jnp.where(low, k2, ka)
        kx_ref[1] = jnp.where(low, ka, k2)
        vx_ref[0] = jnp.where(low, v2, one)
        vx_ref[1] = jnp.where(low, one, v2)

    q2 = q_ref[0]
    qa = qa_ref[0]
    qx = [jnp.where(low, q2, qa), jnp.where(low, qa, q2)]
    r = lax.broadcasted_iota(jnp.int32, (tq, tq), 0)
    c = lax.broadcasted_iota(jnp.int32, (tq, tq), 1)
    causal = c <= r

    for n in range(nb):
        @pl.when(qi == n)
        def _(n=n):
            outs = []
            for h in range(2):
                sd = jnp.where(causal, _dot_nt(qx[h], kx_ref[h, n * tq:(n + 1) * tq, :]), NEG_INF)
                m = jnp.max(sd, axis=1, keepdims=True)
                if n > 0:
                    sp = _dot_nt(qx[h], kx_ref[h, 0:n * tq, :])
                    m = jnp.maximum(m, jnp.max(sp, axis=1, keepdims=True))
                acc = _dot(_bf16(jnp.exp2(sd - m)), vx_ref[h, n * tq:(n + 1) * tq, :])
                if n > 0:
                    acc = acc + _dot(_bf16(jnp.exp2(sp - m)), vx_ref[h, 0:n * tq, :])
                outs.append(acc / pltpu.roll(acc, HEAD_DIM, 1))
            o_ref[0] = _bf16(jnp.where(low, outs[0], outs[1]))


def _attention(moba, q, qa, k, v, ka=None):
    b, t, w = q.shape
    tq = SEQ_TILE
    n_pairs = w // LANES
    qspec = pl.BlockSpec((1, tq, LANES), lambda i, p, j: (i, j, p))
    kspec = pl.BlockSpec((1, t, LANES), lambda i, p, j: (i, 0, p))
    args, specs = [q, qa, k, v], [qspec, qspec, kspec, kspec]
    if not moba:
        args.append(ka)
        specs.append(kspec)
    kern = lambda *refs: _attn_kernel(*refs, moba=moba)
    return pl.pallas_call(
        kern,
        grid=(b, n_pairs, t // tq),
        in_specs=specs,
        out_specs=qspec,
        out_shape=jax.ShapeDtypeStruct((b, t, w), jnp.bfloat16),
        scratch_shapes=[pltpu.VMEM((2, t, LANES), jnp.bfloat16),
                        pltpu.VMEM((2, t, LANES), jnp.bfloat16)],
        compiler_params=pltpu.CompilerParams(vmem_limit_bytes=VMEM_LIMIT),
        name="moba" if moba else "fox",
    )(*args)


def _ffn_kernel(x_ref, om_ref, of_ref, wo_ref, gt1_ref, g_ref, sc_ref, sh_ref, wup_ref, cv_ref,
                wdn_ref, gt2_ref, out_ref, x1_ref, hn_ref, acc_ref, halo_ref, ubuf_ref):
    tm = x_ref.shape[1]

    @pl.when(pl.program_id(1) == 0)
    def _():
        halo_ref[...] = jnp.zeros(halo_ref.shape, jnp.float32)

    o = jnp.concatenate([om_ref[0], of_ref[0]], axis=1)
    x1 = x_ref[0] + gt1_ref[0] * _dot(o, wo_ref[...])
    x1_ref[...] = x1
    hn_ref[...] = _bf16(_rms_mod(x1, g_ref[...], sc_ref[0], sh_ref[0]))
    acc_ref[...] = jnp.zeros(acc_ref.shape, jnp.float32)

    def chunk(c, carry):
        u = _dot(hn_ref[...], wup_ref[c])
        ubuf_ref[0:HALO, :] = halo_ref[c]
        ubuf_ref[HALO:HALO + tm, :] = u
        halo_ref[c] = u[tm - HALO:tm, :]
        cv = cv_ref[c]
        u1 = ubuf_ref[HALO - 1:HALO - 1 + tm, :]
        u2 = ubuf_ref[HALO - 2:HALO - 2 + tm, :]
        uc = cv[0:1, :] * u2 + cv[1:2, :] * u1 + cv[2:3, :] * u + cv[3:4, :]
        a = uc[:, :FF_CHUNK]
        val = uc[:, FF_CHUNK:]
        hmid = _bf16((a * _sigmoid(a)) * val)
        acc_ref[...] += _dot(hmid, wdn_ref[c])
        return carry

    lax.fori_loop(0, N_FF_CHUNKS, chunk, 0)
    out_ref[0] = x1_ref[...] + gt2_ref[0] * acc_ref[...]


def _ffn(x, om, of, wo, gt1, g_ffn, sc2, sh2, wup_c, cv_c, wdn_c, gt2):
    b, t, d = x.shape
    tm = SEQ_TILE
    row = lambda i, j: (i, 0, 0)
    tile = lambda i, j: (i, j, 0)
    const2 = lambda i, j: (0, 0)
    const3 = lambda i, j: (0, 0, 0)
    once = dict(pipeline_mode=pl.Buffered(1))
    return pl.pallas_call(
        _ffn_kernel,
        grid=(b, t // tm),
        in_specs=[pl.BlockSpec((1, tm, d), tile),
                  pl.BlockSpec((1, tm, MOBA_WIDTH), tile),
                  pl.BlockSpec((1, tm, FOX_WIDTH), tile),
                  pl.BlockSpec((d, d), const2, **once),
                  pl.BlockSpec((1, 1, d), row),
                  pl.BlockSpec((1, d), const2),
                  pl.BlockSpec((1, 1, d), row),
                  pl.BlockSpec((1, 1, d), row),
                  pl.BlockSpec((N_FF_CHUNKS, d, 2 * FF_CHUNK), const3, **once),
                  pl.BlockSpec((N_FF_CHUNKS, HALO, 2 * FF_CHUNK), const3, **once),
                  pl.BlockSpec((N_FF_CHUNKS, FF_CHUNK, d), const3, **once),
                  pl.BlockSpec((1, 1, d), row)],
        out_specs=pl.BlockSpec((1, tm, d), tile),
        out_shape=jax.ShapeDtypeStruct((b, t, d), jnp.float32),
        scratch_shapes=[pltpu.VMEM((tm, d), jnp.float32),
                        pltpu.VMEM((tm, d), jnp.bfloat16),
                        pltpu.VMEM((tm, d), jnp.float32),
                        pltpu.VMEM((N_FF_CHUNKS, HALO, 2 * FF_CHUNK), jnp.float32),
                        pltpu.VMEM((HALO + tm, 2 * FF_CHUNK), jnp.float32)],
        compiler_params=pltpu.CompilerParams(vmem_limit_bytes=VMEM_LIMIT),
        name="ffn",
    )(x, om, of, wo, gt1, g_ffn, sc2, sh2, wup_c, cv_c, wdn_c, gt2)


def _layer(x, c, w_ada, b_ada, g_mix, w_in, b_forget, moba_q_gain, moba_k_gain, fox_q_gain,
           fox_k_gain, w_out, g_ffn, w_up, conv_w, conv_b, w_down):
    b, t, d = x.shape
    nb = t // MOBA_BLOCK
    assert d == D_MODEL and t % SEQ_TILE == 0 and nb <= HEAD_DIM - 2 * N_SPLIT
    assert nb * N_MOBA_HEADS <= LANES

    mod = _adaln(c, w_ada, b_ada).reshape(b, 6, 1, d)
    sh1, sc1, gt1, sh2, sc2, gt2 = [mod[:, i] for i in range(6)]

    w_pad = _bf16(jnp.pad(w_in, ((0, 0), (0, IN_COLS_PAD - w_in.shape[1]))))
    gains = jnp.stack([jnp.tile(g, N_MOBA_HEADS) for g in
                       (moba_q_gain, moba_k_gain, fox_q_gain, fox_k_gain)])
    r = np.arange(2 * LANES) // HEAD_DIM
    bd = jnp.asarray(r[:, None] == r[None, :], jnp.bfloat16)
    mq, mk, mv, fq, fk, fv, fl = _inproj(x, sc1, sh1, g_mix.reshape(1, d), w_pad, gains,
                                         _rope_tables(t), bd)

    bf_pad = jnp.pad(b_forget, (0, LANES - N_FOX_HEADS)).reshape(1, LANES)
    mqa, fqa, fka = _routing(mq, mk, fl, bf_pad, _placements(nb))
    o_moba = _attention(True, mq, mqa, mk, mv)
    o_fox = _attention(False, fq, fqa, fk, fv, fka)

    wu = _bf16(w_up).reshape(d, 2, N_FF_CHUNKS, FF_CHUNK).transpose(2, 0, 1, 3)
    wu = wu.reshape(N_FF_CHUNKS, d, 2 * FF_CHUNK)
    cv = jnp.concatenate([conv_w, conv_b[None, :],
                          jnp.zeros((HALO - CONV_WIDTH - 1, 2 * D_FF), jnp.float32)], axis=0)
    cv = cv.reshape(HALO, 2, N_FF_CHUNKS, FF_CHUNK).transpose(2, 0, 1, 3)
    cv = cv.reshape(N_FF_CHUNKS, HALO, 2 * FF_CHUNK)
    wd = _bf16(w_down).reshape(N_FF_CHUNKS, FF_CHUNK, d)
    return _ffn(x, o_moba, o_fox, _bf16(w_out), gt1, g_ffn.reshape(1, d), sc2, sh2, wu, cv, wd, gt2)


def kernel(x, c, w_ada, b_ada, g_mix, w_in, b_forget, moba_q_gain, moba_k_gain, fox_q_gain,
           fox_k_gain, w_out, g_ffn, w_up, conv_w, conv_b, w_down):
    for l in range(w_ada.shape[0]):
        x = _layer(x, c, w_ada[l], b_ada[l], g_mix[l], w_in[l], b_forget[l], moba_q_gain[l],
                   moba_k_gain[l], fox_q_gain[l], fox_k_gain[l], w_out[l], g_ffn[l], w_up[l],
                   conv_w[l], conv_b[l], w_down[l])
    return x
```

```python
import math

import jax
import jax.numpy as jnp
import numpy as np
from jax import lax
from jax.experimental import pallas as pl
from jax.experimental.pallas import tpu as pltpu

D_MODEL = 1024
HEAD_DIM = 64
N_MOBA_HEADS = 8
N_FOX_HEADS = 8
MOBA_WIDTH = N_MOBA_HEADS * HEAD_DIM
FOX_WIDTH = N_FOX_HEADS * HEAD_DIM
MOBA_BLOCK = 256
MOBA_TOPK = 3
ROPE_THETA = 500000.0
ROPE_DIM = HEAD_DIM // 4
D_FF = 2816
CONV_WIDTH = 3
NORM_EPS = 1e-6
NEG_INF = -1e30
LOG2E = math.log2(math.e)
Q_SCALE = HEAD_DIM ** -0.5 * LOG2E
QKV_COLS = 3 * MOBA_WIDTH + 3 * FOX_WIDTH
LANES = 128
IN_COLS_PAD = QKV_COLS + LANES
SEQ_TILE = 256
FF_CHUNK = 256
N_FF_CHUNKS = D_FF // FF_CHUNK
HALO = 8
N_SPLIT = 3
VMEM_LIMIT = 56 * 1024 * 1024

_NT = (((1,), (1,)), ((), ()))


def _bf16(a):
    return a.astype(jnp.bfloat16)


def _dot(a, b):
    return jnp.dot(a, b, preferred_element_type=jnp.float32)


def _dot_nt(a, b):
    return lax.dot_general(a, b, _NT, preferred_element_type=jnp.float32)


def _sigmoid(a):
    return 1.0 / (1.0 + jnp.exp(-a))


def _adaln_kernel(c_ref, w_ref, b_ref, o_ref):
    c = c_ref[...]
    s = c * _sigmoid(c)
    o_ref[...] = _dot(_bf16(s), _bf16(w_ref[...])) + b_ref[...]


def _adaln(c, w_ada, b_ada):
    b, d = c.shape
    n = w_ada.shape[1]
    tn = 1536
    return pl.pallas_call(
        _adaln_kernel,
        grid=(n // tn,),
        in_specs=[pl.BlockSpec((b, d), lambda j: (0, 0)),
                  pl.BlockSpec((d, tn), lambda j: (0, j)),
                  pl.BlockSpec((1, tn), lambda j: (0, j))],
        out_specs=pl.BlockSpec((b, tn), lambda j: (0, j)),
        out_shape=jax.ShapeDtypeStruct((b, n), jnp.float32),
        compiler_params=pltpu.CompilerParams(vmem_limit_bytes=VMEM_LIMIT),
        name="adaln",
    )(c, w_ada, b_ada.reshape(1, n))


def _rms_mod(x, g, sc, sh):
    ms = jnp.mean(x * x, axis=-1, keepdims=True)
    y = x * lax.rsqrt(ms + NORM_EPS)
    return (y * g) * (1.0 + sc) + sh


def _head_norm(p, gain, bd):
    sq = _bf16(p * p)
    half = 2 * LANES
    ss = jnp.concatenate([_dot(sq[:, :half], bd), _dot(sq[:, half:], bd)], axis=1)
    return (p * lax.rsqrt(ss * (1.0 / HEAD_DIM) + NORM_EPS)) * gain


def _rope(y, c, sa, sb):
    half = ROPE_DIM // 2
    outs = []
    for i in range(y.shape[1] // LANES):
        yc = y[:, i * LANES:(i + 1) * LANES]
        up = pltpu.roll(yc, LANES - half, 1)
        dn = pltpu.roll(yc, half, 1)
        outs.append(yc * c + up * sa + dn * sb)
    return jnp.concatenate(outs, axis=1)


def _inproj_kernel(x_ref, sc_ref, sh_ref, g_ref, w_ref, gains_ref, rc_ref, rsa_ref, rsb_ref, bd_ref,
                   mq_ref, mk_ref, mv_ref, fq_ref, fk_ref, fv_ref, fl_ref):
    hn = _rms_mod(x_ref[0], g_ref[...], sc_ref[0], sh_ref[0])
    proj = _dot(_bf16(hn), w_ref[...])
    bd = bd_ref[...]
    c, sa, sb = rc_ref[...], rsa_ref[...], rsb_ref[...]
    w = MOBA_WIDTH
    mq = _rope(_head_norm(proj[:, 0:w], gains_ref[0:1, :], bd), c, sa, sb)
    mq_ref[0] = _bf16(mq * Q_SCALE)
    mk = _rope(_head_norm(proj[:, w:2 * w], gains_ref[1:2, :], bd), c, sa, sb)
    mk_ref[0] = _bf16(mk)
    mv_ref[0] = _bf16(proj[:, 2 * w:3 * w])
    fq = _head_norm(proj[:, 3 * w:4 * w], gains_ref[2:3, :], bd)
    fq_ref[0] = _bf16(fq * Q_SCALE)
    fk = _head_norm(proj[:, 4 * w:5 * w], gains_ref[3:4, :], bd)
    fk_ref[0] = _bf16(fk)
    fv_ref[0] = _bf16(proj[:, 5 * w:6 * w])
    fl_ref[0] = proj[:, QKV_COLS:IN_COLS_PAD]


def _rope_tables(t):
    half = ROPE_DIM // 2
    inv_freq = jnp.power(ROPE_THETA, -2.0 * jnp.arange(half, dtype=jnp.float32) / ROPE_DIM)
    ang = jnp.arange(t, dtype=jnp.float32)[:, None] * inv_freq[None, :]
    cos, sin = jnp.cos(ang), jnp.sin(ang)
    d = np.arange(LANES) % HEAD_DIM
    first = jnp.asarray(d < half)[None, :]
    second = jnp.asarray((d >= half) & (d < ROPE_DIM))[None, :]
    idx = np.where(d < ROPE_DIM, d % half, 0)
    cos_l, sin_l = cos[:, idx], sin[:, idx]
    c = jnp.where(first | second, cos_l, 1.0)
    sa = jnp.where(first, -sin_l, 0.0)
    sb = jnp.where(second, sin_l, 0.0)
    return c, sa, sb


def _inproj(x, sc1, sh1, g_mix, w_pad, gains, tables, bd):
    b, t, d = x.shape
    tm = SEQ_TILE
    row = lambda i, j: (i, 0, 0)
    tile = lambda i, j: (i, j, 0)
    const = lambda i, j: (0, 0)
    wide = jax.ShapeDtypeStruct((b, t, MOBA_WIDTH), jnp.bfloat16)
    out_specs = [pl.BlockSpec((1, tm, MOBA_WIDTH), tile)] * 6 + [pl.BlockSpec((1, tm, LANES), tile)]
    return pl.pallas_call(
        _inproj_kernel,
        grid=(b, t // tm),
        in_specs=[pl.BlockSpec((1, tm, d), tile),
                  pl.BlockSpec((1, 1, d), row),
                  pl.BlockSpec((1, 1, d), row),
                  pl.BlockSpec((1, d), const),
                  pl.BlockSpec((d, IN_COLS_PAD), const),
                  pl.BlockSpec((4, MOBA_WIDTH), const),
                  pl.BlockSpec((tm, LANES), lambda i, j: (j, 0)),
                  pl.BlockSpec((tm, LANES), lambda i, j: (j, 0)),
                  pl.BlockSpec((tm, LANES), lambda i, j: (j, 0)),
                  pl.BlockSpec((2 * LANES, 2 * LANES), const)],
        out_specs=out_specs,
        out_shape=[wide] * 6 + [jax.ShapeDtypeStruct((b, t, LANES), jnp.float32)],
        compiler_params=pltpu.CompilerParams(vmem_limit_bytes=VMEM_LIMIT),
        name="inproj",
    )(x, sc1, sh1, g_mix, w_pad, gains, *tables, bd)


def _split3(a):
    hi = _bf16(a)
    r1 = a - hi.astype(jnp.float32)
    mid = _bf16(r1)
    lo = _bf16(r1 - mid.astype(jnp.float32))
    return hi, mid, lo


def _aux_base(head):
    pair, h = divmod(head, 2)
    return pair * LANES + (HEAD_DIM if h == 0 else 0)


def _placements(nb):
    pm = np.zeros((LANES, MOBA_WIDTH), np.float32)
    pq = np.zeros((N_SPLIT, LANES, FOX_WIDTH), np.float32)
    pk = np.zeros((N_SPLIT, LANES, FOX_WIDTH), np.float32)
    ones = np.zeros((2, FOX_WIDTH), np.float32)
    for head in range(N_MOBA_HEADS):
        base = _aux_base(head)
        for j in range(nb):
            pm[head * nb + j, base + j] = 1.0
        for s in range(N_SPLIT):
            pq[s, head, base + s] = 1.0
            pk[s, head, base + N_SPLIT + s] = 1.0
            ones[0, base + N_SPLIT + s] = 1.0
            ones[1, base + s] = 1.0
    return (jnp.asarray(pm, jnp.bfloat16), jnp.asarray(pq, jnp.bfloat16),
            jnp.asarray(pk, jnp.bfloat16), jnp.asarray(ones))


def _routing_kernel(mq_ref, mk_ref, fl_ref, bf_ref, pm_ref, pq_ref, pk_ref, ones_ref,
                    mqa_ref, fqa_ref, fka_ref):
    t = mq_ref.shape[1]
    nb = t // MOBA_BLOCK
    q = mq_ref[0]
    k = mk_ref[0]
    blk_of_col = lax.broadcasted_iota(jnp.int32, (nb, t), 1) // MOBA_BLOCK
    blk_row = lax.broadcasted_iota(jnp.int32, (nb, t), 0)
    ind = _bf16(jnp.where(blk_of_col == blk_row, 1.0, 0.0))
    kmean = _dot(ind, k) * (1.0 / MOBA_BLOCK)
    reps = LANES // nb
    kmt = jnp.concatenate([kmean] * reps, axis=0)
    r_head = lax.broadcasted_iota(jnp.int32, kmt.shape, 0) // nb
    c_head = lax.broadcasted_iota(jnp.int32, kmt.shape, 1) // HEAD_DIM
    kmt = jnp.where(r_head == c_head, kmt, 0.0)
    k_hi = _bf16(kmt)
    k_lo = _bf16(kmt - k_hi.astype(jnp.float32))
    gate = (_dot_nt(q, k_hi) + _dot_nt(q, k_lo)) * (1.0 / Q_SCALE)

    lane = lax.broadcasted_iota(jnp.int32, gate.shape, 1)
    j = lane % nb
    own = lax.broadcasted_iota(jnp.int32, gate.shape, 0) // MOBA_BLOCK
    rank = jnp.zeros(gate.shape, jnp.int32)
    for d in range(1, nb):
        up = pltpu.roll(gate, LANES - d, 1)
        dn = pltpu.roll(gate, nb - d, 1)
        wrap = (j + d) >= nb
        other = jnp.where(wrap, dn, up)
        jo = jnp.where(wrap, j + d - nb, j + d)
        beats = (other > gate) | ((other == gate) & (jo < j))
        rank = rank + jnp.where(beats & (jo < own), 1, 0)
    attend = ((j < own) & (rank < MOBA_TOPK)) | (j == own)
    selb = _bf16(jnp.where(attend, 0.0, NEG_INF))
    mqa_ref[0] = _bf16(_dot(selb, pm_ref[...]))

    z = fl_ref[0] + bf_ref[...]
    logf = -(jnp.maximum(-z, 0.0) + jnp.log1p(jnp.exp(-jnp.abs(z))))
    rr = lax.broadcasted_iota(jnp.int32, (MOBA_BLOCK, MOBA_BLOCK), 0)
    cc = lax.broadcasted_iota(jnp.int32, (MOBA_BLOCK, MOBA_BLOCK), 1)
    tri = _bf16(jnp.where(cc <= rr, 1.0, 0.0))
    carry = jnp.zeros((1, LANES), jnp.float32)
    for i in range(nb):
        rows = slice(i * MOBA_BLOCK, (i + 1) * MOBA_BLOCK)
        hi, mid, lo = _split3(logf[rows, :])
        cum = (_dot(tri, hi) + _dot(tri, mid)) + _dot(tri, lo) + carry
        carry = cum[MOBA_BLOCK - 1:MOBA_BLOCK, :]
        parts = _split3(cum * LOG2E)
        cq = sum(_dot(parts[s], pq_ref[s]) for s in range(N_SPLIT))
        ck = sum(_dot(parts[s], pk_ref[s]) for s in range(N_SPLIT))
        fqa_ref[0, rows, :] = _bf16(cq + ones_ref[0:1, :])
        fka_ref[0, rows, :] = _bf16(ones_ref[1:2, :] - ck)


def _routing(mq, mk, fl, bf_pad, placements):
    b, t, w = mq.shape
    pm, pq, pk, ones = placements
    full = lambda i: (i, 0, 0)
    c2 = lambda i: (0, 0)
    c3 = lambda i: (0, 0, 0)
    wide = jax.ShapeDtypeStruct((b, t, w), jnp.bfloat16)
    return pl.pallas_call(
        _routing_kernel,
        grid=(b,),
        in_specs=[pl.BlockSpec((1, t, w), full),
                  pl.BlockSpec((1, t, w), full),
                  pl.BlockSpec((1, t, LANES), full),
                  pl.BlockSpec((1, LANES), c2),
                  pl.BlockSpec(pm.shape, c2),
                  pl.BlockSpec(pq.shape, c3),
                  pl.BlockSpec(pk.shape, c3),
                  pl.BlockSpec(ones.shape, c2)],
        out_specs=[pl.BlockSpec((1, t, w), full)] * 3,
        out_shape=[wide] * 3,
        compiler_params=pltpu.CompilerParams(vmem_limit_bytes=VMEM_LIMIT),
        name="routing",
    )(mq, mk, fl, bf_pad, pm, pq, pk, ones)


def _attn_kernel(*refs, moba):
    if moba:
        q_ref, qa_ref, k_ref, v_ref, o_ref, kx_ref, vx_ref = refs
    else:
        q_ref, qa_ref, k_ref, v_ref, ka_ref, o_ref, kx_ref, vx_ref = refs
    tq = q_ref.shape[1]
    t = k_ref.shape[1]
    nb = t // tq
    qi = pl.program_id(2)
    low = lax.broadcasted_iota(jnp.int32, (1, LANES), 1) < HEAD_DIM

    @pl.when(qi == 0)
    def _():
        k2 = k_ref[0]
        v2 = v_ref[0]
        if moba:
            blk = lax.broadcasted_iota(jnp.int32, (t, LANES), 0) // MOBA_BLOCK
            ln = lax.broadcasted_iota(jnp.int32, (t, LANES), 1) % HEAD_DIM
            ka = _bf16(jnp.where(ln == blk, 1.0, 0.0))
        else:
            ka = ka_ref[0]
        one = jnp.ones_like(v2)
        kx_ref[0] = jnp.where(low, k2, ka)
        kx_ref[1] = jnp.where(low, ka, k2)
        vx_ref[0] = jnp.where(low, v2, one)
        vx_ref[1] = jnp.where(low, one, v2)

    q2 = q_ref[0]
    qa = qa_ref[0]
    qx = [jnp.where(low, q2, qa), jnp.where(low, qa, q2)]
    r = lax.broadcasted_iota(jnp.int32, (tq, tq), 0)
    c = lax.broadcasted_iota(jnp.int32, (tq, tq), 1)
    causal = c <= r

    for n in range(nb):
        @pl.when(qi == n)
        def _(n=n):
            outs = []
            for h in range(2):
                sd = jnp.where(causal, _dot_nt(qx[h], kx_ref[h, n * tq:(n + 1) * tq, :]), NEG_INF)
                m = jnp.max(sd, axis=1, keepdims=True)
                if n > 0:
                    sp = _dot_nt(qx[h], kx_ref[h, 0:n * tq, :])
                    m = jnp.maximum(m, jnp.max(sp, axis=1, keepdims=True))
                acc = _dot(_bf16(jnp.exp2(sd - m)), vx_ref[h, n * tq:(n + 1) * tq, :])
                if n > 0:
                    acc = acc + _dot(_bf16(jnp.exp2(sp - m)), vx_ref[h, 0:n * tq, :])
                outs.append(acc / pltpu.roll(acc, HEAD_DIM, 1))
            o_ref[0] = _bf16(jnp.where(low, outs[0], outs[1]))


def _attention(moba, q, qa, k, v, ka=None):
    b, t, w = q.shape
    tq = SEQ_TILE
    n_pairs = w // LANES
    qspec = pl.BlockSpec((1, tq, LANES), lambda i, p, j: (i, j, p))
    kspec = pl.BlockSpec((1, t, LANES), lambda i, p, j: (i, 0, p))
    args, specs = [q, qa, k, v], [qspec, qspec, kspec, kspec]
    if not moba:
        args.append(ka)
        specs.append(kspec)
    kern = lambda *refs: _attn_kernel(*refs, moba=moba)
    return pl.pallas_call(
        kern,
        grid=(b, n_pairs, t // tq),
        in_specs=specs,
        out_specs=qspec,
        out_shape=jax.ShapeDtypeStruct((b, t, w), jnp.bfloat16),
        scratch_shapes=[pltpu.VMEM((2, t, LANES), jnp.bfloat16),
                        pltpu.VMEM((2, t, LANES), jnp.bfloat16)],
        compiler_params=pltpu.CompilerParams(vmem_limit_bytes=VMEM_LIMIT),
        name="moba" if moba else "fox",
    )(*args)


def _ffn_kernel(x_ref, om_ref, of_ref, wo_ref, gt1_ref, g_ref, sc_ref, sh_ref, wup_ref, cv_ref,
                wdn_ref, gt2_ref, out_ref, x1_ref, halo_ref, ubuf_ref):
    tm = x_ref.shape[1]

    @pl.when(pl.program_id(1) == 0)
    def _():
        halo_ref[...] = jnp.zeros(halo_ref.shape, jnp.float32)

    o = jnp.concatenate([om_ref[0], of_ref[0]], axis=1)
    x1 = x_ref[0] + gt1_ref[0] * _dot(o, wo_ref[...])
    x1_ref[...] = x1
    hn = _bf16(_rms_mod(x1, g_ref[...], sc_ref[0], sh_ref[0]))

    hmid = []
    for c in range(N_FF_CHUNKS):
        u = _dot(hn, wup_ref[c])
        ubuf_ref[c, 0:HALO, :] = halo_ref[c]
        ubuf_ref[c, HALO:HALO + tm, :] = u
        halo_ref[c] = u[tm - HALO:tm, :]
        cv = cv_ref[c]
        u1 = ubuf_ref[c, HALO - 1:HALO - 1 + tm, :]
        u2 = ubuf_ref[c, HALO - 2:HALO - 2 + tm, :]
        uc = cv[0:1, :] * u2 + cv[1:2, :] * u1 + cv[2:3, :] * u + cv[3:4, :]
        a = uc[:, :FF_CHUNK]
        val = uc[:, FF_CHUNK:]
        hmid.append(_bf16((a * _sigmoid(a)) * val))
    ffn = _dot(jnp.concatenate(hmid, axis=1), wdn_ref[...])
    out_ref[0] = x1_ref[...] + gt2_ref[0] * ffn


def _ffn(x, om, of, wo, gt1, g_ffn, sc2, sh2, wup_c, cv_c, wdn_c, gt2):
    b, t, d = x.shape
    tm = SEQ_TILE
    row = lambda i, j: (i, 0, 0)
    tile = lambda i, j: (i, j, 0)
    const2 = lambda i, j: (0, 0)
    const3 = lambda i, j: (0, 0, 0)
    once = dict(pipeline_mode=pl.Buffered(1))
    return pl.pallas_call(
        _ffn_kernel,
        grid=(b, t // tm),
        in_specs=[pl.BlockSpec((1, tm, d), tile),
                  pl.BlockSpec((1, tm, MOBA_WIDTH), tile),
                  pl.BlockSpec((1, tm, FOX_WIDTH), tile),
                  pl.BlockSpec((d, d), const2, **once),
                  pl.BlockSpec((1, 1, d), row),
                  pl.BlockSpec((1, d), const2),
                  pl.BlockSpec((1, 1, d), row),
                  pl.BlockSpec((1, 1, d), row),
                  pl.BlockSpec((N_FF_CHUNKS, d, 2 * FF_CHUNK), const3, **once),
                  pl.BlockSpec((N_FF_CHUNKS, HALO, 2 * FF_CHUNK), const3, **once),
                  pl.BlockSpec((D_FF, d), const2, **once),
                  pl.BlockSpec((1, 1, d), row)],
        out_specs=pl.BlockSpec((1, tm, d), tile),
        out_shape=jax.ShapeDtypeStruct((b, t, d), jnp.float32),
        scratch_shapes=[pltpu.VMEM((tm, d), jnp.float32),
                        pltpu.VMEM((N_FF_CHUNKS, HALO, 2 * FF_CHUNK), jnp.float32),
                        pltpu.VMEM((N_FF_CHUNKS, HALO + tm, 2 * FF_CHUNK), jnp.float32)],
        compiler_params=pltpu.CompilerParams(vmem_limit_bytes=VMEM_LIMIT),
        name="ffn",
    )(x, om, of, wo, gt1, g_ffn, sc2, sh2, wup_c, cv_c, wdn_c, gt2)


def _layer(x, c, w_ada, b_ada, g_mix, w_in, b_forget, moba_q_gain, moba_k_gain, fox_q_gain,
           fox_k_gain, w_out, g_ffn, w_up, conv_w, conv_b, w_down):
    b, t, d = x.shape
    nb = t // MOBA_BLOCK
    assert d == D_MODEL and t % SEQ_TILE == 0 and nb <= HEAD_DIM - 2 * N_SPLIT
    assert nb * N_MOBA_HEADS <= LANES

    mod = _adaln(c, w_ada, b_ada).reshape(b, 6, 1, d)
    sh1, sc1, gt1, sh2, sc2, gt2 = [mod[:, i] for i in range(6)]

    w_pad = _bf16(jnp.pad(w_in, ((0, 0), (0, IN_COLS_PAD - w_in.shape[1]))))
    gains = jnp.stack([jnp.tile(g, N_MOBA_HEADS) for g in
                       (moba_q_gain, moba_k_gain, fox_q_gain, fox_k_gain)])
    r = np.arange(2 * LANES) // HEAD_DIM
    bd = jnp.asarray(r[:, None] == r[None, :], jnp.bfloat16)
    mq, mk, mv, fq, fk, fv, fl = _inproj(x, sc1, sh1, g_mix.reshape(1, d), w_pad, gains,
                                         _rope_tables(t), bd)

    bf_pad = jnp.pad(b_forget, (0, LANES - N_FOX_HEADS)).reshape(1, LANES)
    mqa, fqa, fka = _routing(mq, mk, fl, bf_pad, _placements(nb))
    o_moba = _attention(True, mq, mqa, mk, mv)
    o_fox = _attention(False, fq, fqa, fk, fv, fka)

    wu = _bf16(w_up).reshape(d, 2, N_FF_CHUNKS, FF_CHUNK).transpose(2, 0, 1, 3)
    wu = wu.reshape(N_FF_CHUNKS, d, 2 * FF_CHUNK)
    cv = jnp.concatenate([conv_w, conv_b[None, :],
                          jnp.zeros((HALO - CONV_WIDTH - 1, 2 * D_FF), jnp.float32)], axis=0)
    cv = cv.reshape(HALO, 2, N_FF_CHUNKS, FF_CHUNK).transpose(2, 0, 1, 3)
    cv = cv.reshape(N_FF_CHUNKS, HALO, 2 * FF_CHUNK)
    return _ffn(x, o_moba, o_fox, _bf16(w_out), gt1, g_ffn.reshape(1, d), sc2, sh2, wu, cv,
                _bf16(w_down), gt2)


def kernel(x, c, w_ada, b_ada, g_mix, w_in, b_forget, moba_q_gain, moba_k_gain, fox_q_gain,
           fox_k_gain, w_out, g_ffn, w_up, conv_w, conv_b, w_down):
    for l in range(w_ada.shape[0]):
        x = _layer(x, c, w_ada[l], b_ada[l], g_mix[l], w_in[l], b_forget[l], moba_q_gain[l],
                   moba_k_gain[l], fox_q_gain[l], fox_k_gain[l], w_out[l], g_ffn[l], w_up[l],
                   conv_w[l], conv_b[l], w_down[l])
    return x
```

```python
import math

import jax
import jax.numpy as jnp
import numpy as np
from jax import lax
from jax.experimental import pallas as pl
from jax.experimental.pallas import tpu as pltpu

D_MODEL = 1024
HEAD_DIM = 64
N_MOBA_HEADS = 8
N_FOX_HEADS = 8
MOBA_WIDTH = N_MOBA_HEADS * HEAD_DIM
FOX_WIDTH = N_FOX_HEADS * HEAD_DIM
MOBA_BLOCK = 256
MOBA_TOPK = 3
ROPE_THETA = 500000.0
ROPE_DIM = HEAD_DIM // 4
D_FF = 2816
CONV_WIDTH = 3
NORM_EPS = 1e-6
NEG_INF = -1e30
LOG2E = math.log2(math.e)
Q_SCALE = HEAD_DIM ** -0.5 * LOG2E
QKV_COLS = 3 * MOBA_WIDTH + 3 * FOX_WIDTH
LANES = 128
IN_COLS_PAD = QKV_COLS + LANES
SEQ_TILE = 256
FF_CHUNK = 256
N_FF_CHUNKS = D_FF // FF_CHUNK
HALO = 8
N_SPLIT = 3
VMEM_LIMIT = 56 * 1024 * 1024

_NT = (((1,), (1,)), ((), ()))


def _bf16(a):
    return a.astype(jnp.bfloat16)


def _dot(a, b):
    return jnp.dot(a, b, preferred_element_type=jnp.float32)


def _dot_nt(a, b):
    return lax.dot_general(a, b, _NT, preferred_element_type=jnp.float32)


def _sigmoid(a):
    return 1.0 / (1.0 + jnp.exp(-a))


def _adaln_kernel(c_ref, w_ref, b_ref, o_ref):
    c = c_ref[...]
    s = c * _sigmoid(c)
    o_ref[...] = _dot(_bf16(s), _bf16(w_ref[...])) + b_ref[...]


def _adaln(c, w_ada, b_ada):
    b, d = c.shape
    n = w_ada.shape[1]
    tn = 1536
    return pl.pallas_call(
        _adaln_kernel,
        grid=(n // tn,),
        in_specs=[pl.BlockSpec((b, d), lambda j: (0, 0)),
                  pl.BlockSpec((d, tn), lambda j: (0, j)),
                  pl.BlockSpec((1, tn), lambda j: (0, j))],
        out_specs=pl.BlockSpec((b, tn), lambda j: (0, j)),
        out_shape=jax.ShapeDtypeStruct((b, n), jnp.float32),
        compiler_params=pltpu.CompilerParams(vmem_limit_bytes=VMEM_LIMIT),
        name="adaln",
    )(c, w_ada, b_ada.reshape(1, n))


def _rms_mod(x, g, sc, sh):
    ms = jnp.mean(x * x, axis=-1, keepdims=True)
    y = x * lax.rsqrt(ms + NORM_EPS)
    return (y * g) * (1.0 + sc) + sh


def _head_norm(p, gain, bd):
    sq = _bf16(p * p)
    half = 2 * LANES
    ss = jnp.concatenate([_dot(sq[:, :half], bd), _dot(sq[:, half:], bd)], axis=1)
    return (p * lax.rsqrt(ss * (1.0 / HEAD_DIM) + NORM_EPS)) * gain


def _rope(y, c, sa, sb):
    half = ROPE_DIM // 2
    outs = []
    for i in range(y.shape[1] // LANES):
        yc = y[:, i * LANES:(i + 1) * LANES]
        up = pltpu.roll(yc, LANES - half, 1)
        dn = pltpu.roll(yc, half, 1)
        outs.append(yc * c + up * sa + dn * sb)
    return jnp.concatenate(outs, axis=1)


def _inproj_kernel(x_ref, sc_ref, sh_ref, g_ref, w_ref, gains_ref, rc_ref, rsa_ref, rsb_ref, bd_ref,
                   mq_ref, mk_ref, mv_ref, fq_ref, fk_ref, fv_ref, fl_ref):
    hn = _rms_mod(x_ref[0], g_ref[...], sc_ref[0], sh_ref[0])
    proj = _dot(_bf16(hn), w_ref[...])
    bd = bd_ref[...]
    c, sa, sb = rc_ref[...], rsa_ref[...], rsb_ref[...]
    w = MOBA_WIDTH
    mq = _rope(_head_norm(proj[:, 0:w], gains_ref[0:1, :], bd), c, sa, sb)
    mq_ref[0] = _bf16(mq * Q_SCALE)
    mk = _rope(_head_norm(proj[:, w:2 * w], gains_ref[1:2, :], bd), c, sa, sb)
    mk_ref[0] = _bf16(mk)
    mv_ref[0] = _bf16(proj[:, 2 * w:3 * w])
    fq = _head_norm(proj[:, 3 * w:4 * w], gains_ref[2:3, :], bd)
    fq_ref[0] = _bf16(fq * Q_SCALE)
    fk = _head_norm(proj[:, 4 * w:5 * w], gains_ref[3:4, :], bd)
    fk_ref[0] = _bf16(fk)
    fv_ref[0] = _bf16(proj[:, 5 * w:6 * w])
    fl_ref[0] = proj[:, QKV_COLS:IN_COLS_PAD]


def _rope_tables(t):
    half = ROPE_DIM // 2
    inv_freq = jnp.power(ROPE_THETA, -2.0 * jnp.arange(half, dtype=jnp.float32) / ROPE_DIM)
    ang = jnp.arange(t, dtype=jnp.float32)[:, None] * inv_freq[None, :]
    cos, sin = jnp.cos(ang), jnp.sin(ang)
    d = np.arange(LANES) % HEAD_DIM
    first = jnp.asarray(d < half)[None, :]
    second = jnp.asarray((d >= half) & (d < ROPE_DIM))[None, :]
    idx = np.where(d < ROPE_DIM, d % half, 0)
    cos_l, sin_l = cos[:, idx], sin[:, idx]
    c = jnp.where(first | second, cos_l, 1.0)
    sa = jnp.where(first, -sin_l, 0.0)
    sb = jnp.where(second, sin_l, 0.0)
    return c, sa, sb


def _inproj(x, sc1, sh1, g_mix, w_pad, gains, tables, bd):
    b, t, d = x.shape
    tm = SEQ_TILE
    row = lambda i, j: (i, 0, 0)
    tile = lambda i, j: (i, j, 0)
    const = lambda i, j: (0, 0)
    wide = jax.ShapeDtypeStruct((b, t, MOBA_WIDTH), jnp.bfloat16)
    out_specs = [pl.BlockSpec((1, tm, MOBA_WIDTH), tile)] * 6 + [pl.BlockSpec((1, tm, LANES), tile)]
    return pl.pallas_call(
        _inproj_kernel,
        grid=(b, t // tm),
        in_specs=[pl.BlockSpec((1, tm, d), tile),
                  pl.BlockSpec((1, 1, d), row),
                  pl.BlockSpec((1, 1, d), row),
                  pl.BlockSpec((1, d), const),
                  pl.BlockSpec((d, IN_COLS_PAD), const),
                  pl.BlockSpec((4, MOBA_WIDTH), const),
                  pl.BlockSpec((tm, LANES), lambda i, j: (j, 0)),
                  pl.BlockSpec((tm, LANES), lambda i, j: (j, 0)),
                  pl.BlockSpec((tm, LANES), lambda i, j: (j, 0)),
                  pl.BlockSpec((2 * LANES, 2 * LANES), const)],
        out_specs=out_specs,
        out_shape=[wide] * 6 + [jax.ShapeDtypeStruct((b, t, LANES), jnp.float32)],
        compiler_params=pltpu.CompilerParams(vmem_limit_bytes=VMEM_LIMIT),
        name="inproj",
    )(x, sc1, sh1, g_mix, w_pad, gains, *tables, bd)


def _split3(a):
    hi = _bf16(a)
    r1 = a - hi.astype(jnp.float32)
    mid = _bf16(r1)
    lo = _bf16(r1 - mid.astype(jnp.float32))
    return hi, mid, lo


def _aux_base(head):
    pair, h = divmod(head, 2)
    return pair * LANES + (HEAD_DIM if h == 0 else 0)


def _placements(nb):
    pm = np.zeros((LANES, MOBA_WIDTH), np.float32)
    pq = np.zeros((N_SPLIT, LANES, FOX_WIDTH), np.float32)
    pk = np.zeros((N_SPLIT, LANES, FOX_WIDTH), np.float32)
    ones = np.zeros((2, FOX_WIDTH), np.float32)
    for head in range(N_MOBA_HEADS):
        base = _aux_base(head)
        for j in range(nb):
            pm[head * nb + j, base + j] = 1.0
        for s in range(N_SPLIT):
            pq[s, head, base + s] = 1.0
            pk[s, head, base + N_SPLIT + s] = 1.0
            ones[0, base + N_SPLIT + s] = 1.0
            ones[1, base + s] = 1.0
    return (jnp.asarray(pm, jnp.bfloat16), jnp.asarray(pq, jnp.bfloat16),
            jnp.asarray(pk, jnp.bfloat16), jnp.asarray(ones))


def _routing_kernel(mq_ref, mk_ref, fl_ref, bf_ref, pm_ref, pq_ref, pk_ref, ones_ref,
                    mqa_ref, fqa_ref, fka_ref):
    t = mq_ref.shape[1]
    nb = t // MOBA_BLOCK
    q = mq_ref[0]
    k = mk_ref[0]
    blk_of_col = lax.broadcasted_iota(jnp.int32, (nb, t), 1) // MOBA_BLOCK
    blk_row = lax.broadcasted_iota(jnp.int32, (nb, t), 0)
    ind = _bf16(jnp.where(blk_of_col == blk_row, 1.0, 0.0))
    kmean = _dot(ind, k) * (1.0 / MOBA_BLOCK)
    reps = LANES // nb
    kmt = jnp.concatenate([kmean] * reps, axis=0)
    r_head = lax.broadcasted_iota(jnp.int32, kmt.shape, 0) // nb
    c_head = lax.broadcasted_iota(jnp.int32, kmt.shape, 1) // HEAD_DIM
    kmt = jnp.where(r_head == c_head, kmt, 0.0)
    k_hi = _bf16(kmt)
    k_lo = _bf16(kmt - k_hi.astype(jnp.float32))
    gate = (_dot_nt(q, k_hi) + _dot_nt(q, k_lo)) * (1.0 / Q_SCALE)

    lane = lax.broadcasted_iota(jnp.int32, gate.shape, 1)
    j = lane % nb
    own = lax.broadcasted_iota(jnp.int32, gate.shape, 0) // MOBA_BLOCK
    rank = jnp.zeros(gate.shape, jnp.int32)
    for d in range(1, nb):
        up = pltpu.roll(gate, LANES - d, 1)
        dn = pltpu.roll(gate, nb - d, 1)
        wrap = (j + d) >= nb
        other = jnp.where(wrap, dn, up)
        jo = jnp.where(wrap, j + d - nb, j + d)
        beats = (other > gate) | ((other == gate) & (jo < j))
        rank = rank + jnp.where(beats & (jo < own), 1, 0)
    attend = ((j < own) & (rank < MOBA_TOPK)) | (j == own)
    selb = _bf16(jnp.where(attend, 0.0, NEG_INF))
    mqa_ref[0] = _bf16(_dot(selb, pm_ref[...]))

    z = fl_ref[0] + bf_ref[...]
    logf = -(jnp.maximum(-z, 0.0) + jnp.log1p(jnp.exp(-jnp.abs(z))))
    rr = lax.broadcasted_iota(jnp.int32, (MOBA_BLOCK, MOBA_BLOCK), 0)
    cc = lax.broadcasted_iota(jnp.int32, (MOBA_BLOCK, MOBA_BLOCK), 1)
    tri = _bf16(jnp.where(cc <= rr, 1.0, 0.0))
    carry = jnp.zeros((1, LANES), jnp.float32)
    for i in range(nb):
        rows = slice(i * MOBA_BLOCK, (i + 1) * MOBA_BLOCK)
        hi, mid, lo = _split3(logf[rows, :])
        cum = (_dot(tri, hi) + _dot(tri, mid)) + _dot(tri, lo) + carry
        carry = cum[MOBA_BLOCK - 1:MOBA_BLOCK, :]
        parts = _split3(cum * LOG2E)
        cq = sum(_dot(parts[s], pq_ref[s]) for s in range(N_SPLIT))
        ck = sum(_dot(parts[s], pk_ref[s]) for s in range(N_SPLIT))
        fqa_ref[0, rows, :] = _bf16(cq + ones_ref[0:1, :])
        fka_ref[0, rows, :] = _bf16(ones_ref[1:2, :] - ck)


def _routing(mq, mk, fl, bf_pad, placements):
    b, t, w = mq.shape
    pm, pq, pk, ones = placements
    full = lambda i: (i, 0, 0)
    c2 = lambda i: (0, 0)
    c3 = lambda i: (0, 0, 0)
    wide = jax.ShapeDtypeStruct((b, t, w), jnp.bfloat16)
    return pl.pallas_call(
        _routing_kernel,
        grid=(b,),
        in_specs=[pl.BlockSpec((1, t, w), full),
                  pl.BlockSpec((1, t, w), full),
                  pl.BlockSpec((1, t, LANES), full),
                  pl.BlockSpec((1, LANES), c2),
                  pl.BlockSpec(pm.shape, c2),
                  pl.BlockSpec(pq.shape, c3),
                  pl.BlockSpec(pk.shape, c3),
                  pl.BlockSpec(ones.shape, c2)],
        out_specs=[pl.BlockSpec((1, t, w), full)] * 3,
        out_shape=[wide] * 3,
        compiler_params=pltpu.CompilerParams(vmem_limit_bytes=VMEM_LIMIT),
        name="routing",
    )(mq, mk, fl, bf_pad, pm, pq, pk, ones)


def _attn_kernel(*refs, moba):
    if moba:
        q_ref, qa_ref, k_ref, v_ref, o_ref, kx_ref, vx_ref = refs
    else:
        q_ref, qa_ref, k_ref, v_ref, ka_ref, o_ref, kx_ref, vx_ref = refs
    tq = SEQ_TILE
    t = k_ref.shape[1]
    nb = t // tq
    step = pl.program_id(2)
    low = lax.broadcasted_iota(jnp.int32, (1, LANES), 1) < HEAD_DIM

    @pl.when(step == 0)
    def _():
        k2 = k_ref[0]
        v2 = v_ref[0]
        if moba:
            blk = lax.broadcasted_iota(jnp.int32, (t, LANES), 0) // MOBA_BLOCK
            ln = lax.broadcasted_iota(jnp.int32, (t, LANES), 1) % HEAD_DIM
            ka = _bf16(jnp.where(ln == blk, 1.0, 0.0))
        else:
            ka = ka_ref[0]
        one = jnp.ones_like(v2)
        kx_ref[0] = jnp.where(low, k2, ka)
        kx_ref[1] = jnp.where(low, ka, k2)
        vx_ref[0] = jnp.where(low, v2, one)
        vx_ref[1] = jnp.where(low, one, v2)

    r = lax.broadcasted_iota(jnp.int32, (tq, tq), 0)
    c = lax.broadcasted_iota(jnp.int32, (tq, tq), 1)
    causal = c <= r

    def q_tile(n):
        rows = slice(n * tq, (n + 1) * tq)
        q2 = q_ref[0, rows, :]
        qa = qa_ref[0, rows, :]
        qx = [jnp.where(low, q2, qa), jnp.where(low, qa, q2)]
        outs = []
        for h in range(2):
            sd = jnp.where(causal, _dot_nt(qx[h], kx_ref[h, rows, :]), NEG_INF)
            m = jnp.max(sd, axis=1, keepdims=True)
            if n > 0:
                sp = _dot_nt(qx[h], kx_ref[h, 0:n * tq, :])
                m = jnp.maximum(m, jnp.max(sp, axis=1, keepdims=True))
            acc = _dot(_bf16(jnp.exp2(sd - m)), vx_ref[h, rows, :])
            if n > 0:
                acc = acc + _dot(_bf16(jnp.exp2(sp - m)), vx_ref[h, 0:n * tq, :])
            outs.append(acc / pltpu.roll(acc, HEAD_DIM, 1))
        o_ref[0, rows, :] = _bf16(jnp.where(low, outs[0], outs[1]))

    for s in range(nb // 2):
        @pl.when(step == s)
        def _(s=s):
            q_tile(nb - 1 - s)
            q_tile(s)


def _attention(moba, q, qa, k, v, ka=None):
    b, t, w = q.shape
    n_pairs = w // LANES
    nb = t // SEQ_TILE
    assert nb % 2 == 0
    spec = pl.BlockSpec((1, t, LANES), lambda i, p, j: (i, 0, p))
    args = [q, qa, k, v] if moba else [q, qa, k, v, ka]
    kern = lambda *refs: _attn_kernel(*refs, moba=moba)
    return pl.pallas_call(
        kern,
        grid=(b, n_pairs, nb // 2),
        in_specs=[spec] * len(args),
        out_specs=spec,
        out_shape=jax.ShapeDtypeStruct((b, t, w), jnp.bfloat16),
        scratch_shapes=[pltpu.VMEM((2, t, LANES), jnp.bfloat16),
                        pltpu.VMEM((2, t, LANES), jnp.bfloat16)],
        compiler_params=pltpu.CompilerParams(vmem_limit_bytes=VMEM_LIMIT),
        name="moba" if moba else "fox",
    )(*args)


def _ffn_kernel(x_ref, om_ref, of_ref, wo_ref, gt1_ref, g_ref, sc_ref, sh_ref, wup_ref, cv_ref,
                wdn_ref, gt2_ref, out_ref, x1_ref, halo_ref, ubuf_ref):
    tm = x_ref.shape[1]

    @pl.when(pl.program_id(1) == 0)
    def _():
        halo_ref[...] = jnp.zeros(halo_ref.shape, jnp.float32)

    o = jnp.concatenate([om_ref[0], of_ref[0]], axis=1)
    x1 = x_ref[0] + gt1_ref[0] * _dot(o, wo_ref[...])
    x1_ref[...] = x1
    hn = _bf16(_rms_mod(x1, g_ref[...], sc_ref[0], sh_ref[0]))

    hmid = []
    for c in range(N_FF_CHUNKS):
        u = _dot(hn, wup_ref[c])
        ubuf_ref[c, 0:HALO, :] = halo_ref[c]
        ubuf_ref[c, HALO:HALO + tm, :] = u
        halo_ref[c] = u[tm - HALO:tm, :]
        cv = cv_ref[c]
        u1 = ubuf_ref[c, HALO - 1:HALO - 1 + tm, :]
        u2 = ubuf_ref[c, HALO - 2:HALO - 2 + tm, :]
        uc = cv[0:1, :] * u2 + cv[1:2, :] * u1 + cv[2:3, :] * u + cv[3:4, :]
        a = uc[:, :FF_CHUNK]
        val = uc[:, FF_CHUNK:]
        hmid.append(_bf16((a * _sigmoid(a)) * val))
    ffn = _dot(jnp.concatenate(hmid, axis=1), wdn_ref[...])
    out_ref[0] = x1_ref[...] + gt2_ref[0] * ffn


def _ffn(x, om, of, wo, gt1, g_ffn, sc2, sh2, wup_c, cv_c, wdn_c, gt2):
    b, t, d = x.shape
    tm = SEQ_TILE
    row = lambda i, j: (i, 0, 0)
    tile = lambda i, j: (i, j, 0)
    const2 = lambda i, j: (0, 0)
    const3 = lambda i, j: (0, 0, 0)
    once = dict(pipeline_mode=pl.Buffered(1))
    return pl.pallas_call(
        _ffn_kernel,
        grid=(b, t // tm),
        in_specs=[pl.BlockSpec((1, tm, d), tile),
                  pl.BlockSpec((1, tm, MOBA_WIDTH), tile),
                  pl.BlockSpec((1, tm, FOX_WIDTH), tile),
                  pl.BlockSpec((d, d), const2, **once),
                  pl.BlockSpec((1, 1, d), row),
                  pl.BlockSpec((1, d), const2),
                  pl.BlockSpec((1, 1, d), row),
                  pl.BlockSpec((1, 1, d), row),
                  pl.BlockSpec((N_FF_CHUNKS, d, 2 * FF_CHUNK), const3, **once),
                  pl.BlockSpec((N_FF_CHUNKS, HALO, 2 * FF_CHUNK), const3, **once),
                  pl.BlockSpec((D_FF, d), const2, **once),
                  pl.BlockSpec((1, 1, d), row)],
        out_specs=pl.BlockSpec((1, tm, d), tile),
        out_shape=jax.ShapeDtypeStruct((b, t, d), jnp.float32),
        scratch_shapes=[pltpu.VMEM((tm, d), jnp.float32),
                        pltpu.VMEM((N_FF_CHUNKS, HALO, 2 * FF_CHUNK), jnp.float32),
                        pltpu.VMEM((N_FF_CHUNKS, HALO + tm, 2 * FF_CHUNK), jnp.float32)],
        compiler_params=pltpu.CompilerParams(vmem_limit_bytes=VMEM_LIMIT),
        name="ffn",
    )(x, om, of, wo, gt1, g_ffn, sc2, sh2, wup_c, cv_c, wdn_c, gt2)


def _layer(x, c, w_ada, b_ada, g_mix, w_in, b_forget, moba_q_gain, moba_k_gain, fox_q_gain,
           fox_k_gain, w_out, g_ffn, w_up, conv_w, conv_b, w_down):
    b, t, d = x.shape
    nb = t // MOBA_BLOCK
    assert d == D_MODEL and t % SEQ_TILE == 0 and nb <= HEAD_DIM - 2 * N_SPLIT
    assert nb * N_MOBA_HEADS <= LANES

    mod = _adaln(c, w_ada, b_ada).reshape(b, 6, 1, d)
    sh1, sc1, gt1, sh2, sc2, gt2 = [mod[:, i] for i in range(6)]

    w_pad = _bf16(jnp.pad(w_in, ((0, 0), (0, IN_COLS_PAD - w_in.shape[1]))))
    gains = jnp.stack([jnp.tile(g, N_MOBA_HEADS) for g in
                       (moba_q_gain, moba_k_gain, fox_q_gain, fox_k_gain)])
    r = np.arange(2 * LANES) // HEAD_DIM
    bd = jnp.asarray(r[:, None] == r[None, :], jnp.bfloat16)
    mq, mk, mv, fq, fk, fv, fl = _inproj(x, sc1, sh1, g_mix.reshape(1, d), w_pad, gains,
                                         _rope_tables(t), bd)

    bf_pad = jnp.pad(b_forget, (0, LANES - N_FOX_HEADS)).reshape(1, LANES)
    mqa, fqa, fka = _routing(mq, mk, fl, bf_pad, _placements(nb))
    o_moba = _attention(True, mq, mqa, mk, mv)
    o_fox = _attention(False, fq, fqa, fk, fv, fka)

    wu = _bf16(w_up).reshape(d, 2, N_FF_CHUNKS, FF_CHUNK).transpose(2, 0, 1, 3)
    wu = wu.reshape(N_FF_CHUNKS, d, 2 * FF_CHUNK)
    cv = jnp.concatenate([conv_w, conv_b[None, :],
                          jnp.zeros((HALO - CONV_WIDTH - 1, 2 * D_FF), jnp.float32)], axis=0)
    cv = cv.reshape(HALO, 2, N_FF_CHUNKS, FF_CHUNK).transpose(2, 0, 1, 3)
    cv = cv.reshape(N_FF_CHUNKS, HALO, 2 * FF_CHUNK)
    return _ffn(x, o_moba, o_fox, _bf16(w_out), gt1, g_ffn.reshape(1, d), sc2, sh2, wu, cv,
                _bf16(w_down), gt2)


def kernel(x, c, w_ada, b_ada, g_mix, w_in, b_forget, moba_q_gain, moba_k_gain, fox_q_gain,
           fox_k_gain, w_out, g_ffn, w_up, conv_w, conv_b, w_down):
    for l in range(w_ada.shape[0]):
        x = _layer(x, c, w_ada[l], b_ada[l], g_mix[l], w_in[l], b_forget[l], moba_q_gain[l],
                   moba_k_gain[l], fox_q_gain[l], fox_k_gain[l], w_out[l], g_ffn[l], w_up[l],
                   conv_w[l], conv_b[l], w_down[l])
    return x
```

```python
import math

import jax
import jax.numpy as jnp
import numpy as np
from jax import lax
from jax.experimental import pallas as pl
from jax.experimental.pallas import tpu as pltpu

D_MODEL = 1024
HEAD_DIM = 64
N_MOBA_HEADS = 8
N_FOX_HEADS = 8
MOBA_WIDTH = N_MOBA_HEADS * HEAD_DIM
FOX_WIDTH = N_FOX_HEADS * HEAD_DIM
MOBA_BLOCK = 256
MOBA_TOPK = 3
ROPE_THETA = 500000.0
ROPE_DIM = HEAD_DIM // 4
D_FF = 2816
CONV_WIDTH = 3
NORM_EPS = 1e-6
NEG_INF = -1e30
LOG2E = math.log2(math.e)
Q_SCALE = HEAD_DIM ** -0.5 * LOG2E
LANES = 128
BF16_ROWS = 16
QK_COLS = 2 * MOBA_WIDTH + 2 * FOX_WIDTH
QK_COLS_PAD = QK_COLS + LANES
V_ROWS = HEAD_DIM + BF16_ROWS
SEQ_TILE = 256
ATTN_TILES_PER_STEP = 4
FF_CHUNK = 256
N_FF_CHUNKS = D_FF // FF_CHUNK
HALO = 8
N_SPLIT = 3
VMEM_LIMIT = 56 * 1024 * 1024

_NT = (((1,), (1,)), ((), ()))


def _bf16(a):
    return a.astype(jnp.bfloat16)


def _dot(a, b):
    return jnp.dot(a, b, preferred_element_type=jnp.float32)


def _dot_nt(a, b):
    return lax.dot_general(a, b, _NT, preferred_element_type=jnp.float32)


def _sigmoid(a):
    return 1.0 / (1.0 + jnp.exp(-a))


def _adaln_kernel(c_ref, w_ref, b_ref, o_ref):
    c = c_ref[...]
    s = c * _sigmoid(c)
    o_ref[...] = _dot(_bf16(s), _bf16(w_ref[...])) + b_ref[...]


def _adaln(c, w_ada, b_ada):
    b, d = c.shape
    n = w_ada.shape[1]
    tn = 1536
    return pl.pallas_call(
        _adaln_kernel,
        grid=(n // tn,),
        in_specs=[pl.BlockSpec((b, d), lambda j: (0, 0)),
                  pl.BlockSpec((d, tn), lambda j: (0, j)),
                  pl.BlockSpec((1, tn), lambda j: (0, j))],
        out_specs=pl.BlockSpec((b, tn), lambda j: (0, j)),
        out_shape=jax.ShapeDtypeStruct((b, n), jnp.float32),
        compiler_params=pltpu.CompilerParams(vmem_limit_bytes=VMEM_LIMIT),
        name="adaln",
    )(c, w_ada, b_ada.reshape(1, n))


def _rms_mod(x, g, sc, sh):
    ms = jnp.mean(x * x, axis=-1, keepdims=True)
    y = x * lax.rsqrt(ms + NORM_EPS)
    return (y * g) * (1.0 + sc) + sh


def _head_norm(p, gain, bd):
    sq = _bf16(p * p)
    half = 2 * LANES
    ss = jnp.concatenate([_dot(sq[:, :half], bd), _dot(sq[:, half:], bd)], axis=1)
    return (p * lax.rsqrt(ss * (1.0 / HEAD_DIM) + NORM_EPS)) * gain


def _rope(y, c, sa, sb):
    half = ROPE_DIM // 2
    outs = []
    for i in range(y.shape[1] // LANES):
        yc = y[:, i * LANES:(i + 1) * LANES]
        up = pltpu.roll(yc, LANES - half, 1)
        dn = pltpu.roll(yc, half, 1)
        outs.append(yc * c + up * sa + dn * sb)
    return jnp.concatenate(outs, axis=1)


def _inproj_kernel(x_ref, sc_ref, sh_ref, g_ref, w_ref, wvt_ref, gains_ref, rc_ref, rsa_ref, rsb_ref,
                   bd_ref, mq_ref, mk_ref, fq_ref, fk_ref, fl_ref, mvt_ref, fvt_ref):
    hn = _bf16(_rms_mod(x_ref[0], g_ref[...], sc_ref[0], sh_ref[0]))
    proj = _dot(hn, w_ref[...])
    vt = _dot_nt(wvt_ref[...], hn)
    bd = bd_ref[...]
    c, sa, sb = rc_ref[...], rsa_ref[...], rsb_ref[...]
    w = MOBA_WIDTH
    mq = _rope(_head_norm(proj[:, 0:w], gains_ref[0:1, :], bd), c, sa, sb)
    mq_ref[0] = _bf16(mq * Q_SCALE)
    mk = _rope(_head_norm(proj[:, w:2 * w], gains_ref[1:2, :], bd), c, sa, sb)
    mk_ref[0] = _bf16(mk)
    fq = _head_norm(proj[:, 2 * w:3 * w], gains_ref[2:3, :], bd)
    fq_ref[0] = _bf16(fq * Q_SCALE)
    fk = _head_norm(proj[:, 3 * w:4 * w], gains_ref[3:4, :], bd)
    fk_ref[0] = _bf16(fk)
    fl_ref[0] = proj[:, QK_COLS:QK_COLS_PAD]
    mvt_ref[0] = _bf16(vt[0:w, :])
    fvt_ref[0] = _bf16(vt[w:2 * w, :])


def _rope_tables(t):
    half = ROPE_DIM // 2
    inv_freq = jnp.power(ROPE_THETA, -2.0 * jnp.arange(half, dtype=jnp.float32) / ROPE_DIM)
    ang = jnp.arange(t, dtype=jnp.float32)[:, None] * inv_freq[None, :]
    cos, sin = jnp.cos(ang), jnp.sin(ang)
    d = np.arange(LANES) % HEAD_DIM
    first = jnp.asarray(d < half)[None, :]
    second = jnp.asarray((d >= half) & (d < ROPE_DIM))[None, :]
    idx = np.where(d < ROPE_DIM, d % half, 0)
    cos_l, sin_l = cos[:, idx], sin[:, idx]
    c = jnp.where(first | second, cos_l, 1.0)
    sa = jnp.where(first, -sin_l, 0.0)
    sb = jnp.where(second, sin_l, 0.0)
    return c, sa, sb


def _inproj(x, sc1, sh1, g_mix, w_qk, w_vt, gains, tables, bd):
    b, t, d = x.shape
    tm = SEQ_TILE
    w = MOBA_WIDTH
    row = lambda i, j: (i, 0, 0)
    tile = lambda i, j: (i, j, 0)
    tile_t = lambda i, j: (i, 0, j)
    const = lambda i, j: (0, 0)
    wide = jax.ShapeDtypeStruct((b, t, w), jnp.bfloat16)
    wide_t = jax.ShapeDtypeStruct((b, w, t), jnp.bfloat16)
    out_specs = ([pl.BlockSpec((1, tm, w), tile)] * 4 + [pl.BlockSpec((1, tm, LANES), tile)]
                 + [pl.BlockSpec((1, w, tm), tile_t)] * 2)
    return pl.pallas_call(
        _inproj_kernel,
        grid=(b, t // tm),
        in_specs=[pl.BlockSpec((1, tm, d), tile),
                  pl.BlockSpec((1, 1, d), row),
                  pl.BlockSpec((1, 1, d), row),
                  pl.BlockSpec((1, d), const),
                  pl.BlockSpec((d, QK_COLS_PAD), const),
                  pl.BlockSpec((2 * w, d), const),
                  pl.BlockSpec((4, w), const),
                  pl.BlockSpec((tm, LANES), lambda i, j: (j, 0)),
                  pl.BlockSpec((tm, LANES), lambda i, j: (j, 0)),
                  pl.BlockSpec((tm, LANES), lambda i, j: (j, 0)),
                  pl.BlockSpec((2 * LANES, 2 * LANES), const)],
        out_specs=out_specs,
        out_shape=[wide] * 4 + [jax.ShapeDtypeStruct((b, t, LANES), jnp.float32)] + [wide_t] * 2,
        compiler_params=pltpu.CompilerParams(vmem_limit_bytes=VMEM_LIMIT),
        name="inproj",
    )(x, sc1, sh1, g_mix, w_qk, w_vt, gains, *tables, bd)


def _split3(a):
    hi = _bf16(a)
    r1 = a - hi.astype(jnp.float32)
    mid = _bf16(r1)
    lo = _bf16(r1 - mid.astype(jnp.float32))
    return hi, mid, lo


def _aux_base(head):
    pair, h = divmod(head, 2)
    return pair * LANES + (HEAD_DIM if h == 0 else 0)


def _placements(nb):
    pm = np.zeros((LANES, MOBA_WIDTH), np.float32)
    pq = np.zeros((N_SPLIT, LANES, FOX_WIDTH), np.float32)
    pk = np.zeros((N_SPLIT, LANES, FOX_WIDTH), np.float32)
    ones = np.zeros((2, FOX_WIDTH), np.float32)
    for head in range(N_MOBA_HEADS):
        base = _aux_base(head)
        for j in range(nb):
            pm[head * nb + j, base + j] = 1.0
        for s in range(N_SPLIT):
            pq[s, head, base + s] = 1.0
            pk[s, head, base + N_SPLIT + s] = 1.0
            ones[0, base + N_SPLIT + s] = 1.0
            ones[1, base + s] = 1.0
    return (jnp.asarray(pm, jnp.bfloat16), jnp.asarray(pq, jnp.bfloat16),
            jnp.asarray(pk, jnp.bfloat16), jnp.asarray(ones))


def _routing_kernel(mq_ref, mk_ref, fl_ref, bf_ref, pm_ref, pq_ref, pk_ref, ones_ref,
                    mqa_ref, fqa_ref, fka_ref):
    t = mq_ref.shape[1]
    nb = t // MOBA_BLOCK
    q = mq_ref[0]
    k = mk_ref[0]
    blk_of_col = lax.broadcasted_iota(jnp.int32, (nb, t), 1) // MOBA_BLOCK
    blk_row = lax.broadcasted_iota(jnp.int32, (nb, t), 0)
    ind = _bf16(jnp.where(blk_of_col == blk_row, 1.0, 0.0))
    kmean = _dot(ind, k) * (1.0 / MOBA_BLOCK)
    reps = LANES // nb
    kmt = jnp.concatenate([kmean] * reps, axis=0)
    r_head = lax.broadcasted_iota(jnp.int32, kmt.shape, 0) // nb
    c_head = lax.broadcasted_iota(jnp.int32, kmt.shape, 1) // HEAD_DIM
    kmt = jnp.where(r_head == c_head, kmt, 0.0)
    k_hi = _bf16(kmt)
    k_lo = _bf16(kmt - k_hi.astype(jnp.float32))
    gate = (_dot_nt(q, k_hi) + _dot_nt(q, k_lo)) * (1.0 / Q_SCALE)

    lane = lax.broadcasted_iota(jnp.int32, gate.shape, 1)
    j = lane % nb
    own = lax.broadcasted_iota(jnp.int32, gate.shape, 0) // MOBA_BLOCK
    rank = jnp.zeros(gate.shape, jnp.int32)
    for d in range(1, nb):
        up = pltpu.roll(gate, LANES - d, 1)
        dn = pltpu.roll(gate, nb - d, 1)
        wrap = (j + d) >= nb
        other = jnp.where(wrap, dn, up)
        jo = jnp.where(wrap, j + d - nb, j + d)
        beats = (other > gate) | ((other == gate) & (jo < j))
        rank = rank + jnp.where(beats & (jo < own), 1, 0)
    attend = ((j < own) & (rank < MOBA_TOPK)) | (j == own)
    selb = _bf16(jnp.where(attend, 0.0, NEG_INF))
    mqa_ref[0] = _bf16(_dot(selb, pm_ref[...]))

    z = fl_ref[0] + bf_ref[...]
    logf = -(jnp.maximum(-z, 0.0) + jnp.log1p(jnp.exp(-jnp.abs(z))))
    rr = lax.broadcasted_iota(jnp.int32, (MOBA_BLOCK, MOBA_BLOCK), 0)
    cc = lax.broadcasted_iota(jnp.int32, (MOBA_BLOCK, MOBA_BLOCK), 1)
    tri = _bf16(jnp.where(cc <= rr, 1.0, 0.0))
    carry = jnp.zeros((1, LANES), jnp.float32)
    for i in range(nb):
        rows = slice(i * MOBA_BLOCK, (i + 1) * MOBA_BLOCK)
        hi, mid, lo = _split3(logf[rows, :])
        cum = (_dot(tri, hi) + _dot(tri, mid)) + _dot(tri, lo) + carry
        carry = cum[MOBA_BLOCK - 1:MOBA_BLOCK, :]
        parts = _split3(cum * LOG2E)
        cq = sum(_dot(parts[s], pq_ref[s]) for s in range(N_SPLIT))
        ck = sum(_dot(parts[s], pk_ref[s]) for s in range(N_SPLIT))
        fqa_ref[0, rows, :] = _bf16(cq + ones_ref[0:1, :])
        fka_ref[0, rows, :] = _bf16(ones_ref[1:2, :] - ck)


def _routing(mq, mk, fl, bf_pad, placements):
    b, t, w = mq.shape
    pm, pq, pk, ones = placements
    full = lambda i: (i, 0, 0)
    c2 = lambda i: (0, 0)
    c3 = lambda i: (0, 0, 0)
    wide = jax.ShapeDtypeStruct((b, t, w), jnp.bfloat16)
    return pl.pallas_call(
        _routing_kernel,
        grid=(b,),
        in_specs=[pl.BlockSpec((1, t, w), full),
                  pl.BlockSpec((1, t, w), full),
                  pl.BlockSpec((1, t, LANES), full),
                  pl.BlockSpec((1, LANES), c2),
                  pl.BlockSpec(pm.shape, c2),
                  pl.BlockSpec(pq.shape, c3),
                  pl.BlockSpec(pk.shape, c3),
                  pl.BlockSpec(ones.shape, c2)],
        out_specs=[pl.BlockSpec((1, t, w), full)] * 3,
        out_shape=[wide] * 3,
        compiler_params=pltpu.CompilerParams(vmem_limit_bytes=VMEM_LIMIT),
        name="routing",
    )(mq, mk, fl, bf_pad, pm, pq, pk, ones)


def _attn_kernel(*refs, moba):
    if moba:
        q_ref, qa_ref, k_ref, vt_ref, o_ref, kx_ref, vx_ref = refs
    else:
        q_ref, qa_ref, k_ref, vt_ref, ka_ref, o_ref, kx_ref, vx_ref = refs
    tq = SEQ_TILE
    t = k_ref.shape[1]
    nb = t // tq
    step = pl.program_id(2)
    low = lax.broadcasted_iota(jnp.int32, (1, LANES), 1) < HEAD_DIM

    @pl.when(step == 0)
    def _():
        k2 = k_ref[0]
        if moba:
            blk = lax.broadcasted_iota(jnp.int32, (t, LANES), 0) // MOBA_BLOCK
            ln = lax.broadcasted_iota(jnp.int32, (t, LANES), 1) % HEAD_DIM
            ka = _bf16(jnp.where(ln == blk, 1.0, 0.0))
        else:
            ka = ka_ref[0]
        kx_ref[0] = jnp.where(low, k2, ka)
        kx_ref[1] = jnp.where(low, ka, k2)
        for h in range(2):
            vx_ref[h, 0:HEAD_DIM, :] = vt_ref[0, h * HEAD_DIM:(h + 1) * HEAD_DIM, :]
            vx_ref[h, HEAD_DIM:V_ROWS, :] = jnp.ones((V_ROWS - HEAD_DIM, t), jnp.bfloat16)

    key = lax.broadcasted_iota(jnp.int32, (tq, tq), 0)
    qry = lax.broadcasted_iota(jnp.int32, (tq, tq), 1)
    causal = key <= qry

    def scores(n, h):
        rows = slice(n * tq, (n + 1) * tq)
        q2 = q_ref[0, rows, :]
        qa = qa_ref[0, rows, :]
        qx = jnp.where(low, q2, qa) if h == 0 else jnp.where(low, qa, q2)
        sd = jnp.where(causal, _dot_nt(kx_ref[h, rows, :], qx), NEG_INF)
        m = jnp.max(sd, axis=0, keepdims=True)
        sp = None
        if n > 0:
            sp = _dot_nt(kx_ref[h, 0:n * tq, :], qx)
            m = jnp.maximum(m, jnp.max(sp, axis=0, keepdims=True))
        return n, h, sd, sp, m

    def weighted_values(n, h, sd, sp, m):
        rows = slice(n * tq, (n + 1) * tq)
        acc = _dot(vx_ref[h, :, rows], _bf16(jnp.exp2(sd - m)))
        if n > 0:
            acc = acc + _dot(vx_ref[h, :, 0:n * tq], _bf16(jnp.exp2(sp - m)))
        return acc[0:HEAD_DIM, :] / acc[HEAD_DIM:HEAD_DIM + 1, :]

    def run(tiles):
        chains = [(n, h) for n in tiles for h in range(2)]
        outs = {}
        pending = scores(*chains[0])
        for nxt in chains[1:] + [None]:
            ahead = scores(*nxt) if nxt is not None else None
            outs[pending[:2]] = weighted_values(*pending)
            pending = ahead
        for n in tiles:
            o_t = jnp.concatenate([outs[(n, 0)], outs[(n, 1)]], axis=0)
            o_ref[0, n * tq:(n + 1) * tq, :] = _bf16(o_t.T)

    pairs = [(nb - 1 - i, i) for i in range(nb // 2)]
    n_static = _attn_steps(nb)
    for s in range(n_static):
        @pl.when(step == s)
        def _(s=s):
            run([n for pair in pairs[s::n_static] for n in pair])


def _attn_steps(nb):
    return max(1, nb // ATTN_TILES_PER_STEP)


def _attention(moba, q, qa, k, vt, ka=None):
    b, t, w = q.shape
    n_pairs = w // LANES
    nb = t // SEQ_TILE
    assert nb % 2 == 0 and (nb // 2) % _attn_steps(nb) == 0
    spec = pl.BlockSpec((1, t, LANES), lambda i, p, j: (i, 0, p))
    spec_t = pl.BlockSpec((1, LANES, t), lambda i, p, j: (i, p, 0))
    args = [q, qa, k, vt] if moba else [q, qa, k, vt, ka]
    specs = [spec, spec, spec, spec_t] + ([] if moba else [spec])
    kern = lambda *refs: _attn_kernel(*refs, moba=moba)
    return pl.pallas_call(
        kern,
        grid=(b, n_pairs, _attn_steps(nb)),
        in_specs=specs,
        out_specs=spec,
        out_shape=jax.ShapeDtypeStruct((b, t, w), jnp.bfloat16),
        scratch_shapes=[pltpu.VMEM((2, t, LANES), jnp.bfloat16),
                        pltpu.VMEM((2, V_ROWS, t), jnp.bfloat16)],
        compiler_params=pltpu.CompilerParams(vmem_limit_bytes=VMEM_LIMIT),
        name="moba" if moba else "fox",
    )(*args)


def _ffn_kernel(x_ref, om_ref, of_ref, wo_ref, gt1_ref, g_ref, sc_ref, sh_ref, wup_ref, cv_ref,
                wdn_ref, gt2_ref, out_ref, x1_ref, halo_ref, ubuf_ref):
    tm = x_ref.shape[1]

    @pl.when(pl.program_id(1) == 0)
    def _():
        halo_ref[...] = jnp.zeros(halo_ref.shape, jnp.float32)

    o = jnp.concatenate([om_ref[0], of_ref[0]], axis=1)
    x1 = x_ref[0] + gt1_ref[0] * _dot(o, wo_ref[...])
    x1_ref[...] = x1
    hn = _bf16(_rms_mod(x1, g_ref[...], sc_ref[0], sh_ref[0]))

    hmid = []
    for c in range(N_FF_CHUNKS):
        u = _dot(hn, wup_ref[c])
        ubuf_ref[c, 0:HALO, :] = halo_ref[c]
        ubuf_ref[c, HALO:HALO + tm, :] = u
        halo_ref[c] = u[tm - HALO:tm, :]
        cv = cv_ref[c]
        u1 = ubuf_ref[c, HALO - 1:HALO - 1 + tm, :]
        u2 = ubuf_ref[c, HALO - 2:HALO - 2 + tm, :]
        uc = cv[0:1, :] * u2 + cv[1:2, :] * u1 + cv[2:3, :] * u + cv[3:4, :]
        a = uc[:, :FF_CHUNK]
        val = uc[:, FF_CHUNK:]
        hmid.append(_bf16((a * _sigmoid(a)) * val))
    ffn = _dot(jnp.concatenate(hmid, axis=1), wdn_ref[...])
    out_ref[0] = x1_ref[...] + gt2_ref[0] * ffn


def _ffn(x, om, of, wo, gt1, g_ffn, sc2, sh2, wup_c, cv_c, wdn_c, gt2):
    b, t, d = x.shape
    tm = SEQ_TILE
    row = lambda i, j: (i, 0, 0)
    tile = lambda i, j: (i, j, 0)
    const2 = lambda i, j: (0, 0)
    const3 = lambda i, j: (0, 0, 0)
    once = dict(pipeline_mode=pl.Buffered(1))
    return pl.pallas_call(
        _ffn_kernel,
        grid=(b, t // tm),
        in_specs=[pl.BlockSpec((1, tm, d), tile),
                  pl.BlockSpec((1, tm, MOBA_WIDTH), tile),
                  pl.BlockSpec((1, tm, FOX_WIDTH), tile),
                  pl.BlockSpec((d, d), const2, **once),
                  pl.BlockSpec((1, 1, d), row),
                  pl.BlockSpec((1, d), const2),
                  pl.BlockSpec((1, 1, d), row),
                  pl.BlockSpec((1, 1, d), row),
                  pl.BlockSpec((N_FF_CHUNKS, d, 2 * FF_CHUNK), const3, **once),
                  pl.BlockSpec((N_FF_CHUNKS, HALO, 2 * FF_CHUNK), const3, **once),
                  pl.BlockSpec((D_FF, d), const2, **once),
                  pl.BlockSpec((1, 1, d), row)],
        out_specs=pl.BlockSpec((1, tm, d), tile),
        out_shape=jax.ShapeDtypeStruct((b, t, d), jnp.float32),
        scratch_shapes=[pltpu.VMEM((tm, d), jnp.float32),
                        pltpu.VMEM((N_FF_CHUNKS, HALO, 2 * FF_CHUNK), jnp.float32),
                        pltpu.VMEM((N_FF_CHUNKS, HALO + tm, 2 * FF_CHUNK), jnp.float32)],
        compiler_params=pltpu.CompilerParams(vmem_limit_bytes=VMEM_LIMIT),
        name="ffn",
    )(x, om, of, wo, gt1, g_ffn, sc2, sh2, wup_c, cv_c, wdn_c, gt2)


def _layer(x, c, w_ada, b_ada, g_mix, w_in, b_forget, moba_q_gain, moba_k_gain, fox_q_gain,
           fox_k_gain, w_out, g_ffn, w_up, conv_w, conv_b, w_down):
    b, t, d = x.shape
    nb = t // MOBA_BLOCK
    assert d == D_MODEL and t % SEQ_TILE == 0 and nb <= HEAD_DIM - 2 * N_SPLIT
    assert nb * N_MOBA_HEADS <= LANES

    mod = _adaln(c, w_ada, b_ada).reshape(b, 6, 1, d)
    sh1, sc1, gt1, sh2, sc2, gt2 = [mod[:, i] for i in range(6)]

    w = MOBA_WIDTH
    cols = lambda i: w_in[:, i * w:(i + 1) * w]
    logits = jnp.pad(w_in[:, 6 * w:], ((0, 0), (0, LANES - N_FOX_HEADS)))
    w_qk = _bf16(jnp.concatenate([cols(0), cols(1), cols(3), cols(4), logits], axis=1))
    w_vt = _bf16(jnp.concatenate([cols(2), cols(5)], axis=1).T)
    gains = jnp.stack([jnp.tile(g, N_MOBA_HEADS) for g in
                       (moba_q_gain, moba_k_gain, fox_q_gain, fox_k_gain)])
    r = np.arange(2 * LANES) // HEAD_DIM
    bd = jnp.asarray(r[:, None] == r[None, :], jnp.bfloat16)
    mq, mk, fq, fk, fl, mvt, fvt = _inproj(x, sc1, sh1, g_mix.reshape(1, d), w_qk, w_vt, gains,
                                           _rope_tables(t), bd)

    bf_pad = jnp.pad(b_forget, (0, LANES - N_FOX_HEADS)).reshape(1, LANES)
    mqa, fqa, fka = _routing(mq, mk, fl, bf_pad, _placements(nb))
    o_moba = _attention(True, mq, mqa, mk, mvt)
    o_fox = _attention(False, fq, fqa, fk, fvt, fka)

    wu = _bf16(w_up).reshape(d, 2, N_FF_CHUNKS, FF_CHUNK).transpose(2, 0, 1, 3)
    wu = wu.reshape(N_FF_CHUNKS, d, 2 * FF_CHUNK)
    cv = jnp.concatenate([conv_w, conv_b[None, :],
                          jnp.zeros((HALO - CONV_WIDTH - 1, 2 * D_FF), jnp.float32)], axis=0)
    cv = cv.reshape(HALO, 2, N_FF_CHUNKS, FF_CHUNK).transpose(2, 0, 1, 3)
    cv = cv.reshape(N_FF_CHUNKS, HALO, 2 * FF_CHUNK)
    return _ffn(x, o_moba, o_fox, _bf16(w_out), gt1, g_ffn.reshape(1, d), sc2, sh2, wu, cv,
                _bf16(w_down), gt2)


def kernel(x, c, w_ada, b_ada, g_mix, w_in, b_forget, moba_q_gain, moba_k_gain, fox_q_gain,
           fox_k_gain, w_out, g_ffn, w_up, conv_w, conv_b, w_down):
    for l in range(w_ada.shape[0]):
        x = _layer(x, c, w_ada[l], b_ada[l], g_mix[l], w_in[l], b_forget[l], moba_q_gain[l],
                   moba_k_gain[l], fox_q_gain[l], fox_k_gain[l], w_out[l], g_ffn[l], w_up[l],
                   conv_w[l], conv_b[l], w_down[l])
    return x
```

```python
import math

import jax
import jax.numpy as jnp
import numpy as np
from jax import lax
from jax.experimental import pallas as pl
from jax.experimental.pallas import tpu as pltpu

D_MODEL = 1024
HEAD_DIM = 64
N_MOBA_HEADS = 8
N_FOX_HEADS = 8
MOBA_WIDTH = N_MOBA_HEADS * HEAD_DIM
FOX_WIDTH = N_FOX_HEADS * HEAD_DIM
MOBA_BLOCK = 256
MOBA_TOPK = 3
ROPE_THETA = 500000.0
ROPE_DIM = HEAD_DIM // 4
D_FF = 2816
CONV_WIDTH = 3
NORM_EPS = 1e-6
NEG_INF = -1e30
LOG2E = math.log2(math.e)
Q_SCALE = HEAD_DIM ** -0.5 * LOG2E
LANES = 128
BF16_ROWS = 16
QK_COLS = 2 * MOBA_WIDTH + 2 * FOX_WIDTH
QK_COLS_PAD = QK_COLS + LANES
V_ROWS = HEAD_DIM + BF16_ROWS
SEQ_TILE = 256
ATTN_TILES_PER_STEP = 4
FF_CHUNK = 256
N_FF_CHUNKS = D_FF // FF_CHUNK
HALO = 8
N_SPLIT = 3
AUX_GROUP = 16
VMEM_LIMIT = 56 * 1024 * 1024

_NT = (((1,), (1,)), ((), ()))


def _bf16(a):
    return a.astype(jnp.bfloat16)


def _dot(a, b):
    return jnp.dot(a, b, preferred_element_type=jnp.float32)


def _dot_nt(a, b):
    return lax.dot_general(a, b, _NT, preferred_element_type=jnp.float32)


def _sigmoid(a):
    return 1.0 / (1.0 + jnp.exp(-a))


def _adaln_kernel(c_ref, w_ref, b_ref, o_ref):
    c = c_ref[...]
    s = c * _sigmoid(c)
    o_ref[...] = _dot(_bf16(s), _bf16(w_ref[...])) + b_ref[...]


def _adaln(c, w_ada, b_ada):
    b, d = c.shape
    n = w_ada.shape[1]
    tn = 1536
    return pl.pallas_call(
        _adaln_kernel,
        grid=(n // tn,),
        in_specs=[pl.BlockSpec((b, d), lambda j: (0, 0)),
                  pl.BlockSpec((d, tn), lambda j: (0, j)),
                  pl.BlockSpec((1, tn), lambda j: (0, j))],
        out_specs=pl.BlockSpec((b, tn), lambda j: (0, j)),
        out_shape=jax.ShapeDtypeStruct((b, n), jnp.float32),
        compiler_params=pltpu.CompilerParams(vmem_limit_bytes=VMEM_LIMIT),
        name="adaln",
    )(c, w_ada, b_ada.reshape(1, n))


def _rms_mod(x, g, sc, sh):
    ms = jnp.mean(x * x, axis=-1, keepdims=True)
    y = x * lax.rsqrt(ms + NORM_EPS)
    return (y * g) * (1.0 + sc) + sh


def _head_norm(p, gain, bd):
    sq = _bf16(p * p)
    half = 2 * LANES
    ss = jnp.concatenate([_dot(sq[:, :half], bd), _dot(sq[:, half:], bd)], axis=1)
    return (p * lax.rsqrt(ss * (1.0 / HEAD_DIM) + NORM_EPS)) * gain


def _rope(y, c, sa, sb):
    half = ROPE_DIM // 2
    outs = []
    for i in range(y.shape[1] // LANES):
        yc = y[:, i * LANES:(i + 1) * LANES]
        up = pltpu.roll(yc, LANES - half, 1)
        dn = pltpu.roll(yc, half, 1)
        outs.append(yc * c + up * sa + dn * sb)
    return jnp.concatenate(outs, axis=1)


def _inproj_kernel(x_ref, sc_ref, sh_ref, g_ref, w_ref, wvt_ref, gains_ref, rc_ref, rsa_ref, rsb_ref,
                   bd_ref, mq_ref, mk_ref, fq_ref, fk_ref, fl_ref, mvt_ref, fvt_ref):
    hn = _bf16(_rms_mod(x_ref[0], g_ref[...], sc_ref[0], sh_ref[0]))
    proj = _dot(hn, w_ref[...])
    vt = _dot_nt(wvt_ref[...], hn)
    bd = bd_ref[...]
    c, sa, sb = rc_ref[...], rsa_ref[...], rsb_ref[...]
    w = MOBA_WIDTH
    mq = _rope(_head_norm(proj[:, 0:w], gains_ref[0:1, :], bd), c, sa, sb)
    mq_ref[0] = _bf16(mq * Q_SCALE)
    mk = _rope(_head_norm(proj[:, w:2 * w], gains_ref[1:2, :], bd), c, sa, sb)
    mk_ref[0] = _bf16(mk)
    fq = _head_norm(proj[:, 2 * w:3 * w], gains_ref[2:3, :], bd)
    fq_ref[0] = _bf16(fq * Q_SCALE)
    fk = _head_norm(proj[:, 3 * w:4 * w], gains_ref[3:4, :], bd)
    fk_ref[0] = _bf16(fk)
    fl_ref[0] = proj[:, QK_COLS:QK_COLS_PAD]
    mvt_ref[0] = _bf16(vt[0:w, :])
    fvt_ref[0] = _bf16(vt[w:2 * w, :])


def _rope_tables(t):
    half = ROPE_DIM // 2
    inv_freq = jnp.power(ROPE_THETA, -2.0 * jnp.arange(half, dtype=jnp.float32) / ROPE_DIM)
    ang = jnp.arange(t, dtype=jnp.float32)[:, None] * inv_freq[None, :]
    cos, sin = jnp.cos(ang), jnp.sin(ang)
    d = np.arange(LANES) % HEAD_DIM
    first = jnp.asarray(d < half)[None, :]
    second = jnp.asarray((d >= half) & (d < ROPE_DIM))[None, :]
    idx = np.where(d < ROPE_DIM, d % half, 0)
    cos_l, sin_l = cos[:, idx], sin[:, idx]
    c = jnp.where(first | second, cos_l, 1.0)
    sa = jnp.where(first, -sin_l, 0.0)
    sb = jnp.where(second, sin_l, 0.0)
    return c, sa, sb


def _inproj(x, sc1, sh1, g_mix, w_qk, w_vt, gains, tables, bd):
    b, t, d = x.shape
    tm = SEQ_TILE
    w = MOBA_WIDTH
    row = lambda i, j: (i, 0, 0)
    tile = lambda i, j: (i, j, 0)
    tile_t = lambda i, j: (i, 0, j)
    const = lambda i, j: (0, 0)
    wide = jax.ShapeDtypeStruct((b, t, w), jnp.bfloat16)
    wide_t = jax.ShapeDtypeStruct((b, w, t), jnp.bfloat16)
    out_specs = ([pl.BlockSpec((1, tm, w), tile)] * 4 + [pl.BlockSpec((1, tm, LANES), tile)]
                 + [pl.BlockSpec((1, w, tm), tile_t)] * 2)
    return pl.pallas_call(
        _inproj_kernel,
        grid=(b, t // tm),
        in_specs=[pl.BlockSpec((1, tm, d), tile),
                  pl.BlockSpec((1, 1, d), row),
                  pl.BlockSpec((1, 1, d), row),
                  pl.BlockSpec((1, d), const),
                  pl.BlockSpec((d, QK_COLS_PAD), const),
                  pl.BlockSpec((2 * w, d), const),
                  pl.BlockSpec((4, w), const),
                  pl.BlockSpec((tm, LANES), lambda i, j: (j, 0)),
                  pl.BlockSpec((tm, LANES), lambda i, j: (j, 0)),
                  pl.BlockSpec((tm, LANES), lambda i, j: (j, 0)),
                  pl.BlockSpec((2 * LANES, 2 * LANES), const)],
        out_specs=out_specs,
        out_shape=[wide] * 4 + [jax.ShapeDtypeStruct((b, t, LANES), jnp.float32)] + [wide_t] * 2,
        compiler_params=pltpu.CompilerParams(vmem_limit_bytes=VMEM_LIMIT),
        name="inproj",
    )(x, sc1, sh1, g_mix, w_qk, w_vt, gains, *tables, bd)


def _split3(a):
    hi = _bf16(a)
    r1 = a - hi.astype(jnp.float32)
    mid = _bf16(r1)
    lo = _bf16(r1 - mid.astype(jnp.float32))
    return hi, mid, lo


def _group_base(head):
    pair, h = divmod(head, 2)
    return (0 if h == 1 else HEAD_DIM) + pair * AUX_GROUP


def _group_head(lane_idx):
    return 2 * ((lane_idx % HEAD_DIM) // AUX_GROUP) + jnp.where(lane_idx < HEAD_DIM, 1, 0)


def _routing_kernel(mq_ref, mk_ref, fl_ref, bf_ref, mqa_ref, fqa_ref, fka_ref):
    t = mq_ref.shape[1]
    nb = t // MOBA_BLOCK
    q = mq_ref[0]
    k = mk_ref[0]
    blk_of_col = lax.broadcasted_iota(jnp.int32, (nb, t), 1) // MOBA_BLOCK
    blk_row = lax.broadcasted_iota(jnp.int32, (nb, t), 0)
    ind = _bf16(jnp.where(blk_of_col == blk_row, 1.0, 0.0))
    kmean = _dot(ind, k) * (1.0 / MOBA_BLOCK)
    kmt = jnp.concatenate([kmean] * (LANES // nb), axis=0)
    r_head = _group_head(lax.broadcasted_iota(jnp.int32, kmt.shape, 0))
    c_head = lax.broadcasted_iota(jnp.int32, kmt.shape, 1) // HEAD_DIM
    kmt = jnp.where(r_head == c_head, kmt, 0.0)
    k_hi = _bf16(kmt)
    k_lo = _bf16(kmt - k_hi.astype(jnp.float32))
    gate = (_dot_nt(q, k_hi) + _dot_nt(q, k_lo)) * (1.0 / Q_SCALE)

    lane = lax.broadcasted_iota(jnp.int32, (1, LANES), 1)
    j = lane % nb
    own = lax.broadcasted_iota(jnp.int32, (t, 1), 0) // MOBA_BLOCK
    past = j < own
    g = jnp.where(past, gate, -jnp.inf)
    rank = jnp.zeros(gate.shape, jnp.int32)
    for d in range(1, nb):
        other = pltpu.roll(g, LANES - d, 1)
        first = ((j + d) % nb) < j
        beats = (other > g) | (first & (other == g))
        rank = rank + jnp.where(beats, 1, 0)
    attend = (past & (rank < MOBA_TOPK)) | (j == own)
    mqa_ref[0] = jnp.where(attend, 0.0, NEG_INF)

    z = fl_ref[0] + bf_ref[...]
    logf = -(jnp.maximum(-z, 0.0) + jnp.log1p(jnp.exp(-jnp.abs(z))))
    rr = lax.broadcasted_iota(jnp.int32, (MOBA_BLOCK, MOBA_BLOCK), 0)
    cc = lax.broadcasted_iota(jnp.int32, (MOBA_BLOCK, MOBA_BLOCK), 1)
    tri = _bf16(jnp.where(cc <= rr, 1.0, 0.0))
    slot = lane % AUX_GROUP
    carry = jnp.zeros((1, LANES), jnp.float32)
    for i in range(nb):
        rows = slice(i * MOBA_BLOCK, (i + 1) * MOBA_BLOCK)
        hi, mid, lo = _split3(logf[rows, :])
        cum = (_dot(tri, hi) + _dot(tri, mid)) + _dot(tri, lo) + carry
        carry = cum[MOBA_BLOCK - 1:MOBA_BLOCK, :]
        parts = [p.astype(jnp.float32) for p in _split3(cum * LOG2E)]
        fq = jnp.where(slot < 2 * N_SPLIT, 1.0, 0.0)
        fk = jnp.where(slot < N_SPLIT, 1.0, 0.0)
        for s in range(N_SPLIT):
            fq = jnp.where(slot == s, parts[s], fq)
            fk = jnp.where(slot == N_SPLIT + s, -parts[s], fk)
        fqa_ref[0, rows, :] = fq
        fka_ref[0, rows, :] = fk


def _routing(mq, mk, fl, bf_lanes):
    b, t, w = mq.shape
    full = lambda i: (i, 0, 0)
    aux = jax.ShapeDtypeStruct((b, t, LANES), jnp.float32)
    return pl.pallas_call(
        _routing_kernel,
        grid=(b,),
        in_specs=[pl.BlockSpec((1, t, w), full),
                  pl.BlockSpec((1, t, w), full),
                  pl.BlockSpec((1, t, LANES), full),
                  pl.BlockSpec((1, LANES), lambda i: (0, 0))],
        out_specs=[pl.BlockSpec((1, t, LANES), full)] * 3,
        out_shape=[aux] * 3,
        compiler_params=pltpu.CompilerParams(vmem_limit_bytes=VMEM_LIMIT),
        name="routing",
    )(mq, mk, fl, bf_lanes)


def _attn_kernel(*refs, moba):
    if moba:
        q_ref, qa_ref, k_ref, vt_ref, o_ref, kx_ref, vx_ref = refs
    else:
        q_ref, qa_ref, k_ref, vt_ref, ka_ref, o_ref, kx_ref, vx_ref = refs
    tq = SEQ_TILE
    t = k_ref.shape[1]
    nb = t // tq
    step = pl.program_id(2)
    lane = lax.broadcasted_iota(jnp.int32, (1, LANES), 1)
    low = lane < HEAD_DIM
    shift = (LANES - pl.program_id(1) * AUX_GROUP) % LANES
    in_group = (lane % HEAD_DIM) < AUX_GROUP // 2

    @pl.when(step == 0)
    def _():
        k2 = k_ref[0]
        if moba:
            blk = lax.broadcasted_iota(jnp.int32, (t, LANES), 0) // MOBA_BLOCK
            ln = lax.broadcasted_iota(jnp.int32, (t, LANES), 1) % HEAD_DIM
            ka = _bf16(jnp.where(ln == blk, 1.0, 0.0))
        else:
            ka = _bf16(pltpu.roll(ka_ref[0], shift, 1))
        kx_ref[0] = jnp.where(low, k2, ka)
        kx_ref[1] = jnp.where(low, ka, k2)
        for h in range(2):
            vx_ref[h, 0:HEAD_DIM, :] = vt_ref[0, h * HEAD_DIM:(h + 1) * HEAD_DIM, :]
            vx_ref[h, HEAD_DIM:V_ROWS, :] = jnp.ones((V_ROWS - HEAD_DIM, t), jnp.bfloat16)

    key = lax.broadcasted_iota(jnp.int32, (tq, tq), 0)
    qry = lax.broadcasted_iota(jnp.int32, (tq, tq), 1)
    causal = key <= qry

    def scores(n, h):
        rows = slice(n * tq, (n + 1) * tq)
        q2 = q_ref[0, rows, :]
        qa = pltpu.roll(qa_ref[0, rows, :], shift, 1)
        qa = _bf16(jnp.where(in_group, qa, 0.0))
        qx = jnp.where(low, q2, qa) if h == 0 else jnp.where(low, qa, q2)
        sd = jnp.where(causal, _dot_nt(kx_ref[h, rows, :], qx), NEG_INF)
        m = jnp.max(sd, axis=0, keepdims=True)
        sp = None
        if n > 0:
            sp = _dot_nt(kx_ref[h, 0:n * tq, :], qx)
            m = jnp.maximum(m, jnp.max(sp, axis=0, keepdims=True))
        return n, h, sd, sp, m

    def weighted_values(n, h, sd, sp, m):
        rows = slice(n * tq, (n + 1) * tq)
        acc = _dot(vx_ref[h, :, rows], _bf16(jnp.exp2(sd - m)))
        if n > 0:
            acc = acc + _dot(vx_ref[h, :, 0:n * tq], _bf16(jnp.exp2(sp - m)))
        return acc[0:HEAD_DIM, :] / acc[HEAD_DIM:HEAD_DIM + 1, :]

    def run(tiles):
        chains = [(n, h) for n in tiles for h in range(2)]
        outs = {}
        pending = scores(*chains[0])
        for nxt in chains[1:] + [None]:
            ahead = scores(*nxt) if nxt is not None else None
            outs[pending[:2]] = weighted_values(*pending)
            pending = ahead
        for n in tiles:
            o_t = jnp.concatenate([outs[(n, 0)], outs[(n, 1)]], axis=0)
            o_ref[0, n * tq:(n + 1) * tq, :] = _bf16(o_t.T)

    pairs = [(nb - 1 - i, i) for i in range(nb // 2)]
    n_static = _attn_steps(nb)
    for s in range(n_static):
        @pl.when(step == s)
        def _(s=s):
            run([n for pair in pairs[s::n_static] for n in pair])


def _attn_steps(nb):
    return max(1, nb // ATTN_TILES_PER_STEP)


def _attention(moba, q, qa, k, vt, ka=None):
    b, t, w = q.shape
    n_pairs = w // LANES
    nb = t // SEQ_TILE
    assert nb % 2 == 0 and (nb // 2) % _attn_steps(nb) == 0
    spec = pl.BlockSpec((1, t, LANES), lambda i, p, j: (i, 0, p))
    spec_t = pl.BlockSpec((1, LANES, t), lambda i, p, j: (i, p, 0))
    spec_aux = pl.BlockSpec((1, t, LANES), lambda i, p, j: (i, 0, 0))
    args = [q, qa, k, vt] if moba else [q, qa, k, vt, ka]
    specs = [spec, spec_aux, spec, spec_t] + ([] if moba else [spec_aux])
    kern = lambda *refs: _attn_kernel(*refs, moba=moba)
    return pl.pallas_call(
        kern,
        grid=(b, n_pairs, _attn_steps(nb)),
        in_specs=specs,
        out_specs=spec,
        out_shape=jax.ShapeDtypeStruct((b, t, w), jnp.bfloat16),
        scratch_shapes=[pltpu.VMEM((2, t, LANES), jnp.bfloat16),
                        pltpu.VMEM((2, V_ROWS, t), jnp.bfloat16)],
        compiler_params=pltpu.CompilerParams(vmem_limit_bytes=VMEM_LIMIT),
        name="moba" if moba else "fox",
    )(*args)


def _ffn_kernel(x_ref, om_ref, of_ref, wo_ref, gt1_ref, g_ref, sc_ref, sh_ref, wup_ref, cv_ref,
                wdn_ref, gt2_ref, out_ref, x1_ref, halo_ref, ubuf_ref):
    tm = x_ref.shape[1]

    @pl.when(pl.program_id(1) == 0)
    def _():
        halo_ref[...] = jnp.zeros(halo_ref.shape, jnp.float32)

    o = jnp.concatenate([om_ref[0], of_ref[0]], axis=1)
    x1 = x_ref[0] + gt1_ref[0] * _dot(o, wo_ref[...])
    x1_ref[...] = x1
    hn = _bf16(_rms_mod(x1, g_ref[...], sc_ref[0], sh_ref[0]))

    hmid = []
    for c in range(N_FF_CHUNKS):
        gate_cols = slice(c * FF_CHUNK, (c + 1) * FF_CHUNK)
        val_cols = slice(D_FF + c * FF_CHUNK, D_FF + (c + 1) * FF_CHUNK)
        u = jnp.concatenate([_dot(hn, wup_ref[:, gate_cols]), _dot(hn, wup_ref[:, val_cols])],
                            axis=1)
        ubuf_ref[c, 0:HALO, :] = halo_ref[c]
        ubuf_ref[c, HALO:HALO + tm, :] = u
        halo_ref[c] = u[tm - HALO:tm, :]
        cv = jnp.concatenate([cv_ref[:, gate_cols], cv_ref[:, val_cols]], axis=1)
        u1 = ubuf_ref[c, HALO - 1:HALO - 1 + tm, :]
        u2 = ubuf_ref[c, HALO - 2:HALO - 2 + tm, :]
        uc = cv[0:1, :] * u2 + cv[1:2, :] * u1 + cv[2:3, :] * u + cv[3:4, :]
        a = uc[:, :FF_CHUNK]
        val = uc[:, FF_CHUNK:]
        hmid.append(_bf16((a * _sigmoid(a)) * val))
    ffn = _dot(jnp.concatenate(hmid, axis=1), wdn_ref[...])
    out_ref[0] = x1_ref[...] + gt2_ref[0] * ffn


def _ffn(x, om, of, wo, gt1, g_ffn, sc2, sh2, wup, cv, wdn, gt2):
    b, t, d = x.shape
    tm = SEQ_TILE
    row = lambda i, j: (i, 0, 0)
    tile = lambda i, j: (i, j, 0)
    const2 = lambda i, j: (0, 0)
    once = dict(pipeline_mode=pl.Buffered(1))
    return pl.pallas_call(
        _ffn_kernel,
        grid=(b, t // tm),
        in_specs=[pl.BlockSpec((1, tm, d), tile),
                  pl.BlockSpec((1, tm, MOBA_WIDTH), tile),
                  pl.BlockSpec((1, tm, FOX_WIDTH), tile),
                  pl.BlockSpec((d, d), const2, **once),
                  pl.BlockSpec((1, 1, d), row),
                  pl.BlockSpec((1, d), const2),
                  pl.BlockSpec((1, 1, d), row),
                  pl.BlockSpec((1, 1, d), row),
                  pl.BlockSpec((d, 2 * D_FF), const2, **once),
                  pl.BlockSpec((HALO, 2 * D_FF), const2, **once),
                  pl.BlockSpec((D_FF, d), const2, **once),
                  pl.BlockSpec((1, 1, d), row)],
        out_specs=pl.BlockSpec((1, tm, d), tile),
        out_shape=jax.ShapeDtypeStruct((b, t, d), jnp.float32),
        scratch_shapes=[pltpu.VMEM((tm, d), jnp.float32),
                        pltpu.VMEM((N_FF_CHUNKS, HALO, 2 * FF_CHUNK), jnp.float32),
                        pltpu.VMEM((N_FF_CHUNKS, HALO + tm, 2 * FF_CHUNK), jnp.float32)],
        compiler_params=pltpu.CompilerParams(vmem_limit_bytes=VMEM_LIMIT),
        name="ffn",
    )(x, om, of, wo, gt1, g_ffn, sc2, sh2, wup, cv, wdn, gt2)


def _layer(x, c, w_ada, b_ada, g_mix, w_in, b_forget, moba_q_gain, moba_k_gain, fox_q_gain,
           fox_k_gain, w_out, g_ffn, w_up, conv_w, conv_b, w_down):
    b, t, d = x.shape
    nb = t // MOBA_BLOCK
    assert d == D_MODEL and t % SEQ_TILE == 0
    assert AUX_GROUP % nb == 0 and 2 * nb <= AUX_GROUP and 2 * N_SPLIT <= AUX_GROUP // 2

    mod = _adaln(c, w_ada, b_ada).reshape(b, 6, 1, d)
    sh1, sc1, gt1, sh2, sc2, gt2 = [mod[:, i] for i in range(6)]

    w = MOBA_WIDTH
    cols = lambda i: w_in[:, i * w:(i + 1) * w]
    lane_head = np.full(LANES, -1)
    for head in range(N_FOX_HEADS):
        lane_head[_group_base(head):_group_base(head) + 2 * N_SPLIT] = head
    used = jnp.asarray(lane_head >= 0)
    logits = jnp.where(used[None, :], w_in[:, 6 * w + np.maximum(lane_head, 0)], 0.0)
    w_qk = _bf16(jnp.concatenate([cols(0), cols(1), cols(3), cols(4), logits], axis=1))
    w_vt = _bf16(jnp.concatenate([cols(2), cols(5)], axis=1).T)
    gains = jnp.stack([jnp.tile(g, N_MOBA_HEADS) for g in
                       (moba_q_gain, moba_k_gain, fox_q_gain, fox_k_gain)])
    r = np.arange(2 * LANES) // HEAD_DIM
    bd = jnp.asarray(r[:, None] == r[None, :], jnp.bfloat16)
    mq, mk, fq, fk, fl, mvt, fvt = _inproj(x, sc1, sh1, g_mix.reshape(1, d), w_qk, w_vt, gains,
                                           _rope_tables(t), bd)

    bf_lanes = jnp.where(used, b_forget[np.maximum(lane_head, 0)], 0.0).reshape(1, LANES)
    mqa, fqa, fka = _routing(mq, mk, fl, bf_lanes)
    o_moba = _attention(True, mq, mqa, mk, mvt)
    o_fox = _attention(False, fq, fqa, fk, fvt, fka)

    cv = jnp.concatenate([conv_w, conv_b[None, :],
                          jnp.zeros((HALO - CONV_WIDTH - 1, 2 * D_FF), jnp.float32)], axis=0)
    return _ffn(x, o_moba, o_fox, _bf16(w_out), gt1, g_ffn.reshape(1, d), sc2, sh2, _bf16(w_up), cv,
                _bf16(w_down), gt2)


def kernel(x, c, w_ada, b_ada, g_mix, w_in, b_forget, moba_q_gain, moba_k_gain, fox_q_gain,
           fox_k_gain, w_out, g_ffn, w_up, conv_w, conv_b, w_down):
    for l in range(w_ada.shape[0]):
        x = _layer(x, c, w_ada[l], b_ada[l], g_mix[l], w_in[l], b_forget[l], moba_q_gain[l],
                   moba_k_gain[l], fox_q_gain[l], fox_k_gain[l], w_out[l], g_ffn[l], w_up[l],
                   conv_w[l], conv_b[l], w_down[l])
    return x
```

```python
import math

import jax
import jax.numpy as jnp
import numpy as np
from jax import lax
from jax.experimental import pallas as pl
from jax.experimental.pallas import tpu as pltpu

D_MODEL = 1024
HEAD_DIM = 64
N_MOBA_HEADS = 8
N_FOX_HEADS = 8
MOBA_WIDTH = N_MOBA_HEADS * HEAD_DIM
FOX_WIDTH = N_FOX_HEADS * HEAD_DIM
MOBA_BLOCK = 256
MOBA_TOPK = 3
ROPE_THETA = 500000.0
ROPE_DIM = HEAD_DIM // 4
D_FF = 2816
CONV_WIDTH = 3
NORM_EPS = 1e-6
NEG_INF = -1e30
LOG2E = math.log2(math.e)
Q_SCALE = HEAD_DIM ** -0.5 * LOG2E
LANES = 128
BF16_ROWS = 16
QK_COLS = 2 * MOBA_WIDTH + 2 * FOX_WIDTH
QK_COLS_PAD = QK_COLS + LANES
V_ROWS = HEAD_DIM + BF16_ROWS
SEQ_TILE = 256
ATTN_TILES_PER_STEP = 8
FF_CHUNK = 256
N_FF_CHUNKS = D_FF // FF_CHUNK
HALO = 8
N_SPLIT = 3
AUX_GROUP = 16
VMEM_LIMIT = 56 * 1024 * 1024

_NT = (((1,), (1,)), ((), ()))


def _bf16(a):
    return a.astype(jnp.bfloat16)


def _dot(a, b):
    return jnp.dot(a, b, preferred_element_type=jnp.float32)


def _dot_nt(a, b):
    return lax.dot_general(a, b, _NT, preferred_element_type=jnp.float32)


def _sigmoid(a):
    return 1.0 / (1.0 + jnp.exp(-a))


def _adaln_kernel(c_ref, w_ref, b_ref, o_ref):
    c = c_ref[...]
    s = c * _sigmoid(c)
    o_ref[...] = _dot(_bf16(s), _bf16(w_ref[...])) + b_ref[...]


def _adaln(c, w_ada, b_ada):
    b, d = c.shape
    n = w_ada.shape[1]
    tn = 1536
    return pl.pallas_call(
        _adaln_kernel,
        grid=(n // tn,),
        in_specs=[pl.BlockSpec((b, d), lambda j: (0, 0)),
                  pl.BlockSpec((d, tn), lambda j: (0, j)),
                  pl.BlockSpec((1, tn), lambda j: (0, j))],
        out_specs=pl.BlockSpec((b, tn), lambda j: (0, j)),
        out_shape=jax.ShapeDtypeStruct((b, n), jnp.float32),
        compiler_params=pltpu.CompilerParams(vmem_limit_bytes=VMEM_LIMIT),
        name="adaln",
    )(c, w_ada, b_ada.reshape(1, n))


def _rms_mod(x, g, sc, sh):
    ms = jnp.mean(x * x, axis=-1, keepdims=True)
    y = x * lax.rsqrt(ms + NORM_EPS)
    return (y * g) * (1.0 + sc) + sh


def _head_norm(p, gain, bd):
    sq = _bf16(p * p)
    half = 2 * LANES
    ss = jnp.concatenate([_dot(sq[:, :half], bd), _dot(sq[:, half:], bd)], axis=1)
    return (p * lax.rsqrt(ss * (1.0 / HEAD_DIM) + NORM_EPS)) * gain


def _rope(y, c, sa, sb):
    half = ROPE_DIM // 2
    outs = []
    for i in range(y.shape[1] // LANES):
        yc = y[:, i * LANES:(i + 1) * LANES]
        up = pltpu.roll(yc, LANES - half, 1)
        dn = pltpu.roll(yc, half, 1)
        outs.append(yc * c + up * sa + dn * sb)
    return jnp.concatenate(outs, axis=1)


def _inproj_kernel(x_ref, sc_ref, sh_ref, g_ref, w_ref, wvt_ref, gains_ref, rc_ref, rsa_ref, rsb_ref,
                   bd_ref, mq_ref, mk_ref, fq_ref, fk_ref, fl_ref, mvt_ref, fvt_ref):
    hn = _bf16(_rms_mod(x_ref[0], g_ref[...], sc_ref[0], sh_ref[0]))
    proj = _dot(hn, w_ref[...])
    vt = _dot_nt(wvt_ref[...], hn)
    bd = bd_ref[...]
    c, sa, sb = rc_ref[...], rsa_ref[...], rsb_ref[...]
    w = MOBA_WIDTH
    mq = _rope(_head_norm(proj[:, 0:w], gains_ref[0:1, :], bd), c, sa, sb)
    mq_ref[0] = _bf16(mq * Q_SCALE)
    mk = _rope(_head_norm(proj[:, w:2 * w], gains_ref[1:2, :], bd), c, sa, sb)
    mk_ref[0] = _bf16(mk)
    fq = _head_norm(proj[:, 2 * w:3 * w], gains_ref[2:3, :], bd)
    fq_ref[0] = _bf16(fq * Q_SCALE)
    fk = _head_norm(proj[:, 3 * w:4 * w], gains_ref[3:4, :], bd)
    fk_ref[0] = _bf16(fk)
    fl_ref[0] = proj[:, QK_COLS:QK_COLS_PAD]
    mvt_ref[0] = _bf16(vt[0:w, :])
    fvt_ref[0] = _bf16(vt[w:2 * w, :])


def _rope_tables(t):
    half = ROPE_DIM // 2
    inv_freq = jnp.power(ROPE_THETA, -2.0 * jnp.arange(half, dtype=jnp.float32) / ROPE_DIM)
    ang = jnp.arange(t, dtype=jnp.float32)[:, None] * inv_freq[None, :]
    cos, sin = jnp.cos(ang), jnp.sin(ang)
    d = np.arange(LANES) % HEAD_DIM
    first = jnp.asarray(d < half)[None, :]
    second = jnp.asarray((d >= half) & (d < ROPE_DIM))[None, :]
    idx = np.where(d < ROPE_DIM, d % half, 0)
    cos_l, sin_l = cos[:, idx], sin[:, idx]
    c = jnp.where(first | second, cos_l, 1.0)
    sa = jnp.where(first, -sin_l, 0.0)
    sb = jnp.where(second, sin_l, 0.0)
    return c, sa, sb


def _inproj(x, sc1, sh1, g_mix, w_qk, w_vt, gains, tables, bd):
    b, t, d = x.shape
    tm = SEQ_TILE
    w = MOBA_WIDTH
    row = lambda i, j: (i, 0, 0)
    tile = lambda i, j: (i, j, 0)
    tile_t = lambda i, j: (i, 0, j)
    const = lambda i, j: (0, 0)
    wide = jax.ShapeDtypeStruct((b, t, w), jnp.bfloat16)
    wide_t = jax.ShapeDtypeStruct((b, w, t), jnp.bfloat16)
    out_specs = ([pl.BlockSpec((1, tm, w), tile)] * 4 + [pl.BlockSpec((1, tm, LANES), tile)]
                 + [pl.BlockSpec((1, w, tm), tile_t)] * 2)
    return pl.pallas_call(
        _inproj_kernel,
        grid=(b, t // tm),
        in_specs=[pl.BlockSpec((1, tm, d), tile),
                  pl.BlockSpec((1, 1, d), row),
                  pl.BlockSpec((1, 1, d), row),
                  pl.BlockSpec((1, d), const),
                  pl.BlockSpec((d, QK_COLS_PAD), const),
                  pl.BlockSpec((2 * w, d), const),
                  pl.BlockSpec((4, w), const),
                  pl.BlockSpec((tm, LANES), lambda i, j: (j, 0)),
                  pl.BlockSpec((tm, LANES), lambda i, j: (j, 0)),
                  pl.BlockSpec((tm, LANES), lambda i, j: (j, 0)),
                  pl.BlockSpec((2 * LANES, 2 * LANES), const)],
        out_specs=out_specs,
        out_shape=[wide] * 4 + [jax.ShapeDtypeStruct((b, t, LANES), jnp.float32)] + [wide_t] * 2,
        compiler_params=pltpu.CompilerParams(vmem_limit_bytes=VMEM_LIMIT),
        name="inproj",
    )(x, sc1, sh1, g_mix, w_qk, w_vt, gains, *tables, bd)


def _split3(a):
    hi = _bf16(a)
    r1 = a - hi.astype(jnp.float32)
    mid = _bf16(r1)
    lo = _bf16(r1 - mid.astype(jnp.float32))
    return hi, mid, lo


def _group_base(head):
    pair, h = divmod(head, 2)
    return (0 if h == 1 else HEAD_DIM) + pair * AUX_GROUP


def _group_head(lane_idx):
    return 2 * ((lane_idx % HEAD_DIM) // AUX_GROUP) + jnp.where(lane_idx < HEAD_DIM, 1, 0)


def _routing_kernel(mq_ref, mk_ref, fl_ref, bf_ref, mqa_ref, fqa_ref, fka_ref):
    t = mq_ref.shape[1]
    nb = t // MOBA_BLOCK
    q = mq_ref[0]
    k = mk_ref[0]
    blk_of_col = lax.broadcasted_iota(jnp.int32, (nb, t), 1) // MOBA_BLOCK
    blk_row = lax.broadcasted_iota(jnp.int32, (nb, t), 0)
    ind = _bf16(jnp.where(blk_of_col == blk_row, 1.0, 0.0))
    kmean = _dot(ind, k) * (1.0 / MOBA_BLOCK)
    kmt = jnp.concatenate([kmean] * (LANES // nb), axis=0)
    r_head = _group_head(lax.broadcasted_iota(jnp.int32, kmt.shape, 0))
    c_head = lax.broadcasted_iota(jnp.int32, kmt.shape, 1) // HEAD_DIM
    kmt = jnp.where(r_head == c_head, kmt, 0.0)
    k_hi = _bf16(kmt)
    k_lo = _bf16(kmt - k_hi.astype(jnp.float32))
    gate = (_dot_nt(q, k_hi) + _dot_nt(q, k_lo)) * (1.0 / Q_SCALE)

    lane = lax.broadcasted_iota(jnp.int32, (1, LANES), 1)
    j = lane % nb
    own = lax.broadcasted_iota(jnp.int32, (t, 1), 0) // MOBA_BLOCK
    past = j < own
    g = jnp.where(past, gate, -jnp.inf)
    rank = jnp.zeros(gate.shape, jnp.int32)
    for d in range(1, nb):
        other = pltpu.roll(g, LANES - d, 1)
        first = ((j + d) % nb) < j
        beats = (other > g) | (first & (other == g))
        rank = rank + jnp.where(beats, 1, 0)
    attend = (past & (rank < MOBA_TOPK)) | (j == own)
    mqa_ref[0] = jnp.where(attend, 0.0, NEG_INF)

    z = fl_ref[0] + bf_ref[...]
    logf = -(jnp.maximum(-z, 0.0) + jnp.log1p(jnp.exp(-jnp.abs(z))))
    rr = lax.broadcasted_iota(jnp.int32, (MOBA_BLOCK, MOBA_BLOCK), 0)
    cc = lax.broadcasted_iota(jnp.int32, (MOBA_BLOCK, MOBA_BLOCK), 1)
    tri = _bf16(jnp.where(cc <= rr, 1.0, 0.0))
    slot = lane % AUX_GROUP
    carry = jnp.zeros((1, LANES), jnp.float32)
    for i in range(nb):
        rows = slice(i * MOBA_BLOCK, (i + 1) * MOBA_BLOCK)
        hi, mid, lo = _split3(logf[rows, :])
        cum = (_dot(tri, hi) + _dot(tri, mid)) + _dot(tri, lo) + carry
        carry = cum[MOBA_BLOCK - 1:MOBA_BLOCK, :]
        parts = [p.astype(jnp.float32) for p in _split3(cum * LOG2E)]
        fq = jnp.where(slot < 2 * N_SPLIT, 1.0, 0.0)
        fk = jnp.where(slot < N_SPLIT, 1.0, 0.0)
        for s in range(N_SPLIT):
            fq = jnp.where(slot == s, parts[s], fq)
            fk = jnp.where(slot == N_SPLIT + s, -parts[s], fk)
        fqa_ref[0, rows, :] = fq
        fka_ref[0, rows, :] = fk


def _routing(mq, mk, fl, bf_lanes):
    b, t, w = mq.shape
    full = lambda i: (i, 0, 0)
    aux = jax.ShapeDtypeStruct((b, t, LANES), jnp.float32)
    return pl.pallas_call(
        _routing_kernel,
        grid=(b,),
        in_specs=[pl.BlockSpec((1, t, w), full),
                  pl.BlockSpec((1, t, w), full),
                  pl.BlockSpec((1, t, LANES), full),
                  pl.BlockSpec((1, LANES), lambda i: (0, 0))],
        out_specs=[pl.BlockSpec((1, t, LANES), full)] * 3,
        out_shape=[aux] * 3,
        compiler_params=pltpu.CompilerParams(vmem_limit_bytes=VMEM_LIMIT),
        name="routing",
    )(mq, mk, fl, bf_lanes)


def _attn_kernel(*refs, moba):
    if moba:
        q_ref, qa_ref, k_ref, vt_ref, o_ref, kx_ref, vx_ref = refs
    else:
        q_ref, qa_ref, k_ref, vt_ref, ka_ref, o_ref, kx_ref, vx_ref = refs
    tq = SEQ_TILE
    t = k_ref.shape[1]
    nb = t // tq
    step = pl.program_id(2)
    lane = lax.broadcasted_iota(jnp.int32, (1, LANES), 1)
    low = lane < HEAD_DIM
    shift = (LANES - pl.program_id(1) * AUX_GROUP) % LANES
    in_group = (lane % HEAD_DIM) < AUX_GROUP // 2

    @pl.when(step == 0)
    def _():
        k2 = k_ref[0]
        if moba:
            blk = lax.broadcasted_iota(jnp.int32, (t, LANES), 0) // MOBA_BLOCK
            ln = lax.broadcasted_iota(jnp.int32, (t, LANES), 1) % HEAD_DIM
            ka = _bf16(jnp.where(ln == blk, 1.0, 0.0))
        else:
            ka = _bf16(pltpu.roll(ka_ref[0], shift, 1))
        kx_ref[0] = jnp.where(low, k2, ka)
        kx_ref[1] = jnp.where(low, ka, k2)
        for h in range(2):
            vx_ref[h, 0:HEAD_DIM, :] = vt_ref[0, h * HEAD_DIM:(h + 1) * HEAD_DIM, :]
            vx_ref[h, HEAD_DIM:V_ROWS, :] = jnp.ones((V_ROWS - HEAD_DIM, t), jnp.bfloat16)

    key = lax.broadcasted_iota(jnp.int32, (tq, tq), 0)
    qry = lax.broadcasted_iota(jnp.int32, (tq, tq), 1)
    causal = key <= qry

    def scores(n, h):
        rows = slice(n * tq, (n + 1) * tq)
        q2 = q_ref[0, rows, :]
        qa = pltpu.roll(qa_ref[0, rows, :], shift, 1)
        qa = _bf16(jnp.where(in_group, qa, 0.0))
        qx = jnp.where(low, q2, qa) if h == 0 else jnp.where(low, qa, q2)
        sd = jnp.where(causal, _dot_nt(kx_ref[h, rows, :], qx), NEG_INF)
        m = jnp.max(sd, axis=0, keepdims=True)
        sp = None
        if n > 0:
            sp = _dot_nt(kx_ref[h, 0:n * tq, :], qx)
            m = jnp.maximum(m, jnp.max(sp, axis=0, keepdims=True))
        return n, h, sd, sp, m

    def weighted_values(n, h, sd, sp, m):
        rows = slice(n * tq, (n + 1) * tq)
        acc = _dot(vx_ref[h, :, rows], _bf16(jnp.exp2(sd - m)))
        if n > 0:
            acc = acc + _dot(vx_ref[h, :, 0:n * tq], _bf16(jnp.exp2(sp - m)))
        return acc[0:HEAD_DIM, :] / acc[HEAD_DIM:HEAD_DIM + 1, :]

    def run(tiles):
        chains = [(n, h) for n in tiles for h in range(2)]
        outs = {}
        pending = scores(*chains[0])
        for nxt in chains[1:] + [None]:
            ahead = scores(*nxt) if nxt is not None else None
            outs[pending[:2]] = weighted_values(*pending)
            pending = ahead
        for n in tiles:
            o_t = jnp.concatenate([outs[(n, 0)], outs[(n, 1)]], axis=0)
            o_ref[0, n * tq:(n + 1) * tq, :] = _bf16(o_t.T)

    pairs = [(nb - 1 - i, i) for i in range(nb // 2)]
    n_static = _attn_steps(nb)
    for s in range(n_static):
        @pl.when(step == s)
        def _(s=s):
            run([n for pair in pairs[s::n_static] for n in pair])


def _attn_steps(nb):
    return max(1, nb // ATTN_TILES_PER_STEP)


def _attention(moba, q, qa, k, vt, ka=None):
    b, t, w = q.shape
    n_pairs = w // LANES
    nb = t // SEQ_TILE
    assert nb % 2 == 0 and (nb // 2) % _attn_steps(nb) == 0
    spec = pl.BlockSpec((1, t, LANES), lambda i, p, j: (i, 0, p))
    spec_t = pl.BlockSpec((1, LANES, t), lambda i, p, j: (i, p, 0))
    spec_aux = pl.BlockSpec((1, t, LANES), lambda i, p, j: (i, 0, 0))
    args = [q, qa, k, vt] if moba else [q, qa, k, vt, ka]
    specs = [spec, spec_aux, spec, spec_t] + ([] if moba else [spec_aux])
    kern = lambda *refs: _attn_kernel(*refs, moba=moba)
    return pl.pallas_call(
        kern,
        grid=(b, n_pairs, _attn_steps(nb)),
        in_specs=specs,
        out_specs=spec,
        out_shape=jax.ShapeDtypeStruct((b, t, w), jnp.bfloat16),
        scratch_shapes=[pltpu.VMEM((2, t, LANES), jnp.bfloat16),
                        pltpu.VMEM((2, V_ROWS, t), jnp.bfloat16)],
        compiler_params=pltpu.CompilerParams(vmem_limit_bytes=VMEM_LIMIT),
        name="moba" if moba else "fox",
    )(*args)


def _ffn_kernel(x_ref, om_ref, of_ref, wo_ref, gt1_ref, g_ref, sc_ref, sh_ref, wup_ref, cv_ref,
                wdn_ref, gt2_ref, out_ref, x1_ref, halo_ref, ubuf_ref):
    tm = x_ref.shape[1]

    @pl.when(pl.program_id(1) == 0)
    def _():
        halo_ref[...] = jnp.zeros(halo_ref.shape, jnp.float32)

    o = jnp.concatenate([om_ref[0], of_ref[0]], axis=1)
    x1 = x_ref[0] + gt1_ref[0] * _dot(o, wo_ref[...])
    x1_ref[...] = x1
    hn = _bf16(_rms_mod(x1, g_ref[...], sc_ref[0], sh_ref[0]))

    hmid = []
    for c in range(N_FF_CHUNKS):
        gate_cols = slice(c * FF_CHUNK, (c + 1) * FF_CHUNK)
        val_cols = slice(D_FF + c * FF_CHUNK, D_FF + (c + 1) * FF_CHUNK)
        u = jnp.concatenate([_dot(hn, wup_ref[:, gate_cols]), _dot(hn, wup_ref[:, val_cols])],
                            axis=1)
        ubuf_ref[c, 0:HALO, :] = halo_ref[c]
        ubuf_ref[c, HALO:HALO + tm, :] = u
        halo_ref[c] = u[tm - HALO:tm, :]
        cv = jnp.concatenate([cv_ref[:, gate_cols], cv_ref[:, val_cols]], axis=1)
        u1 = ubuf_ref[c, HALO - 1:HALO - 1 + tm, :]
        u2 = ubuf_ref[c, HALO - 2:HALO - 2 + tm, :]
        uc = cv[0:1, :] * u2 + cv[1:2, :] * u1 + cv[2:3, :] * u + cv[3:4, :]
        a = uc[:, :FF_CHUNK]
        val = uc[:, FF_CHUNK:]
        hmid.append(_bf16((a * _sigmoid(a)) * val))
    ffn = _dot(jnp.concatenate(hmid, axis=1), wdn_ref[...])
    out_ref[0] = x1_ref[...] + gt2_ref[0] * ffn


def _ffn(x, om, of, wo, gt1, g_ffn, sc2, sh2, wup, cv, wdn, gt2):
    b, t, d = x.shape
    tm = SEQ_TILE
    row = lambda i, j: (i, 0, 0)
    tile = lambda i, j: (i, j, 0)
    const2 = lambda i, j: (0, 0)
    once = dict(pipeline_mode=pl.Buffered(1))
    return pl.pallas_call(
        _ffn_kernel,
        grid=(b, t // tm),
        in_specs=[pl.BlockSpec((1, tm, d), tile),
                  pl.BlockSpec((1, tm, MOBA_WIDTH), tile),
                  pl.BlockSpec((1, tm, FOX_WIDTH), tile),
                  pl.BlockSpec((d, d), const2, **once),
                  pl.BlockSpec((1, 1, d), row),
                  pl.BlockSpec((1, d), const2),
                  pl.BlockSpec((1, 1, d), row),
                  pl.BlockSpec((1, 1, d), row),
                  pl.BlockSpec((d, 2 * D_FF), const2, **once),
                  pl.BlockSpec((HALO, 2 * D_FF), const2, **once),
                  pl.BlockSpec((D_FF, d), const2, **once),
                  pl.BlockSpec((1, 1, d), row)],
        out_specs=pl.BlockSpec((1, tm, d), tile),
        out_shape=jax.ShapeDtypeStruct((b, t, d), jnp.float32),
        scratch_shapes=[pltpu.VMEM((tm, d), jnp.float32),
                        pltpu.VMEM((N_FF_CHUNKS, HALO, 2 * FF_CHUNK), jnp.float32),
                        pltpu.VMEM((N_FF_CHUNKS, HALO + tm, 2 * FF_CHUNK), jnp.float32)],
        compiler_params=pltpu.CompilerParams(vmem_limit_bytes=VMEM_LIMIT),
        name="ffn",
    )(x, om, of, wo, gt1, g_ffn, sc2, sh2, wup, cv, wdn, gt2)


def _layer(x, c, w_ada, b_ada, g_mix, w_in, b_forget, moba_q_gain, moba_k_gain, fox_q_gain,
           fox_k_gain, w_out, g_ffn, w_up, conv_w, conv_b, w_down):
    b, t, d = x.shape
    nb = t // MOBA_BLOCK
    assert d == D_MODEL and t % SEQ_TILE == 0
    assert AUX_GROUP % nb == 0 and 2 * nb <= AUX_GROUP and 2 * N_SPLIT <= AUX_GROUP // 2

    mod = _adaln(c, w_ada, b_ada).reshape(b, 6, 1, d)
    sh1, sc1, gt1, sh2, sc2, gt2 = [mod[:, i] for i in range(6)]

    w = MOBA_WIDTH
    cols = lambda i: w_in[:, i * w:(i + 1) * w]
    lane_head = np.full(LANES, -1)
    for head in range(N_FOX_HEADS):
        lane_head[_group_base(head):_group_base(head) + 2 * N_SPLIT] = head
    used = jnp.asarray(lane_head >= 0)
    logits = jnp.where(used[None, :], w_in[:, 6 * w + np.maximum(lane_head, 0)], 0.0)
    w_qk = _bf16(jnp.concatenate([cols(0), cols(1), cols(3), cols(4), logits], axis=1))
    w_vt = _bf16(jnp.concatenate([cols(2), cols(5)], axis=1).T)
    gains = jnp.stack([jnp.tile(g, N_MOBA_HEADS) for g in
                       (moba_q_gain, moba_k_gain, fox_q_gain, fox_k_gain)])
    r = np.arange(2 * LANES) // HEAD_DIM
    bd = jnp.asarray(r[:, None] == r[None, :], jnp.bfloat16)
    mq, mk, fq, fk, fl, mvt, fvt = _inproj(x, sc1, sh1, g_mix.reshape(1, d), w_qk, w_vt, gains,
                                           _rope_tables(t), bd)

    bf_lanes = jnp.where(used, b_forget[np.maximum(lane_head, 0)], 0.0).reshape(1, LANES)
    mqa, fqa, fka = _routing(mq, mk, fl, bf_lanes)
    o_moba = _attention(True, mq, mqa, mk, mvt)
    o_fox = _attention(False, fq, fqa, fk, fvt, fka)

    cv = jnp.concatenate([conv_w, conv_b[None, :],
                          jnp.zeros((HALO - CONV_WIDTH - 1, 2 * D_FF), jnp.float32)], axis=0)
    return _ffn(x, o_moba, o_fox, _bf16(w_out), gt1, g_ffn.reshape(1, d), sc2, sh2, _bf16(w_up), cv,
                _bf16(w_down), gt2)


def kernel(x, c, w_ada, b_ada, g_mix, w_in, b_forget, moba_q_gain, moba_k_gain, fox_q_gain,
           fox_k_gain, w_out, g_ffn, w_up, conv_w, conv_b, w_down):
    for l in range(w_ada.shape[0]):
        x = _layer(x, c, w_ada[l], b_ada[l], g_mix[l], w_in[l], b_forget[l], moba_q_gain[l],
                   moba_k_gain[l], fox_q_gain[l], fox_k_gain[l], w_out[l], g_ffn[l], w_up[l],
                   conv_w[l], conv_b[l], w_down[l])
    return x
```

```python
import math

import jax
import jax.numpy as jnp
import numpy as np
from jax import lax
from jax.experimental import pallas as pl
from jax.experimental.pallas import tpu as pltpu

D_MODEL = 1024
HEAD_DIM = 64
N_MOBA_HEADS = 8
N_FOX_HEADS = 8
MOBA_WIDTH = N_MOBA_HEADS * HEAD_DIM
FOX_WIDTH = N_FOX_HEADS * HEAD_DIM
MOBA_BLOCK = 256
MOBA_TOPK = 3
ROPE_THETA = 500000.0
ROPE_DIM = HEAD_DIM // 4
D_FF = 2816
CONV_WIDTH = 3
NORM_EPS = 1e-6
NEG_INF = -1e30
LOG2E = math.log2(math.e)
Q_SCALE = HEAD_DIM ** -0.5 * LOG2E
LANES = 128
BF16_ROWS = 16
QK_COLS = 2 * MOBA_WIDTH + 2 * FOX_WIDTH
QK_COLS_PAD = QK_COLS + LANES
V_ROWS = HEAD_DIM + BF16_ROWS
SEQ_TILE = 256
ATTN_TILES_PER_STEP = 8
FFN_LOOKAHEAD = 2
ATTN_LOOKAHEAD = 5
FF_CHUNK = 256
N_FF_CHUNKS = D_FF // FF_CHUNK
HALO = 8
N_SPLIT = 3
AUX_GROUP = 16
VMEM_LIMIT = 56 * 1024 * 1024

_NT = (((1,), (1,)), ((), ()))


def _bf16(a):
    return a.astype(jnp.bfloat16)


def _dot(a, b):
    return jnp.dot(a, b, preferred_element_type=jnp.float32)


def _dot_nt(a, b):
    return lax.dot_general(a, b, _NT, preferred_element_type=jnp.float32)


def _sigmoid(a):
    return 1.0 / (1.0 + jnp.exp(-a))


def _adaln_kernel(c_ref, w_ref, b_ref, o_ref):
    c = c_ref[...]
    s = c * _sigmoid(c)
    o_ref[...] = _dot(_bf16(s), _bf16(w_ref[...])) + b_ref[...]


def _adaln(c, w_ada, b_ada):
    b, d = c.shape
    n = w_ada.shape[1]
    tn = 1536
    return pl.pallas_call(
        _adaln_kernel,
        grid=(n // tn,),
        in_specs=[pl.BlockSpec((b, d), lambda j: (0, 0)),
                  pl.BlockSpec((d, tn), lambda j: (0, j)),
                  pl.BlockSpec((1, tn), lambda j: (0, j))],
        out_specs=pl.BlockSpec((b, tn), lambda j: (0, j)),
        out_shape=jax.ShapeDtypeStruct((b, n), jnp.float32),
        compiler_params=pltpu.CompilerParams(vmem_limit_bytes=VMEM_LIMIT),
        name="adaln",
    )(c, w_ada, b_ada.reshape(1, n))


def _rms_mod(x, g, sc, sh):
    ms = jnp.mean(x * x, axis=-1, keepdims=True)
    y = x * lax.rsqrt(ms + NORM_EPS)
    return (y * g) * (1.0 + sc) + sh


def _head_norm(p, gain, bd):
    sq = _bf16(p * p)
    half = 2 * LANES
    ss = jnp.concatenate([_dot(sq[:, :half], bd), _dot(sq[:, half:], bd)], axis=1)
    return (p * lax.rsqrt(ss * (1.0 / HEAD_DIM) + NORM_EPS)) * gain


def _rope(y, c, sa, sb):
    half = ROPE_DIM // 2
    outs = []
    for i in range(y.shape[1] // LANES):
        yc = y[:, i * LANES:(i + 1) * LANES]
        up = pltpu.roll(yc, LANES - half, 1)
        dn = pltpu.roll(yc, half, 1)
        outs.append(yc * c + up * sa + dn * sb)
    return jnp.concatenate(outs, axis=1)


def _inproj_kernel(x_ref, sc_ref, sh_ref, g_ref, w_ref, wvt_ref, gains_ref, rc_ref, rsa_ref, rsb_ref,
                   bd_ref, mq_ref, mk_ref, fq_ref, fk_ref, fl_ref, mvt_ref, fvt_ref):
    hn = _bf16(_rms_mod(x_ref[0], g_ref[...], sc_ref[0], sh_ref[0]))
    proj = _dot(hn, w_ref[...])
    vt = _dot_nt(wvt_ref[...], hn)
    bd = bd_ref[...]
    c, sa, sb = rc_ref[...], rsa_ref[...], rsb_ref[...]
    w = MOBA_WIDTH
    mq = _rope(_head_norm(proj[:, 0:w], gains_ref[0:1, :], bd), c, sa, sb)
    mq_ref[0] = _bf16(mq * Q_SCALE)
    mk = _rope(_head_norm(proj[:, w:2 * w], gains_ref[1:2, :], bd), c, sa, sb)
    mk_ref[0] = _bf16(mk)
    fq = _head_norm(proj[:, 2 * w:3 * w], gains_ref[2:3, :], bd)
    fq_ref[0] = _bf16(fq * Q_SCALE)
    fk = _head_norm(proj[:, 3 * w:4 * w], gains_ref[3:4, :], bd)
    fk_ref[0] = _bf16(fk)
    fl_ref[0] = proj[:, QK_COLS:QK_COLS_PAD]
    mvt_ref[0] = _bf16(vt[0:w, :])
    fvt_ref[0] = _bf16(vt[w:2 * w, :])


def _rope_tables(t):
    half = ROPE_DIM // 2
    inv_freq = jnp.power(ROPE_THETA, -2.0 * jnp.arange(half, dtype=jnp.float32) / ROPE_DIM)
    ang = jnp.arange(t, dtype=jnp.float32)[:, None] * inv_freq[None, :]
    cos, sin = jnp.cos(ang), jnp.sin(ang)
    d = np.arange(LANES) % HEAD_DIM
    first = jnp.asarray(d < half)[None, :]
    second = jnp.asarray((d >= half) & (d < ROPE_DIM))[None, :]
    idx = np.where(d < ROPE_DIM, d % half, 0)
    cos_l, sin_l = cos[:, idx], sin[:, idx]
    c = jnp.where(first | second, cos_l, 1.0)
    sa = jnp.where(first, -sin_l, 0.0)
    sb = jnp.where(second, sin_l, 0.0)
    return c, sa, sb


def _inproj(x, sc1, sh1, g_mix, w_qk, w_vt, gains, tables, bd):
    b, t, d = x.shape
    tm = SEQ_TILE
    w = MOBA_WIDTH
    row = lambda i, j: (i, 0, 0)
    tile = lambda i, j: (i, j, 0)
    tile_t = lambda i, j: (i, 0, j)
    const = lambda i, j: (0, 0)
    wide = jax.ShapeDtypeStruct((b, t, w), jnp.bfloat16)
    wide_t = jax.ShapeDtypeStruct((b, w, t), jnp.bfloat16)
    out_specs = ([pl.BlockSpec((1, tm, w), tile)] * 4 + [pl.BlockSpec((1, tm, LANES), tile)]
                 + [pl.BlockSpec((1, w, tm), tile_t)] * 2)
    return pl.pallas_call(
        _inproj_kernel,
        grid=(b, t // tm),
        in_specs=[pl.BlockSpec((1, tm, d), tile),
                  pl.BlockSpec((1, 1, d), row),
                  pl.BlockSpec((1, 1, d), row),
                  pl.BlockSpec((1, d), const),
                  pl.BlockSpec((d, QK_COLS_PAD), const),
                  pl.BlockSpec((2 * w, d), const),
                  pl.BlockSpec((4, w), const),
                  pl.BlockSpec((tm, LANES), lambda i, j: (j, 0)),
                  pl.BlockSpec((tm, LANES), lambda i, j: (j, 0)),
                  pl.BlockSpec((tm, LANES), lambda i, j: (j, 0)),
                  pl.BlockSpec((2 * LANES, 2 * LANES), const)],
        out_specs=out_specs,
        out_shape=[wide] * 4 + [jax.ShapeDtypeStruct((b, t, LANES), jnp.float32)] + [wide_t] * 2,
        compiler_params=pltpu.CompilerParams(vmem_limit_bytes=VMEM_LIMIT),
        name="inproj",
    )(x, sc1, sh1, g_mix, w_qk, w_vt, gains, *tables, bd)


def _split3(a):
    hi = _bf16(a)
    r1 = a - hi.astype(jnp.float32)
    mid = _bf16(r1)
    lo = _bf16(r1 - mid.astype(jnp.float32))
    return hi, mid, lo


def _group_base(head):
    pair, h = divmod(head, 2)
    return (0 if h == 1 else HEAD_DIM) + pair * AUX_GROUP


def _group_head(lane_idx):
    return 2 * ((lane_idx % HEAD_DIM) // AUX_GROUP) + jnp.where(lane_idx < HEAD_DIM, 1, 0)


def _routing_kernel(mq_ref, mk_ref, fl_ref, bf_ref, mqa_ref, fqa_ref, fka_ref):
    t = mq_ref.shape[1]
    nb = t // MOBA_BLOCK
    q = mq_ref[0]
    k = mk_ref[0]
    blk_of_col = lax.broadcasted_iota(jnp.int32, (nb, t), 1) // MOBA_BLOCK
    blk_row = lax.broadcasted_iota(jnp.int32, (nb, t), 0)
    ind = _bf16(jnp.where(blk_of_col == blk_row, 1.0, 0.0))
    kmean = _dot(ind, k) * (1.0 / MOBA_BLOCK)
    kmt = jnp.concatenate([kmean] * (LANES // nb), axis=0)
    r_head = _group_head(lax.broadcasted_iota(jnp.int32, kmt.shape, 0))
    c_head = lax.broadcasted_iota(jnp.int32, kmt.shape, 1) // HEAD_DIM
    kmt = jnp.where(r_head == c_head, kmt, 0.0)
    k_hi = _bf16(kmt)
    k_lo = _bf16(kmt - k_hi.astype(jnp.float32))
    gate = (_dot_nt(q, k_hi) + _dot_nt(q, k_lo)) * (1.0 / Q_SCALE)

    lane = lax.broadcasted_iota(jnp.int32, (1, LANES), 1)
    j = lane % nb
    own = lax.broadcasted_iota(jnp.int32, (t, 1), 0) // MOBA_BLOCK
    past = j < own
    g = jnp.where(past, gate, -jnp.inf)
    rank = jnp.zeros(gate.shape, jnp.int32)
    for d in range(1, nb):
        other = pltpu.roll(g, LANES - d, 1)
        first = ((j + d) % nb) < j
        beats = (other > g) | (first & (other == g))
        rank = rank + jnp.where(beats, 1, 0)
    attend = (past & (rank < MOBA_TOPK)) | (j == own)
    mqa_ref[0] = jnp.where(attend, 0.0, NEG_INF)

    z = fl_ref[0] + bf_ref[...]
    logf = -(jnp.maximum(-z, 0.0) + jnp.log1p(jnp.exp(-jnp.abs(z))))
    rr = lax.broadcasted_iota(jnp.int32, (MOBA_BLOCK, MOBA_BLOCK), 0)
    cc = lax.broadcasted_iota(jnp.int32, (MOBA_BLOCK, MOBA_BLOCK), 1)
    tri = _bf16(jnp.where(cc <= rr, 1.0, 0.0))
    slot = lane % AUX_GROUP
    carry = jnp.zeros((1, LANES), jnp.float32)
    for i in range(nb):
        rows = slice(i * MOBA_BLOCK, (i + 1) * MOBA_BLOCK)
        hi, mid, lo = _split3(logf[rows, :])
        cum = (_dot(tri, hi) + _dot(tri, mid)) + _dot(tri, lo) + carry
        carry = cum[MOBA_BLOCK - 1:MOBA_BLOCK, :]
        parts = [p.astype(jnp.float32) for p in _split3(cum * LOG2E)]
        fq = jnp.where(slot < 2 * N_SPLIT, 1.0, 0.0)
        fk = jnp.where(slot < N_SPLIT, 1.0, 0.0)
        for s in range(N_SPLIT):
            fq = jnp.where(slot == s, parts[s], fq)
            fk = jnp.where(slot == N_SPLIT + s, -parts[s], fk)
        fqa_ref[0, rows, :] = fq
        fka_ref[0, rows, :] = fk


def _routing(mq, mk, fl, bf_lanes):
    b, t, w = mq.shape
    full = lambda i: (i, 0, 0)
    aux = jax.ShapeDtypeStruct((b, t, LANES), jnp.float32)
    return pl.pallas_call(
        _routing_kernel,
        grid=(b,),
        in_specs=[pl.BlockSpec((1, t, w), full),
                  pl.BlockSpec((1, t, w), full),
                  pl.BlockSpec((1, t, LANES), full),
                  pl.BlockSpec((1, LANES), lambda i: (0, 0))],
        out_specs=[pl.BlockSpec((1, t, LANES), full)] * 3,
        out_shape=[aux] * 3,
        compiler_params=pltpu.CompilerParams(vmem_limit_bytes=VMEM_LIMIT),
        name="routing",
    )(mq, mk, fl, bf_lanes)


def _attn_kernel(*refs, moba):
    if moba:
        q_ref, qa_ref, k_ref, vt_ref, o_ref, kx_ref, vx_ref = refs
    else:
        q_ref, qa_ref, k_ref, vt_ref, ka_ref, o_ref, kx_ref, vx_ref = refs
    tq = SEQ_TILE
    t = k_ref.shape[1]
    nb = t // tq
    step = pl.program_id(2)
    lane = lax.broadcasted_iota(jnp.int32, (1, LANES), 1)
    low = lane < HEAD_DIM
    shift = (LANES - pl.program_id(1) * AUX_GROUP) % LANES
    in_group = (lane % HEAD_DIM) < AUX_GROUP // 2

    @pl.when(step == 0)
    def _():
        k2 = k_ref[0]
        if moba:
            blk = lax.broadcasted_iota(jnp.int32, (t, LANES), 0) // MOBA_BLOCK
            ln = lax.broadcasted_iota(jnp.int32, (t, LANES), 1) % HEAD_DIM
            ka = _bf16(jnp.where(ln == blk, 1.0, 0.0))
        else:
            ka = _bf16(pltpu.roll(ka_ref[0], shift, 1))
        kx_ref[0] = jnp.where(low, k2, ka)
        kx_ref[1] = jnp.where(low, ka, k2)
        for h in range(2):
            vx_ref[h, 0:HEAD_DIM, :] = vt_ref[0, h * HEAD_DIM:(h + 1) * HEAD_DIM, :]
            vx_ref[h, HEAD_DIM:V_ROWS, :] = jnp.ones((V_ROWS - HEAD_DIM, t), jnp.bfloat16)

    key = lax.broadcasted_iota(jnp.int32, (tq, tq), 0)
    qry = lax.broadcasted_iota(jnp.int32, (tq, tq), 1)
    causal = key <= qry

    def scores(n, h):
        rows = slice(n * tq, (n + 1) * tq)
        q2 = q_ref[0, rows, :]
        qa = pltpu.roll(qa_ref[0, rows, :], shift, 1)
        qa = _bf16(jnp.where(in_group, qa, 0.0))
        qx = jnp.where(low, q2, qa) if h == 0 else jnp.where(low, qa, q2)
        sd = jnp.where(causal, _dot_nt(kx_ref[h, rows, :], qx), NEG_INF)
        m = jnp.max(sd, axis=0, keepdims=True)
        sp = None
        if n > 0:
            sp = _dot_nt(kx_ref[h, 0:n * tq, :], qx)
            m = jnp.maximum(m, jnp.max(sp, axis=0, keepdims=True))
        return n, h, sd, sp, m

    def weighted_values(n, h, sd, sp, m):
        rows = slice(n * tq, (n + 1) * tq)
        acc = _dot(vx_ref[h, :, rows], _bf16(jnp.exp2(sd - m)))
        if n > 0:
            acc = acc + _dot(vx_ref[h, :, 0:n * tq], _bf16(jnp.exp2(sp - m)))
        return acc[0:HEAD_DIM, :] / acc[HEAD_DIM:HEAD_DIM + 1, :]

    def run(tiles):
        chains = [(n, h) for n in tiles for h in range(2)]
        outs = {}
        queue = [scores(*c) for c in chains[:ATTN_LOOKAHEAD]]
        for nxt in chains[ATTN_LOOKAHEAD:] + [None] * ATTN_LOOKAHEAD:
            if nxt is not None:
                queue.append(scores(*nxt))
            done = queue.pop(0)
            outs[done[:2]] = weighted_values(*done)
        for n in tiles:
            o_t = jnp.concatenate([outs[(n, 0)], outs[(n, 1)]], axis=0)
            o_ref[0, n * tq:(n + 1) * tq, :] = _bf16(o_t.T)

    pairs = [(nb - 1 - i, i) for i in range(nb // 2)]
    n_static = _attn_steps(nb)
    for s in range(n_static):
        @pl.when(step == s)
        def _(s=s):
            run([n for pair in pairs[s::n_static] for n in pair])


def _attn_steps(nb):
    return max(1, nb // ATTN_TILES_PER_STEP)


def _attention(moba, q, qa, k, vt, ka=None):
    b, t, w = q.shape
    n_pairs = w // LANES
    nb = t // SEQ_TILE
    assert nb % 2 == 0 and (nb // 2) % _attn_steps(nb) == 0
    spec = pl.BlockSpec((1, t, LANES), lambda i, p, j: (i, 0, p))
    spec_t = pl.BlockSpec((1, LANES, t), lambda i, p, j: (i, p, 0))
    spec_aux = pl.BlockSpec((1, t, LANES), lambda i, p, j: (i, 0, 0))
    args = [q, qa, k, vt] if moba else [q, qa, k, vt, ka]
    specs = [spec, spec_aux, spec, spec_t] + ([] if moba else [spec_aux])
    kern = lambda *refs: _attn_kernel(*refs, moba=moba)
    return pl.pallas_call(
        kern,
        grid=(b, n_pairs, _attn_steps(nb)),
        in_specs=specs,
        out_specs=spec,
        out_shape=jax.ShapeDtypeStruct((b, t, w), jnp.bfloat16),
        scratch_shapes=[pltpu.VMEM((2, t, LANES), jnp.bfloat16),
                        pltpu.VMEM((2, V_ROWS, t), jnp.bfloat16)],
        compiler_params=pltpu.CompilerParams(vmem_limit_bytes=VMEM_LIMIT),
        name="moba" if moba else "fox",
    )(*args)


def _ffn_kernel(x_ref, om_ref, of_ref, wo_ref, gt1_ref, g_ref, sc_ref, sh_ref, wup_ref, cv_ref,
                wdn_ref, gt2_ref, out_ref, x1_ref, halo_ref, ubuf_ref):
    tm = x_ref.shape[1]

    @pl.when(pl.program_id(1) == 0)
    def _():
        halo_ref[...] = jnp.zeros(halo_ref.shape, jnp.float32)

    o = jnp.concatenate([om_ref[0], of_ref[0]], axis=1)
    x1 = x_ref[0] + gt1_ref[0] * _dot(o, wo_ref[...])
    x1_ref[...] = x1
    hn = _bf16(_rms_mod(x1, g_ref[...], sc_ref[0], sh_ref[0]))

    def up_proj(c):
        gate_cols = slice(c * FF_CHUNK, (c + 1) * FF_CHUNK)
        val_cols = slice(D_FF + c * FF_CHUNK, D_FF + (c + 1) * FF_CHUNK)
        u = jnp.concatenate([_dot(hn, wup_ref[:, gate_cols]), _dot(hn, wup_ref[:, val_cols])],
                            axis=1)
        return u, gate_cols, val_cols

    hmid = []
    ahead = [up_proj(c) for c in range(FFN_LOOKAHEAD)]
    for c in range(N_FF_CHUNKS):
        if c + FFN_LOOKAHEAD < N_FF_CHUNKS:
            ahead.append(up_proj(c + FFN_LOOKAHEAD))
        u, gate_cols, val_cols = ahead.pop(0)
        ubuf_ref[c, 0:HALO, :] = halo_ref[c]
        ubuf_ref[c, HALO:HALO + tm, :] = u
        halo_ref[c] = u[tm - HALO:tm, :]
        cv = jnp.concatenate([cv_ref[:, gate_cols], cv_ref[:, val_cols]], axis=1)
        u1 = ubuf_ref[c, HALO - 1:HALO - 1 + tm, :]
        u2 = ubuf_ref[c, HALO - 2:HALO - 2 + tm, :]
        uc = cv[0:1, :] * u2 + cv[1:2, :] * u1 + cv[2:3, :] * u + cv[3:4, :]
        a = uc[:, :FF_CHUNK]
        val = uc[:, FF_CHUNK:]
        hmid.append(_bf16((a * _sigmoid(a)) * val))
    ffn = _dot(jnp.concatenate(hmid, axis=1), wdn_ref[...])
    out_ref[0] = x1_ref[...] + gt2_ref[0] * ffn


def _ffn(x, om, of, wo, gt1, g_ffn, sc2, sh2, wup, cv, wdn, gt2):
    b, t, d = x.shape
    tm = SEQ_TILE
    row = lambda i, j: (i, 0, 0)
    tile = lambda i, j: (i, j, 0)
    const2 = lambda i, j: (0, 0)
    once = dict(pipeline_mode=pl.Buffered(1))
    return pl.pallas_call(
        _ffn_kernel,
        grid=(b, t // tm),
        in_specs=[pl.BlockSpec((1, tm, d), tile),
                  pl.BlockSpec((1, tm, MOBA_WIDTH), tile),
                  pl.BlockSpec((1, tm, FOX_WIDTH), tile),
                  pl.BlockSpec((d, d), const2, **once),
                  pl.BlockSpec((1, 1, d), row),
                  pl.BlockSpec((1, d), const2),
                  pl.BlockSpec((1, 1, d), row),
                  pl.BlockSpec((1, 1, d), row),
                  pl.BlockSpec((d, 2 * D_FF), const2, **once),
                  pl.BlockSpec((HALO, 2 * D_FF), const2, **once),
                  pl.BlockSpec((D_FF, d), const2, **once),
                  pl.BlockSpec((1, 1, d), row)],
        out_specs=pl.BlockSpec((1, tm, d), tile),
        out_shape=jax.ShapeDtypeStruct((b, t, d), jnp.float32),
        scratch_shapes=[pltpu.VMEM((tm, d), jnp.float32),
                        pltpu.VMEM((N_FF_CHUNKS, HALO, 2 * FF_CHUNK), jnp.float32),
                        pltpu.VMEM((N_FF_CHUNKS, HALO + tm, 2 * FF_CHUNK), jnp.float32)],
        compiler_params=pltpu.CompilerParams(vmem_limit_bytes=VMEM_LIMIT),
        name="ffn",
    )(x, om, of, wo, gt1, g_ffn, sc2, sh2, wup, cv, wdn, gt2)


def _layer(x, c, w_ada, b_ada, g_mix, w_in, b_forget, moba_q_gain, moba_k_gain, fox_q_gain,
           fox_k_gain, w_out, g_ffn, w_up, conv_w, conv_b, w_down):
    b, t, d = x.shape
    nb = t // MOBA_BLOCK
    assert d == D_MODEL and t % SEQ_TILE == 0
    assert AUX_GROUP % nb == 0 and 2 * nb <= AUX_GROUP and 2 * N_SPLIT <= AUX_GROUP // 2

    mod = _adaln(c, w_ada, b_ada).reshape(b, 6, 1, d)
    sh1, sc1, gt1, sh2, sc2, gt2 = [mod[:, i] for i in range(6)]

    w = MOBA_WIDTH
    cols = lambda i: w_in[:, i * w:(i + 1) * w]
    lane_head = np.full(LANES, -1)
    for head in range(N_FOX_HEADS):
        lane_head[_group_base(head):_group_base(head) + 2 * N_SPLIT] = head
    used = jnp.asarray(lane_head >= 0)
    logits = jnp.where(used[None, :], w_in[:, 6 * w + np.maximum(lane_head, 0)], 0.0)
    w_qk = _bf16(jnp.concatenate([cols(0), cols(1), cols(3), cols(4), logits], axis=1))
    w_vt = _bf16(jnp.concatenate([cols(2), cols(5)], axis=1).T)
    gains = jnp.stack([jnp.tile(g, N_MOBA_HEADS) for g in
                       (moba_q_gain, moba_k_gain, fox_q_gain, fox_k_gain)])
    r = np.arange(2 * LANES) // HEAD_DIM
    bd = jnp.asarray(r[:, None] == r[None, :], jnp.bfloat16)
    mq, mk, fq, fk, fl, mvt, fvt = _inproj(x, sc1, sh1, g_mix.reshape(1, d), w_qk, w_vt, gains,
                                           _rope_tables(t), bd)

    bf_lanes = jnp.where(used, b_forget[np.maximum(lane_head, 0)], 0.0).reshape(1, LANES)
    mqa, fqa, fka = _routing(mq, mk, fl, bf_lanes)
    o_moba = _attention(True, mq, mqa, mk, mvt)
    o_fox = _attention(False, fq, fqa, fk, fvt, fka)

    cv = jnp.concatenate([conv_w, conv_b[None, :],
                          jnp.zeros((HALO - CONV_WIDTH - 1, 2 * D_FF), jnp.float32)], axis=0)
    return _ffn(x, o_moba, o_fox, _bf16(w_out), gt1, g_ffn.reshape(1, d), sc2, sh2, _bf16(w_up), cv,
                _bf16(w_down), gt2)


def kernel(x, c, w_ada, b_ada, g_mix, w_in, b_forget, moba_q_gain, moba_k_gain, fox_q_gain,
           fox_k_gain, w_out, g_ffn, w_up, conv_w, conv_b, w_down):
    for l in range(w_ada.shape[0]):
        x = _layer(x, c, w_ada[l], b_ada[l], g_mix[l], w_in[l], b_forget[l], moba_q_gain[l],
                   moba_k_gain[l], fox_q_gain[l], fox_k_gain[l], w_out[l], g_ffn[l], w_up[l],
                   conv_w[l], conv_b[l], w_down[l])
    return x
```

```python
import math

import jax
import jax.numpy as jnp
import numpy as np
from jax import lax
from jax.experimental import pallas as pl
from jax.experimental.pallas import tpu as pltpu

D_MODEL = 1024
HEAD_DIM = 64
N_MOBA_HEADS = 8
N_FOX_HEADS = 8
MOBA_WIDTH = N_MOBA_HEADS * HEAD_DIM
FOX_WIDTH = N_FOX_HEADS * HEAD_DIM
MOBA_BLOCK = 256
MOBA_TOPK = 3
ROPE_THETA = 500000.0
ROPE_DIM = HEAD_DIM // 4
D_FF = 2816
CONV_WIDTH = 3
NORM_EPS = 1e-6
NEG_INF = -1e30
LOG2E = math.log2(math.e)
Q_SCALE = HEAD_DIM ** -0.5 * LOG2E
LANES = 128
BF16_ROWS = 16
QK_COLS = 2 * MOBA_WIDTH + 2 * FOX_WIDTH
QK_COLS_PAD = QK_COLS + LANES
V_ROWS = HEAD_DIM + BF16_ROWS
SEQ_TILE = 256
ROW_TILE = 512
ATTN_TILES_PER_STEP = 8
ATTN_LOOKAHEAD = 5
FFN_LOOKAHEAD = 2
FF_CHUNK = 256
N_FF_CHUNKS = D_FF // FF_CHUNK
HALO = 8
N_SPLIT = 3
AUX_GROUP = 16
VMEM_LIMIT = 56 * 1024 * 1024

_NT = (((1,), (1,)), ((), ()))


def _bf16(a):
    return a.astype(jnp.bfloat16)


def _dot(a, b):
    return jnp.dot(a, b, preferred_element_type=jnp.float32)


def _dot_nt(a, b):
    return lax.dot_general(a, b, _NT, preferred_element_type=jnp.float32)


def _sigmoid(a):
    return 1.0 / (1.0 + jnp.exp(-a))


def _adaln_kernel(c_ref, w_ref, b_ref, o_ref):
    c = c_ref[...]
    s = c * _sigmoid(c)
    o_ref[...] = _dot(_bf16(s), _bf16(w_ref[...])) + b_ref[...]


def _adaln(c, w_ada, b_ada):
    b, d = c.shape
    n = w_ada.shape[1]
    tn = 1536
    return pl.pallas_call(
        _adaln_kernel,
        grid=(n // tn,),
        in_specs=[pl.BlockSpec((b, d), lambda j: (0, 0)),
                  pl.BlockSpec((d, tn), lambda j: (0, j)),
                  pl.BlockSpec((1, tn), lambda j: (0, j))],
        out_specs=pl.BlockSpec((b, tn), lambda j: (0, j)),
        out_shape=jax.ShapeDtypeStruct((b, n), jnp.float32),
        compiler_params=pltpu.CompilerParams(vmem_limit_bytes=VMEM_LIMIT),
        name="adaln",
    )(c, w_ada, b_ada.reshape(1, n))


def _rms_mod(x, g, sc, sh):
    ms = jnp.mean(x * x, axis=-1, keepdims=True)
    y = x * lax.rsqrt(ms + NORM_EPS)
    return (y * g) * (1.0 + sc) + sh


def _head_norm(p, gain, bd):
    sq = _bf16(p * p)
    half = 2 * LANES
    ss = jnp.concatenate([_dot(sq[:, :half], bd), _dot(sq[:, half:], bd)], axis=1)
    return (p * lax.rsqrt(ss * (1.0 / HEAD_DIM) + NORM_EPS)) * gain


def _rope(y, c, sa, sb):
    half = ROPE_DIM // 2
    outs = []
    for i in range(y.shape[1] // LANES):
        yc = y[:, i * LANES:(i + 1) * LANES]
        up = pltpu.roll(yc, LANES - half, 1)
        dn = pltpu.roll(yc, half, 1)
        outs.append(yc * c + up * sa + dn * sb)
    return jnp.concatenate(outs, axis=1)


def _inproj_kernel(x_ref, sc_ref, sh_ref, g_ref, w_ref, wvt_ref, gains_ref, rc_ref, rsa_ref, rsb_ref,
                   bd_ref, mq_ref, mk_ref, fq_ref, fk_ref, fl_ref, mvt_ref, fvt_ref):
    hn = _bf16(_rms_mod(x_ref[0], g_ref[...], sc_ref[0], sh_ref[0]))
    proj = _dot(hn, w_ref[...])
    vt = _dot_nt(wvt_ref[...], hn)
    bd = bd_ref[...]
    c, sa, sb = rc_ref[...], rsa_ref[...], rsb_ref[...]
    w = MOBA_WIDTH
    mq = _rope(_head_norm(proj[:, 0:w], gains_ref[0:1, :], bd), c, sa, sb)
    mq_ref[0] = _bf16(mq * Q_SCALE)
    mk = _rope(_head_norm(proj[:, w:2 * w], gains_ref[1:2, :], bd), c, sa, sb)
    mk_ref[0] = _bf16(mk)
    fq = _head_norm(proj[:, 2 * w:3 * w], gains_ref[2:3, :], bd)
    fq_ref[0] = _bf16(fq * Q_SCALE)
    fk = _head_norm(proj[:, 3 * w:4 * w], gains_ref[3:4, :], bd)
    fk_ref[0] = _bf16(fk)
    fl_ref[0] = proj[:, QK_COLS:QK_COLS_PAD]
    mvt_ref[0] = _bf16(vt[0:w, :])
    fvt_ref[0] = _bf16(vt[w:2 * w, :])


def _rope_tables(t):
    half = ROPE_DIM // 2
    inv_freq = jnp.power(ROPE_THETA, -2.0 * jnp.arange(half, dtype=jnp.float32) / ROPE_DIM)
    ang = jnp.arange(t, dtype=jnp.float32)[:, None] * inv_freq[None, :]
    cos, sin = jnp.cos(ang), jnp.sin(ang)
    d = np.arange(LANES) % HEAD_DIM
    first = jnp.asarray(d < half)[None, :]
    second = jnp.asarray((d >= half) & (d < ROPE_DIM))[None, :]
    idx = np.where(d < ROPE_DIM, d % half, 0)
    cos_l, sin_l = cos[:, idx], sin[:, idx]
    c = jnp.where(first | second, cos_l, 1.0)
    sa = jnp.where(first, -sin_l, 0.0)
    sb = jnp.where(second, sin_l, 0.0)
    return c, sa, sb


def _inproj(x, sc1, sh1, g_mix, w_qk, w_vt, gains, tables, bd):
    b, t, d = x.shape
    tm = min(ROW_TILE, t)
    w = MOBA_WIDTH
    row = lambda i, j: (i, 0, 0)
    tile = lambda i, j: (i, j, 0)
    tile_t = lambda i, j: (i, 0, j)
    const = lambda i, j: (0, 0)
    wide = jax.ShapeDtypeStruct((b, t, w), jnp.bfloat16)
    wide_t = jax.ShapeDtypeStruct((b, w, t), jnp.bfloat16)
    out_specs = ([pl.BlockSpec((1, tm, w), tile)] * 4 + [pl.BlockSpec((1, tm, LANES), tile)]
                 + [pl.BlockSpec((1, w, tm), tile_t)] * 2)
    return pl.pallas_call(
        _inproj_kernel,
        grid=(b, t // tm),
        in_specs=[pl.BlockSpec((1, tm, d), tile),
                  pl.BlockSpec((1, 1, d), row),
                  pl.BlockSpec((1, 1, d), row),
                  pl.BlockSpec((1, d), const),
                  pl.BlockSpec((d, QK_COLS_PAD), const),
                  pl.BlockSpec((2 * w, d), const),
                  pl.BlockSpec((4, w), const),
                  pl.BlockSpec((tm, LANES), lambda i, j: (j, 0)),
                  pl.BlockSpec((tm, LANES), lambda i, j: (j, 0)),
                  pl.BlockSpec((tm, LANES), lambda i, j: (j, 0)),
                  pl.BlockSpec((2 * LANES, 2 * LANES), const)],
        out_specs=out_specs,
        out_shape=[wide] * 4 + [jax.ShapeDtypeStruct((b, t, LANES), jnp.float32)] + [wide_t] * 2,
        compiler_params=pltpu.CompilerParams(vmem_limit_bytes=VMEM_LIMIT),
        name="inproj",
    )(x, sc1, sh1, g_mix, w_qk, w_vt, gains, *tables, bd)


def _split3(a):
    hi = _bf16(a)
    r1 = a - hi.astype(jnp.float32)
    mid = _bf16(r1)
    lo = _bf16(r1 - mid.astype(jnp.float32))
    return hi, mid, lo


def _group_base(head):
    pair, h = divmod(head, 2)
    return (0 if h == 1 else HEAD_DIM) + pair * AUX_GROUP


def _group_head(lane_idx):
    return 2 * ((lane_idx % HEAD_DIM) // AUX_GROUP) + jnp.where(lane_idx < HEAD_DIM, 1, 0)


def _routing_kernel(mq_ref, mk_ref, fl_ref, bf_ref, mqa_ref, fqa_ref, fka_ref):
    t = mq_ref.shape[1]
    nb = t // MOBA_BLOCK
    q = mq_ref[0]
    k = mk_ref[0]
    blk_of_col = lax.broadcasted_iota(jnp.int32, (nb, t), 1) // MOBA_BLOCK
    blk_row = lax.broadcasted_iota(jnp.int32, (nb, t), 0)
    ind = _bf16(jnp.where(blk_of_col == blk_row, 1.0, 0.0))
    kmean = _dot(ind, k) * (1.0 / MOBA_BLOCK)
    kmt = jnp.concatenate([kmean] * (LANES // nb), axis=0)
    r_head = _group_head(lax.broadcasted_iota(jnp.int32, kmt.shape, 0))
    c_head = lax.broadcasted_iota(jnp.int32, kmt.shape, 1) // HEAD_DIM
    kmt = jnp.where(r_head == c_head, kmt, 0.0)
    k_hi = _bf16(kmt)
    k_lo = _bf16(kmt - k_hi.astype(jnp.float32))
    gate = (_dot_nt(q, k_hi) + _dot_nt(q, k_lo)) * (1.0 / Q_SCALE)

    lane = lax.broadcasted_iota(jnp.int32, (1, LANES), 1)
    j = lane % nb
    own = lax.broadcasted_iota(jnp.int32, (t, 1), 0) // MOBA_BLOCK
    past = j < own
    g = jnp.where(past, gate, -jnp.inf)
    rank = jnp.zeros(gate.shape, jnp.int32)
    for d in range(1, nb):
        other = pltpu.roll(g, LANES - d, 1)
        first = ((j + d) % nb) < j
        beats = (other > g) | (first & (other == g))
        rank = rank + jnp.where(beats, 1, 0)
    attend = (past & (rank < MOBA_TOPK)) | (j == own)
    mqa_ref[0] = jnp.where(attend, 0.0, NEG_INF)

    z = fl_ref[0] + bf_ref[...]
    logf = -(jnp.maximum(-z, 0.0) + jnp.log1p(jnp.exp(-jnp.abs(z))))
    rr = lax.broadcasted_iota(jnp.int32, (MOBA_BLOCK, MOBA_BLOCK), 0)
    cc = lax.broadcasted_iota(jnp.int32, (MOBA_BLOCK, MOBA_BLOCK), 1)
    tri = _bf16(jnp.where(cc <= rr, 1.0, 0.0))
    slot = lane % AUX_GROUP
    carry = jnp.zeros((1, LANES), jnp.float32)
    for i in range(nb):
        rows = slice(i * MOBA_BLOCK, (i + 1) * MOBA_BLOCK)
        hi, mid, lo = _split3(logf[rows, :])
        cum = (_dot(tri, hi) + _dot(tri, mid)) + _dot(tri, lo) + carry
        carry = cum[MOBA_BLOCK - 1:MOBA_BLOCK, :]
        parts = [p.astype(jnp.float32) for p in _split3(cum * LOG2E)]
        fq = jnp.where(slot < 2 * N_SPLIT, 1.0, 0.0)
        fk = jnp.where(slot < N_SPLIT, 1.0, 0.0)
        for s in range(N_SPLIT):
            fq = jnp.where(slot == s, parts[s], fq)
            fk = jnp.where(slot == N_SPLIT + s, -parts[s], fk)
        fqa_ref[0, rows, :] = fq
        fka_ref[0, rows, :] = fk


def _routing(mq, mk, fl, bf_lanes):
    b, t, w = mq.shape
    full = lambda i: (i, 0, 0)
    aux = jax.ShapeDtypeStruct((b, t, LANES), jnp.float32)
    return pl.pallas_call(
        _routing_kernel,
        grid=(b,),
        in_specs=[pl.BlockSpec((1, t, w), full),
                  pl.BlockSpec((1, t, w), full),
                  pl.BlockSpec((1, t, LANES), full),
                  pl.BlockSpec((1, LANES), lambda i: (0, 0))],
        out_specs=[pl.BlockSpec((1, t, LANES), full)] * 3,
        out_shape=[aux] * 3,
        compiler_params=pltpu.CompilerParams(vmem_limit_bytes=VMEM_LIMIT),
        name="routing",
    )(mq, mk, fl, bf_lanes)


def _attn_kernel(*refs, moba):
    if moba:
        q_ref, qa_ref, k_ref, vt_ref, o_ref, kx_ref, vx_ref = refs
    else:
        q_ref, qa_ref, k_ref, vt_ref, ka_ref, o_ref, kx_ref, vx_ref = refs
    tq = SEQ_TILE
    t = k_ref.shape[1]
    nb = t // tq
    step = pl.program_id(2)
    lane = lax.broadcasted_iota(jnp.int32, (1, LANES), 1)
    low = lane < HEAD_DIM
    shift = (LANES - pl.program_id(1) * AUX_GROUP) % LANES
    in_group = (lane % HEAD_DIM) < AUX_GROUP // 2

    @pl.when(step == 0)
    def _():
        k2 = k_ref[0]
        if moba:
            blk = lax.broadcasted_iota(jnp.int32, (t, LANES), 0) // MOBA_BLOCK
            ln = lax.broadcasted_iota(jnp.int32, (t, LANES), 1) % HEAD_DIM
            ka = _bf16(jnp.where(ln == blk, 1.0, 0.0))
        else:
            ka = _bf16(pltpu.roll(ka_ref[0], shift, 1))
        kx_ref[0] = jnp.where(low, k2, ka)
        kx_ref[1] = jnp.where(low, ka, k2)
        for h in range(2):
            vx_ref[h, 0:HEAD_DIM, :] = vt_ref[0, h * HEAD_DIM:(h + 1) * HEAD_DIM, :]
            vx_ref[h, HEAD_DIM:V_ROWS, :] = jnp.ones((V_ROWS - HEAD_DIM, t), jnp.bfloat16)

    key = lax.broadcasted_iota(jnp.int32, (tq, tq), 0)
    qry = lax.broadcasted_iota(jnp.int32, (tq, tq), 1)
    causal = key <= qry

    def scores(n, h):
        rows = slice(n * tq, (n + 1) * tq)
        q2 = q_ref[0, rows, :]
        qa = pltpu.roll(qa_ref[0, rows, :], shift, 1)
        qa = _bf16(jnp.where(in_group, qa, 0.0))
        qx = jnp.where(low, q2, qa) if h == 0 else jnp.where(low, qa, q2)
        sd = jnp.where(causal, _dot_nt(kx_ref[h, rows, :], qx), NEG_INF)
        m = jnp.max(sd, axis=0, keepdims=True)
        sp = None
        if n > 0:
            sp = _dot_nt(kx_ref[h, 0:n * tq, :], qx)
            m = jnp.maximum(m, jnp.max(sp, axis=0, keepdims=True))
        return n, h, sd, sp, m

    def weighted_values(n, h, sd, sp, m):
        rows = slice(n * tq, (n + 1) * tq)
        acc = _dot(vx_ref[h, :, rows], _bf16(jnp.exp2(sd - m)))
        if n > 0:
            acc = acc + _dot(vx_ref[h, :, 0:n * tq], _bf16(jnp.exp2(sp - m)))
        return acc[0:HEAD_DIM, :] / acc[HEAD_DIM:HEAD_DIM + 1, :]

    def run(tiles):
        chains = [(n, h) for n in tiles for h in range(2)]
        outs = {}
        queue = [scores(*c) for c in chains[:ATTN_LOOKAHEAD]]
        for nxt in chains[ATTN_LOOKAHEAD:] + [None] * ATTN_LOOKAHEAD:
            if nxt is not None:
                queue.append(scores(*nxt))
            done = queue.pop(0)
            outs[done[:2]] = weighted_values(*done)
        for n in tiles:
            o_t = jnp.concatenate([outs[(n, 0)], outs[(n, 1)]], axis=0)
            o_ref[0, n * tq:(n + 1) * tq, :] = _bf16(o_t.T)

    pairs = [(nb - 1 - i, i) for i in range(nb // 2)]
    n_static = _attn_steps(nb)
    for s in range(n_static):
        @pl.when(step == s)
        def _(s=s):
            run([n for pair in pairs[s::n_static] for n in pair])


def _attn_steps(nb):
    return max(1, nb // ATTN_TILES_PER_STEP)


def _attention(moba, q, qa, k, vt, ka=None):
    b, t, w = q.shape
    n_pairs = w // LANES
    nb = t // SEQ_TILE
    assert nb % 2 == 0 and (nb // 2) % _attn_steps(nb) == 0
    spec = pl.BlockSpec((1, t, LANES), lambda i, p, j: (i, 0, p))
    spec_t = pl.BlockSpec((1, LANES, t), lambda i, p, j: (i, p, 0))
    spec_aux = pl.BlockSpec((1, t, LANES), lambda i, p, j: (i, 0, 0))
    args = [q, qa, k, vt] if moba else [q, qa, k, vt, ka]
    specs = [spec, spec_aux, spec, spec_t] + ([] if moba else [spec_aux])
    kern = lambda *refs: _attn_kernel(*refs, moba=moba)
    return pl.pallas_call(
        kern,
        grid=(b, n_pairs, _attn_steps(nb)),
        in_specs=specs,
        out_specs=spec,
        out_shape=jax.ShapeDtypeStruct((b, t, w), jnp.bfloat16),
        scratch_shapes=[pltpu.VMEM((2, t, LANES), jnp.bfloat16),
                        pltpu.VMEM((2, V_ROWS, t), jnp.bfloat16)],
        compiler_params=pltpu.CompilerParams(vmem_limit_bytes=VMEM_LIMIT),
        name="moba" if moba else "fox",
    )(*args)


def _ffn_kernel(x_ref, om_ref, of_ref, wo_ref, gt1_ref, g_ref, sc_ref, sh_ref, wup_ref, cv_ref,
                wdn_ref, gt2_ref, out_ref, x1_ref, halo_ref):
    tm = x_ref.shape[1]

    @pl.when(pl.program_id(1) == 0)
    def _():
        halo_ref[...] = jnp.zeros(halo_ref.shape, jnp.float32)

    o = jnp.concatenate([om_ref[0], of_ref[0]], axis=1)
    x1 = x_ref[0] + gt1_ref[0] * _dot(o, wo_ref[...])
    x1_ref[...] = x1
    hn = _bf16(_rms_mod(x1, g_ref[...], sc_ref[0], sh_ref[0]))

    def up_proj(c):
        gate_cols = slice(c * FF_CHUNK, (c + 1) * FF_CHUNK)
        val_cols = slice(D_FF + c * FF_CHUNK, D_FF + (c + 1) * FF_CHUNK)
        u = jnp.concatenate([_dot(hn, wup_ref[:, gate_cols]), _dot(hn, wup_ref[:, val_cols])],
                            axis=1)
        return u, gate_cols, val_cols

    hmid = []
    ahead = [up_proj(c) for c in range(FFN_LOOKAHEAD)]
    for c in range(N_FF_CHUNKS):
        if c + FFN_LOOKAHEAD < N_FF_CHUNKS:
            ahead.append(up_proj(c + FFN_LOOKAHEAD))
        u, gate_cols, val_cols = ahead.pop(0)
        ext = jnp.concatenate([halo_ref[c], u], axis=0)
        halo_ref[c] = u[tm - HALO:tm, :]
        cv = jnp.concatenate([cv_ref[:, gate_cols], cv_ref[:, val_cols]], axis=1)
        u1 = pltpu.roll(ext, 1, 0)[HALO:, :]
        u2 = pltpu.roll(ext, 2, 0)[HALO:, :]
        uc = cv[0:1, :] * u2 + cv[1:2, :] * u1 + cv[2:3, :] * u + cv[3:4, :]
        a = uc[:, :FF_CHUNK]
        val = uc[:, FF_CHUNK:]
        half = 0.5 * a
        hmid.append(_bf16((half + half * jnp.tanh(half)) * val))
    ffn = _dot(jnp.concatenate(hmid, axis=1), wdn_ref[...])
    out_ref[0] = x1_ref[...] + gt2_ref[0] * ffn


def _ffn(x, om, of, wo, gt1, g_ffn, sc2, sh2, wup, cv, wdn, gt2):
    b, t, d = x.shape
    tm = min(ROW_TILE, t)
    row = lambda i, j: (i, 0, 0)
    tile = lambda i, j: (i, j, 0)
    const2 = lambda i, j: (0, 0)
    once = dict(pipeline_mode=pl.Buffered(1))
    return pl.pallas_call(
        _ffn_kernel,
        grid=(b, t // tm),
        in_specs=[pl.BlockSpec((1, tm, d), tile),
                  pl.BlockSpec((1, tm, MOBA_WIDTH), tile),
                  pl.BlockSpec((1, tm, FOX_WIDTH), tile),
                  pl.BlockSpec((d, d), const2, **once),
                  pl.BlockSpec((1, 1, d), row),
                  pl.BlockSpec((1, d), const2),
                  pl.BlockSpec((1, 1, d), row),
                  pl.BlockSpec((1, 1, d), row),
                  pl.BlockSpec((d, 2 * D_FF), const2, **once),
                  pl.BlockSpec((HALO, 2 * D_FF), const2, **once),
                  pl.BlockSpec((D_FF, d), const2, **once),
                  pl.BlockSpec((1, 1, d), row)],
        out_specs=pl.BlockSpec((1, tm, d), tile),
        out_shape=jax.ShapeDtypeStruct((b, t, d), jnp.float32),
        scratch_shapes=[pltpu.VMEM((tm, d), jnp.float32),
                        pltpu.VMEM((N_FF_CHUNKS, HALO, 2 * FF_CHUNK), jnp.float32)],
        compiler_params=pltpu.CompilerParams(vmem_limit_bytes=VMEM_LIMIT),
        name="ffn",
    )(x, om, of, wo, gt1, g_ffn, sc2, sh2, wup, cv, wdn, gt2)


def _layer(x, c, w_ada, b_ada, g_mix, w_in, b_forget, moba_q_gain, moba_k_gain, fox_q_gain,
           fox_k_gain, w_out, g_ffn, w_up, conv_w, conv_b, w_down):
    b, t, d = x.shape
    nb = t // MOBA_BLOCK
    assert d == D_MODEL and t % SEQ_TILE == 0
    assert AUX_GROUP % nb == 0 and 2 * nb <= AUX_GROUP and 2 * N_SPLIT <= AUX_GROUP // 2

    mod = _adaln(c, w_ada, b_ada).reshape(b, 6, 1, d)
    sh1, sc1, gt1, sh2, sc2, gt2 = [mod[:, i] for i in range(6)]

    w = MOBA_WIDTH
    cols = lambda i: w_in[:, i * w:(i + 1) * w]
    lane_head = np.full(LANES, -1)
    for head in range(N_FOX_HEADS):
        lane_head[_group_base(head):_group_base(head) + 2 * N_SPLIT] = head
    used = jnp.asarray(lane_head >= 0)
    logits = jnp.where(used[None, :], w_in[:, 6 * w + np.maximum(lane_head, 0)], 0.0)
    w_qk = _bf16(jnp.concatenate([cols(0), cols(1), cols(3), cols(4), logits], axis=1))
    w_vt = _bf16(jnp.concatenate([cols(2), cols(5)], axis=1).T)
    gains = jnp.stack([jnp.tile(g, N_MOBA_HEADS) for g in
                       (moba_q_gain, moba_k_gain, fox_q_gain, fox_k_gain)])
    r = np.arange(2 * LANES) // HEAD_DIM
    bd = jnp.asarray(r[:, None] == r[None, :], jnp.bfloat16)
    mq, mk, fq, fk, fl, mvt, fvt = _inproj(x, sc1, sh1, g_mix.reshape(1, d), w_qk, w_vt, gains,
                                           _rope_tables(t), bd)

    bf_lanes = jnp.where(used, b_forget[np.maximum(lane_head, 0)], 0.0).reshape(1, LANES)
    mqa, fqa, fka = _routing(mq, mk, fl, bf_lanes)
    o_moba = _attention(True, mq, mqa, mk, mvt)
    o_fox = _attention(False, fq, fqa, fk, fvt, fka)

    cv = jnp.concatenate([conv_w, conv_b[None, :],
                          jnp.zeros((HALO - CONV_WIDTH - 1, 2 * D_FF), jnp.float32)], axis=0)
    return _ffn(x, o_moba, o_fox, _bf16(w_out), gt1, g_ffn.reshape(1, d), sc2, sh2, _bf16(w_up), cv,
                _bf16(w_down), gt2)


def kernel(x, c, w_ada, b_ada, g_mix, w_in, b_forget, moba_q_gain, moba_k_gain, fox_q_gain,
           fox_k_gain, w_out, g_ffn, w_up, conv_w, conv_b, w_down):
    for l in range(w_ada.shape[0]):
        x = _layer(x, c, w_ada[l], b_ada[l], g_mix[l], w_in[l], b_forget[l], moba_q_gain[l],
                   moba_k_gain[l], fox_q_gain[l], fox_k_gain[l], w_out[l], g_ffn[l], w_up[l],
                   conv_w[l], conv_b[l], w_down[l])
    return x
```

```python
import math

import jax
import jax.numpy as jnp
import numpy as np
from jax import lax
from jax.experimental import pallas as pl
from jax.experimental.pallas import tpu as pltpu

D_MODEL = 1024
HEAD_DIM = 64
N_MOBA_HEADS = 8
N_FOX_HEADS = 8
MOBA_WIDTH = N_MOBA_HEADS * HEAD_DIM
FOX_WIDTH = N_FOX_HEADS * HEAD_DIM
MOBA_BLOCK = 256
MOBA_TOPK = 3
ROPE_THETA = 500000.0
ROPE_DIM = HEAD_DIM // 4
D_FF = 2816
CONV_WIDTH = 3
NORM_EPS = 1e-6
NEG_INF = -1e30
LOG2E = math.log2(math.e)
Q_SCALE = HEAD_DIM ** -0.5 * LOG2E
LANES = 128
BF16_ROWS = 16
QK_COLS = 2 * MOBA_WIDTH + 2 * FOX_WIDTH
QK_COLS_PAD = QK_COLS + LANES
V_ROWS = HEAD_DIM + BF16_ROWS
SEQ_TILE = 256
ROW_TILE = 512
ATTN_TILES_PER_STEP = 8
ATTN_LOOKAHEAD = 5
FFN_LOOKAHEAD = 2
FF_CHUNK = 256
N_FF_CHUNKS = D_FF // FF_CHUNK
HALO = 8
N_SPLIT = 3
AUX_GROUP = 16
VMEM_LIMIT = 56 * 1024 * 1024

_NT = (((1,), (1,)), ((), ()))


def _bf16(a):
    return a.astype(jnp.bfloat16)


def _dot(a, b):
    return jnp.dot(a, b, preferred_element_type=jnp.float32)


def _dot_nt(a, b):
    return lax.dot_general(a, b, _NT, preferred_element_type=jnp.float32)


def _sigmoid(a):
    return 1.0 / (1.0 + jnp.exp(-a))


def _adaln_kernel(c_ref, w_ref, b_ref, o_ref):
    c = c_ref[...]
    s = c * _sigmoid(c)
    o_ref[...] = _dot(_bf16(s), _bf16(w_ref[...])) + b_ref[...]


def _adaln(c, w_ada, b_ada):
    b, d = c.shape
    n = w_ada.shape[1]
    tn = 1536
    return pl.pallas_call(
        _adaln_kernel,
        grid=(n // tn,),
        in_specs=[pl.BlockSpec((b, d), lambda j: (0, 0)),
                  pl.BlockSpec((d, tn), lambda j: (0, j)),
                  pl.BlockSpec((1, tn), lambda j: (0, j))],
        out_specs=pl.BlockSpec((b, tn), lambda j: (0, j)),
        out_shape=jax.ShapeDtypeStruct((b, n), jnp.float32),
        compiler_params=pltpu.CompilerParams(vmem_limit_bytes=VMEM_LIMIT),
        name="adaln",
    )(c, w_ada, b_ada.reshape(1, n))


def _rms_mod(x, g, sc, sh):
    ms = jnp.mean(x * x, axis=-1, keepdims=True)
    y = x * lax.rsqrt(ms + NORM_EPS)
    return (y * g) * (1.0 + sc) + sh


def _head_norm(p, gain, bd):
    sq = _bf16(p * p)
    half = 2 * LANES
    ss = jnp.concatenate([_dot(sq[:, :half], bd), _dot(sq[:, half:], bd)], axis=1)
    return (p * lax.rsqrt(ss * (1.0 / HEAD_DIM) + NORM_EPS)) * gain


def _rope(y, c, sa, sb):
    half = ROPE_DIM // 2
    outs = []
    for i in range(y.shape[1] // LANES):
        yc = y[:, i * LANES:(i + 1) * LANES]
        up = pltpu.roll(yc, LANES - half, 1)
        dn = pltpu.roll(yc, half, 1)
        outs.append(yc * c + up * sa + dn * sb)
    return jnp.concatenate(outs, axis=1)


def _inproj_kernel(x_ref, sc_ref, sh_ref, g_ref, w_ref, wvt_ref, gains_ref, rc_ref, rsa_ref, rsb_ref,
                   bd_ref, mq_ref, mk_ref, fq_ref, fk_ref, fl_ref, mvt_ref, fvt_ref):
    hn = _bf16(_rms_mod(x_ref[0], g_ref[...], sc_ref[0], sh_ref[0]))
    bd = bd_ref[...]
    c, sa, sb = rc_ref[...], rsa_ref[...], rsb_ref[...]
    w = MOBA_WIDTH
    proj = lambda i: _dot(hn, w_ref[:, i * w:(i + 1) * w])

    p_mq = proj(0)
    p_mk = proj(1)
    mq = _rope(_head_norm(p_mq, gains_ref[0:1, :], bd), c, sa, sb)
    mq_ref[0] = _bf16(mq * Q_SCALE)
    p_fq = proj(2)
    mk = _rope(_head_norm(p_mk, gains_ref[1:2, :], bd), c, sa, sb)
    mk_ref[0] = _bf16(mk)
    p_fk = proj(3)
    fq = _head_norm(p_fq, gains_ref[2:3, :], bd)
    fq_ref[0] = _bf16(fq * Q_SCALE)
    fl_ref[0] = _dot(hn, w_ref[:, QK_COLS:QK_COLS_PAD])
    mvt = _dot_nt(wvt_ref[0:w, :], hn)
    fk = _head_norm(p_fk, gains_ref[3:4, :], bd)
    fk_ref[0] = _bf16(fk)
    mvt_ref[0] = _bf16(mvt)
    fvt_ref[0] = _bf16(_dot_nt(wvt_ref[w:2 * w, :], hn))


def _rope_tables(t):
    half = ROPE_DIM // 2
    inv_freq = jnp.power(ROPE_THETA, -2.0 * jnp.arange(half, dtype=jnp.float32) / ROPE_DIM)
    ang = jnp.arange(t, dtype=jnp.float32)[:, None] * inv_freq[None, :]
    cos, sin = jnp.cos(ang), jnp.sin(ang)
    d = np.arange(LANES) % HEAD_DIM
    first = jnp.asarray(d < half)[None, :]
    second = jnp.asarray((d >= half) & (d < ROPE_DIM))[None, :]
    idx = np.where(d < ROPE_DIM, d % half, 0)
    cos_l, sin_l = cos[:, idx], sin[:, idx]
    c = jnp.where(first | second, cos_l, 1.0)
    sa = jnp.where(first, -sin_l, 0.0)
    sb = jnp.where(second, sin_l, 0.0)
    return c, sa, sb


def _inproj(x, sc1, sh1, g_mix, w_qk, w_vt, gains, tables, bd):
    b, t, d = x.shape
    tm = min(ROW_TILE, t)
    w = MOBA_WIDTH
    row = lambda i, j: (i, 0, 0)
    tile = lambda i, j: (i, j, 0)
    tile_t = lambda i, j: (i, 0, j)
    const = lambda i, j: (0, 0)
    wide = jax.ShapeDtypeStruct((b, t, w), jnp.bfloat16)
    wide_t = jax.ShapeDtypeStruct((b, w, t), jnp.bfloat16)
    out_specs = ([pl.BlockSpec((1, tm, w), tile)] * 4 + [pl.BlockSpec((1, tm, LANES), tile)]
                 + [pl.BlockSpec((1, w, tm), tile_t)] * 2)
    return pl.pallas_call(
        _inproj_kernel,
        grid=(b, t // tm),
        in_specs=[pl.BlockSpec((1, tm, d), tile),
                  pl.BlockSpec((1, 1, d), row),
                  pl.BlockSpec((1, 1, d), row),
                  pl.BlockSpec((1, d), const),
                  pl.BlockSpec((d, QK_COLS_PAD), const),
                  pl.BlockSpec((2 * w, d), const),
                  pl.BlockSpec((4, w), const),
                  pl.BlockSpec((tm, LANES), lambda i, j: (j, 0)),
                  pl.BlockSpec((tm, LANES), lambda i, j: (j, 0)),
                  pl.BlockSpec((tm, LANES), lambda i, j: (j, 0)),
                  pl.BlockSpec((2 * LANES, 2 * LANES), const)],
        out_specs=out_specs,
        out_shape=[wide] * 4 + [jax.ShapeDtypeStruct((b, t, LANES), jnp.float32)] + [wide_t] * 2,
        compiler_params=pltpu.CompilerParams(vmem_limit_bytes=VMEM_LIMIT),
        name="inproj",
    )(x, sc1, sh1, g_mix, w_qk, w_vt, gains, *tables, bd)


def _split3(a):
    hi = _bf16(a)
    r1 = a - hi.astype(jnp.float32)
    mid = _bf16(r1)
    lo = _bf16(r1 - mid.astype(jnp.float32))
    return hi, mid, lo


def _group_base(head):
    pair, h = divmod(head, 2)
    return (0 if h == 1 else HEAD_DIM) + pair * AUX_GROUP


def _group_head(lane_idx):
    return 2 * ((lane_idx % HEAD_DIM) // AUX_GROUP) + jnp.where(lane_idx < HEAD_DIM, 1, 0)


def _routing_kernel(mq_ref, mk_ref, fl_ref, bf_ref, mqa_ref, fqa_ref, fka_ref):
    t = mq_ref.shape[1]
    nb = t // MOBA_BLOCK
    q = mq_ref[0]
    k = mk_ref[0]
    blk_of_col = lax.broadcasted_iota(jnp.int32, (nb, t), 1) // MOBA_BLOCK
    blk_row = lax.broadcasted_iota(jnp.int32, (nb, t), 0)
    ind = _bf16(jnp.where(blk_of_col == blk_row, 1.0, 0.0))
    kmean = _dot(ind, k) * (1.0 / MOBA_BLOCK)
    kmt = jnp.concatenate([kmean] * (LANES // nb), axis=0)
    r_head = _group_head(lax.broadcasted_iota(jnp.int32, kmt.shape, 0))
    c_head = lax.broadcasted_iota(jnp.int32, kmt.shape, 1) // HEAD_DIM
    kmt = jnp.where(r_head == c_head, kmt, 0.0)
    k_hi = _bf16(kmt)
    k_lo = _bf16(kmt - k_hi.astype(jnp.float32))
    gate = (_dot_nt(q, k_hi) + _dot_nt(q, k_lo)) * (1.0 / Q_SCALE)

    lane = lax.broadcasted_iota(jnp.int32, (1, LANES), 1)
    j = lane % nb
    own = lax.broadcasted_iota(jnp.int32, (t, 1), 0) // MOBA_BLOCK
    past = j < own
    g = jnp.where(past, gate, -jnp.inf)
    rank = jnp.zeros(gate.shape, jnp.int32)
    for d in range(1, nb):
        other = pltpu.roll(g, LANES - d, 1)
        first = ((j + d) % nb) < j
        beats = (other > g) | (first & (other == g))
        rank = rank + jnp.where(beats, 1, 0)
    attend = (past & (rank < MOBA_TOPK)) | (j == own)
    mqa_ref[0] = jnp.where(attend, 0.0, NEG_INF)

    z = fl_ref[0] + bf_ref[...]
    logf = -(jnp.maximum(-z, 0.0) + jnp.log1p(jnp.exp(-jnp.abs(z))))
    rr = lax.broadcasted_iota(jnp.int32, (MOBA_BLOCK, MOBA_BLOCK), 0)
    cc = lax.broadcasted_iota(jnp.int32, (MOBA_BLOCK, MOBA_BLOCK), 1)
    tri = _bf16(jnp.where(cc <= rr, 1.0, 0.0))
    slot = lane % AUX_GROUP
    carry = jnp.zeros((1, LANES), jnp.float32)
    for i in range(nb):
        rows = slice(i * MOBA_BLOCK, (i + 1) * MOBA_BLOCK)
        hi, mid, lo = _split3(logf[rows, :])
        cum = (_dot(tri, hi) + _dot(tri, mid)) + _dot(tri, lo) + carry
        carry = cum[MOBA_BLOCK - 1:MOBA_BLOCK, :]
        parts = [p.astype(jnp.float32) for p in _split3(cum * LOG2E)]
        fq = jnp.where(slot < 2 * N_SPLIT, 1.0, 0.0)
        fk = jnp.where(slot < N_SPLIT, 1.0, 0.0)
        for s in range(N_SPLIT):
            fq = jnp.where(slot == s, parts[s], fq)
            fk = jnp.where(slot == N_SPLIT + s, -parts[s], fk)
        fqa_ref[0, rows, :] = fq
        fka_ref[0, rows, :] = fk


def _routing(mq, mk, fl, bf_lanes):
    b, t, w = mq.shape
    full = lambda i: (i, 0, 0)
    aux = jax.ShapeDtypeStruct((b, t, LANES), jnp.float32)
    return pl.pallas_call(
        _routing_kernel,
        grid=(b,),
        in_specs=[pl.BlockSpec((1, t, w), full),
                  pl.BlockSpec((1, t, w), full),
                  pl.BlockSpec((1, t, LANES), full),
                  pl.BlockSpec((1, LANES), lambda i: (0, 0))],
        out_specs=[pl.BlockSpec((1, t, LANES), full)] * 3,
        out_shape=[aux] * 3,
        compiler_params=pltpu.CompilerParams(vmem_limit_bytes=VMEM_LIMIT),
        name="routing",
    )(mq, mk, fl, bf_lanes)


def _attn_kernel(*refs, moba):
    if moba:
        q_ref, qa_ref, k_ref, vt_ref, o_ref, kx_ref, vx_ref = refs
    else:
        q_ref, qa_ref, k_ref, vt_ref, ka_ref, o_ref, kx_ref, vx_ref = refs
    tq = SEQ_TILE
    t = k_ref.shape[1]
    nb = t // tq
    step = pl.program_id(2)
    lane = lax.broadcasted_iota(jnp.int32, (1, LANES), 1)
    low = lane < HEAD_DIM
    shift = (LANES - pl.program_id(1) * AUX_GROUP) % LANES
    in_group = (lane % HEAD_DIM) < AUX_GROUP // 2

    @pl.when(step == 0)
    def _():
        k2 = k_ref[0]
        if moba:
            blk = lax.broadcasted_iota(jnp.int32, (t, LANES), 0) // MOBA_BLOCK
            ln = lax.broadcasted_iota(jnp.int32, (t, LANES), 1) % HEAD_DIM
            ka = _bf16(jnp.where(ln == blk, 1.0, 0.0))
        else:
            ka = _bf16(pltpu.roll(ka_ref[0], shift, 1))
        kx_ref[0] = jnp.where(low, k2, ka)
        kx_ref[1] = jnp.where(low, ka, k2)
        for h in range(2):
            vx_ref[h, 0:HEAD_DIM, :] = vt_ref[0, h * HEAD_DIM:(h + 1) * HEAD_DIM, :]
            vx_ref[h, HEAD_DIM:V_ROWS, :] = jnp.ones((V_ROWS - HEAD_DIM, t), jnp.bfloat16)

    key = lax.broadcasted_iota(jnp.int32, (tq, tq), 0)
    qry = lax.broadcasted_iota(jnp.int32, (tq, tq), 1)
    causal = key <= qry

    def scores(n, h):
        rows = slice(n * tq, (n + 1) * tq)
        q2 = q_ref[0, rows, :]
        qa = pltpu.roll(qa_ref[0, rows, :], shift, 1)
        qa = _bf16(jnp.where(in_group, qa, 0.0))
        qx = jnp.where(low, q2, qa) if h == 0 else jnp.where(low, qa, q2)
        sd = jnp.where(causal, _dot_nt(kx_ref[h, rows, :], qx), NEG_INF)
        m = jnp.max(sd, axis=0, keepdims=True)
        sp = None
        if n > 0:
            sp = _dot_nt(kx_ref[h, 0:n * tq, :], qx)
            m = jnp.maximum(m, jnp.max(sp, axis=0, keepdims=True))
        return n, h, sd, sp, m

    def weighted_values(n, h, sd, sp, m):
        rows = slice(n * tq, (n + 1) * tq)
        acc = _dot(vx_ref[h, :, rows], _bf16(jnp.exp2(sd - m)))
        if n > 0:
            acc = acc + _dot(vx_ref[h, :, 0:n * tq], _bf16(jnp.exp2(sp - m)))
        return acc[0:HEAD_DIM, :] / acc[HEAD_DIM:HEAD_DIM + 1, :]

    def run(tiles):
        chains = [(n, h) for n in tiles for h in range(2)]
        outs = {}
        queue = [scores(*c) for c in chains[:ATTN_LOOKAHEAD]]
        for nxt in chains[ATTN_LOOKAHEAD:] + [None] * ATTN_LOOKAHEAD:
            if nxt is not None:
                queue.append(scores(*nxt))
            done = queue.pop(0)
            outs[done[:2]] = weighted_values(*done)
        for n in tiles:
            o_t = jnp.concatenate([outs[(n, 0)], outs[(n, 1)]], axis=0)
            o_ref[0, n * tq:(n + 1) * tq, :] = _bf16(o_t.T)

    pairs = [(nb - 1 - i, i) for i in range(nb // 2)]
    n_static = _attn_steps(nb)
    for s in range(n_static):
        @pl.when(step == s)
        def _(s=s):
            run([n for pair in pairs[s::n_static] for n in pair])


def _attn_steps(nb):
    return max(1, nb // ATTN_TILES_PER_STEP)


def _attention(moba, q, qa, k, vt, ka=None):
    b, t, w = q.shape
    n_pairs = w // LANES
    nb = t // SEQ_TILE
    assert nb % 2 == 0 and (nb // 2) % _attn_steps(nb) == 0
    spec = pl.BlockSpec((1, t, LANES), lambda i, p, j: (i, 0, p))
    spec_t = pl.BlockSpec((1, LANES, t), lambda i, p, j: (i, p, 0))
    spec_aux = pl.BlockSpec((1, t, LANES), lambda i, p, j: (i, 0, 0))
    args = [q, qa, k, vt] if moba else [q, qa, k, vt, ka]
    specs = [spec, spec_aux, spec, spec_t] + ([] if moba else [spec_aux])
    kern = lambda *refs: _attn_kernel(*refs, moba=moba)
    return pl.pallas_call(
        kern,
        grid=(b, n_pairs, _attn_steps(nb)),
        in_specs=specs,
        out_specs=spec,
        out_shape=jax.ShapeDtypeStruct((b, t, w), jnp.bfloat16),
        scratch_shapes=[pltpu.VMEM((2, t, LANES), jnp.bfloat16),
                        pltpu.VMEM((2, V_ROWS, t), jnp.bfloat16)],
        compiler_params=pltpu.CompilerParams(vmem_limit_bytes=VMEM_LIMIT),
        name="moba" if moba else "fox",
    )(*args)


def _ffn_kernel(x_ref, om_ref, of_ref, wo_ref, gt1_ref, g_ref, sc_ref, sh_ref, wup_ref, cv_ref,
                wdn_ref, gt2_ref, out_ref, x1_ref, halo_ref):
    tm = x_ref.shape[1]

    @pl.when(pl.program_id(1) == 0)
    def _():
        halo_ref[...] = jnp.zeros(halo_ref.shape, jnp.float32)

    o = jnp.concatenate([om_ref[0], of_ref[0]], axis=1)
    x1 = x_ref[0] + gt1_ref[0] * _dot(o, wo_ref[...])
    x1_ref[...] = x1
    hn = _bf16(_rms_mod(x1, g_ref[...], sc_ref[0], sh_ref[0]))

    def up_proj(c):
        gate_cols = slice(c * FF_CHUNK, (c + 1) * FF_CHUNK)
        val_cols = slice(D_FF + c * FF_CHUNK, D_FF + (c + 1) * FF_CHUNK)
        u = jnp.concatenate([_dot(hn, wup_ref[:, gate_cols]), _dot(hn, wup_ref[:, val_cols])],
                            axis=1)
        return u, gate_cols, val_cols

    hmid = []
    ahead = [up_proj(c) for c in range(FFN_LOOKAHEAD)]
    for c in range(N_FF_CHUNKS):
        if c + FFN_LOOKAHEAD < N_FF_CHUNKS:
            ahead.append(up_proj(c + FFN_LOOKAHEAD))
        u, gate_cols, val_cols = ahead.pop(0)
        ext = jnp.concatenate([halo_ref[c], u], axis=0)
        halo_ref[c] = u[tm - HALO:tm, :]
        cv = jnp.concatenate([cv_ref[:, gate_cols], cv_ref[:, val_cols]], axis=1)
        u1 = pltpu.roll(ext, 1, 0)[HALO:, :]
        u2 = pltpu.roll(ext, 2, 0)[HALO:, :]
        uc = cv[0:1, :] * u2 + cv[1:2, :] * u1 + cv[2:3, :] * u + cv[3:4, :]
        a = uc[:, :FF_CHUNK]
        val = uc[:, FF_CHUNK:]
        half = 0.5 * a
        hmid.append(_bf16((half + half * jnp.tanh(half)) * val))
    ffn = _dot(jnp.concatenate(hmid, axis=1), wdn_ref[...])
    out_ref[0] = x1_ref[...] + gt2_ref[0] * ffn


def _ffn(x, om, of, wo, gt1, g_ffn, sc2, sh2, wup, cv, wdn, gt2):
    b, t, d = x.shape
    tm = min(ROW_TILE, t)
    row = lambda i, j: (i, 0, 0)
    tile = lambda i, j: (i, j, 0)
    const2 = lambda i, j: (0, 0)
    once = dict(pipeline_mode=pl.Buffered(1))
    return pl.pallas_call(
        _ffn_kernel,
        grid=(b, t // tm),
        in_specs=[pl.BlockSpec((1, tm, d), tile),
                  pl.BlockSpec((1, tm, MOBA_WIDTH), tile),
                  pl.BlockSpec((1, tm, FOX_WIDTH), tile),
                  pl.BlockSpec((d, d), const2, **once),
                  pl.BlockSpec((1, 1, d), row),
                  pl.BlockSpec((1, d), const2),
                  pl.BlockSpec((1, 1, d), row),
                  pl.BlockSpec((1, 1, d), row),
                  pl.BlockSpec((d, 2 * D_FF), const2, **once),
                  pl.BlockSpec((HALO, 2 * D_FF), const2, **once),
                  pl.BlockSpec((D_FF, d), const2, **once),
                  pl.BlockSpec((1, 1, d), row)],
        out_specs=pl.BlockSpec((1, tm, d), tile),
        out_shape=jax.ShapeDtypeStruct((b, t, d), jnp.float32),
        scratch_shapes=[pltpu.VMEM((tm, d), jnp.float32),
                        pltpu.VMEM((N_FF_CHUNKS, HALO, 2 * FF_CHUNK), jnp.float32)],
        compiler_params=pltpu.CompilerParams(vmem_limit_bytes=VMEM_LIMIT),
        name="ffn",
    )(x, om, of, wo, gt1, g_ffn, sc2, sh2, wup, cv, wdn, gt2)


def _layer(x, c, w_ada, b_ada, g_mix, w_in, b_forget, moba_q_gain, moba_k_gain, fox_q_gain,
           fox_k_gain, w_out, g_ffn, w_up, conv_w, conv_b, w_down):
    b, t, d = x.shape
    nb = t // MOBA_BLOCK
    assert d == D_MODEL and t % SEQ_TILE == 0
    assert AUX_GROUP % nb == 0 and 2 * nb <= AUX_GROUP and 2 * N_SPLIT <= AUX_GROUP // 2

    mod = _adaln(c, w_ada, b_ada).reshape(b, 6, 1, d)
    sh1, sc1, gt1, sh2, sc2, gt2 = [mod[:, i] for i in range(6)]

    w = MOBA_WIDTH
    cols = lambda i: w_in[:, i * w:(i + 1) * w]
    lane_head = np.full(LANES, -1)
    for head in range(N_FOX_HEADS):
        lane_head[_group_base(head):_group_base(head) + 2 * N_SPLIT] = head
    used = jnp.asarray(lane_head >= 0)
    logits = jnp.where(used[None, :], w_in[:, 6 * w + np.maximum(lane_head, 0)], 0.0)
    w_qk = _bf16(jnp.concatenate([cols(0), cols(1), cols(3), cols(4), logits], axis=1))
    w_vt = _bf16(jnp.concatenate([cols(2), cols(5)], axis=1).T)
    gains = jnp.stack([jnp.tile(g, N_MOBA_HEADS) for g in
                       (moba_q_gain, moba_k_gain, fox_q_gain, fox_k_gain)])
    r = np.arange(2 * LANES) // HEAD_DIM
    bd = jnp.asarray(r[:, None] == r[None, :], jnp.bfloat16)
    mq, mk, fq, fk, fl, mvt, fvt = _inproj(x, sc1, sh1, g_mix.reshape(1, d), w_qk, w_vt, gains,
                                           _rope_tables(t), bd)

    bf_lanes = jnp.where(used, b_forget[np.maximum(lane_head, 0)], 0.0).reshape(1, LANES)
    mqa, fqa, fka = _routing(mq, mk, fl, bf_lanes)
    o_moba = _attention(True, mq, mqa, mk, mvt)
    o_fox = _attention(False, fq, fqa, fk, fvt, fka)

    cv = jnp.concatenate([conv_w, conv_b[None, :],
                          jnp.zeros((HALO - CONV_WIDTH - 1, 2 * D_FF), jnp.float32)], axis=0)
    return _ffn(x, o_moba, o_fox, _bf16(w_out), gt1, g_ffn.reshape(1, d), sc2, sh2, _bf16(w_up), cv,
                _bf16(w_down), gt2)


def kernel(x, c, w_ada, b_ada, g_mix, w_in, b_forget, moba_q_gain, moba_k_gain, fox_q_gain,
           fox_k_gain, w_out, g_ffn, w_up, conv_w, conv_b, w_down):
    for l in range(w_ada.shape[0]):
        x = _layer(x, c, w_ada[l], b_ada[l], g_mix[l], w_in[l], b_forget[l], moba_q_gain[l],
                   moba_k_gain[l], fox_q_gain[l], fox_k_gain[l], w_out[l], g_ffn[l], w_up[l],
                   conv_w[l], conv_b[l], w_down[l])
    return x
```

```python
import math

import jax
import jax.numpy as jnp
import numpy as np
from jax import lax
from jax.experimental import pallas as pl
from jax.experimental.pallas import tpu as pltpu

D_MODEL = 1024
HEAD_DIM = 64
N_MOBA_HEADS = 8
N_FOX_HEADS = 8
MOBA_WIDTH = N_MOBA_HEADS * HEAD_DIM
FOX_WIDTH = N_FOX_HEADS * HEAD_DIM
MOBA_BLOCK = 256
MOBA_TOPK = 3
ROPE_THETA = 500000.0
ROPE_DIM = HEAD_DIM // 4
D_FF = 2816
CONV_WIDTH = 3
NORM_EPS = 1e-6
NEG_INF = -1e30
LOG2E = math.log2(math.e)
Q_SCALE = HEAD_DIM ** -0.5 * LOG2E
LANES = 128
BF16_ROWS = 16
QK_COLS = 2 * MOBA_WIDTH + 2 * FOX_WIDTH
QK_COLS_PAD = QK_COLS + LANES
V_ROWS = HEAD_DIM + BF16_ROWS
SEQ_TILE = 256
ROW_TILE = 512
ATTN_TILES_PER_STEP = 8
ATTN_LOOKAHEAD = 5
FFN_PARTS = 2
FFN_LOOKAHEAD = 2
FF_CHUNK = 256
N_FF_CHUNKS = D_FF // FF_CHUNK
HALO = 8
N_SPLIT = 3
AUX_GROUP = 16
VMEM_LIMIT = 56 * 1024 * 1024

_NT = (((1,), (1,)), ((), ()))


def _bf16(a):
    return a.astype(jnp.bfloat16)


def _dot(a, b):
    return jnp.dot(a, b, preferred_element_type=jnp.float32)


def _dot_nt(a, b):
    return lax.dot_general(a, b, _NT, preferred_element_type=jnp.float32)


def _sigmoid(a):
    return 1.0 / (1.0 + jnp.exp(-a))


def _adaln_kernel(c_ref, w_ref, b_ref, o_ref):
    c = c_ref[...]
    s = c * _sigmoid(c)
    o_ref[...] = _dot(_bf16(s), _bf16(w_ref[...])) + b_ref[...]


def _adaln(c, w_ada, b_ada):
    b, d = c.shape
    n = w_ada.shape[1]
    tn = 1536
    return pl.pallas_call(
        _adaln_kernel,
        grid=(n // tn,),
        in_specs=[pl.BlockSpec((b, d), lambda j: (0, 0)),
                  pl.BlockSpec((d, tn), lambda j: (0, j)),
                  pl.BlockSpec((1, tn), lambda j: (0, j))],
        out_specs=pl.BlockSpec((b, tn), lambda j: (0, j)),
        out_shape=jax.ShapeDtypeStruct((b, n), jnp.float32),
        compiler_params=pltpu.CompilerParams(vmem_limit_bytes=VMEM_LIMIT),
        name="adaln",
    )(c, w_ada, b_ada.reshape(1, n))


def _rms_mod(x, g, sc, sh):
    ms = jnp.mean(x * x, axis=-1, keepdims=True)
    y = x * lax.rsqrt(ms + NORM_EPS)
    return (y * g) * (1.0 + sc) + sh


def _head_norm(p, gain, bd):
    sq = _bf16(p * p)
    half = 2 * LANES
    ss = jnp.concatenate([_dot(sq[:, :half], bd), _dot(sq[:, half:], bd)], axis=1)
    return (p * lax.rsqrt(ss * (1.0 / HEAD_DIM) + NORM_EPS)) * gain


def _rope(y, c, sa, sb):
    half = ROPE_DIM // 2
    outs = []
    for i in range(y.shape[1] // LANES):
        yc = y[:, i * LANES:(i + 1) * LANES]
        up = pltpu.roll(yc, LANES - half, 1)
        dn = pltpu.roll(yc, half, 1)
        outs.append(yc * c + up * sa + dn * sb)
    return jnp.concatenate(outs, axis=1)


def _inproj_kernel(x_ref, sc_ref, sh_ref, g_ref, w_ref, wvt_ref, gains_ref, rc_ref, rsa_ref, rsb_ref,
                   bd_ref, mq_ref, mk_ref, fq_ref, fk_ref, fl_ref, mvt_ref, fvt_ref):
    hn = _bf16(_rms_mod(x_ref[0], g_ref[...], sc_ref[0], sh_ref[0]))
    bd = bd_ref[...]
    c, sa, sb = rc_ref[...], rsa_ref[...], rsb_ref[...]
    w = MOBA_WIDTH
    proj = lambda i: _dot(hn, w_ref[:, i * w:(i + 1) * w])

    p_mq = proj(0)
    p_mk = proj(1)
    mq = _rope(_head_norm(p_mq, gains_ref[0:1, :], bd), c, sa, sb)
    mq_ref[0] = _bf16(mq * Q_SCALE)
    p_fq = proj(2)
    mk = _rope(_head_norm(p_mk, gains_ref[1:2, :], bd), c, sa, sb)
    mk_ref[0] = _bf16(mk)
    p_fk = proj(3)
    fq = _head_norm(p_fq, gains_ref[2:3, :], bd)
    fq_ref[0] = _bf16(fq * Q_SCALE)
    fl_ref[0] = _dot(hn, w_ref[:, QK_COLS:QK_COLS_PAD])
    mvt = _dot_nt(wvt_ref[0:w, :], hn)
    fk = _head_norm(p_fk, gains_ref[3:4, :], bd)
    fk_ref[0] = _bf16(fk)
    mvt_ref[0] = _bf16(mvt)
    fvt_ref[0] = _bf16(_dot_nt(wvt_ref[w:2 * w, :], hn))


def _rope_tables(t):
    half = ROPE_DIM // 2
    inv_freq = jnp.power(ROPE_THETA, -2.0 * jnp.arange(half, dtype=jnp.float32) / ROPE_DIM)
    ang = jnp.arange(t, dtype=jnp.float32)[:, None] * inv_freq[None, :]
    cos, sin = jnp.cos(ang), jnp.sin(ang)
    d = np.arange(LANES) % HEAD_DIM
    first = jnp.asarray(d < half)[None, :]
    second = jnp.asarray((d >= half) & (d < ROPE_DIM))[None, :]
    idx = np.where(d < ROPE_DIM, d % half, 0)
    cos_l, sin_l = cos[:, idx], sin[:, idx]
    c = jnp.where(first | second, cos_l, 1.0)
    sa = jnp.where(first, -sin_l, 0.0)
    sb = jnp.where(second, sin_l, 0.0)
    return c, sa, sb


def _inproj(x, sc1, sh1, g_mix, w_qk, w_vt, gains, tables, bd):
    b, t, d = x.shape
    tm = min(ROW_TILE, t)
    w = MOBA_WIDTH
    row = lambda i, j: (i, 0, 0)
    tile = lambda i, j: (i, j, 0)
    tile_t = lambda i, j: (i, 0, j)
    const = lambda i, j: (0, 0)
    wide = jax.ShapeDtypeStruct((b, t, w), jnp.bfloat16)
    wide_t = jax.ShapeDtypeStruct((b, w, t), jnp.bfloat16)
    out_specs = ([pl.BlockSpec((1, tm, w), tile)] * 4 + [pl.BlockSpec((1, tm, LANES), tile)]
                 + [pl.BlockSpec((1, w, tm), tile_t)] * 2)
    return pl.pallas_call(
        _inproj_kernel,
        grid=(b, t // tm),
        in_specs=[pl.BlockSpec((1, tm, d), tile),
                  pl.BlockSpec((1, 1, d), row),
                  pl.BlockSpec((1, 1, d), row),
                  pl.BlockSpec((1, d), const),
                  pl.BlockSpec((d, QK_COLS_PAD), const),
                  pl.BlockSpec((2 * w, d), const),
                  pl.BlockSpec((4, w), const),
                  pl.BlockSpec((tm, LANES), lambda i, j: (j, 0)),
                  pl.BlockSpec((tm, LANES), lambda i, j: (j, 0)),
                  pl.BlockSpec((tm, LANES), lambda i, j: (j, 0)),
                  pl.BlockSpec((2 * LANES, 2 * LANES), const)],
        out_specs=out_specs,
        out_shape=[wide] * 4 + [jax.ShapeDtypeStruct((b, t, LANES), jnp.float32)] + [wide_t] * 2,
        compiler_params=pltpu.CompilerParams(vmem_limit_bytes=VMEM_LIMIT),
        name="inproj",
    )(x, sc1, sh1, g_mix, w_qk, w_vt, gains, *tables, bd)


def _split3(a):
    hi = _bf16(a)
    r1 = a - hi.astype(jnp.float32)
    mid = _bf16(r1)
    lo = _bf16(r1 - mid.astype(jnp.float32))
    return hi, mid, lo


def _group_base(head):
    pair, h = divmod(head, 2)
    return (0 if h == 1 else HEAD_DIM) + pair * AUX_GROUP


def _group_head(lane_idx):
    return 2 * ((lane_idx % HEAD_DIM) // AUX_GROUP) + jnp.where(lane_idx < HEAD_DIM, 1, 0)


def _routing_kernel(mq_ref, mk_ref, fl_ref, bf_ref, mqa_ref, fqa_ref, fka_ref):
    t = mq_ref.shape[1]
    nb = t // MOBA_BLOCK
    q = mq_ref[0]
    k = mk_ref[0]
    blk_of_col = lax.broadcasted_iota(jnp.int32, (nb, t), 1) // MOBA_BLOCK
    blk_row = lax.broadcasted_iota(jnp.int32, (nb, t), 0)
    ind = _bf16(jnp.where(blk_of_col == blk_row, 1.0, 0.0))
    kmean = _dot(ind, k) * (1.0 / MOBA_BLOCK)
    kmt = jnp.concatenate([kmean] * (LANES // nb), axis=0)
    r_head = _group_head(lax.broadcasted_iota(jnp.int32, kmt.shape, 0))
    c_head = lax.broadcasted_iota(jnp.int32, kmt.shape, 1) // HEAD_DIM
    kmt = jnp.where(r_head == c_head, kmt, 0.0)
    k_hi = _bf16(kmt)
    k_lo = _bf16(kmt - k_hi.astype(jnp.float32))
    gate = (_dot_nt(q, k_hi) + _dot_nt(q, k_lo)) * (1.0 / Q_SCALE)

    lane = lax.broadcasted_iota(jnp.int32, (1, LANES), 1)
    j = lane % nb
    own = lax.broadcasted_iota(jnp.int32, (t, 1), 0) // MOBA_BLOCK
    past = j < own
    g = jnp.where(past, gate, -jnp.inf)
    rank = jnp.zeros(gate.shape, jnp.int32)
    for d in range(1, nb):
        other = pltpu.roll(g, LANES - d, 1)
        first = ((j + d) % nb) < j
        beats = (other > g) | (first & (other == g))
        rank = rank + jnp.where(beats, 1, 0)
    attend = (past & (rank < MOBA_TOPK)) | (j == own)
    mqa_ref[0] = jnp.where(attend, 0.0, NEG_INF)

    z = fl_ref[0] + bf_ref[...]
    logf = -(jnp.maximum(-z, 0.0) + jnp.log1p(jnp.exp(-jnp.abs(z))))
    rr = lax.broadcasted_iota(jnp.int32, (MOBA_BLOCK, MOBA_BLOCK), 0)
    cc = lax.broadcasted_iota(jnp.int32, (MOBA_BLOCK, MOBA_BLOCK), 1)
    tri = _bf16(jnp.where(cc <= rr, 1.0, 0.0))
    slot = lane % AUX_GROUP
    carry = jnp.zeros((1, LANES), jnp.float32)
    for i in range(nb):
        rows = slice(i * MOBA_BLOCK, (i + 1) * MOBA_BLOCK)
        hi, mid, lo = _split3(logf[rows, :])
        cum = (_dot(tri, hi) + _dot(tri, mid)) + _dot(tri, lo) + carry
        carry = cum[MOBA_BLOCK - 1:MOBA_BLOCK, :]
        parts = [p.astype(jnp.float32) for p in _split3(cum * LOG2E)]
        fq = jnp.where(slot < 2 * N_SPLIT, 1.0, 0.0)
        fk = jnp.where(slot < N_SPLIT, 1.0, 0.0)
        for s in range(N_SPLIT):
            fq = jnp.where(slot == s, parts[s], fq)
            fk = jnp.where(slot == N_SPLIT + s, -parts[s], fk)
        fqa_ref[0, rows, :] = fq
        fka_ref[0, rows, :] = fk


def _routing(mq, mk, fl, bf_lanes):
    b, t, w = mq.shape
    full = lambda i: (i, 0, 0)
    aux = jax.ShapeDtypeStruct((b, t, LANES), jnp.float32)
    return pl.pallas_call(
        _routing_kernel,
        grid=(b,),
        in_specs=[pl.BlockSpec((1, t, w), full),
                  pl.BlockSpec((1, t, w), full),
                  pl.BlockSpec((1, t, LANES), full),
                  pl.BlockSpec((1, LANES), lambda i: (0, 0))],
        out_specs=[pl.BlockSpec((1, t, LANES), full)] * 3,
        out_shape=[aux] * 3,
        compiler_params=pltpu.CompilerParams(vmem_limit_bytes=VMEM_LIMIT),
        name="routing",
    )(mq, mk, fl, bf_lanes)


def _attn_kernel(*refs, moba):
    if moba:
        q_ref, qa_ref, k_ref, vt_ref, o_ref, kx_ref, vx_ref = refs
    else:
        q_ref, qa_ref, k_ref, vt_ref, ka_ref, o_ref, kx_ref, vx_ref = refs
    tq = SEQ_TILE
    t = k_ref.shape[1]
    nb = t // tq
    step = pl.program_id(2)
    lane = lax.broadcasted_iota(jnp.int32, (1, LANES), 1)
    low = lane < HEAD_DIM
    shift = (LANES - pl.program_id(1) * AUX_GROUP) % LANES
    in_group = (lane % HEAD_DIM) < AUX_GROUP // 2

    @pl.when(step == 0)
    def _():
        k2 = k_ref[0]
        if moba:
            blk = lax.broadcasted_iota(jnp.int32, (t, LANES), 0) // MOBA_BLOCK
            ln = lax.broadcasted_iota(jnp.int32, (t, LANES), 1) % HEAD_DIM
            ka = _bf16(jnp.where(ln == blk, 1.0, 0.0))
        else:
            ka = _bf16(pltpu.roll(ka_ref[0], shift, 1))
        kx_ref[0] = jnp.where(low, k2, ka)
        kx_ref[1] = jnp.where(low, ka, k2)
        for h in range(2):
            vx_ref[h, 0:HEAD_DIM, :] = vt_ref[0, h * HEAD_DIM:(h + 1) * HEAD_DIM, :]
            vx_ref[h, HEAD_DIM:V_ROWS, :] = jnp.ones((V_ROWS - HEAD_DIM, t), jnp.bfloat16)

    key = lax.broadcasted_iota(jnp.int32, (tq, tq), 0)
    qry = lax.broadcasted_iota(jnp.int32, (tq, tq), 1)
    causal = key <= qry

    def scores(n, h):
        rows = slice(n * tq, (n + 1) * tq)
        q2 = q_ref[0, rows, :]
        qa = pltpu.roll(qa_ref[0, rows, :], shift, 1)
        qa = _bf16(jnp.where(in_group, qa, 0.0))
        qx = jnp.where(low, q2, qa) if h == 0 else jnp.where(low, qa, q2)
        sd = jnp.where(causal, _dot_nt(kx_ref[h, rows, :], qx), NEG_INF)
        m = jnp.max(sd, axis=0, keepdims=True)
        sp = None
        if n > 0:
            sp = _dot_nt(kx_ref[h, 0:n * tq, :], qx)
            m = jnp.maximum(m, jnp.max(sp, axis=0, keepdims=True))
        return n, h, sd, sp, m

    def weighted_values(n, h, sd, sp, m):
        rows = slice(n * tq, (n + 1) * tq)
        acc = _dot(vx_ref[h, :, rows], _bf16(jnp.exp2(sd - m)))
        if n > 0:
            acc = acc + _dot(vx_ref[h, :, 0:n * tq], _bf16(jnp.exp2(sp - m)))
        return acc[0:HEAD_DIM, :] / acc[HEAD_DIM:HEAD_DIM + 1, :]

    def run(tiles):
        chains = [(n, h) for n in tiles for h in range(2)]
        outs = {}
        queue = [scores(*c) for c in chains[:ATTN_LOOKAHEAD]]
        for nxt in chains[ATTN_LOOKAHEAD:] + [None] * ATTN_LOOKAHEAD:
            if nxt is not None:
                queue.append(scores(*nxt))
            done = queue.pop(0)
            outs[done[:2]] = weighted_values(*done)
        for n in tiles:
            o_t = jnp.concatenate([outs[(n, 0)], outs[(n, 1)]], axis=0)
            o_ref[0, n * tq:(n + 1) * tq, :] = _bf16(o_t.T)

    pairs = [(nb - 1 - i, i) for i in range(nb // 2)]
    n_static = _attn_steps(nb)
    for s in range(n_static):
        @pl.when(step == s)
        def _(s=s):
            run([n for pair in pairs[s::n_static] for n in pair])


def _attn_steps(nb):
    return max(1, nb // ATTN_TILES_PER_STEP)


def _attention(moba, q, qa, k, vt, ka=None):
    b, t, w = q.shape
    n_pairs = w // LANES
    nb = t // SEQ_TILE
    assert nb % 2 == 0 and (nb // 2) % _attn_steps(nb) == 0
    spec = pl.BlockSpec((1, t, LANES), lambda i, p, j: (i, 0, p))
    spec_t = pl.BlockSpec((1, LANES, t), lambda i, p, j: (i, p, 0))
    spec_aux = pl.BlockSpec((1, t, LANES), lambda i, p, j: (i, 0, 0))
    args = [q, qa, k, vt] if moba else [q, qa, k, vt, ka]
    specs = [spec, spec_aux, spec, spec_t] + ([] if moba else [spec_aux])
    kern = lambda *refs: _attn_kernel(*refs, moba=moba)
    return pl.pallas_call(
        kern,
        grid=(b, n_pairs, _attn_steps(nb)),
        in_specs=specs,
        out_specs=spec,
        out_shape=jax.ShapeDtypeStruct((b, t, w), jnp.bfloat16),
        scratch_shapes=[pltpu.VMEM((2, t, LANES), jnp.bfloat16),
                        pltpu.VMEM((2, V_ROWS, t), jnp.bfloat16)],
        compiler_params=pltpu.CompilerParams(vmem_limit_bytes=VMEM_LIMIT),
        name="moba" if moba else "fox",
    )(*args)


def _ffn_kernel(x_ref, om_ref, of_ref, wo_ref, gt1_ref, g_ref, sc_ref, sh_ref, wup_ref, cv_ref,
                wdn_ref, gt2_ref, out_ref, x1_ref, halo_ref):
    tm = x_ref.shape[1]

    @pl.when(pl.program_id(1) == 0)
    def _():
        halo_ref[...] = jnp.zeros(halo_ref.shape, jnp.float32)

    pm = tm // FFN_PARTS
    slabs = [slice(p * pm, (p + 1) * pm) for p in range(FFN_PARTS)]
    attn = [_dot(jnp.concatenate([om_ref[0, r, :], of_ref[0, r, :]], axis=1), wo_ref[...])
            for r in slabs]

    def normed(p):
        x1 = x_ref[0, slabs[p], :] + gt1_ref[0] * attn[p]
        x1_ref[slabs[p], :] = x1
        return _bf16(_rms_mod(x1, g_ref[...], sc_ref[0], sh_ref[0]))

    def up_proj(hn, c):
        gate_cols = slice(c * FF_CHUNK, (c + 1) * FF_CHUNK)
        val_cols = slice(D_FF + c * FF_CHUNK, D_FF + (c + 1) * FF_CHUNK)
        u = jnp.concatenate([_dot(hn, wup_ref[:, gate_cols]), _dot(hn, wup_ref[:, val_cols])],
                            axis=1)
        return u, gate_cols, val_cols

    hn = normed(0)
    tails = [halo_ref[c] for c in range(N_FF_CHUNKS)]
    for p in range(FFN_PARTS):
        hn_next = None
        hmid = []
        ahead = [up_proj(hn, c) for c in range(FFN_LOOKAHEAD)]
        for c in range(N_FF_CHUNKS):
            if c + FFN_LOOKAHEAD < N_FF_CHUNKS:
                ahead.append(up_proj(hn, c + FFN_LOOKAHEAD))
            u, gate_cols, val_cols = ahead.pop(0)
            ext = jnp.concatenate([tails[c], u], axis=0)
            tails[c] = u[pm - HALO:pm, :]
            cv = jnp.concatenate([cv_ref[:, gate_cols], cv_ref[:, val_cols]], axis=1)
            u1 = pltpu.roll(ext, 1, 0)[HALO:, :]
            u2 = pltpu.roll(ext, 2, 0)[HALO:, :]
            uc = cv[0:1, :] * u2 + cv[1:2, :] * u1 + cv[2:3, :] * u + cv[3:4, :]
            a = uc[:, :FF_CHUNK]
            val = uc[:, FF_CHUNK:]
            half = 0.5 * a
            hmid.append(_bf16((half + half * jnp.tanh(half)) * val))
            if c == FFN_LOOKAHEAD and p + 1 < FFN_PARTS:
                hn_next = normed(p + 1)
        ffn = _dot(jnp.concatenate(hmid, axis=1), wdn_ref[...])
        out_ref[0, slabs[p], :] = x1_ref[slabs[p], :] + gt2_ref[0] * ffn
        hn = hn_next
    for c in range(N_FF_CHUNKS):
        halo_ref[c] = tails[c]


def _ffn(x, om, of, wo, gt1, g_ffn, sc2, sh2, wup, cv, wdn, gt2):
    b, t, d = x.shape
    tm = min(ROW_TILE, t)
    row = lambda i, j: (i, 0, 0)
    tile = lambda i, j: (i, j, 0)
    const2 = lambda i, j: (0, 0)
    once = dict(pipeline_mode=pl.Buffered(1))
    return pl.pallas_call(
        _ffn_kernel,
        grid=(b, t // tm),
        in_specs=[pl.BlockSpec((1, tm, d), tile),
                  pl.BlockSpec((1, tm, MOBA_WIDTH), tile),
                  pl.BlockSpec((1, tm, FOX_WIDTH), tile),
                  pl.BlockSpec((d, d), const2, **once),
                  pl.BlockSpec((1, 1, d), row),
                  pl.BlockSpec((1, d), const2),
                  pl.BlockSpec((1, 1, d), row),
                  pl.BlockSpec((1, 1, d), row),
                  pl.BlockSpec((d, 2 * D_FF), const2, **once),
                  pl.BlockSpec((HALO, 2 * D_FF), const2, **once),
                  pl.BlockSpec((D_FF, d), const2, **once),
                  pl.BlockSpec((1, 1, d), row)],
        out_specs=pl.BlockSpec((1, tm, d), tile),
        out_shape=jax.ShapeDtypeStruct((b, t, d), jnp.float32),
        scratch_shapes=[pltpu.VMEM((tm, d), jnp.float32),
                        pltpu.VMEM((N_FF_CHUNKS, HALO, 2 * FF_CHUNK), jnp.float32)],
        compiler_params=pltpu.CompilerParams(vmem_limit_bytes=VMEM_LIMIT),
        name="ffn",
    )(x, om, of, wo, gt1, g_ffn, sc2, sh2, wup, cv, wdn, gt2)


def _layer(x, c, w_ada, b_ada, g_mix, w_in, b_forget, moba_q_gain, moba_k_gain, fox_q_gain,
           fox_k_gain, w_out, g_ffn, w_up, conv_w, conv_b, w_down):
    b, t, d = x.shape
    nb = t // MOBA_BLOCK
    assert d == D_MODEL and t % SEQ_TILE == 0
    assert AUX_GROUP % nb == 0 and 2 * nb <= AUX_GROUP and 2 * N_SPLIT <= AUX_GROUP // 2

    mod = _adaln(c, w_ada, b_ada).reshape(b, 6, 1, d)
    sh1, sc1, gt1, sh2, sc2, gt2 = [mod[:, i] for i in range(6)]

    w = MOBA_WIDTH
    cols = lambda i: w_in[:, i * w:(i + 1) * w]
    lane_head = np.full(LANES, -1)
    for head in range(N_FOX_HEADS):
        lane_head[_group_base(head):_group_base(head) + 2 * N_SPLIT] = head
    used = jnp.asarray(lane_head >= 0)
    logits = jnp.where(used[None, :], w_in[:, 6 * w + np.maximum(lane_head, 0)], 0.0)
    w_qk = _bf16(jnp.concatenate([cols(0), cols(1), cols(3), cols(4), logits], axis=1))
    w_vt = _bf16(jnp.concatenate([cols(2), cols(5)], axis=1).T)
    gains = jnp.stack([jnp.tile(g, N_MOBA_HEADS) for g in
                       (moba_q_gain, moba_k_gain, fox_q_gain, fox_k_gain)])
    r = np.arange(2 * LANES) // HEAD_DIM
    bd = jnp.asarray(r[:, None] == r[None, :], jnp.bfloat16)
    mq, mk, fq, fk, fl, mvt, fvt = _inproj(x, sc1, sh1, g_mix.reshape(1, d), w_qk, w_vt, gains,
                                           _rope_tables(t), bd)

    bf_lanes = jnp.where(used, b_forget[np.maximum(lane_head, 0)], 0.0).reshape(1, LANES)
    mqa, fqa, fka = _routing(mq, mk, fl, bf_lanes)
    o_moba = _attention(True, mq, mqa, mk, mvt)
    o_fox = _attention(False, fq, fqa, fk, fvt, fka)

    cv = jnp.concatenate([conv_w, conv_b[None, :],
                          jnp.zeros((HALO - CONV_WIDTH - 1, 2 * D_FF), jnp.float32)], axis=0)
    return _ffn(x, o_moba, o_fox, _bf16(w_out), gt1, g_ffn.reshape(1, d), sc2, sh2, _bf16(w_up), cv,
                _bf16(w_down), gt2)


def kernel(x, c, w_ada, b_ada, g_mix, w_in, b_forget, moba_q_gain, moba_k_gain, fox_q_gain,
           fox_k_gain, w_out, g_ffn, w_up, conv_w, conv_b, w_down):
    for l in range(w_ada.shape[0]):
        x = _layer(x, c, w_ada[l], b_ada[l], g_mix[l], w_in[l], b_forget[l], moba_q_gain[l],
                   moba_k_gain[l], fox_q_gain[l], fox_k_gain[l], w_out[l], g_ffn[l], w_up[l],
                   conv_w[l], conv_b[l], w_down[l])
    return x
```

```python
import math

import jax
import jax.numpy as jnp
import numpy as np
from jax import lax
from jax.experimental import pallas as pl
from jax.experimental.pallas import tpu as pltpu

D_MODEL = 1024
HEAD_DIM = 64
N_MOBA_HEADS = 8
N_FOX_HEADS = 8
MOBA_WIDTH = N_MOBA_HEADS * HEAD_DIM
FOX_WIDTH = N_FOX_HEADS * HEAD_DIM
MOBA_BLOCK = 256
MOBA_TOPK = 3
ROPE_THETA = 500000.0
ROPE_DIM = HEAD_DIM // 4
D_FF = 2816
CONV_WIDTH = 3
NORM_EPS = 1e-6
NEG_INF = -1e30
LOG2E = math.log2(math.e)
Q_SCALE = HEAD_DIM ** -0.5 * LOG2E
LANES = 128
BF16_ROWS = 16
QK_COLS = 2 * MOBA_WIDTH + 2 * FOX_WIDTH
QK_COLS_PAD = QK_COLS + LANES
V_ROWS = HEAD_DIM + BF16_ROWS
SEQ_TILE = 256
ROW_TILE = 512
ATTN_TILES_PER_STEP = 8
ATTN_LOOKAHEAD = 5
FFN_PARTS = 2
FFN_LOOKAHEAD = 2
FF_CHUNK = 256
N_FF_CHUNKS = D_FF // FF_CHUNK
HALO = 8
N_SPLIT = 3
AUX_GROUP = 16
VMEM_LIMIT = 56 * 1024 * 1024

_NT = (((1,), (1,)), ((), ()))


def _bf16(a):
    return a.astype(jnp.bfloat16)


def _dot(a, b):
    return jnp.dot(a, b, preferred_element_type=jnp.float32)


def _dot_nt(a, b):
    return lax.dot_general(a, b, _NT, preferred_element_type=jnp.float32)


def _sigmoid(a):
    return 1.0 / (1.0 + jnp.exp(-a))


def _adaln_kernel(c_ref, w_ref, b_ref, o_ref):
    c = c_ref[...]
    s = c * _sigmoid(c)
    o_ref[...] = _dot(_bf16(s), _bf16(w_ref[...])) + b_ref[...]


def _adaln(c, w_ada, b_ada):
    b, d = c.shape
    n = w_ada.shape[1]
    tn = 1536
    return pl.pallas_call(
        _adaln_kernel,
        grid=(n // tn,),
        in_specs=[pl.BlockSpec((b, d), lambda j: (0, 0)),
                  pl.BlockSpec((d, tn), lambda j: (0, j)),
                  pl.BlockSpec((1, tn), lambda j: (0, j))],
        out_specs=pl.BlockSpec((b, tn), lambda j: (0, j)),
        out_shape=jax.ShapeDtypeStruct((b, n), jnp.float32),
        compiler_params=pltpu.CompilerParams(vmem_limit_bytes=VMEM_LIMIT),
        name="adaln",
    )(c, w_ada, b_ada.reshape(1, n))


def _rms_mod(x, g, sc, sh):
    ms = jnp.mean(x * x, axis=-1, keepdims=True)
    y = x * lax.rsqrt(ms + NORM_EPS)
    return (y * g) * (1.0 + sc) + sh


def _head_norm(p, gain, bd):
    sq = _bf16(p * p)
    half = 2 * LANES
    ss = jnp.concatenate([_dot(sq[:, :half], bd), _dot(sq[:, half:], bd)], axis=1)
    return (p * lax.rsqrt(ss * (1.0 / HEAD_DIM) + NORM_EPS)) * gain


def _rope(y, c, sa, sb):
    half = ROPE_DIM // 2
    outs = []
    for i in range(y.shape[1] // LANES):
        yc = y[:, i * LANES:(i + 1) * LANES]
        up = pltpu.roll(yc, LANES - half, 1)
        dn = pltpu.roll(yc, half, 1)
        outs.append(yc * c + up * sa + dn * sb)
    return jnp.concatenate(outs, axis=1)


def _inproj_kernel(x_ref, sc_ref, sh_ref, g_ref, win_ref, wlog_ref, gains_ref, rc_ref, rsa_ref, rsb_ref,
                   bd_ref, mq_ref, mk_ref, fq_ref, fk_ref, fl_ref, mvt_ref, fvt_ref, w_ref, wvt_ref):
    @pl.when((pl.program_id(0) == 0) & (pl.program_id(1) == 0))
    def _():
        wd = MOBA_WIDTH
        for dst, src in enumerate((0, 1, 3, 4)):
            w_ref[:, dst * wd:(dst + 1) * wd] = _bf16(win_ref[:, src * wd:(src + 1) * wd])
        w_ref[:, QK_COLS:QK_COLS_PAD] = _bf16(wlog_ref[...])
        for dst, src in enumerate((2, 5)):
            wvt_ref[dst * wd:(dst + 1) * wd, :] = _bf16(win_ref[:, src * wd:(src + 1) * wd].T)

    hn = _bf16(_rms_mod(x_ref[0], g_ref[...], sc_ref[0], sh_ref[0]))
    bd = bd_ref[...]
    c, sa, sb = rc_ref[...], rsa_ref[...], rsb_ref[...]
    w = MOBA_WIDTH
    proj = lambda i: _dot(hn, w_ref[:, i * w:(i + 1) * w])

    p_mq = proj(0)
    p_mk = proj(1)
    mq = _rope(_head_norm(p_mq, gains_ref[0:1, :], bd), c, sa, sb)
    mq_ref[0] = _bf16(mq * Q_SCALE)
    p_fq = proj(2)
    mk = _rope(_head_norm(p_mk, gains_ref[1:2, :], bd), c, sa, sb)
    mk_ref[0] = _bf16(mk)
    p_fk = proj(3)
    fq = _head_norm(p_fq, gains_ref[2:3, :], bd)
    fq_ref[0] = _bf16(fq * Q_SCALE)
    fl_ref[0] = _dot(hn, w_ref[:, QK_COLS:QK_COLS_PAD])
    mvt = _dot_nt(wvt_ref[0:w, :], hn)
    fk = _head_norm(p_fk, gains_ref[3:4, :], bd)
    fk_ref[0] = _bf16(fk)
    mvt_ref[0] = _bf16(mvt)
    fvt_ref[0] = _bf16(_dot_nt(wvt_ref[w:2 * w, :], hn))


def _rope_tables(t):
    half = ROPE_DIM // 2
    inv_freq = np.power(ROPE_THETA, -2.0 * np.arange(half, dtype=np.float64) / ROPE_DIM)
    ang = np.arange(t, dtype=np.float64)[:, None] * inv_freq[None, :]
    cos, sin = np.cos(ang), np.sin(ang)
    d = np.arange(LANES) % HEAD_DIM
    first = (d < half)[None, :]
    second = ((d >= half) & (d < ROPE_DIM))[None, :]
    idx = np.where(d < ROPE_DIM, d % half, 0)
    cos_l, sin_l = cos[:, idx], sin[:, idx]
    c = np.where(first | second, cos_l, 1.0)
    sa = np.where(first, -sin_l, 0.0)
    sb = np.where(second, sin_l, 0.0)
    return tuple(jnp.asarray(a, jnp.float32) for a in (c, sa, sb))


def _inproj(x, sc1, sh1, g_mix, w_in, w_logits, gains, tables, bd):
    b, t, d = x.shape
    tm = min(ROW_TILE, t)
    w = MOBA_WIDTH
    row = lambda i, j: (i, 0, 0)
    tile = lambda i, j: (i, j, 0)
    tile_t = lambda i, j: (i, 0, j)
    const = lambda i, j: (0, 0)
    wide = jax.ShapeDtypeStruct((b, t, w), jnp.bfloat16)
    wide_t = jax.ShapeDtypeStruct((b, w, t), jnp.bfloat16)
    out_specs = ([pl.BlockSpec((1, tm, w), tile)] * 4 + [pl.BlockSpec((1, tm, LANES), tile)]
                 + [pl.BlockSpec((1, w, tm), tile_t)] * 2)
    return pl.pallas_call(
        _inproj_kernel,
        grid=(b, t // tm),
        in_specs=[pl.BlockSpec((1, tm, d), tile),
                  pl.BlockSpec((1, 1, d), row),
                  pl.BlockSpec((1, 1, d), row),
                  pl.BlockSpec((1, d), const),
                  pl.BlockSpec(w_in.shape, const, pipeline_mode=pl.Buffered(1)),
                  pl.BlockSpec((d, LANES), const),
                  pl.BlockSpec((4, w), const),
                  pl.BlockSpec((tm, LANES), lambda i, j: (j, 0)),
                  pl.BlockSpec((tm, LANES), lambda i, j: (j, 0)),
                  pl.BlockSpec((tm, LANES), lambda i, j: (j, 0)),
                  pl.BlockSpec((2 * LANES, 2 * LANES), const)],
        out_specs=out_specs,
        out_shape=[wide] * 4 + [jax.ShapeDtypeStruct((b, t, LANES), jnp.float32)] + [wide_t] * 2,
        scratch_shapes=[pltpu.VMEM((d, QK_COLS_PAD), jnp.bfloat16),
                        pltpu.VMEM((2 * w, d), jnp.bfloat16)],
        compiler_params=pltpu.CompilerParams(vmem_limit_bytes=VMEM_LIMIT),
        name="inproj",
    )(x, sc1, sh1, g_mix, w_in, w_logits, gains, *tables, bd)


def _split3(a):
    hi = _bf16(a)
    r1 = a - hi.astype(jnp.float32)
    mid = _bf16(r1)
    lo = _bf16(r1 - mid.astype(jnp.float32))
    return hi, mid, lo


def _group_base(head):
    pair, h = divmod(head, 2)
    return (0 if h == 1 else HEAD_DIM) + pair * AUX_GROUP


def _group_head(lane_idx):
    return 2 * ((lane_idx % HEAD_DIM) // AUX_GROUP) + jnp.where(lane_idx < HEAD_DIM, 1, 0)


def _routing_kernel(mq_ref, mk_ref, fl_ref, bf_ref, mqa_ref, fqa_ref, fka_ref):
    t = mq_ref.shape[1]
    nb = t // MOBA_BLOCK
    q = mq_ref[0]
    k = mk_ref[0]
    blk_of_col = lax.broadcasted_iota(jnp.int32, (nb, t), 1) // MOBA_BLOCK
    blk_row = lax.broadcasted_iota(jnp.int32, (nb, t), 0)
    ind = _bf16(jnp.where(blk_of_col == blk_row, 1.0, 0.0))
    kmean = _dot(ind, k) * (1.0 / MOBA_BLOCK)
    kmt = jnp.concatenate([kmean] * (LANES // nb), axis=0)
    r_head = _group_head(lax.broadcasted_iota(jnp.int32, kmt.shape, 0))
    c_head = lax.broadcasted_iota(jnp.int32, kmt.shape, 1) // HEAD_DIM
    kmt = jnp.where(r_head == c_head, kmt, 0.0)
    k_hi = _bf16(kmt)
    k_lo = _bf16(kmt - k_hi.astype(jnp.float32))
    gate = (_dot_nt(q, k_hi) + _dot_nt(q, k_lo)) * (1.0 / Q_SCALE)

    lane = lax.broadcasted_iota(jnp.int32, (1, LANES), 1)
    j = lane % nb
    own = lax.broadcasted_iota(jnp.int32, (t, 1), 0) // MOBA_BLOCK
    past = j < own
    g = jnp.where(past, gate, -jnp.inf)
    rank = jnp.zeros(gate.shape, jnp.int32)
    for d in range(1, nb):
        other = pltpu.roll(g, LANES - d, 1)
        first = ((j + d) % nb) < j
        beats = (other > g) | (first & (other == g))
        rank = rank + jnp.where(beats, 1, 0)
    attend = (past & (rank < MOBA_TOPK)) | (j == own)
    mqa_ref[0] = jnp.where(attend, 0.0, NEG_INF)

    z = fl_ref[0] + bf_ref[...]
    logf = -(jnp.maximum(-z, 0.0) + jnp.log1p(jnp.exp(-jnp.abs(z))))
    rr = lax.broadcasted_iota(jnp.int32, (MOBA_BLOCK, MOBA_BLOCK), 0)
    cc = lax.broadcasted_iota(jnp.int32, (MOBA_BLOCK, MOBA_BLOCK), 1)
    tri = _bf16(jnp.where(cc <= rr, 1.0, 0.0))
    slot = lane % AUX_GROUP
    carry = jnp.zeros((1, LANES), jnp.float32)
    for i in range(nb):
        rows = slice(i * MOBA_BLOCK, (i + 1) * MOBA_BLOCK)
        hi, mid, lo = _split3(logf[rows, :])
        cum = (_dot(tri, hi) + _dot(tri, mid)) + _dot(tri, lo) + carry
        carry = cum[MOBA_BLOCK - 1:MOBA_BLOCK, :]
        parts = [p.astype(jnp.float32) for p in _split3(cum * LOG2E)]
        fq = jnp.where(slot < 2 * N_SPLIT, 1.0, 0.0)
        fk = jnp.where(slot < N_SPLIT, 1.0, 0.0)
        for s in range(N_SPLIT):
            fq = jnp.where(slot == s, parts[s], fq)
            fk = jnp.where(slot == N_SPLIT + s, -parts[s], fk)
        fqa_ref[0, rows, :] = fq
        fka_ref[0, rows, :] = fk


def _routing(mq, mk, fl, bf_lanes):
    b, t, w = mq.shape
    full = lambda i: (i, 0, 0)
    aux = jax.ShapeDtypeStruct((b, t, LANES), jnp.float32)
    return pl.pallas_call(
        _routing_kernel,
        grid=(b,),
        in_specs=[pl.BlockSpec((1, t, w), full),
                  pl.BlockSpec((1, t, w), full),
                  pl.BlockSpec((1, t, LANES), full),
                  pl.BlockSpec((1, LANES), lambda i: (0, 0))],
        out_specs=[pl.BlockSpec((1, t, LANES), full)] * 3,
        out_shape=[aux] * 3,
        compiler_params=pltpu.CompilerParams(vmem_limit_bytes=VMEM_LIMIT),
        name="routing",
    )(mq, mk, fl, bf_lanes)


def _attn_kernel(*refs, moba):
    if moba:
        q_ref, qa_ref, k_ref, vt_ref, o_ref, kx_ref, vx_ref = refs
    else:
        q_ref, qa_ref, k_ref, vt_ref, ka_ref, o_ref, kx_ref, vx_ref = refs
    tq = SEQ_TILE
    t = k_ref.shape[1]
    nb = t // tq
    step = pl.program_id(2)
    lane = lax.broadcasted_iota(jnp.int32, (1, LANES), 1)
    low = lane < HEAD_DIM
    shift = (LANES - pl.program_id(1) * AUX_GROUP) % LANES
    in_group = (lane % HEAD_DIM) < AUX_GROUP // 2

    @pl.when(step == 0)
    def _():
        k2 = k_ref[0]
        if moba:
            blk = lax.broadcasted_iota(jnp.int32, (t, LANES), 0) // MOBA_BLOCK
            ln = lax.broadcasted_iota(jnp.int32, (t, LANES), 1) % HEAD_DIM
            ka = _bf16(jnp.where(ln == blk, 1.0, 0.0))
        else:
            ka = _bf16(pltpu.roll(ka_ref[0], shift, 1))
        kx_ref[0] = jnp.where(low, k2, ka)
        kx_ref[1] = jnp.where(low, ka, k2)
        for h in range(2):
            vx_ref[h, 0:HEAD_DIM, :] = vt_ref[0, h * HEAD_DIM:(h + 1) * HEAD_DIM, :]
            vx_ref[h, HEAD_DIM:V_ROWS, :] = jnp.ones((V_ROWS - HEAD_DIM, t), jnp.bfloat16)

    key = lax.broadcasted_iota(jnp.int32, (tq, tq), 0)
    qry = lax.broadcasted_iota(jnp.int32, (tq, tq), 1)
    causal = key <= qry

    def scores(n, h):
        rows = slice(n * tq, (n + 1) * tq)
        q2 = q_ref[0, rows, :]
        qa = pltpu.roll(qa_ref[0, rows, :], shift, 1)
        qa = _bf16(jnp.where(in_group, qa, 0.0))
        qx = jnp.where(low, q2, qa) if h == 0 else jnp.where(low, qa, q2)
        sd = jnp.where(causal, _dot_nt(kx_ref[h, rows, :], qx), NEG_INF)
        m = jnp.max(sd, axis=0, keepdims=True)
        sp = None
        if n > 0:
            sp = _dot_nt(kx_ref[h, 0:n * tq, :], qx)
            m = jnp.maximum(m, jnp.max(sp, axis=0, keepdims=True))
        return n, h, sd, sp, m

    def weighted_values(n, h, sd, sp, m):
        rows = slice(n * tq, (n + 1) * tq)
        acc = _dot(vx_ref[h, :, rows], _bf16(jnp.exp2(sd - m)))
        if n > 0:
            acc = acc + _dot(vx_ref[h, :, 0:n * tq], _bf16(jnp.exp2(sp - m)))
        return acc[0:HEAD_DIM, :] / acc[HEAD_DIM:HEAD_DIM + 1, :]

    def run(tiles):
        chains = [(n, h) for n in tiles for h in range(2)]
        outs = {}
        queue = [scores(*c) for c in chains[:ATTN_LOOKAHEAD]]
        for nxt in chains[ATTN_LOOKAHEAD:] + [None] * ATTN_LOOKAHEAD:
            if nxt is not None:
                queue.append(scores(*nxt))
            done = queue.pop(0)
            outs[done[:2]] = weighted_values(*done)
        for n in tiles:
            o_t = jnp.concatenate([outs[(n, 0)], outs[(n, 1)]], axis=0)
            o_ref[0, n * tq:(n + 1) * tq, :] = _bf16(o_t.T)

    pairs = [(nb - 1 - i, i) for i in range(nb // 2)]
    n_static = _attn_steps(nb)
    for s in range(n_static):
        @pl.when(step == s)
        def _(s=s):
            run([n for pair in pairs[s::n_static] for n in pair])


def _attn_steps(nb):
    return max(1, nb // ATTN_TILES_PER_STEP)


def _attention(moba, q, qa, k, vt, ka=None):
    b, t, w = q.shape
    n_pairs = w // LANES
    nb = t // SEQ_TILE
    assert nb % 2 == 0 and (nb // 2) % _attn_steps(nb) == 0
    spec = pl.BlockSpec((1, t, LANES), lambda i, p, j: (i, 0, p))
    spec_t = pl.BlockSpec((1, LANES, t), lambda i, p, j: (i, p, 0))
    spec_aux = pl.BlockSpec((1, t, LANES), lambda i, p, j: (i, 0, 0))
    args = [q, qa, k, vt] if moba else [q, qa, k, vt, ka]
    specs = [spec, spec_aux, spec, spec_t] + ([] if moba else [spec_aux])
    kern = lambda *refs: _attn_kernel(*refs, moba=moba)
    return pl.pallas_call(
        kern,
        grid=(b, n_pairs, _attn_steps(nb)),
        in_specs=specs,
        out_specs=spec,
        out_shape=jax.ShapeDtypeStruct((b, t, w), jnp.bfloat16),
        scratch_shapes=[pltpu.VMEM((2, t, LANES), jnp.bfloat16),
                        pltpu.VMEM((2, V_ROWS, t), jnp.bfloat16)],
        compiler_params=pltpu.CompilerParams(vmem_limit_bytes=VMEM_LIMIT),
        name="moba" if moba else "fox",
    )(*args)


def _ffn_kernel(x_ref, om_ref, of_ref, wo_ref, gt1_ref, g_ref, sc_ref, sh_ref, wup_ref, cv_ref,
                wdn_ref, gt2_ref, out_ref, x1_ref, halo_ref):
    tm = x_ref.shape[1]

    @pl.when(pl.program_id(1) == 0)
    def _():
        halo_ref[...] = jnp.zeros(halo_ref.shape, jnp.float32)

    pm = tm // FFN_PARTS
    slabs = [slice(p * pm, (p + 1) * pm) for p in range(FFN_PARTS)]
    attn = [_dot(jnp.concatenate([om_ref[0, r, :], of_ref[0, r, :]], axis=1), wo_ref[...])
            for r in slabs]

    def normed(p):
        x1 = x_ref[0, slabs[p], :] + gt1_ref[0] * attn[p]
        x1_ref[slabs[p], :] = x1
        return _bf16(_rms_mod(x1, g_ref[...], sc_ref[0], sh_ref[0]))

    def up_proj(hn, c):
        gate_cols = slice(c * FF_CHUNK, (c + 1) * FF_CHUNK)
        val_cols = slice(D_FF + c * FF_CHUNK, D_FF + (c + 1) * FF_CHUNK)
        u = jnp.concatenate([_dot(hn, wup_ref[:, gate_cols]), _dot(hn, wup_ref[:, val_cols])],
                            axis=1)
        return u, gate_cols, val_cols

    hn = normed(0)
    tails = [halo_ref[c] for c in range(N_FF_CHUNKS)]
    for p in range(FFN_PARTS):
        hn_next = None
        hmid = []
        ahead = [up_proj(hn, c) for c in range(FFN_LOOKAHEAD)]
        for c in range(N_FF_CHUNKS):
            if c + FFN_LOOKAHEAD < N_FF_CHUNKS:
                ahead.append(up_proj(hn, c + FFN_LOOKAHEAD))
            u, gate_cols, val_cols = ahead.pop(0)
            ext = jnp.concatenate([tails[c], u], axis=0)
            tails[c] = u[pm - HALO:pm, :]
            cv = jnp.concatenate([cv_ref[:, gate_cols], cv_ref[:, val_cols]], axis=1)
            u1 = pltpu.roll(ext, 1, 0)[HALO:, :]
            u2 = pltpu.roll(ext, 2, 0)[HALO:, :]
            uc = cv[0:1, :] * u2 + cv[1:2, :] * u1 + cv[2:3, :] * u + cv[3:4, :]
            a = uc[:, :FF_CHUNK]
            val = uc[:, FF_CHUNK:]
            half = 0.5 * a
            hmid.append(_bf16((half + half * jnp.tanh(half)) * val))
            if c == FFN_LOOKAHEAD and p + 1 < FFN_PARTS:
                hn_next = normed(p + 1)
        ffn = _dot(jnp.concatenate(hmid, axis=1), wdn_ref[...])
        out_ref[0, slabs[p], :] = x1_ref[slabs[p], :] + gt2_ref[0] * ffn
        hn = hn_next
    for c in range(N_FF_CHUNKS):
        halo_ref[c] = tails[c]


def _ffn(x, om, of, wo, gt1, g_ffn, sc2, sh2, wup, cv, wdn, gt2):
    b, t, d = x.shape
    tm = min(ROW_TILE, t)
    row = lambda i, j: (i, 0, 0)
    tile = lambda i, j: (i, j, 0)
    const2 = lambda i, j: (0, 0)
    once = dict(pipeline_mode=pl.Buffered(1))
    return pl.pallas_call(
        _ffn_kernel,
        grid=(b, t // tm),
        in_specs=[pl.BlockSpec((1, tm, d), tile),
                  pl.BlockSpec((1, tm, MOBA_WIDTH), tile),
                  pl.BlockSpec((1, tm, FOX_WIDTH), tile),
                  pl.BlockSpec((d, d), const2, **once),
                  pl.BlockSpec((1, 1, d), row),
                  pl.BlockSpec((1, d), const2),
                  pl.BlockSpec((1, 1, d), row),
                  pl.BlockSpec((1, 1, d), row),
                  pl.BlockSpec((d, 2 * D_FF), const2, **once),
                  pl.BlockSpec((HALO, 2 * D_FF), const2, **once),
                  pl.BlockSpec((D_FF, d), const2, **once),
                  pl.BlockSpec((1, 1, d), row)],
        out_specs=pl.BlockSpec((1, tm, d), tile),
        out_shape=jax.ShapeDtypeStruct((b, t, d), jnp.float32),
        scratch_shapes=[pltpu.VMEM((tm, d), jnp.float32),
                        pltpu.VMEM((N_FF_CHUNKS, HALO, 2 * FF_CHUNK), jnp.float32)],
        compiler_params=pltpu.CompilerParams(vmem_limit_bytes=VMEM_LIMIT),
        name="ffn",
    )(x, om, of, wo, gt1, g_ffn, sc2, sh2, wup, cv, wdn, gt2)


def _layer(x, c, w_ada, b_ada, g_mix, w_in, b_forget, moba_q_gain, moba_k_gain, fox_q_gain,
           fox_k_gain, w_out, g_ffn, w_up, conv_w, conv_b, w_down):
    b, t, d = x.shape
    nb = t // MOBA_BLOCK
    assert d == D_MODEL and t % SEQ_TILE == 0
    assert AUX_GROUP % nb == 0 and 2 * nb <= AUX_GROUP and 2 * N_SPLIT <= AUX_GROUP // 2

    mod = _adaln(c, w_ada, b_ada).reshape(b, 6, 1, d)
    sh1, sc1, gt1, sh2, sc2, gt2 = [mod[:, i] for i in range(6)]

    lane_head = np.full(LANES, -1)
    for head in range(N_FOX_HEADS):
        lane_head[_group_base(head):_group_base(head) + 2 * N_SPLIT] = head
    used = jnp.asarray(lane_head >= 0)
    w_logits = jnp.where(used[None, :], w_in[:, 6 * MOBA_WIDTH:][:, np.maximum(lane_head, 0)], 0.0)
    gains = jnp.stack([jnp.tile(g, N_MOBA_HEADS) for g in
                       (moba_q_gain, moba_k_gain, fox_q_gain, fox_k_gain)])
    r = np.arange(2 * LANES) // HEAD_DIM
    bd = jnp.asarray(r[:, None] == r[None, :], jnp.bfloat16)
    mq, mk, fq, fk, fl, mvt, fvt = _inproj(x, sc1, sh1, g_mix.reshape(1, d), w_in, w_logits, gains,
                                           _rope_tables(t), bd)

    bf_lanes = jnp.where(used, b_forget[np.maximum(lane_head, 0)], 0.0).reshape(1, LANES)
    mqa, fqa, fka = _routing(mq, mk, fl, bf_lanes)
    o_moba = _attention(True, mq, mqa, mk, mvt)
    o_fox = _attention(False, fq, fqa, fk, fvt, fka)

    cv = jnp.concatenate([conv_w, conv_b[None, :],
                          jnp.zeros((HALO - CONV_WIDTH - 1, 2 * D_FF), jnp.float32)], axis=0)
    return _ffn(x, o_moba, o_fox, _bf16(w_out), gt1, g_ffn.reshape(1, d), sc2, sh2, _bf16(w_up), cv,
                _bf16(w_down), gt2)


def kernel(x, c, w_ada, b_ada, g_mix, w_in, b_forget, moba_q_gain, moba_k_gain, fox_q_gain,
           fox_k_gain, w_out, g_ffn, w_up, conv_w, conv_b, w_down):
    for l in range(w_ada.shape[0]):
        x = _layer(x, c, w_ada[l], b_ada[l], g_mix[l], w_in[l], b_forget[l], moba_q_gain[l],
                   moba_k_gain[l], fox_q_gain[l], fox_k_gain[l], w_out[l], g_ffn[l], w_up[l],
                   conv_w[l], conv_b[l], w_down[l])
    return x
```

```python
import math

import jax
import jax.numpy as jnp
import numpy as np
from jax import lax
from jax.experimental import pallas as pl
from jax.experimental.pallas import tpu as pltpu

D_MODEL = 1024
HEAD_DIM = 64
N_MOBA_HEADS = 8
N_FOX_HEADS = 8
MOBA_WIDTH = N_MOBA_HEADS * HEAD_DIM
FOX_WIDTH = N_FOX_HEADS * HEAD_DIM
MOBA_BLOCK = 256
MOBA_TOPK = 3
ROPE_THETA = 500000.0
ROPE_DIM = HEAD_DIM // 4
D_FF = 2816
CONV_WIDTH = 3
NORM_EPS = 1e-6
NEG_INF = -1e30
LOG2E = math.log2(math.e)
Q_SCALE = HEAD_DIM ** -0.5 * LOG2E
LANES = 128
BF16_ROWS = 16
IN_QKV_COLS = 3 * MOBA_WIDTH + 3 * FOX_WIDTH
V_ROWS = HEAD_DIM + BF16_ROWS
SEQ_TILE = 256
ROW_TILE = 512
ATTN_TILES_PER_STEP = 8
ATTN_LOOKAHEAD = 5
FFN_PARTS = 2
FFN_LOOKAHEAD = 2
FF_CHUNK = 256
N_FF_CHUNKS = D_FF // FF_CHUNK
HALO = 8
N_SPLIT = 3
AUX_GROUP = 16
VMEM_LIMIT = 56 * 1024 * 1024

_NT = (((1,), (1,)), ((), ()))


def _bf16(a):
    return a.astype(jnp.bfloat16)


def _dot(a, b):
    return jnp.dot(a, b, preferred_element_type=jnp.float32)


def _dot_nt(a, b):
    return lax.dot_general(a, b, _NT, preferred_element_type=jnp.float32)


def _sigmoid(a):
    return 1.0 / (1.0 + jnp.exp(-a))


def _adaln_kernel(c_ref, w_ref, b_ref, o_ref):
    c = c_ref[...]
    s = c * _sigmoid(c)
    o_ref[...] = _dot(_bf16(s), _bf16(w_ref[...])) + b_ref[...]


def _adaln(c, w_ada, b_ada):
    b, d = c.shape
    n = w_ada.shape[1]
    tn = 1536
    return pl.pallas_call(
        _adaln_kernel,
        grid=(n // tn,),
        in_specs=[pl.BlockSpec((b, d), lambda j: (0, 0)),
                  pl.BlockSpec((d, tn), lambda j: (0, j)),
                  pl.BlockSpec((1, tn), lambda j: (0, j))],
        out_specs=pl.BlockSpec((b, tn), lambda j: (0, j)),
        out_shape=jax.ShapeDtypeStruct((b, n), jnp.float32),
        compiler_params=pltpu.CompilerParams(vmem_limit_bytes=VMEM_LIMIT),
        name="adaln",
    )(c, w_ada, b_ada.reshape(1, n))


def _rms_mod(x, g, sc, sh):
    ms = jnp.mean(x * x, axis=-1, keepdims=True)
    y = x * lax.rsqrt(ms + NORM_EPS)
    return (y * g) * (1.0 + sc) + sh


def _head_norm(p, gain, bd):
    sq = _bf16(p * p)
    half = 2 * LANES
    ss = jnp.concatenate([_dot(sq[:, :half], bd), _dot(sq[:, half:], bd)], axis=1)
    return (p * lax.rsqrt(ss * (1.0 / HEAD_DIM) + NORM_EPS)) * gain


def _rope(y, c, sa, sb):
    half = ROPE_DIM // 2
    outs = []
    for i in range(y.shape[1] // LANES):
        yc = y[:, i * LANES:(i + 1) * LANES]
        up = pltpu.roll(yc, LANES - half, 1)
        dn = pltpu.roll(yc, half, 1)
        outs.append(yc * c + up * sa + dn * sb)
    return jnp.concatenate(outs, axis=1)


def _inproj_kernel(x_ref, sc_ref, sh_ref, g_ref, wint_ref, gains_ref, rc_ref, rsa_ref, rsb_ref,
                   bd_ref, mq_ref, mk_ref, fq_ref, fk_ref, fl_ref, mvt_ref, fvt_ref, w_ref):
    w = MOBA_WIDTH
    n_groups = IN_QKV_COLS // w

    @pl.when((pl.program_id(0) == 0) & (pl.program_id(1) == 0))
    def _():
        for i in range(n_groups):
            w_ref[i * w:(i + 1) * w, :] = _bf16(wint_ref[i * w:(i + 1) * w, :])
        logit_rows = wint_ref[IN_QKV_COLS:IN_QKV_COLS + N_FOX_HEADS, :]
        r = lax.broadcasted_iota(jnp.int32, (LANES, 1), 0)
        r_head = jnp.where(r % AUX_GROUP < 2 * N_SPLIT, _group_head(r), -1)
        wlog = jnp.zeros((LANES, logit_rows.shape[1]), jnp.float32)
        for head in range(N_FOX_HEADS):
            wlog = jnp.where(r_head == head, logit_rows[head:head + 1, :], wlog)
        w_ref[IN_QKV_COLS:IN_QKV_COLS + LANES, :] = _bf16(wlog)

    hn = _bf16(_rms_mod(x_ref[0], g_ref[...], sc_ref[0], sh_ref[0]))
    bd = bd_ref[...]
    c, sa, sb = rc_ref[...], rsa_ref[...], rsb_ref[...]
    proj = lambda i: _dot_nt(hn, w_ref[i * w:(i + 1) * w, :])

    p_mq = proj(0)
    p_mk = proj(1)
    mq = _rope(_head_norm(p_mq, gains_ref[0:1, :], bd), c, sa, sb)
    mq_ref[0] = _bf16(mq * Q_SCALE)
    p_fq = proj(3)
    mk = _rope(_head_norm(p_mk, gains_ref[1:2, :], bd), c, sa, sb)
    mk_ref[0] = _bf16(mk)
    p_fk = proj(4)
    fq = _head_norm(p_fq, gains_ref[2:3, :], bd)
    fq_ref[0] = _bf16(fq * Q_SCALE)
    fl_ref[0] = _dot_nt(hn, w_ref[IN_QKV_COLS:IN_QKV_COLS + LANES, :])
    mvt = _dot_nt(w_ref[2 * w:3 * w, :], hn)
    fk = _head_norm(p_fk, gains_ref[3:4, :], bd)
    fk_ref[0] = _bf16(fk)
    mvt_ref[0] = _bf16(mvt)
    fvt_ref[0] = _bf16(_dot_nt(w_ref[5 * w:6 * w, :], hn))


def _rope_tables(t):
    half = ROPE_DIM // 2
    inv_freq = np.power(ROPE_THETA, -2.0 * np.arange(half, dtype=np.float64) / ROPE_DIM)
    ang = np.arange(t, dtype=np.float64)[:, None] * inv_freq[None, :]
    cos, sin = np.cos(ang), np.sin(ang)
    d = np.arange(LANES) % HEAD_DIM
    first = (d < half)[None, :]
    second = ((d >= half) & (d < ROPE_DIM))[None, :]
    idx = np.where(d < ROPE_DIM, d % half, 0)
    cos_l, sin_l = cos[:, idx], sin[:, idx]
    c = np.where(first | second, cos_l, 1.0)
    sa = np.where(first, -sin_l, 0.0)
    sb = np.where(second, sin_l, 0.0)
    return tuple(jnp.asarray(a, jnp.float32) for a in (c, sa, sb))


def _inproj(x, sc1, sh1, g_mix, w_in_t, gains, tables, bd):
    b, t, d = x.shape
    tm = min(ROW_TILE, t)
    w = MOBA_WIDTH
    row = lambda i, j: (i, 0, 0)
    tile = lambda i, j: (i, j, 0)
    tile_t = lambda i, j: (i, 0, j)
    const = lambda i, j: (0, 0)
    wide = jax.ShapeDtypeStruct((b, t, w), jnp.bfloat16)
    wide_t = jax.ShapeDtypeStruct((b, w, t), jnp.bfloat16)
    out_specs = ([pl.BlockSpec((1, tm, w), tile)] * 4 + [pl.BlockSpec((1, tm, LANES), tile)]
                 + [pl.BlockSpec((1, w, tm), tile_t)] * 2)
    return pl.pallas_call(
        _inproj_kernel,
        grid=(b, t // tm),
        in_specs=[pl.BlockSpec((1, tm, d), tile),
                  pl.BlockSpec((1, 1, d), row),
                  pl.BlockSpec((1, 1, d), row),
                  pl.BlockSpec((1, d), const),
                  pl.BlockSpec(w_in_t.shape, const, pipeline_mode=pl.Buffered(1)),
                  pl.BlockSpec((4, w), const),
                  pl.BlockSpec((tm, LANES), lambda i, j: (j, 0)),
                  pl.BlockSpec((tm, LANES), lambda i, j: (j, 0)),
                  pl.BlockSpec((tm, LANES), lambda i, j: (j, 0)),
                  pl.BlockSpec((2 * LANES, 2 * LANES), const)],
        out_specs=out_specs,
        out_shape=[wide] * 4 + [jax.ShapeDtypeStruct((b, t, LANES), jnp.float32)] + [wide_t] * 2,
        scratch_shapes=[pltpu.VMEM((IN_QKV_COLS + LANES, d), jnp.bfloat16)],
        compiler_params=pltpu.CompilerParams(vmem_limit_bytes=VMEM_LIMIT),
        name="inproj",
    )(x, sc1, sh1, g_mix, w_in_t, gains, *tables, bd)


def _split3(a):
    hi = _bf16(a)
    r1 = a - hi.astype(jnp.float32)
    mid = _bf16(r1)
    lo = _bf16(r1 - mid.astype(jnp.float32))
    return hi, mid, lo


def _group_head(lane_idx):
    return 2 * ((lane_idx % HEAD_DIM) // AUX_GROUP) + jnp.where(lane_idx < HEAD_DIM, 1, 0)


def _routing_kernel(mq_ref, mk_ref, fl_ref, bf_ref, mqa_ref, fqa_ref, fka_ref):
    t = mq_ref.shape[1]
    nb = t // MOBA_BLOCK
    q = mq_ref[0]
    k = mk_ref[0]
    blk_of_col = lax.broadcasted_iota(jnp.int32, (nb, t), 1) // MOBA_BLOCK
    blk_row = lax.broadcasted_iota(jnp.int32, (nb, t), 0)
    ind = _bf16(jnp.where(blk_of_col == blk_row, 1.0, 0.0))
    kmean = _dot(ind, k) * (1.0 / MOBA_BLOCK)
    kmt = jnp.concatenate([kmean] * (LANES // nb), axis=0)
    r_head = _group_head(lax.broadcasted_iota(jnp.int32, kmt.shape, 0))
    c_head = lax.broadcasted_iota(jnp.int32, kmt.shape, 1) // HEAD_DIM
    kmt = jnp.where(r_head == c_head, kmt, 0.0)
    k_hi = _bf16(kmt)
    k_lo = _bf16(kmt - k_hi.astype(jnp.float32))
    gate = (_dot_nt(q, k_hi) + _dot_nt(q, k_lo)) * (1.0 / Q_SCALE)

    lane = lax.broadcasted_iota(jnp.int32, (1, LANES), 1)
    j = lane % nb
    own = lax.broadcasted_iota(jnp.int32, (t, 1), 0) // MOBA_BLOCK
    past = j < own
    g = jnp.where(past, gate, -jnp.inf)
    rank = jnp.zeros(gate.shape, jnp.int32)
    for d in range(1, nb):
        other = pltpu.roll(g, LANES - d, 1)
        first = ((j + d) % nb) < j
        beats = (other > g) | (first & (other == g))
        rank = rank + jnp.where(beats, 1, 0)
    attend = (past & (rank < MOBA_TOPK)) | (j == own)
    mqa_ref[0] = jnp.where(attend, 0.0, NEG_INF)

    lane_head = _group_head(lane)
    bias = jnp.zeros((1, LANES), jnp.float32)
    for head in range(N_FOX_HEADS):
        bias = jnp.where(lane_head == head, bf_ref[:, head:head + 1], bias)
    z = fl_ref[0] + bias
    logf = -(jnp.maximum(-z, 0.0) + jnp.log1p(jnp.exp(-jnp.abs(z))))
    rr = lax.broadcasted_iota(jnp.int32, (MOBA_BLOCK, MOBA_BLOCK), 0)
    cc = lax.broadcasted_iota(jnp.int32, (MOBA_BLOCK, MOBA_BLOCK), 1)
    tri = _bf16(jnp.where(cc <= rr, 1.0, 0.0))
    slot = lane % AUX_GROUP
    carry = jnp.zeros((1, LANES), jnp.float32)
    for i in range(nb):
        rows = slice(i * MOBA_BLOCK, (i + 1) * MOBA_BLOCK)
        hi, mid, lo = _split3(logf[rows, :])
        cum = (_dot(tri, hi) + _dot(tri, mid)) + _dot(tri, lo) + carry
        carry = cum[MOBA_BLOCK - 1:MOBA_BLOCK, :]
        parts = [p.astype(jnp.float32) for p in _split3(cum * LOG2E)]
        fq = jnp.where(slot < 2 * N_SPLIT, 1.0, 0.0)
        fk = jnp.where(slot < N_SPLIT, 1.0, 0.0)
        for s in range(N_SPLIT):
            fq = jnp.where(slot == s, parts[s], fq)
            fk = jnp.where(slot == N_SPLIT + s, -parts[s], fk)
        fqa_ref[0, rows, :] = fq
        fka_ref[0, rows, :] = fk


def _routing(mq, mk, fl, b_forget):
    b, t, w = mq.shape
    full = lambda i: (i, 0, 0)
    aux = jax.ShapeDtypeStruct((b, t, LANES), jnp.float32)
    return pl.pallas_call(
        _routing_kernel,
        grid=(b,),
        in_specs=[pl.BlockSpec((1, t, w), full),
                  pl.BlockSpec((1, t, w), full),
                  pl.BlockSpec((1, t, LANES), full),
                  pl.BlockSpec((1, N_FOX_HEADS), lambda i: (0, 0))],
        out_specs=[pl.BlockSpec((1, t, LANES), full)] * 3,
        out_shape=[aux] * 3,
        compiler_params=pltpu.CompilerParams(vmem_limit_bytes=VMEM_LIMIT),
        name="routing",
    )(mq, mk, fl, b_forget)


def _attn_kernel(*refs, moba):
    if moba:
        q_ref, qa_ref, k_ref, vt_ref, o_ref, kx_ref, vx_ref = refs
    else:
        q_ref, qa_ref, k_ref, vt_ref, ka_ref, o_ref, kx_ref, vx_ref = refs
    tq = SEQ_TILE
    t = k_ref.shape[1]
    nb = t // tq
    step = pl.program_id(2)
    lane = lax.broadcasted_iota(jnp.int32, (1, LANES), 1)
    low = lane < HEAD_DIM
    shift = (LANES - pl.program_id(1) * AUX_GROUP) % LANES
    in_group = (lane % HEAD_DIM) < AUX_GROUP // 2

    @pl.when(step == 0)
    def _():
        k2 = k_ref[0]
        if moba:
            blk = lax.broadcasted_iota(jnp.int32, (t, LANES), 0) // MOBA_BLOCK
            ln = lax.broadcasted_iota(jnp.int32, (t, LANES), 1) % HEAD_DIM
            ka = _bf16(jnp.where(ln == blk, 1.0, 0.0))
        else:
            ka = _bf16(pltpu.roll(ka_ref[0], shift, 1))
        kx_ref[0] = jnp.where(low, k2, ka)
        kx_ref[1] = jnp.where(low, ka, k2)
        for h in range(2):
            vx_ref[h, 0:HEAD_DIM, :] = vt_ref[0, h * HEAD_DIM:(h + 1) * HEAD_DIM, :]
            vx_ref[h, HEAD_DIM:V_ROWS, :] = jnp.ones((V_ROWS - HEAD_DIM, t), jnp.bfloat16)

    key = lax.broadcasted_iota(jnp.int32, (tq, tq), 0)
    qry = lax.broadcasted_iota(jnp.int32, (tq, tq), 1)
    causal = key <= qry

    def scores(n, h):
        rows = slice(n * tq, (n + 1) * tq)
        q2 = q_ref[0, rows, :]
        qa = pltpu.roll(qa_ref[0, rows, :], shift, 1)
        qa = _bf16(jnp.where(in_group, qa, 0.0))
        qx = jnp.where(low, q2, qa) if h == 0 else jnp.where(low, qa, q2)
        sd = jnp.where(causal, _dot_nt(kx_ref[h, rows, :], qx), NEG_INF)
        m = jnp.max(sd, axis=0, keepdims=True)
        sp = None
        if n > 0:
            sp = _dot_nt(kx_ref[h, 0:n * tq, :], qx)
            m = jnp.maximum(m, jnp.max(sp, axis=0, keepdims=True))
        return n, h, sd, sp, m

    def weighted_values(n, h, sd, sp, m):
        rows = slice(n * tq, (n + 1) * tq)
        acc = _dot(vx_ref[h, :, rows], _bf16(jnp.exp2(sd - m)))
        if n > 0:
            acc = acc + _dot(vx_ref[h, :, 0:n * tq], _bf16(jnp.exp2(sp - m)))
        return acc[0:HEAD_DIM, :] / acc[HEAD_DIM:HEAD_DIM + 1, :]

    def run(tiles):
        chains = [(n, h) for n in tiles for h in range(2)]
        outs = {}
        queue = [scores(*c) for c in chains[:ATTN_LOOKAHEAD]]
        for nxt in chains[ATTN_LOOKAHEAD:] + [None] * ATTN_LOOKAHEAD:
            if nxt is not None:
                queue.append(scores(*nxt))
            done = queue.pop(0)
            outs[done[:2]] = weighted_values(*done)
        for n in tiles:
            o_t = jnp.concatenate([outs[(n, 0)], outs[(n, 1)]], axis=0)
            o_ref[0, n * tq:(n + 1) * tq, :] = _bf16(o_t.T)

    pairs = [(nb - 1 - i, i) for i in range(nb // 2)]
    n_static = _attn_steps(nb)
    for s in range(n_static):
        @pl.when(step == s)
        def _(s=s):
            run([n for pair in pairs[s::n_static] for n in pair])


def _attn_steps(nb):
    return max(1, nb // ATTN_TILES_PER_STEP)


def _attention(moba, q, qa, k, vt, ka=None):
    b, t, w = q.shape
    n_pairs = w // LANES
    nb = t // SEQ_TILE
    assert nb % 2 == 0 and (nb // 2) % _attn_steps(nb) == 0
    spec = pl.BlockSpec((1, t, LANES), lambda i, p, j: (i, 0, p))
    spec_t = pl.BlockSpec((1, LANES, t), lambda i, p, j: (i, p, 0))
    spec_aux = pl.BlockSpec((1, t, LANES), lambda i, p, j: (i, 0, 0))
    args = [q, qa, k, vt] if moba else [q, qa, k, vt, ka]
    specs = [spec, spec_aux, spec, spec_t] + ([] if moba else [spec_aux])
    kern = lambda *refs: _attn_kernel(*refs, moba=moba)
    return pl.pallas_call(
        kern,
        grid=(b, n_pairs, _attn_steps(nb)),
        in_specs=specs,
        out_specs=spec,
        out_shape=jax.ShapeDtypeStruct((b, t, w), jnp.bfloat16),
        scratch_shapes=[pltpu.VMEM((2, t, LANES), jnp.bfloat16),
                        pltpu.VMEM((2, V_ROWS, t), jnp.bfloat16)],
        compiler_params=pltpu.CompilerParams(vmem_limit_bytes=VMEM_LIMIT),
        name="moba" if moba else "fox",
    )(*args)


def _ffn_kernel(x_ref, om_ref, of_ref, wo_ref, gt1_ref, g_ref, sc_ref, sh_ref, wup_ref, cv_ref,
                wdn_ref, gt2_ref, out_ref, x1_ref, halo_ref):
    tm = x_ref.shape[1]

    @pl.when(pl.program_id(1) == 0)
    def _():
        halo_ref[...] = jnp.zeros(halo_ref.shape, jnp.float32)

    pm = tm // FFN_PARTS
    slabs = [slice(p * pm, (p + 1) * pm) for p in range(FFN_PARTS)]
    attn = [_dot(jnp.concatenate([om_ref[0, r, :], of_ref[0, r, :]], axis=1), wo_ref[...])
            for r in slabs]

    def normed(p):
        x1 = x_ref[0, slabs[p], :] + gt1_ref[0] * attn[p]
        x1_ref[slabs[p], :] = x1
        return _bf16(_rms_mod(x1, g_ref[...], sc_ref[0], sh_ref[0]))

    def up_proj(hn, c):
        gate_cols = slice(c * FF_CHUNK, (c + 1) * FF_CHUNK)
        val_cols = slice(D_FF + c * FF_CHUNK, D_FF + (c + 1) * FF_CHUNK)
        u = jnp.concatenate([_dot(hn, wup_ref[:, gate_cols]), _dot(hn, wup_ref[:, val_cols])],
                            axis=1)
        return u, gate_cols, val_cols

    hn = normed(0)
    tails = [halo_ref[c] for c in range(N_FF_CHUNKS)]
    for p in range(FFN_PARTS):
        hn_next = None
        hmid = []
        ahead = [up_proj(hn, c) for c in range(FFN_LOOKAHEAD)]
        for c in range(N_FF_CHUNKS):
            if c + FFN_LOOKAHEAD < N_FF_CHUNKS:
                ahead.append(up_proj(hn, c + FFN_LOOKAHEAD))
            u, gate_cols, val_cols = ahead.pop(0)
            ext = jnp.concatenate([tails[c], u], axis=0)
            tails[c] = u[pm - HALO:pm, :]
            cv = jnp.concatenate([cv_ref[:, gate_cols], cv_ref[:, val_cols]], axis=1)
            u1 = pltpu.roll(ext, 1, 0)[HALO:, :]
            u2 = pltpu.roll(ext, 2, 0)[HALO:, :]
            uc = cv[0:1, :] * u2 + cv[1:2, :] * u1 + cv[2:3, :] * u + cv[3:4, :]
            a = uc[:, :FF_CHUNK]
            val = uc[:, FF_CHUNK:]
            half = 0.5 * a
            hmid.append(_bf16((half + half * jnp.tanh(half)) * val))
            if c == FFN_LOOKAHEAD and p + 1 < FFN_PARTS:
                hn_next = normed(p + 1)
        ffn = _dot(jnp.concatenate(hmid, axis=1), wdn_ref[...])
        out_ref[0, slabs[p], :] = x1_ref[slabs[p], :] + gt2_ref[0] * ffn
        hn = hn_next
    for c in range(N_FF_CHUNKS):
        halo_ref[c] = tails[c]


def _ffn(x, om, of, wo, gt1, g_ffn, sc2, sh2, wup, cv, wdn, gt2):
    b, t, d = x.shape
    tm = min(ROW_TILE, t)
    row = lambda i, j: (i, 0, 0)
    tile = lambda i, j: (i, j, 0)
    const2 = lambda i, j: (0, 0)
    once = dict(pipeline_mode=pl.Buffered(1))
    return pl.pallas_call(
        _ffn_kernel,
        grid=(b, t // tm),
        in_specs=[pl.BlockSpec((1, tm, d), tile),
                  pl.BlockSpec((1, tm, MOBA_WIDTH), tile),
                  pl.BlockSpec((1, tm, FOX_WIDTH), tile),
                  pl.BlockSpec((d, d), const2, **once),
                  pl.BlockSpec((1, 1, d), row),
                  pl.BlockSpec((1, d), const2),
                  pl.BlockSpec((1, 1, d), row),
                  pl.BlockSpec((1, 1, d), row),
                  pl.BlockSpec((d, 2 * D_FF), const2, **once),
                  pl.BlockSpec((HALO, 2 * D_FF), const2, **once),
                  pl.BlockSpec((D_FF, d), const2, **once),
                  pl.BlockSpec((1, 1, d), row)],
        out_specs=pl.BlockSpec((1, tm, d), tile),
        out_shape=jax.ShapeDtypeStruct((b, t, d), jnp.float32),
        scratch_shapes=[pltpu.VMEM((tm, d), jnp.float32),
                        pltpu.VMEM((N_FF_CHUNKS, HALO, 2 * FF_CHUNK), jnp.float32)],
        compiler_params=pltpu.CompilerParams(vmem_limit_bytes=VMEM_LIMIT),
        name="ffn",
    )(x, om, of, wo, gt1, g_ffn, sc2, sh2, wup, cv, wdn, gt2)


def _layer(x, c, w_ada, b_ada, g_mix, w_in, b_forget, moba_q_gain, moba_k_gain, fox_q_gain,
           fox_k_gain, w_out, g_ffn, w_up, conv_w, conv_b, w_down):
    b, t, d = x.shape
    nb = t // MOBA_BLOCK
    assert d == D_MODEL and t % SEQ_TILE == 0
    assert AUX_GROUP % nb == 0 and 2 * nb <= AUX_GROUP and 2 * N_SPLIT <= AUX_GROUP // 2

    mod = _adaln(c, w_ada, b_ada).reshape(b, 6, 1, d)
    sh1, sc1, gt1, sh2, sc2, gt2 = [mod[:, i] for i in range(6)]

    assert w_in.shape == (d, IN_QKV_COLS + N_FOX_HEADS)
    gains = jnp.stack([jnp.tile(g, N_MOBA_HEADS) for g in
                       (moba_q_gain, moba_k_gain, fox_q_gain, fox_k_gain)])
    r = np.arange(2 * LANES) // HEAD_DIM
    bd = jnp.asarray(r[:, None] == r[None, :], jnp.bfloat16)
    mq, mk, fq, fk, fl, mvt, fvt = _inproj(x, sc1, sh1, g_mix.reshape(1, d), w_in.T, gains,
                                           _rope_tables(t), bd)

    mqa, fqa, fka = _routing(mq, mk, fl, b_forget.reshape(1, N_FOX_HEADS))
    o_moba = _attention(True, mq, mqa, mk, mvt)
    o_fox = _attention(False, fq, fqa, fk, fvt, fka)

    cv = jnp.concatenate([conv_w, conv_b[None, :],
                          jnp.zeros((HALO - CONV_WIDTH - 1, 2 * D_FF), jnp.float32)], axis=0)
    return _ffn(x, o_moba, o_fox, _bf16(w_out), gt1, g_ffn.reshape(1, d), sc2, sh2, _bf16(w_up), cv,
                _bf16(w_down), gt2)


def kernel(x, c, w_ada, b_ada, g_mix, w_in, b_forget, moba_q_gain, moba_k_gain, fox_q_gain,
           fox_k_gain, w_out, g_ffn, w_up, conv_w, conv_b, w_down):
    for l in range(w_ada.shape[0]):
        x = _layer(x, c, w_ada[l], b_ada[l], g_mix[l], w_in[l], b_forget[l], moba_q_gain[l],
                   moba_k_gain[l], fox_q_gain[l], fox_k_gain[l], w_out[l], g_ffn[l], w_up[l],
                   conv_w[l], conv_b[l], w_down[l])
    return x
```

```python
import math

import jax
import jax.numpy as jnp
import numpy as np
from jax import lax
from jax.experimental import pallas as pl
from jax.experimental.pallas import tpu as pltpu

D_MODEL = 1024
HEAD_DIM = 64
N_MOBA_HEADS = 8
N_FOX_HEADS = 8
MOBA_WIDTH = N_MOBA_HEADS * HEAD_DIM
FOX_WIDTH = N_FOX_HEADS * HEAD_DIM
MOBA_BLOCK = 256
MOBA_TOPK = 3
ROPE_THETA = 500000.0
ROPE_DIM = HEAD_DIM // 4
D_FF = 2816
CONV_WIDTH = 3
NORM_EPS = 1e-6
NEG_INF = -1e30
LOG2E = math.log2(math.e)
Q_SCALE = HEAD_DIM ** -0.5 * LOG2E
LANES = 128
BF16_ROWS = 16
IN_QKV_COLS = 3 * MOBA_WIDTH + 3 * FOX_WIDTH
V_ROWS = HEAD_DIM + BF16_ROWS
SEQ_TILE = 256
ROW_TILE = 512
ATTN_TILES_PER_STEP = 8
ATTN_LOOKAHEAD = 5
FFN_PARTS = 2
FFN_LOOKAHEAD = 2
FF_CHUNK = 256
N_FF_CHUNKS = D_FF // FF_CHUNK
HALO = 8
N_SPLIT = 3
AUX_GROUP = 16
VMEM_LIMIT = 56 * 1024 * 1024

_NT = (((1,), (1,)), ((), ()))


def _bf16(a):
    return a.astype(jnp.bfloat16)


def _dot(a, b):
    return jnp.dot(a, b, preferred_element_type=jnp.float32)


def _dot_nt(a, b):
    return lax.dot_general(a, b, _NT, preferred_element_type=jnp.float32)


def _sigmoid(a):
    return 1.0 / (1.0 + jnp.exp(-a))


def _adaln_kernel(c_ref, w_ref, b_ref, o_ref):
    c = c_ref[...]
    s = c * _sigmoid(c)
    o_ref[...] = _dot(_bf16(s), _bf16(w_ref[...])) + b_ref[...]


def _adaln(c, w_ada, b_ada):
    b, d = c.shape
    n = w_ada.shape[1]
    tn = 1536
    return pl.pallas_call(
        _adaln_kernel,
        grid=(n // tn,),
        in_specs=[pl.BlockSpec((b, d), lambda j: (0, 0)),
                  pl.BlockSpec((d, tn), lambda j: (0, j)),
                  pl.BlockSpec((1, tn), lambda j: (0, j))],
        out_specs=pl.BlockSpec((b, tn), lambda j: (0, j)),
        out_shape=jax.ShapeDtypeStruct((b, n), jnp.float32),
        compiler_params=pltpu.CompilerParams(vmem_limit_bytes=VMEM_LIMIT),
        name="adaln",
    )(c, w_ada, b_ada.reshape(1, n))


def _rms_mod(x, g, sc, sh):
    ms = jnp.mean(x * x, axis=-1, keepdims=True)
    y = x * lax.rsqrt(ms + NORM_EPS)
    return (y * g) * (1.0 + sc) + sh


def _head_norm(p, gain, bd):
    sq = _bf16(p * p)
    half = 2 * LANES
    ss = jnp.concatenate([_dot(sq[:, :half], bd), _dot(sq[:, half:], bd)], axis=1)
    return (p * lax.rsqrt(ss * (1.0 / HEAD_DIM) + NORM_EPS)) * gain


def _rope(y, c, sa, sb):
    half = ROPE_DIM // 2
    outs = []
    for i in range(y.shape[1] // LANES):
        yc = y[:, i * LANES:(i + 1) * LANES]
        up = pltpu.roll(yc, LANES - half, 1)
        dn = pltpu.roll(yc, half, 1)
        outs.append(yc * c + up * sa + dn * sb)
    return jnp.concatenate(outs, axis=1)


def _inproj_kernel(x_ref, sc_ref, sh_ref, g_ref, wint_ref, gains_ref, rc_ref, rsa_ref, rsb_ref,
                   bd_ref, *refs):
    n_cast = (len(refs) - 8) // 2
    cast_in, refs = refs[:n_cast], refs[n_cast:]
    mq_ref, mk_ref, fq_ref, fk_ref, fl_ref, mvt_ref, fvt_ref = refs[:7]
    cast_out, w_ref = refs[7:7 + n_cast], refs[-1]
    for src, dst in zip(cast_in, cast_out):
        dst[...] = _bf16(src[...])
    w = MOBA_WIDTH
    n_groups = IN_QKV_COLS // w

    @pl.when((pl.program_id(0) == 0) & (pl.program_id(1) == 0))
    def _():
        for i in range(n_groups):
            w_ref[i * w:(i + 1) * w, :] = _bf16(wint_ref[i * w:(i + 1) * w, :])
        logit_rows = wint_ref[IN_QKV_COLS:IN_QKV_COLS + N_FOX_HEADS, :]
        r = lax.broadcasted_iota(jnp.int32, (LANES, 1), 0)
        r_head = jnp.where(r % AUX_GROUP < 2 * N_SPLIT, _group_head(r), -1)
        wlog = jnp.zeros((LANES, logit_rows.shape[1]), jnp.float32)
        for head in range(N_FOX_HEADS):
            wlog = jnp.where(r_head == head, logit_rows[head:head + 1, :], wlog)
        w_ref[IN_QKV_COLS:IN_QKV_COLS + LANES, :] = _bf16(wlog)

    hn = _bf16(_rms_mod(x_ref[0], g_ref[...], sc_ref[0], sh_ref[0]))
    bd = bd_ref[...]
    c, sa, sb = rc_ref[...], rsa_ref[...], rsb_ref[...]
    proj = lambda i: _dot_nt(hn, w_ref[i * w:(i + 1) * w, :])

    p_mq = proj(0)
    p_mk = proj(1)
    mq = _rope(_head_norm(p_mq, gains_ref[0:1, :], bd), c, sa, sb)
    mq_ref[0] = _bf16(mq * Q_SCALE)
    p_fq = proj(3)
    mk = _rope(_head_norm(p_mk, gains_ref[1:2, :], bd), c, sa, sb)
    mk_ref[0] = _bf16(mk)
    p_fk = proj(4)
    fq = _head_norm(p_fq, gains_ref[2:3, :], bd)
    fq_ref[0] = _bf16(fq * Q_SCALE)
    fl_ref[0] = _dot_nt(hn, w_ref[IN_QKV_COLS:IN_QKV_COLS + LANES, :])
    mvt = _dot_nt(w_ref[2 * w:3 * w, :], hn)
    fk = _head_norm(p_fk, gains_ref[3:4, :], bd)
    fk_ref[0] = _bf16(fk)
    mvt_ref[0] = _bf16(mvt)
    fvt_ref[0] = _bf16(_dot_nt(w_ref[5 * w:6 * w, :], hn))


def _rope_tables(t):
    half = ROPE_DIM // 2
    inv_freq = np.power(ROPE_THETA, -2.0 * np.arange(half, dtype=np.float64) / ROPE_DIM)
    ang = np.arange(t, dtype=np.float64)[:, None] * inv_freq[None, :]
    cos, sin = np.cos(ang), np.sin(ang)
    d = np.arange(LANES) % HEAD_DIM
    first = (d < half)[None, :]
    second = ((d >= half) & (d < ROPE_DIM))[None, :]
    idx = np.where(d < ROPE_DIM, d % half, 0)
    cos_l, sin_l = cos[:, idx], sin[:, idx]
    c = np.where(first | second, cos_l, 1.0)
    sa = np.where(first, -sin_l, 0.0)
    sb = np.where(second, sin_l, 0.0)
    return tuple(jnp.asarray(a, jnp.float32) for a in (c, sa, sb))


def _slab_rows(n_rows, n_steps):
    tiles = n_rows // BF16_ROWS
    assert tiles * BF16_ROWS == n_rows
    n_slabs = max(g for g in range(1, tiles + 1) if tiles % g == 0 and g <= n_steps)
    return n_rows // n_slabs


def _inproj(x, sc1, sh1, g_mix, w_in_t, gains, tables, bd, to_cast):
    b, t, d = x.shape
    tm = min(ROW_TILE, t)
    w = MOBA_WIDTH
    nt = t // tm
    cast_specs, cast_shapes = [], []
    for a in to_cast:
        rows = _slab_rows(a.shape[0], b * nt)
        last = a.shape[0] // rows - 1
        slab = lambda i, j, last=last: (jnp.minimum(i * nt + j, last), 0)
        cast_specs.append(pl.BlockSpec((rows, a.shape[1]), slab))
        cast_shapes.append(jax.ShapeDtypeStruct(a.shape, jnp.bfloat16))
    row = lambda i, j: (i, 0, 0)
    tile = lambda i, j: (i, j, 0)
    tile_t = lambda i, j: (i, 0, j)
    const = lambda i, j: (0, 0)
    wide = jax.ShapeDtypeStruct((b, t, w), jnp.bfloat16)
    wide_t = jax.ShapeDtypeStruct((b, w, t), jnp.bfloat16)
    out_specs = ([pl.BlockSpec((1, tm, w), tile)] * 4 + [pl.BlockSpec((1, tm, LANES), tile)]
                 + [pl.BlockSpec((1, w, tm), tile_t)] * 2)
    return pl.pallas_call(
        _inproj_kernel,
        grid=(b, t // tm),
        in_specs=[pl.BlockSpec((1, tm, d), tile),
                  pl.BlockSpec((1, 1, d), row),
                  pl.BlockSpec((1, 1, d), row),
                  pl.BlockSpec((1, d), const),
                  pl.BlockSpec(w_in_t.shape, const, pipeline_mode=pl.Buffered(1)),
                  pl.BlockSpec((4, w), const),
                  pl.BlockSpec((tm, LANES), lambda i, j: (j, 0)),
                  pl.BlockSpec((tm, LANES), lambda i, j: (j, 0)),
                  pl.BlockSpec((tm, LANES), lambda i, j: (j, 0)),
                  pl.BlockSpec((2 * LANES, 2 * LANES), const)] + cast_specs,
        out_specs=out_specs + cast_specs,
        out_shape=([wide] * 4 + [jax.ShapeDtypeStruct((b, t, LANES), jnp.float32)] + [wide_t] * 2
                   + cast_shapes),
        scratch_shapes=[pltpu.VMEM((IN_QKV_COLS + LANES, d), jnp.bfloat16)],
        compiler_params=pltpu.CompilerParams(vmem_limit_bytes=VMEM_LIMIT),
        name="inproj",
    )(x, sc1, sh1, g_mix, w_in_t, gains, *tables, bd, *to_cast)


def _split3(a):
    hi = _bf16(a)
    r1 = a - hi.astype(jnp.float32)
    mid = _bf16(r1)
    lo = _bf16(r1 - mid.astype(jnp.float32))
    return hi, mid, lo


def _group_head(lane_idx):
    return 2 * ((lane_idx % HEAD_DIM) // AUX_GROUP) + jnp.where(lane_idx < HEAD_DIM, 1, 0)


def _routing_kernel(mq_ref, mk_ref, fl_ref, bf_ref, mqa_ref, fqa_ref, fka_ref):
    t = mq_ref.shape[1]
    nb = t // MOBA_BLOCK
    q = mq_ref[0]
    k = mk_ref[0]
    blk_of_col = lax.broadcasted_iota(jnp.int32, (nb, t), 1) // MOBA_BLOCK
    blk_row = lax.broadcasted_iota(jnp.int32, (nb, t), 0)
    ind = _bf16(jnp.where(blk_of_col == blk_row, 1.0, 0.0))
    kmean = _dot(ind, k) * (1.0 / MOBA_BLOCK)
    kmt = jnp.concatenate([kmean] * (LANES // nb), axis=0)
    r_head = _group_head(lax.broadcasted_iota(jnp.int32, kmt.shape, 0))
    c_head = lax.broadcasted_iota(jnp.int32, kmt.shape, 1) // HEAD_DIM
    kmt = jnp.where(r_head == c_head, kmt, 0.0)
    k_hi = _bf16(kmt)
    k_lo = _bf16(kmt - k_hi.astype(jnp.float32))
    gate = (_dot_nt(q, k_hi) + _dot_nt(q, k_lo)) * (1.0 / Q_SCALE)

    lane = lax.broadcasted_iota(jnp.int32, (1, LANES), 1)
    j = lane % nb
    own = lax.broadcasted_iota(jnp.int32, (t, 1), 0) // MOBA_BLOCK
    past = j < own
    g = jnp.where(past, gate, -jnp.inf)
    rank = jnp.zeros(gate.shape, jnp.int32)
    for d in range(1, nb):
        other = pltpu.roll(g, LANES - d, 1)
        first = ((j + d) % nb) < j
        beats = (other > g) | (first & (other == g))
        rank = rank + jnp.where(beats, 1, 0)
    attend = (past & (rank < MOBA_TOPK)) | (j == own)
    mqa_ref[0] = jnp.where(attend, 0.0, NEG_INF)

    lane_head = _group_head(lane)
    bias = jnp.zeros((1, LANES), jnp.float32)
    for head in range(N_FOX_HEADS):
        bias = jnp.where(lane_head == head, bf_ref[:, head:head + 1], bias)
    z = fl_ref[0] + bias
    logf = -(jnp.maximum(-z, 0.0) + jnp.log1p(jnp.exp(-jnp.abs(z))))
    rr = lax.broadcasted_iota(jnp.int32, (MOBA_BLOCK, MOBA_BLOCK), 0)
    cc = lax.broadcasted_iota(jnp.int32, (MOBA_BLOCK, MOBA_BLOCK), 1)
    tri = _bf16(jnp.where(cc <= rr, 1.0, 0.0))
    slot = lane % AUX_GROUP
    carry = jnp.zeros((1, LANES), jnp.float32)
    for i in range(nb):
        rows = slice(i * MOBA_BLOCK, (i + 1) * MOBA_BLOCK)
        hi, mid, lo = _split3(logf[rows, :])
        cum = (_dot(tri, hi) + _dot(tri, mid)) + _dot(tri, lo) + carry
        carry = cum[MOBA_BLOCK - 1:MOBA_BLOCK, :]
        parts = [p.astype(jnp.float32) for p in _split3(cum * LOG2E)]
        fq = jnp.where(slot < 2 * N_SPLIT, 1.0, 0.0)
        fk = jnp.where(slot < N_SPLIT, 1.0, 0.0)
        for s in range(N_SPLIT):
            fq = jnp.where(slot == s, parts[s], fq)
            fk = jnp.where(slot == N_SPLIT + s, -parts[s], fk)
        fqa_ref[0, rows, :] = fq
        fka_ref[0, rows, :] = fk


def _routing(mq, mk, fl, b_forget):
    b, t, w = mq.shape
    full = lambda i: (i, 0, 0)
    aux = jax.ShapeDtypeStruct((b, t, LANES), jnp.float32)
    return pl.pallas_call(
        _routing_kernel,
        grid=(b,),
        in_specs=[pl.BlockSpec((1, t, w), full),
                  pl.BlockSpec((1, t, w), full),
                  pl.BlockSpec((1, t, LANES), full),
                  pl.BlockSpec((1, N_FOX_HEADS), lambda i: (0, 0))],
        out_specs=[pl.BlockSpec((1, t, LANES), full)] * 3,
        out_shape=[aux] * 3,
        compiler_params=pltpu.CompilerParams(vmem_limit_bytes=VMEM_LIMIT),
        name="routing",
    )(mq, mk, fl, b_forget)


def _attn_kernel(*refs, moba):
    if moba:
        q_ref, qa_ref, k_ref, vt_ref, o_ref, kx_ref, vx_ref = refs
    else:
        q_ref, qa_ref, k_ref, vt_ref, ka_ref, o_ref, kx_ref, vx_ref = refs
    tq = SEQ_TILE
    t = k_ref.shape[1]
    nb = t // tq
    step = pl.program_id(2)
    lane = lax.broadcasted_iota(jnp.int32, (1, LANES), 1)
    low = lane < HEAD_DIM
    shift = (LANES - pl.program_id(1) * AUX_GROUP) % LANES
    in_group = (lane % HEAD_DIM) < AUX_GROUP // 2

    @pl.when(step == 0)
    def _():
        k2 = k_ref[0]
        if moba:
            blk = lax.broadcasted_iota(jnp.int32, (t, LANES), 0) // MOBA_BLOCK
            ln = lax.broadcasted_iota(jnp.int32, (t, LANES), 1) % HEAD_DIM
            ka = _bf16(jnp.where(ln == blk, 1.0, 0.0))
        else:
            ka = _bf16(pltpu.roll(ka_ref[0], shift, 1))
        kx_ref[0] = jnp.where(low, k2, ka)
        kx_ref[1] = jnp.where(low, ka, k2)
        for h in range(2):
            vx_ref[h, 0:HEAD_DIM, :] = vt_ref[0, h * HEAD_DIM:(h + 1) * HEAD_DIM, :]
            vx_ref[h, HEAD_DIM:V_ROWS, :] = jnp.ones((V_ROWS - HEAD_DIM, t), jnp.bfloat16)

    key = lax.broadcasted_iota(jnp.int32, (tq, tq), 0)
    qry = lax.broadcasted_iota(jnp.int32, (tq, tq), 1)
    causal = key <= qry

    def scores(n, h):
        rows = slice(n * tq, (n + 1) * tq)
        q2 = q_ref[0, rows, :]
        qa = pltpu.roll(qa_ref[0, rows, :], shift, 1)
        qa = _bf16(jnp.where(in_group, qa, 0.0))
        qx = jnp.where(low, q2, qa) if h == 0 else jnp.where(low, qa, q2)
        sd = jnp.where(causal, _dot_nt(kx_ref[h, rows, :], qx), NEG_INF)
        m = jnp.max(sd, axis=0, keepdims=True)
        sp = None
        if n > 0:
            sp = _dot_nt(kx_ref[h, 0:n * tq, :], qx)
            m = jnp.maximum(m, jnp.max(sp, axis=0, keepdims=True))
        return n, h, sd, sp, m

    def weighted_values(n, h, sd, sp, m):
        rows = slice(n * tq, (n + 1) * tq)
        acc = _dot(vx_ref[h, :, rows], _bf16(jnp.exp2(sd - m)))
        if n > 0:
            acc = acc + _dot(vx_ref[h, :, 0:n * tq], _bf16(jnp.exp2(sp - m)))
        return acc[0:HEAD_DIM, :] / acc[HEAD_DIM:HEAD_DIM + 1, :]

    def run(tiles):
        chains = [(n, h) for n in tiles for h in range(2)]
        outs = {}
        queue = [scores(*c) for c in chains[:ATTN_LOOKAHEAD]]
        for nxt in chains[ATTN_LOOKAHEAD:] + [None] * ATTN_LOOKAHEAD:
            if nxt is not None:
                queue.append(scores(*nxt))
            done = queue.pop(0)
            outs[done[:2]] = weighted_values(*done)
        for n in tiles:
            o_t = jnp.concatenate([outs[(n, 0)], outs[(n, 1)]], axis=0)
            o_ref[0, n * tq:(n + 1) * tq, :] = _bf16(o_t.T)

    pairs = [(nb - 1 - i, i) for i in range(nb // 2)]
    n_static = _attn_steps(nb)
    for s in range(n_static):
        @pl.when(step == s)
        def _(s=s):
            run([n for pair in pairs[s::n_static] for n in pair])


def _attn_steps(nb):
    return max(1, nb // ATTN_TILES_PER_STEP)


def _attention(moba, q, qa, k, vt, ka=None):
    b, t, w = q.shape
    n_pairs = w // LANES
    nb = t // SEQ_TILE
    assert nb % 2 == 0 and (nb // 2) % _attn_steps(nb) == 0
    spec = pl.BlockSpec((1, t, LANES), lambda i, p, j: (i, 0, p))
    spec_t = pl.BlockSpec((1, LANES, t), lambda i, p, j: (i, p, 0))
    spec_aux = pl.BlockSpec((1, t, LANES), lambda i, p, j: (i, 0, 0))
    args = [q, qa, k, vt] if moba else [q, qa, k, vt, ka]
    specs = [spec, spec_aux, spec, spec_t] + ([] if moba else [spec_aux])
    kern = lambda *refs: _attn_kernel(*refs, moba=moba)
    return pl.pallas_call(
        kern,
        grid=(b, n_pairs, _attn_steps(nb)),
        in_specs=specs,
        out_specs=spec,
        out_shape=jax.ShapeDtypeStruct((b, t, w), jnp.bfloat16),
        scratch_shapes=[pltpu.VMEM((2, t, LANES), jnp.bfloat16),
                        pltpu.VMEM((2, V_ROWS, t), jnp.bfloat16)],
        compiler_params=pltpu.CompilerParams(vmem_limit_bytes=VMEM_LIMIT),
        name="moba" if moba else "fox",
    )(*args)


def _ffn_kernel(x_ref, om_ref, of_ref, wo_ref, gt1_ref, g_ref, sc_ref, sh_ref, wup_ref, cv_ref,
                wdn_ref, gt2_ref, out_ref, x1_ref, halo_ref):
    tm = x_ref.shape[1]

    @pl.when(pl.program_id(1) == 0)
    def _():
        halo_ref[...] = jnp.zeros(halo_ref.shape, jnp.float32)

    pm = tm // FFN_PARTS
    slabs = [slice(p * pm, (p + 1) * pm) for p in range(FFN_PARTS)]
    attn = [_dot(jnp.concatenate([om_ref[0, r, :], of_ref[0, r, :]], axis=1), wo_ref[...])
            for r in slabs]

    def normed(p):
        x1 = x_ref[0, slabs[p], :] + gt1_ref[0] * attn[p]
        x1_ref[slabs[p], :] = x1
        return _bf16(_rms_mod(x1, g_ref[...], sc_ref[0], sh_ref[0]))

    def up_proj(hn, c):
        gate_cols = slice(c * FF_CHUNK, (c + 1) * FF_CHUNK)
        val_cols = slice(D_FF + c * FF_CHUNK, D_FF + (c + 1) * FF_CHUNK)
        u = jnp.concatenate([_dot(hn, wup_ref[:, gate_cols]), _dot(hn, wup_ref[:, val_cols])],
                            axis=1)
        return u, gate_cols, val_cols

    hn = normed(0)
    tails = [halo_ref[c] for c in range(N_FF_CHUNKS)]
    for p in range(FFN_PARTS):
        hn_next = None
        hmid = []
        ahead = [up_proj(hn, c) for c in range(FFN_LOOKAHEAD)]
        for c in range(N_FF_CHUNKS):
            if c + FFN_LOOKAHEAD < N_FF_CHUNKS:
                ahead.append(up_proj(hn, c + FFN_LOOKAHEAD))
            u, gate_cols, val_cols = ahead.pop(0)
            ext = jnp.concatenate([tails[c], u], axis=0)
            tails[c] = u[pm - HALO:pm, :]
            cv = jnp.concatenate([cv_ref[:, gate_cols], cv_ref[:, val_cols]], axis=1)
            u1 = pltpu.roll(ext, 1, 0)[HALO:, :]
            u2 = pltpu.roll(ext, 2, 0)[HALO:, :]
            uc = cv[0:1, :] * u2 + cv[1:2, :] * u1 + cv[2:3, :] * u + cv[3:4, :]
            a = uc[:, :FF_CHUNK]
            val = uc[:, FF_CHUNK:]
            half = 0.5 * a
            hmid.append(_bf16((half + half * jnp.tanh(half)) * val))
            if c == FFN_LOOKAHEAD and p + 1 < FFN_PARTS:
                hn_next = normed(p + 1)
        ffn = _dot(jnp.concatenate(hmid, axis=1), wdn_ref[...])
        out_ref[0, slabs[p], :] = x1_ref[slabs[p], :] + gt2_ref[0] * ffn
        hn = hn_next
    for c in range(N_FF_CHUNKS):
        halo_ref[c] = tails[c]


def _ffn(x, om, of, wo, gt1, g_ffn, sc2, sh2, wup, cv, wdn, gt2):
    b, t, d = x.shape
    tm = min(ROW_TILE, t)
    row = lambda i, j: (i, 0, 0)
    tile = lambda i, j: (i, j, 0)
    const2 = lambda i, j: (0, 0)
    once = dict(pipeline_mode=pl.Buffered(1))
    return pl.pallas_call(
        _ffn_kernel,
        grid=(b, t // tm),
        in_specs=[pl.BlockSpec((1, tm, d), tile),
                  pl.BlockSpec((1, tm, MOBA_WIDTH), tile),
                  pl.BlockSpec((1, tm, FOX_WIDTH), tile),
                  pl.BlockSpec((d, d), const2, **once),
                  pl.BlockSpec((1, 1, d), row),
                  pl.BlockSpec((1, d), const2),
                  pl.BlockSpec((1, 1, d), row),
                  pl.BlockSpec((1, 1, d), row),
                  pl.BlockSpec((d, 2 * D_FF), const2, **once),
                  pl.BlockSpec((HALO, 2 * D_FF), const2, **once),
                  pl.BlockSpec((D_FF, d), const2, **once),
                  pl.BlockSpec((1, 1, d), row)],
        out_specs=pl.BlockSpec((1, tm, d), tile),
        out_shape=jax.ShapeDtypeStruct((b, t, d), jnp.float32),
        scratch_shapes=[pltpu.VMEM((tm, d), jnp.float32),
                        pltpu.VMEM((N_FF_CHUNKS, HALO, 2 * FF_CHUNK), jnp.float32)],
        compiler_params=pltpu.CompilerParams(vmem_limit_bytes=VMEM_LIMIT),
        name="ffn",
    )(x, om, of, wo, gt1, g_ffn, sc2, sh2, wup, cv, wdn, gt2)


def _layer(x, c, w_ada, b_ada, g_mix, w_in, b_forget, moba_q_gain, moba_k_gain, fox_q_gain,
           fox_k_gain, w_out, g_ffn, w_up, conv_w, conv_b, w_down):
    b, t, d = x.shape
    nb = t // MOBA_BLOCK
    assert d == D_MODEL and t % SEQ_TILE == 0
    assert AUX_GROUP % nb == 0 and 2 * nb <= AUX_GROUP and 2 * N_SPLIT <= AUX_GROUP // 2

    mod = _adaln(c, w_ada, b_ada).reshape(b, 6, 1, d)
    sh1, sc1, gt1, sh2, sc2, gt2 = [mod[:, i] for i in range(6)]

    assert w_in.shape == (d, IN_QKV_COLS + N_FOX_HEADS)
    gains = jnp.stack([jnp.tile(g, N_MOBA_HEADS) for g in
                       (moba_q_gain, moba_k_gain, fox_q_gain, fox_k_gain)])
    r = np.arange(2 * LANES) // HEAD_DIM
    bd = jnp.asarray(r[:, None] == r[None, :], jnp.bfloat16)
    mq, mk, fq, fk, fl, mvt, fvt, wo16, wu16, wd16 = _inproj(
        x, sc1, sh1, g_mix.reshape(1, d), w_in.T, gains, _rope_tables(t), bd, [w_out, w_up, w_down])

    mqa, fqa, fka = _routing(mq, mk, fl, b_forget.reshape(1, N_FOX_HEADS))
    o_moba = _attention(True, mq, mqa, mk, mvt)
    o_fox = _attention(False, fq, fqa, fk, fvt, fka)

    cv = jnp.concatenate([conv_w, conv_b[None, :],
                          jnp.zeros((HALO - CONV_WIDTH - 1, 2 * D_FF), jnp.float32)], axis=0)
    return _ffn(x, o_moba, o_fox, wo16, gt1, g_ffn.reshape(1, d), sc2, sh2, wu16, cv, wd16, gt2)


def kernel(x, c, w_ada, b_ada, g_mix, w_in, b_forget, moba_q_gain, moba_k_gain, fox_q_gain,
           fox_k_gain, w_out, g_ffn, w_up, conv_w, conv_b, w_down):
    for l in range(w_ada.shape[0]):
        x = _layer(x, c, w_ada[l], b_ada[l], g_mix[l], w_in[l], b_forget[l], moba_q_gain[l],
                   moba_k_gain[l], fox_q_gain[l], fox_k_gain[l], w_out[l], g_ffn[l], w_up[l],
                   conv_w[l], conv_b[l], w_down[l])
    return x
```

```python
import math

import jax
import jax.numpy as jnp
import numpy as np
from jax import lax
from jax.experimental import pallas as pl
from jax.experimental.pallas import tpu as pltpu

D_MODEL = 1024
HEAD_DIM = 64
N_MOBA_HEADS = 8
N_FOX_HEADS = 8
MOBA_WIDTH = N_MOBA_HEADS * HEAD_DIM
FOX_WIDTH = N_FOX_HEADS * HEAD_DIM
MOBA_BLOCK = 256
MOBA_TOPK = 3
ROPE_THETA = 500000.0
ROPE_DIM = HEAD_DIM // 4
D_FF = 2816
CONV_WIDTH = 3
NORM_EPS = 1e-6
NEG_INF = -1e30
LOG2E = math.log2(math.e)
Q_SCALE = HEAD_DIM ** -0.5 * LOG2E
LANES = 128
BF16_ROWS = 16
IN_QKV_COLS = 3 * MOBA_WIDTH + 3 * FOX_WIDTH
V_ROWS = HEAD_DIM + BF16_ROWS
SEQ_TILE = 256
ROW_TILE = 512
ATTN_PAIRS_PER_STEP = 2
ATTN_LOOKAHEAD = 5
FFN_PARTS = 2
FFN_LOOKAHEAD = 2
FF_CHUNK = 256
N_FF_CHUNKS = D_FF // FF_CHUNK
HALO = 8
N_SPLIT = 3
AUX_GROUP = 16
VMEM_LIMIT = 56 * 1024 * 1024

_NT = (((1,), (1,)), ((), ()))


def _bf16(a):
    return a.astype(jnp.bfloat16)


def _dot(a, b):
    return jnp.dot(a, b, preferred_element_type=jnp.float32)


def _dot_nt(a, b):
    return lax.dot_general(a, b, _NT, preferred_element_type=jnp.float32)


def _sigmoid(a):
    return 1.0 / (1.0 + jnp.exp(-a))


def _adaln_kernel(c_ref, w_ref, b_ref, o_ref):
    c = c_ref[...]
    s = c * _sigmoid(c)
    o_ref[...] = _dot(_bf16(s), _bf16(w_ref[...])) + b_ref[...]


def _adaln(c, w_ada, b_ada):
    b, d = c.shape
    n = w_ada.shape[1]
    tn = 1536
    return pl.pallas_call(
        _adaln_kernel,
        grid=(n // tn,),
        in_specs=[pl.BlockSpec((b, d), lambda j: (0, 0)),
                  pl.BlockSpec((d, tn), lambda j: (0, j)),
                  pl.BlockSpec((1, tn), lambda j: (0, j))],
        out_specs=pl.BlockSpec((b, tn), lambda j: (0, j)),
        out_shape=jax.ShapeDtypeStruct((b, n), jnp.float32),
        compiler_params=pltpu.CompilerParams(vmem_limit_bytes=VMEM_LIMIT),
        name="adaln",
    )(c, w_ada, b_ada.reshape(1, n))


def _rms_mod(x, g, sc, sh):
    ms = jnp.mean(x * x, axis=-1, keepdims=True)
    y = x * lax.rsqrt(ms + NORM_EPS)
    return (y * g) * (1.0 + sc) + sh


def _head_norm(p, gain, bd):
    sq = _bf16(p * p)
    half = 2 * LANES
    ss = jnp.concatenate([_dot(sq[:, :half], bd), _dot(sq[:, half:], bd)], axis=1)
    return (p * lax.rsqrt(ss * (1.0 / HEAD_DIM) + NORM_EPS)) * gain


def _rope(y, c, sa, sb):
    half = ROPE_DIM // 2
    outs = []
    for i in range(y.shape[1] // LANES):
        yc = y[:, i * LANES:(i + 1) * LANES]
        up = pltpu.roll(yc, LANES - half, 1)
        dn = pltpu.roll(yc, half, 1)
        outs.append(yc * c + up * sa + dn * sb)
    return jnp.concatenate(outs, axis=1)


def _inproj_kernel(x_ref, sc_ref, sh_ref, g_ref, wint_ref, gains_ref, rc_ref, rsa_ref, rsb_ref,
                   bd_ref, *refs):
    n_cast = (len(refs) - 8) // 2
    cast_in, refs = refs[:n_cast], refs[n_cast:]
    mq_ref, mk_ref, fq_ref, fk_ref, fl_ref, mvt_ref, fvt_ref = refs[:7]
    cast_out, w_ref = refs[7:7 + n_cast], refs[-1]
    for src, dst in zip(cast_in, cast_out):
        dst[...] = _bf16(src[...])
    w = MOBA_WIDTH
    n_groups = IN_QKV_COLS // w

    @pl.when((pl.program_id(0) == 0) & (pl.program_id(1) == 0))
    def _():
        for i in range(n_groups):
            w_ref[i * w:(i + 1) * w, :] = _bf16(wint_ref[i * w:(i + 1) * w, :])
        logit_rows = wint_ref[IN_QKV_COLS:IN_QKV_COLS + N_FOX_HEADS, :]
        r = lax.broadcasted_iota(jnp.int32, (LANES, 1), 0)
        r_head = jnp.where(r % AUX_GROUP < 2 * N_SPLIT, _group_head(r), -1)
        wlog = jnp.zeros((LANES, logit_rows.shape[1]), jnp.float32)
        for head in range(N_FOX_HEADS):
            wlog = jnp.where(r_head == head, logit_rows[head:head + 1, :], wlog)
        w_ref[IN_QKV_COLS:IN_QKV_COLS + LANES, :] = _bf16(wlog)

    hn = _bf16(_rms_mod(x_ref[0], g_ref[...], sc_ref[0], sh_ref[0]))
    bd = bd_ref[...]
    c, sa, sb = rc_ref[...], rsa_ref[...], rsb_ref[...]
    proj = lambda i: _dot_nt(hn, w_ref[i * w:(i + 1) * w, :])

    p_mq = proj(0)
    p_mk = proj(1)
    mq = _rope(_head_norm(p_mq, gains_ref[0:1, :], bd), c, sa, sb)
    mq_ref[0] = _bf16(mq * Q_SCALE)
    p_fq = proj(3)
    mk = _rope(_head_norm(p_mk, gains_ref[1:2, :], bd), c, sa, sb)
    mk_ref[0] = _bf16(mk)
    p_fk = proj(4)
    fq = _head_norm(p_fq, gains_ref[2:3, :], bd)
    fq_ref[0] = _bf16(fq * Q_SCALE)
    fl_ref[0] = _dot_nt(hn, w_ref[IN_QKV_COLS:IN_QKV_COLS + LANES, :])
    mvt = _dot_nt(w_ref[2 * w:3 * w, :], hn)
    fk = _head_norm(p_fk, gains_ref[3:4, :], bd)
    fk_ref[0] = _bf16(fk)
    mvt_ref[0] = _bf16(mvt)
    fvt_ref[0] = _bf16(_dot_nt(w_ref[5 * w:6 * w, :], hn))


def _rope_tables(t):
    half = ROPE_DIM // 2
    inv_freq = np.power(ROPE_THETA, -2.0 * np.arange(half, dtype=np.float64) / ROPE_DIM)
    ang = np.arange(t, dtype=np.float64)[:, None] * inv_freq[None, :]
    cos, sin = np.cos(ang), np.sin(ang)
    d = np.arange(LANES) % HEAD_DIM
    first = (d < half)[None, :]
    second = ((d >= half) & (d < ROPE_DIM))[None, :]
    idx = np.where(d < ROPE_DIM, d % half, 0)
    cos_l, sin_l = cos[:, idx], sin[:, idx]
    c = np.where(first | second, cos_l, 1.0)
    sa = np.where(first, -sin_l, 0.0)
    sb = np.where(second, sin_l, 0.0)
    return tuple(jnp.asarray(a, jnp.float32) for a in (c, sa, sb))


def _slab_rows(n_rows, n_steps):
    tiles = n_rows // BF16_ROWS
    assert tiles * BF16_ROWS == n_rows
    n_slabs = max(g for g in range(1, tiles + 1) if tiles % g == 0 and g <= n_steps)
    return n_rows // n_slabs


def _inproj(x, sc1, sh1, g_mix, w_in_t, gains, tables, bd, to_cast):
    b, t, d = x.shape
    tm = min(ROW_TILE, t)
    w = MOBA_WIDTH
    nt = t // tm
    cast_specs, cast_shapes = [], []
    for a in to_cast:
        rows = _slab_rows(a.shape[0], b * nt)
        last = a.shape[0] // rows - 1
        slab = lambda i, j, last=last: (jnp.minimum(i * nt + j, last), 0)
        cast_specs.append(pl.BlockSpec((rows, a.shape[1]), slab))
        cast_shapes.append(jax.ShapeDtypeStruct(a.shape, jnp.bfloat16))
    row = lambda i, j: (i, 0, 0)
    tile = lambda i, j: (i, j, 0)
    tile_t = lambda i, j: (i, 0, j)
    const = lambda i, j: (0, 0)
    wide = jax.ShapeDtypeStruct((b, t, w), jnp.bfloat16)
    wide_t = jax.ShapeDtypeStruct((b, w, t), jnp.bfloat16)
    out_specs = ([pl.BlockSpec((1, tm, w), tile)] * 4 + [pl.BlockSpec((1, tm, LANES), tile)]
                 + [pl.BlockSpec((1, w, tm), tile_t)] * 2)
    return pl.pallas_call(
        _inproj_kernel,
        grid=(b, t // tm),
        in_specs=[pl.BlockSpec((1, tm, d), tile),
                  pl.BlockSpec((1, 1, d), row),
                  pl.BlockSpec((1, 1, d), row),
                  pl.BlockSpec((1, d), const),
                  pl.BlockSpec(w_in_t.shape, const, pipeline_mode=pl.Buffered(1)),
                  pl.BlockSpec((4, w), const),
                  pl.BlockSpec((tm, LANES), lambda i, j: (j, 0)),
                  pl.BlockSpec((tm, LANES), lambda i, j: (j, 0)),
                  pl.BlockSpec((tm, LANES), lambda i, j: (j, 0)),
                  pl.BlockSpec((2 * LANES, 2 * LANES), const)] + cast_specs,
        out_specs=out_specs + cast_specs,
        out_shape=([wide] * 4 + [jax.ShapeDtypeStruct((b, t, LANES), jnp.float32)] + [wide_t] * 2
                   + cast_shapes),
        scratch_shapes=[pltpu.VMEM((IN_QKV_COLS + LANES, d), jnp.bfloat16)],
        compiler_params=pltpu.CompilerParams(vmem_limit_bytes=VMEM_LIMIT),
        name="inproj",
    )(x, sc1, sh1, g_mix, w_in_t, gains, *tables, bd, *to_cast)


def _split3(a):
    hi = _bf16(a)
    r1 = a - hi.astype(jnp.float32)
    mid = _bf16(r1)
    lo = _bf16(r1 - mid.astype(jnp.float32))
    return hi, mid, lo


def _group_head(lane_idx):
    return 2 * ((lane_idx % HEAD_DIM) // AUX_GROUP) + jnp.where(lane_idx < HEAD_DIM, 1, 0)


def _routing_kernel(mq_ref, mk_ref, fl_ref, bf_ref, mqa_ref, fqa_ref, fka_ref):
    t = mq_ref.shape[1]
    nb = t // MOBA_BLOCK
    q = mq_ref[0]
    k = mk_ref[0]
    blk_of_col = lax.broadcasted_iota(jnp.int32, (nb, t), 1) // MOBA_BLOCK
    blk_row = lax.broadcasted_iota(jnp.int32, (nb, t), 0)
    ind = _bf16(jnp.where(blk_of_col == blk_row, 1.0, 0.0))
    kmean = _dot(ind, k) * (1.0 / MOBA_BLOCK)
    kmt = jnp.concatenate([kmean] * (LANES // nb), axis=0)
    r_head = _group_head(lax.broadcasted_iota(jnp.int32, kmt.shape, 0))
    c_head = lax.broadcasted_iota(jnp.int32, kmt.shape, 1) // HEAD_DIM
    kmt = jnp.where(r_head == c_head, kmt, 0.0)
    k_hi = _bf16(kmt)
    k_lo = _bf16(kmt - k_hi.astype(jnp.float32))
    gate = (_dot_nt(q, k_hi) + _dot_nt(q, k_lo)) * (1.0 / Q_SCALE)

    lane = lax.broadcasted_iota(jnp.int32, (1, LANES), 1)
    j = lane % nb
    own = lax.broadcasted_iota(jnp.int32, (t, 1), 0) // MOBA_BLOCK
    past = j < own
    g = jnp.where(past, gate, -jnp.inf)
    rank = jnp.zeros(gate.shape, jnp.int32)
    for d in range(1, nb):
        other = pltpu.roll(g, LANES - d, 1)
        first = ((j + d) % nb) < j
        beats = (other > g) | (first & (other == g))
        rank = rank + jnp.where(beats, 1, 0)
    attend = (past & (rank < MOBA_TOPK)) | (j == own)
    mqa_ref[0] = jnp.where(attend, 0.0, NEG_INF)

    lane_head = _group_head(lane)
    bias = jnp.zeros((1, LANES), jnp.float32)
    for head in range(N_FOX_HEADS):
        bias = jnp.where(lane_head == head, bf_ref[:, head:head + 1], bias)
    z = fl_ref[0] + bias
    logf = -(jnp.maximum(-z, 0.0) + jnp.log1p(jnp.exp(-jnp.abs(z))))
    rr = lax.broadcasted_iota(jnp.int32, (MOBA_BLOCK, MOBA_BLOCK), 0)
    cc = lax.broadcasted_iota(jnp.int32, (MOBA_BLOCK, MOBA_BLOCK), 1)
    tri = _bf16(jnp.where(cc <= rr, 1.0, 0.0))
    slot = lane % AUX_GROUP
    carry = jnp.zeros((1, LANES), jnp.float32)
    for i in range(nb):
        rows = slice(i * MOBA_BLOCK, (i + 1) * MOBA_BLOCK)
        hi, mid, lo = _split3(logf[rows, :])
        cum = (_dot(tri, hi) + _dot(tri, mid)) + _dot(tri, lo) + carry
        carry = cum[MOBA_BLOCK - 1:MOBA_BLOCK, :]
        parts = [p.astype(jnp.float32) for p in _split3(cum * LOG2E)]
        fq = jnp.where(slot < 2 * N_SPLIT, 1.0, 0.0)
        fk = jnp.where(slot < N_SPLIT, 1.0, 0.0)
        for s in range(N_SPLIT):
            fq = jnp.where(slot == s, parts[s], fq)
            fk = jnp.where(slot == N_SPLIT + s, -parts[s], fk)
        fqa_ref[0, rows, :] = fq
        fka_ref[0, rows, :] = fk


def _routing(mq, mk, fl, b_forget):
    b, t, w = mq.shape
    full = lambda i: (i, 0, 0)
    aux = jax.ShapeDtypeStruct((b, t, LANES), jnp.float32)
    return pl.pallas_call(
        _routing_kernel,
        grid=(b,),
        in_specs=[pl.BlockSpec((1, t, w), full),
                  pl.BlockSpec((1, t, w), full),
                  pl.BlockSpec((1, t, LANES), full),
                  pl.BlockSpec((1, N_FOX_HEADS), lambda i: (0, 0))],
        out_specs=[pl.BlockSpec((1, t, LANES), full)] * 3,
        out_shape=[aux] * 3,
        compiler_params=pltpu.CompilerParams(vmem_limit_bytes=VMEM_LIMIT),
        name="routing",
    )(mq, mk, fl, b_forget)


def _attn_kernel(*refs, moba):
    if moba:
        q_ref, qa_ref, k_ref, vt_ref, o_ref, kx_ref, vx_ref = refs
    else:
        q_ref, qa_ref, k_ref, vt_ref, ka_ref, o_ref, kx_ref, vx_ref = refs
    tq = SEQ_TILE
    t = k_ref.shape[1]
    nb = t // tq
    n_local = q_ref.shape[2] // LANES
    lane = lax.broadcasted_iota(jnp.int32, (1, LANES), 1)
    low = lane < HEAD_DIM
    in_group = (lane % HEAD_DIM) < AUX_GROUP // 2

    def aux_shift(pi):
        return (LANES - (pl.program_id(1) * n_local + pi) * AUX_GROUP) % LANES

    for pi in range(n_local):
        cols = slice(pi * LANES, (pi + 1) * LANES)
        k2 = k_ref[0, :, cols]
        if moba:
            blk = lax.broadcasted_iota(jnp.int32, (t, LANES), 0) // MOBA_BLOCK
            ln = lax.broadcasted_iota(jnp.int32, (t, LANES), 1) % HEAD_DIM
            ka = _bf16(jnp.where(ln == blk, 1.0, 0.0))
        else:
            ka = _bf16(pltpu.roll(ka_ref[0], aux_shift(pi), 1))
        kx_ref[2 * pi] = jnp.where(low, k2, ka)
        kx_ref[2 * pi + 1] = jnp.where(low, ka, k2)
        for h in range(2):
            first = pi * LANES + h * HEAD_DIM
            vx_ref[2 * pi + h, 0:HEAD_DIM, :] = vt_ref[0, first:first + HEAD_DIM, :]
            vx_ref[2 * pi + h, HEAD_DIM:V_ROWS, :] = jnp.ones((V_ROWS - HEAD_DIM, t), jnp.bfloat16)

    key = lax.broadcasted_iota(jnp.int32, (tq, tq), 0)
    qry = lax.broadcasted_iota(jnp.int32, (tq, tq), 1)
    causal = key <= qry

    def scores(n, pi, h):
        rows = slice(n * tq, (n + 1) * tq)
        q2 = q_ref[0, rows, pi * LANES:(pi + 1) * LANES]
        qa = pltpu.roll(qa_ref[0, rows, :], aux_shift(pi), 1)
        qa = _bf16(jnp.where(in_group, qa, 0.0))
        qx = jnp.where(low, q2, qa) if h == 0 else jnp.where(low, qa, q2)
        hd = 2 * pi + h
        sd = jnp.where(causal, _dot_nt(kx_ref[hd, rows, :], qx), NEG_INF)
        m = jnp.max(sd, axis=0, keepdims=True)
        sp = None
        if n > 0:
            sp = _dot_nt(kx_ref[hd, 0:n * tq, :], qx)
            m = jnp.maximum(m, jnp.max(sp, axis=0, keepdims=True))
        return (n, pi, h), sd, sp, m

    def weighted_values(chain, sd, sp, m):
        n, pi, h = chain
        hd = 2 * pi + h
        rows = slice(n * tq, (n + 1) * tq)
        acc = _dot(vx_ref[hd, :, rows], _bf16(jnp.exp2(sd - m)))
        if n > 0:
            acc = acc + _dot(vx_ref[hd, :, 0:n * tq], _bf16(jnp.exp2(sp - m)))
        return acc[0:HEAD_DIM, :] / acc[HEAD_DIM:HEAD_DIM + 1, :]

    chains = [(n, pi, h) for n in reversed(range(nb)) for pi in range(n_local) for h in range(2)]
    outs = {}
    queue = [scores(*c) for c in chains[:ATTN_LOOKAHEAD]]
    for nxt in chains[ATTN_LOOKAHEAD:] + [None] * ATTN_LOOKAHEAD:
        if nxt is not None:
            queue.append(scores(*nxt))
        done = queue.pop(0)
        outs[done[0]] = weighted_values(*done)
        n, pi, h = done[0]
        if h == 1:
            o_t = jnp.concatenate([outs.pop((n, pi, 0)), outs.pop((n, pi, 1))], axis=0)
            o_ref[0, n * tq:(n + 1) * tq, pi * LANES:(pi + 1) * LANES] = _bf16(o_t.T)


def _attention(moba, q, qa, k, vt, ka=None):
    b, t, w = q.shape
    wl = ATTN_PAIRS_PER_STEP * LANES
    assert w % wl == 0 and t % SEQ_TILE == 0
    spec = pl.BlockSpec((1, t, wl), lambda i, p: (i, 0, p))
    spec_t = pl.BlockSpec((1, wl, t), lambda i, p: (i, p, 0))
    spec_aux = pl.BlockSpec((1, t, LANES), lambda i, p: (i, 0, 0))
    args = [q, qa, k, vt] if moba else [q, qa, k, vt, ka]
    specs = [spec, spec_aux, spec, spec_t] + ([] if moba else [spec_aux])
    kern = lambda *refs: _attn_kernel(*refs, moba=moba)
    return pl.pallas_call(
        kern,
        grid=(b, w // wl),
        in_specs=specs,
        out_specs=spec,
        out_shape=jax.ShapeDtypeStruct((b, t, w), jnp.bfloat16),
        scratch_shapes=[pltpu.VMEM((2 * ATTN_PAIRS_PER_STEP, t, LANES), jnp.bfloat16),
                        pltpu.VMEM((2 * ATTN_PAIRS_PER_STEP, V_ROWS, t), jnp.bfloat16)],
        compiler_params=pltpu.CompilerParams(vmem_limit_bytes=VMEM_LIMIT),
        name="moba" if moba else "fox",
    )(*args)


def _ffn_kernel(x_ref, om_ref, of_ref, wo_ref, gt1_ref, g_ref, sc_ref, sh_ref, wup_ref, cv_ref,
                wdn_ref, gt2_ref, out_ref, x1_ref, halo_ref):
    tm = x_ref.shape[1]

    @pl.when(pl.program_id(1) == 0)
    def _():
        halo_ref[...] = jnp.zeros(halo_ref.shape, jnp.float32)

    pm = tm // FFN_PARTS
    slabs = [slice(p * pm, (p + 1) * pm) for p in range(FFN_PARTS)]
    attn = [_dot(jnp.concatenate([om_ref[0, r, :], of_ref[0, r, :]], axis=1), wo_ref[...])
            for r in slabs]

    def normed(p):
        x1 = x_ref[0, slabs[p], :] + gt1_ref[0] * attn[p]
        x1_ref[slabs[p], :] = x1
        return _bf16(_rms_mod(x1, g_ref[...], sc_ref[0], sh_ref[0]))

    def up_proj(hn, c):
        gate_cols = slice(c * FF_CHUNK, (c + 1) * FF_CHUNK)
        val_cols = slice(D_FF + c * FF_CHUNK, D_FF + (c + 1) * FF_CHUNK)
        u = jnp.concatenate([_dot(hn, wup_ref[:, gate_cols]), _dot(hn, wup_ref[:, val_cols])],
                            axis=1)
        return u, gate_cols, val_cols

    hn = normed(0)
    tails = [halo_ref[c] for c in range(N_FF_CHUNKS)]
    for p in range(FFN_PARTS):
        hn_next = None
        hmid = []
        ahead = [up_proj(hn, c) for c in range(FFN_LOOKAHEAD)]
        for c in range(N_FF_CHUNKS):
            if c + FFN_LOOKAHEAD < N_FF_CHUNKS:
                ahead.append(up_proj(hn, c + FFN_LOOKAHEAD))
            u, gate_cols, val_cols = ahead.pop(0)
            ext = jnp.concatenate([tails[c], u], axis=0)
            tails[c] = u[pm - HALO:pm, :]
            cv = jnp.concatenate([cv_ref[:, gate_cols], cv_ref[:, val_cols]], axis=1)
            u1 = pltpu.roll(ext, 1, 0)[HALO:, :]
            u2 = pltpu.roll(ext, 2, 0)[HALO:, :]
            uc = cv[0:1, :] * u2 + cv[1:2, :] * u1 + cv[2:3, :] * u + cv[3:4, :]
            a = uc[:, :FF_CHUNK]
            val = uc[:, FF_CHUNK:]
            half = 0.5 * a
            hmid.append(_bf16((half + half * jnp.tanh(half)) * val))
            if c == FFN_LOOKAHEAD and p + 1 < FFN_PARTS:
                hn_next = normed(p + 1)
        ffn = _dot(jnp.concatenate(hmid, axis=1), wdn_ref[...])
        out_ref[0, slabs[p], :] = x1_ref[slabs[p], :] + gt2_ref[0] * ffn
        hn = hn_next
    for c in range(N_FF_CHUNKS):
        halo_ref[c] = tails[c]


def _ffn(x, om, of, wo, gt1, g_ffn, sc2, sh2, wup, cv, wdn, gt2):
    b, t, d = x.shape
    tm = min(ROW_TILE, t)
    row = lambda i, j: (i, 0, 0)
    tile = lambda i, j: (i, j, 0)
    const2 = lambda i, j: (0, 0)
    once = dict(pipeline_mode=pl.Buffered(1))
    return pl.pallas_call(
        _ffn_kernel,
        grid=(b, t // tm),
        in_specs=[pl.BlockSpec((1, tm, d), tile),
                  pl.BlockSpec((1, tm, MOBA_WIDTH), tile),
                  pl.BlockSpec((1, tm, FOX_WIDTH), tile),
                  pl.BlockSpec((d, d), const2, **once),
                  pl.BlockSpec((1, 1, d), row),
                  pl.BlockSpec((1, d), const2),
                  pl.BlockSpec((1, 1, d), row),
                  pl.BlockSpec((1, 1, d), row),
                  pl.BlockSpec((d, 2 * D_FF), const2, **once),
                  pl.BlockSpec((HALO, 2 * D_FF), const2, **once),
                  pl.BlockSpec((D_FF, d), const2, **once),
                  pl.BlockSpec((1, 1, d), row)],
        out_specs=pl.BlockSpec((1, tm, d), tile),
        out_shape=jax.ShapeDtypeStruct((b, t, d), jnp.float32),
        scratch_shapes=[pltpu.VMEM((tm, d), jnp.float32),
                        pltpu.VMEM((N_FF_CHUNKS, HALO, 2 * FF_CHUNK), jnp.float32)],
        compiler_params=pltpu.CompilerParams(vmem_limit_bytes=VMEM_LIMIT),
        name="ffn",
    )(x, om, of, wo, gt1, g_ffn, sc2, sh2, wup, cv, wdn, gt2)


def _layer(x, c, w_ada, b_ada, g_mix, w_in, b_forget, moba_q_gain, moba_k_gain, fox_q_gain,
           fox_k_gain, w_out, g_ffn, w_up, conv_w, conv_b, w_down):
    b, t, d = x.shape
    nb = t // MOBA_BLOCK
    assert d == D_MODEL and t % SEQ_TILE == 0
    assert AUX_GROUP % nb == 0 and 2 * nb <= AUX_GROUP and 2 * N_SPLIT <= AUX_GROUP // 2

    mod = _adaln(c, w_ada, b_ada).reshape(b, 6, 1, d)
    sh1, sc1, gt1, sh2, sc2, gt2 = [mod[:, i] for i in range(6)]

    assert w_in.shape == (d, IN_QKV_COLS + N_FOX_HEADS)
    gains = jnp.stack([jnp.tile(g, N_MOBA_HEADS) for g in
                       (moba_q_gain, moba_k_gain, fox_q_gain, fox_k_gain)])
    r = np.arange(2 * LANES) // HEAD_DIM
    bd = jnp.asarray(r[:, None] == r[None, :], jnp.bfloat16)
    mq, mk, fq, fk, fl, mvt, fvt, wo16, wu16, wd16 = _inproj(
        x, sc1, sh1, g_mix.reshape(1, d), w_in.T, gains, _rope_tables(t), bd, [w_out, w_up, w_down])

    mqa, fqa, fka = _routing(mq, mk, fl, b_forget.reshape(1, N_FOX_HEADS))
    o_moba = _attention(True, mq, mqa, mk, mvt)
    o_fox = _attention(False, fq, fqa, fk, fvt, fka)

    cv = jnp.concatenate([conv_w, conv_b[None, :],
                          jnp.zeros((HALO - CONV_WIDTH - 1, 2 * D_FF), jnp.float32)], axis=0)
    return _ffn(x, o_moba, o_fox, wo16, gt1, g_ffn.reshape(1, d), sc2, sh2, wu16, cv, wd16, gt2)


def kernel(x, c, w_ada, b_ada, g_mix, w_in, b_forget, moba_q_gain, moba_k_gain, fox_q_gain,
           fox_k_gain, w_out, g_ffn, w_up, conv_w, conv_b, w_down):
    for l in range(w_ada.shape[0]):
        x = _layer(x, c, w_ada[l], b_ada[l], g_mix[l], w_in[l], b_forget[l], moba_q_gain[l],
                   moba_k_gain[l], fox_q_gain[l], fox_k_gain[l], w_out[l], g_ffn[l], w_up[l],
                   conv_w[l], conv_b[l], w_down[l])
    return x
```

```python
import math

import jax
import jax.numpy as jnp
import numpy as np
from jax import lax
from jax.experimental import pallas as pl
from jax.experimental.pallas import tpu as pltpu

D_MODEL = 1024
HEAD_DIM = 64
N_MOBA_HEADS = 8
N_FOX_HEADS = 8
MOBA_WIDTH = N_MOBA_HEADS * HEAD_DIM
FOX_WIDTH = N_FOX_HEADS * HEAD_DIM
MOBA_BLOCK = 256
MOBA_TOPK = 3
ROPE_THETA = 500000.0
ROPE_DIM = HEAD_DIM // 4
D_FF = 2816
CONV_WIDTH = 3
NORM_EPS = 1e-6
NEG_INF = -1e30
LOG2E = math.log2(math.e)
Q_SCALE = HEAD_DIM ** -0.5 * LOG2E
LANES = 128
BF16_ROWS = 16
IN_QKV_COLS = 3 * MOBA_WIDTH + 3 * FOX_WIDTH
V_ROWS = HEAD_DIM + BF16_ROWS
SEQ_TILE = 256
ROW_TILE = 512
ADALN_COLS = 768
ATTN_PAIRS_PER_STEP = 2
ATTN_LOOKAHEAD = 5
FFN_PARTS = 2
FFN_LOOKAHEAD = 2
FF_CHUNK = 256
N_FF_CHUNKS = D_FF // FF_CHUNK
HALO = 8
N_SPLIT = 3
AUX_GROUP = 16
VMEM_LIMIT = 56 * 1024 * 1024

_NT = (((1,), (1,)), ((), ()))


def _bf16(a):
    return a.astype(jnp.bfloat16)


def _dot(a, b):
    return jnp.dot(a, b, preferred_element_type=jnp.float32)


def _dot_nt(a, b):
    return lax.dot_general(a, b, _NT, preferred_element_type=jnp.float32)


def _sigmoid(a):
    return 1.0 / (1.0 + jnp.exp(-a))


def _adaln_kernel(c_ref, w_ref, b_ref, o_ref):
    c = c_ref[...]
    s = c * _sigmoid(c)
    o_ref[...] = _dot(_bf16(s), _bf16(w_ref[...])) + b_ref[...]


def _adaln(c, w_ada, b_ada):
    b, d = c.shape
    n = w_ada.shape[1]
    tn = ADALN_COLS
    return pl.pallas_call(
        _adaln_kernel,
        grid=(n // tn,),
        in_specs=[pl.BlockSpec((b, d), lambda j: (0, 0)),
                  pl.BlockSpec((d, tn), lambda j: (0, j)),
                  pl.BlockSpec((1, tn), lambda j: (0, j))],
        out_specs=pl.BlockSpec((b, tn), lambda j: (0, j)),
        out_shape=jax.ShapeDtypeStruct((b, n), jnp.float32),
        compiler_params=pltpu.CompilerParams(vmem_limit_bytes=VMEM_LIMIT),
        name="adaln",
    )(c, w_ada, b_ada.reshape(1, n))


def _rms_mod(x, g, sc, sh):
    ms = jnp.mean(x * x, axis=-1, keepdims=True)
    y = x * lax.rsqrt(ms + NORM_EPS)
    return (y * g) * (1.0 + sc) + sh


def _head_norm(p, gain, bd):
    sq = _bf16(p * p)
    half = 2 * LANES
    ss = jnp.concatenate([_dot(sq[:, :half], bd), _dot(sq[:, half:], bd)], axis=1)
    return (p * lax.rsqrt(ss * (1.0 / HEAD_DIM) + NORM_EPS)) * gain


def _rope(y, c, sa, sb):
    half = ROPE_DIM // 2
    outs = []
    for i in range(y.shape[1] // LANES):
        yc = y[:, i * LANES:(i + 1) * LANES]
        up = pltpu.roll(yc, LANES - half, 1)
        dn = pltpu.roll(yc, half, 1)
        outs.append(yc * c + up * sa + dn * sb)
    return jnp.concatenate(outs, axis=1)


def _inproj_kernel(x_ref, sc_ref, sh_ref, g_ref, wint_ref, gains_ref, rc_ref, rsa_ref, rsb_ref,
                   bd_ref, *refs):
    n_cast = (len(refs) - 8) // 2
    cast_in, refs = refs[:n_cast], refs[n_cast:]
    mq_ref, mk_ref, fq_ref, fk_ref, fl_ref, mvt_ref, fvt_ref = refs[:7]
    cast_out, w_ref = refs[7:7 + n_cast], refs[-1]
    for src, dst in zip(cast_in, cast_out):
        dst[...] = _bf16(src[...])
    w = MOBA_WIDTH
    n_groups = IN_QKV_COLS // w

    @pl.when((pl.program_id(0) == 0) & (pl.program_id(1) == 0))
    def _():
        for i in range(n_groups):
            w_ref[i * w:(i + 1) * w, :] = _bf16(wint_ref[i * w:(i + 1) * w, :])
        logit_rows = wint_ref[IN_QKV_COLS:IN_QKV_COLS + N_FOX_HEADS, :]
        r = lax.broadcasted_iota(jnp.int32, (LANES, 1), 0)
        r_head = jnp.where(r % AUX_GROUP < 2 * N_SPLIT, _group_head(r), -1)
        wlog = jnp.zeros((LANES, logit_rows.shape[1]), jnp.float32)
        for head in range(N_FOX_HEADS):
            wlog = jnp.where(r_head == head, logit_rows[head:head + 1, :], wlog)
        w_ref[IN_QKV_COLS:IN_QKV_COLS + LANES, :] = _bf16(wlog)

    hn = _bf16(_rms_mod(x_ref[0], g_ref[...], sc_ref[0], sh_ref[0]))
    bd = bd_ref[...]
    c, sa, sb = rc_ref[...], rsa_ref[...], rsb_ref[...]
    proj = lambda i: _dot_nt(hn, w_ref[i * w:(i + 1) * w, :])

    p_mq = proj(0)
    p_mk = proj(1)
    mq = _rope(_head_norm(p_mq, gains_ref[0:1, :], bd), c, sa, sb)
    mq_ref[0] = _bf16(mq * Q_SCALE)
    p_fq = proj(3)
    mk = _rope(_head_norm(p_mk, gains_ref[1:2, :], bd), c, sa, sb)
    mk_ref[0] = _bf16(mk)
    p_fk = proj(4)
    fq = _head_norm(p_fq, gains_ref[2:3, :], bd)
    fq_ref[0] = _bf16(fq * Q_SCALE)
    fl_ref[0] = _dot_nt(hn, w_ref[IN_QKV_COLS:IN_QKV_COLS + LANES, :])
    mvt = _dot_nt(w_ref[2 * w:3 * w, :], hn)
    fk = _head_norm(p_fk, gains_ref[3:4, :], bd)
    fk_ref[0] = _bf16(fk)
    mvt_ref[0] = _bf16(mvt)
    fvt_ref[0] = _bf16(_dot_nt(w_ref[5 * w:6 * w, :], hn))


def _rope_tables(t):
    half = ROPE_DIM // 2
    inv_freq = np.power(ROPE_THETA, -2.0 * np.arange(half, dtype=np.float64) / ROPE_DIM)
    ang = np.arange(t, dtype=np.float64)[:, None] * inv_freq[None, :]
    cos, sin = np.cos(ang), np.sin(ang)
    d = np.arange(LANES) % HEAD_DIM
    first = (d < half)[None, :]
    second = ((d >= half) & (d < ROPE_DIM))[None, :]
    idx = np.where(d < ROPE_DIM, d % half, 0)
    cos_l, sin_l = cos[:, idx], sin[:, idx]
    c = np.where(first | second, cos_l, 1.0)
    sa = np.where(first, -sin_l, 0.0)
    sb = np.where(second, sin_l, 0.0)
    return tuple(jnp.asarray(a, jnp.float32) for a in (c, sa, sb))


def _slab_rows(n_rows, n_steps):
    tiles = n_rows // BF16_ROWS
    assert tiles * BF16_ROWS == n_rows
    n_slabs = max(g for g in range(1, tiles + 1) if tiles % g == 0 and g <= n_steps)
    return n_rows // n_slabs


def _inproj(x, sc1, sh1, g_mix, w_in_t, gains, tables, bd, to_cast):
    b, t, d = x.shape
    tm = min(ROW_TILE, t)
    w = MOBA_WIDTH
    nt = t // tm
    cast_specs, cast_shapes = [], []
    for a in to_cast:
        rows = _slab_rows(a.shape[0], b * nt)
        last = a.shape[0] // rows - 1
        slab = lambda i, j, last=last: (jnp.minimum(i * nt + j, last), 0)
        cast_specs.append(pl.BlockSpec((rows, a.shape[1]), slab))
        cast_shapes.append(jax.ShapeDtypeStruct(a.shape, jnp.bfloat16))
    row = lambda i, j: (i, 0, 0)
    tile = lambda i, j: (i, j, 0)
    tile_t = lambda i, j: (i, 0, j)
    const = lambda i, j: (0, 0)
    wide = jax.ShapeDtypeStruct((b, t, w), jnp.bfloat16)
    wide_t = jax.ShapeDtypeStruct((b, w, t), jnp.bfloat16)
    out_specs = ([pl.BlockSpec((1, tm, w), tile)] * 4 + [pl.BlockSpec((1, tm, LANES), tile)]
                 + [pl.BlockSpec((1, w, tm), tile_t)] * 2)
    return pl.pallas_call(
        _inproj_kernel,
        grid=(b, t // tm),
        in_specs=[pl.BlockSpec((1, tm, d), tile),
                  pl.BlockSpec((1, 1, d), row),
                  pl.BlockSpec((1, 1, d), row),
                  pl.BlockSpec((1, d), const),
                  pl.BlockSpec(w_in_t.shape, const, pipeline_mode=pl.Buffered(1)),
                  pl.BlockSpec((4, w), const),
                  pl.BlockSpec((tm, LANES), lambda i, j: (j, 0)),
                  pl.BlockSpec((tm, LANES), lambda i, j: (j, 0)),
                  pl.BlockSpec((tm, LANES), lambda i, j: (j, 0)),
                  pl.BlockSpec((2 * LANES, 2 * LANES), const)] + cast_specs,
        out_specs=out_specs + cast_specs,
        out_shape=([wide] * 4 + [jax.ShapeDtypeStruct((b, t, LANES), jnp.float32)] + [wide_t] * 2
                   + cast_shapes),
        scratch_shapes=[pltpu.VMEM((IN_QKV_COLS + LANES, d), jnp.bfloat16)],
        compiler_params=pltpu.CompilerParams(vmem_limit_bytes=VMEM_LIMIT),
        name="inproj",
    )(x, sc1, sh1, g_mix, w_in_t, gains, *tables, bd, *to_cast)


def _split3(a):
    hi = _bf16(a)
    r1 = a - hi.astype(jnp.float32)
    mid = _bf16(r1)
    lo = _bf16(r1 - mid.astype(jnp.float32))
    return hi, mid, lo


def _group_head(lane_idx):
    return 2 * ((lane_idx % HEAD_DIM) // AUX_GROUP) + jnp.where(lane_idx < HEAD_DIM, 1, 0)


def _routing_kernel(mq_ref, mk_ref, fl_ref, bf_ref, mqa_ref, fqa_ref, fka_ref):
    t = mq_ref.shape[1]
    nb = t // MOBA_BLOCK
    q = mq_ref[0]
    k = mk_ref[0]
    blk_of_col = lax.broadcasted_iota(jnp.int32, (nb, t), 1) // MOBA_BLOCK
    blk_row = lax.broadcasted_iota(jnp.int32, (nb, t), 0)
    ind = _bf16(jnp.where(blk_of_col == blk_row, 1.0, 0.0))
    kmean = _dot(ind, k) * (1.0 / MOBA_BLOCK)
    kmt = jnp.concatenate([kmean] * (LANES // nb), axis=0)
    r_head = _group_head(lax.broadcasted_iota(jnp.int32, kmt.shape, 0))
    c_head = lax.broadcasted_iota(jnp.int32, kmt.shape, 1) // HEAD_DIM
    kmt = jnp.where(r_head == c_head, kmt, 0.0)
    k_hi = _bf16(kmt)
    k_lo = _bf16(kmt - k_hi.astype(jnp.float32))
    gate = (_dot_nt(q, k_hi) + _dot_nt(q, k_lo)) * (1.0 / Q_SCALE)

    lane = lax.broadcasted_iota(jnp.int32, (1, LANES), 1)
    j = lane % nb
    own = lax.broadcasted_iota(jnp.int32, (t, 1), 0) // MOBA_BLOCK
    past = j < own
    g = jnp.where(past, gate, -jnp.inf)
    rank = jnp.zeros(gate.shape, jnp.int32)
    for d in range(1, nb):
        other = pltpu.roll(g, LANES - d, 1)
        first = ((j + d) % nb) < j
        beats = (other > g) | (first & (other == g))
        rank = rank + jnp.where(beats, 1, 0)
    attend = (past & (rank < MOBA_TOPK)) | (j == own)
    mqa_ref[0] = jnp.where(attend, 0.0, NEG_INF)

    lane_head = _group_head(lane)
    bias = jnp.zeros((1, LANES), jnp.float32)
    for head in range(N_FOX_HEADS):
        bias = jnp.where(lane_head == head, bf_ref[:, head:head + 1], bias)
    z = fl_ref[0] + bias
    logf = -(jnp.maximum(-z, 0.0) + jnp.log1p(jnp.exp(-jnp.abs(z))))
    rr = lax.broadcasted_iota(jnp.int32, (MOBA_BLOCK, MOBA_BLOCK), 0)
    cc = lax.broadcasted_iota(jnp.int32, (MOBA_BLOCK, MOBA_BLOCK), 1)
    tri = _bf16(jnp.where(cc <= rr, 1.0, 0.0))
    slot = lane % AUX_GROUP
    carry = jnp.zeros((1, LANES), jnp.float32)
    for i in range(nb):
        rows = slice(i * MOBA_BLOCK, (i + 1) * MOBA_BLOCK)
        hi, mid, lo = _split3(logf[rows, :])
        cum = (_dot(tri, hi) + _dot(tri, mid)) + _dot(tri, lo) + carry
        carry = cum[MOBA_BLOCK - 1:MOBA_BLOCK, :]
        parts = [p.astype(jnp.float32) for p in _split3(cum * LOG2E)]
        fq = jnp.where(slot < 2 * N_SPLIT, 1.0, 0.0)
        fk = jnp.where(slot < N_SPLIT, 1.0, 0.0)
        for s in range(N_SPLIT):
            fq = jnp.where(slot == s, parts[s], fq)
            fk = jnp.where(slot == N_SPLIT + s, -parts[s], fk)
        fqa_ref[0, rows, :] = fq
        fka_ref[0, rows, :] = fk


def _routing(mq, mk, fl, b_forget):
    b, t, w = mq.shape
    full = lambda i: (i, 0, 0)
    aux = jax.ShapeDtypeStruct((b, t, LANES), jnp.float32)
    return pl.pallas_call(
        _routing_kernel,
        grid=(b,),
        in_specs=[pl.BlockSpec((1, t, w), full),
                  pl.BlockSpec((1, t, w), full),
                  pl.BlockSpec((1, t, LANES), full),
                  pl.BlockSpec((1, N_FOX_HEADS), lambda i: (0, 0))],
        out_specs=[pl.BlockSpec((1, t, LANES), full)] * 3,
        out_shape=[aux] * 3,
        compiler_params=pltpu.CompilerParams(vmem_limit_bytes=VMEM_LIMIT),
        name="routing",
    )(mq, mk, fl, b_forget)


def _attn_kernel(*refs, moba):
    if moba:
        q_ref, qa_ref, k_ref, vt_ref, o_ref, kx_ref, vx_ref = refs
    else:
        q_ref, qa_ref, k_ref, vt_ref, ka_ref, o_ref, kx_ref, vx_ref = refs
    tq = SEQ_TILE
    t = k_ref.shape[1]
    nb = t // tq
    n_local = q_ref.shape[2] // LANES
    lane = lax.broadcasted_iota(jnp.int32, (1, LANES), 1)
    low = lane < HEAD_DIM
    in_group = (lane % HEAD_DIM) < AUX_GROUP // 2

    def aux_shift(pi):
        return (LANES - (pl.program_id(1) * n_local + pi) * AUX_GROUP) % LANES

    for pi in range(n_local):
        cols = slice(pi * LANES, (pi + 1) * LANES)
        k2 = k_ref[0, :, cols]
        if moba:
            blk = lax.broadcasted_iota(jnp.int32, (t, LANES), 0) // MOBA_BLOCK
            ln = lax.broadcasted_iota(jnp.int32, (t, LANES), 1) % HEAD_DIM
            ka = _bf16(jnp.where(ln == blk, 1.0, 0.0))
        else:
            ka = _bf16(pltpu.roll(ka_ref[0], aux_shift(pi), 1))
        kx_ref[2 * pi] = jnp.where(low, k2, ka)
        kx_ref[2 * pi + 1] = jnp.where(low, ka, k2)
        for h in range(2):
            first = pi * LANES + h * HEAD_DIM
            vx_ref[2 * pi + h, 0:HEAD_DIM, :] = vt_ref[0, first:first + HEAD_DIM, :]
            vx_ref[2 * pi + h, HEAD_DIM:V_ROWS, :] = jnp.ones((V_ROWS - HEAD_DIM, t), jnp.bfloat16)

    key = lax.broadcasted_iota(jnp.int32, (tq, tq), 0)
    qry = lax.broadcasted_iota(jnp.int32, (tq, tq), 1)
    causal = key <= qry

    def scores(n, pi, h):
        rows = slice(n * tq, (n + 1) * tq)
        q2 = q_ref[0, rows, pi * LANES:(pi + 1) * LANES]
        qa = pltpu.roll(qa_ref[0, rows, :], aux_shift(pi), 1)
        qa = _bf16(jnp.where(in_group, qa, 0.0))
        qx = jnp.where(low, q2, qa) if h == 0 else jnp.where(low, qa, q2)
        hd = 2 * pi + h
        sd = jnp.where(causal, _dot_nt(kx_ref[hd, rows, :], qx), NEG_INF)
        m = jnp.max(sd, axis=0, keepdims=True)
        sp = None
        if n > 0:
            sp = _dot_nt(kx_ref[hd, 0:n * tq, :], qx)
            m = jnp.maximum(m, jnp.max(sp, axis=0, keepdims=True))
        return (n, pi, h), sd, sp, m

    def weighted_values(chain, sd, sp, m):
        n, pi, h = chain
        hd = 2 * pi + h
        rows = slice(n * tq, (n + 1) * tq)
        acc = _dot(vx_ref[hd, :, rows], _bf16(jnp.exp2(sd - m)))
        if n > 0:
            acc = acc + _dot(vx_ref[hd, :, 0:n * tq], _bf16(jnp.exp2(sp - m)))
        return acc[0:HEAD_DIM, :] / acc[HEAD_DIM:HEAD_DIM + 1, :]

    chains = [(n, pi, h) for n in reversed(range(nb)) for pi in range(n_local) for h in range(2)]
    outs = {}
    queue = [scores(*c) for c in chains[:ATTN_LOOKAHEAD]]
    for nxt in chains[ATTN_LOOKAHEAD:] + [None] * ATTN_LOOKAHEAD:
        if nxt is not None:
            queue.append(scores(*nxt))
        done = queue.pop(0)
        outs[done[0]] = weighted_values(*done)
        n, pi, h = done[0]
        if h == 1:
            o_t = jnp.concatenate([outs.pop((n, pi, 0)), outs.pop((n, pi, 1))], axis=0)
            o_ref[0, n * tq:(n + 1) * tq, pi * LANES:(pi + 1) * LANES] = _bf16(o_t.T)


def _attention(moba, q, qa, k, vt, ka=None):
    b, t, w = q.shape
    wl = ATTN_PAIRS_PER_STEP * LANES
    assert w % wl == 0 and t % SEQ_TILE == 0
    spec = pl.BlockSpec((1, t, wl), lambda i, p: (i, 0, p))
    spec_t = pl.BlockSpec((1, wl, t), lambda i, p: (i, p, 0))
    spec_aux = pl.BlockSpec((1, t, LANES), lambda i, p: (i, 0, 0))
    args = [q, qa, k, vt] if moba else [q, qa, k, vt, ka]
    specs = [spec, spec_aux, spec, spec_t] + ([] if moba else [spec_aux])
    kern = lambda *refs: _attn_kernel(*refs, moba=moba)
    return pl.pallas_call(
        kern,
        grid=(b, w // wl),
        in_specs=specs,
        out_specs=spec,
        out_shape=jax.ShapeDtypeStruct((b, t, w), jnp.bfloat16),
        scratch_shapes=[pltpu.VMEM((2 * ATTN_PAIRS_PER_STEP, t, LANES), jnp.bfloat16),
                        pltpu.VMEM((2 * ATTN_PAIRS_PER_STEP, V_ROWS, t), jnp.bfloat16)],
        compiler_params=pltpu.CompilerParams(vmem_limit_bytes=VMEM_LIMIT),
        name="moba" if moba else "fox",
    )(*args)


def _ffn_kernel(x_ref, om_ref, of_ref, wo_ref, gt1_ref, g_ref, sc_ref, sh_ref, wup_ref, cv_ref,
                wdn_ref, gt2_ref, out_ref, x1_ref, halo_ref):
    tm = x_ref.shape[1]

    @pl.when(pl.program_id(1) == 0)
    def _():
        halo_ref[...] = jnp.zeros(halo_ref.shape, jnp.float32)

    pm = tm // FFN_PARTS
    slabs = [slice(p * pm, (p + 1) * pm) for p in range(FFN_PARTS)]
    attn = [_dot(jnp.concatenate([om_ref[0, r, :], of_ref[0, r, :]], axis=1), wo_ref[...])
            for r in slabs]

    def normed(p):
        x1 = x_ref[0, slabs[p], :] + gt1_ref[0] * attn[p]
        x1_ref[slabs[p], :] = x1
        return _bf16(_rms_mod(x1, g_ref[...], sc_ref[0], sh_ref[0]))

    def up_proj(hn, c):
        gate_cols = slice(c * FF_CHUNK, (c + 1) * FF_CHUNK)
        val_cols = slice(D_FF + c * FF_CHUNK, D_FF + (c + 1) * FF_CHUNK)
        u = jnp.concatenate([_dot(hn, wup_ref[:, gate_cols]), _dot(hn, wup_ref[:, val_cols])],
                            axis=1)
        return u, gate_cols, val_cols

    hn = normed(0)
    tails = [halo_ref[c] for c in range(N_FF_CHUNKS)]
    for p in range(FFN_PARTS):
        hn_next = None
        hmid = []
        ahead = [up_proj(hn, c) for c in range(FFN_LOOKAHEAD)]
        for c in range(N_FF_CHUNKS):
            if c + FFN_LOOKAHEAD < N_FF_CHUNKS:
                ahead.append(up_proj(hn, c + FFN_LOOKAHEAD))
            u, gate_cols, val_cols = ahead.pop(0)
            ext = jnp.concatenate([tails[c], u], axis=0)
            tails[c] = u[pm - HALO:pm, :]
            cv = jnp.concatenate([cv_ref[:, gate_cols], cv_ref[:, val_cols]], axis=1)
            u1 = pltpu.roll(ext, 1, 0)[HALO:, :]
            u2 = pltpu.roll(ext, 2, 0)[HALO:, :]
            uc = cv[0:1, :] * u2 + cv[1:2, :] * u1 + cv[2:3, :] * u + cv[3:4, :]
            a = uc[:, :FF_CHUNK]
            val = uc[:, FF_CHUNK:]
            half = 0.5 * a
            hmid.append(_bf16((half + half * jnp.tanh(half)) * val))
            if c == FFN_LOOKAHEAD and p + 1 < FFN_PARTS:
                hn_next = normed(p + 1)
        ffn = _dot(jnp.concatenate(hmid, axis=1), wdn_ref[...])
        out_ref[0, slabs[p], :] = x1_ref[slabs[p], :] + gt2_ref[0] * ffn
        hn = hn_next
    for c in range(N_FF_CHUNKS):
        halo_ref[c] = tails[c]


def _ffn(x, om, of, wo, gt1, g_ffn, sc2, sh2, wup, cv, wdn, gt2):
    b, t, d = x.shape
    tm = min(ROW_TILE, t)
    row = lambda i, j: (i, 0, 0)
    tile = lambda i, j: (i, j, 0)
    const2 = lambda i, j: (0, 0)
    once = dict(pipeline_mode=pl.Buffered(1))
    return pl.pallas_call(
        _ffn_kernel,
        grid=(b, t // tm),
        in_specs=[pl.BlockSpec((1, tm, d), tile),
                  pl.BlockSpec((1, tm, MOBA_WIDTH), tile),
                  pl.BlockSpec((1, tm, FOX_WIDTH), tile),
                  pl.BlockSpec((d, d), const2, **once),
                  pl.BlockSpec((1, 1, d), row),
                  pl.BlockSpec((1, d), const2),
                  pl.BlockSpec((1, 1, d), row),
                  pl.BlockSpec((1, 1, d), row),
                  pl.BlockSpec((d, 2 * D_FF), const2, **once),
                  pl.BlockSpec((HALO, 2 * D_FF), const2, **once),
                  pl.BlockSpec((D_FF, d), const2, **once),
                  pl.BlockSpec((1, 1, d), row)],
        out_specs=pl.BlockSpec((1, tm, d), tile),
        out_shape=jax.ShapeDtypeStruct((b, t, d), jnp.float32),
        scratch_shapes=[pltpu.VMEM((tm, d), jnp.float32),
                        pltpu.VMEM((N_FF_CHUNKS, HALO, 2 * FF_CHUNK), jnp.float32)],
        compiler_params=pltpu.CompilerParams(vmem_limit_bytes=VMEM_LIMIT),
        name="ffn",
    )(x, om, of, wo, gt1, g_ffn, sc2, sh2, wup, cv, wdn, gt2)


def _layer(x, c, w_ada, b_ada, g_mix, w_in, b_forget, moba_q_gain, moba_k_gain, fox_q_gain,
           fox_k_gain, w_out, g_ffn, w_up, conv_w, conv_b, w_down):
    b, t, d = x.shape
    nb = t // MOBA_BLOCK
    assert d == D_MODEL and t % SEQ_TILE == 0
    assert AUX_GROUP % nb == 0 and 2 * nb <= AUX_GROUP and 2 * N_SPLIT <= AUX_GROUP // 2

    mod = _adaln(c, w_ada, b_ada).reshape(b, 6, 1, d)
    sh1, sc1, gt1, sh2, sc2, gt2 = [mod[:, i] for i in range(6)]

    assert w_in.shape == (d, IN_QKV_COLS + N_FOX_HEADS)
    gains = jnp.stack([jnp.tile(g, N_MOBA_HEADS) for g in
                       (moba_q_gain, moba_k_gain, fox_q_gain, fox_k_gain)])
    r = np.arange(2 * LANES) // HEAD_DIM
    bd = jnp.asarray(r[:, None] == r[None, :], jnp.bfloat16)
    mq, mk, fq, fk, fl, mvt, fvt, wo16, wu16, wd16 = _inproj(
        x, sc1, sh1, g_mix.reshape(1, d), w_in.T, gains, _rope_tables(t), bd, [w_out, w_up, w_down])

    mqa, fqa, fka = _routing(mq, mk, fl, b_forget.reshape(1, N_FOX_HEADS))
    o_moba = _attention(True, mq, mqa, mk, mvt)
    o_fox = _attention(False, fq, fqa, fk, fvt, fka)

    cv = jnp.concatenate([conv_w, conv_b[None, :],
                          jnp.zeros((HALO - CONV_WIDTH - 1, 2 * D_FF), jnp.float32)], axis=0)
    return _ffn(x, o_moba, o_fox, wo16, gt1, g_ffn.reshape(1, d), sc2, sh2, wu16, cv, wd16, gt2)


def kernel(x, c, w_ada, b_ada, g_mix, w_in, b_forget, moba_q_gain, moba_k_gain, fox_q_gain,
           fox_k_gain, w_out, g_ffn, w_up, conv_w, conv_b, w_down):
    for l in range(w_ada.shape[0]):
        x = _layer(x, c, w_ada[l], b_ada[l], g_mix[l], w_in[l], b_forget[l], moba_q_gain[l],
                   moba_k_gain[l], fox_q_gain[l], fox_k_gain[l], w_out[l], g_ffn[l], w_up[l],
                   conv_w[l], conv_b[l], w_down[l])
    return x
```

```python
import math

import jax
import jax.numpy as jnp
import numpy as np
from jax import lax
from jax.experimental import pallas as pl
from jax.experimental.pallas import tpu as pltpu

D_MODEL = 1024
HEAD_DIM = 64
N_MOBA_HEADS = 8
N_FOX_HEADS = 8
MOBA_WIDTH = N_MOBA_HEADS * HEAD_DIM
FOX_WIDTH = N_FOX_HEADS * HEAD_DIM
MOBA_BLOCK = 256
MOBA_TOPK = 3
ROPE_THETA = 500000.0
ROPE_DIM = HEAD_DIM // 4
D_FF = 2816
CONV_WIDTH = 3
NORM_EPS = 1e-6
NEG_INF = -1e30
LOG2E = math.log2(math.e)
Q_SCALE = HEAD_DIM ** -0.5 * LOG2E
LANES = 128
BF16_ROWS = 16
IN_QKV_COLS = 3 * MOBA_WIDTH + 3 * FOX_WIDTH
V_ROWS = HEAD_DIM + BF16_ROWS
SEQ_TILE = 256
ROW_TILE = 512
ADALN_COLS = 1536
ATTN_PAIRS_PER_STEP = 2
ATTN_LOOKAHEAD = 5
FFN_PARTS = 2
FFN_LOOKAHEAD = 2
FF_CHUNK = 256
N_FF_CHUNKS = D_FF // FF_CHUNK
HALO = 8
N_SPLIT = 3
AUX_GROUP = 16
VMEM_LIMIT = 56 * 1024 * 1024

_NT = (((1,), (1,)), ((), ()))


def _bf16(a):
    return a.astype(jnp.bfloat16)


def _dot(a, b):
    return jnp.dot(a, b, preferred_element_type=jnp.float32)


def _dot_nt(a, b):
    return lax.dot_general(a, b, _NT, preferred_element_type=jnp.float32)


def _sigmoid(a):
    return 1.0 / (1.0 + jnp.exp(-a))


def _adaln_kernel(c_ref, w_ref, b_ref, o_ref):
    c = c_ref[...]
    s = c * _sigmoid(c)
    o_ref[...] = _dot(_bf16(s), _bf16(w_ref[...])) + b_ref[...]


def _adaln(c, w_ada, b_ada):
    b, d = c.shape
    n = w_ada.shape[1]
    tn = ADALN_COLS
    return pl.pallas_call(
        _adaln_kernel,
        grid=(n // tn,),
        in_specs=[pl.BlockSpec((b, d), lambda j: (0, 0)),
                  pl.BlockSpec((d, tn), lambda j: (0, j)),
                  pl.BlockSpec((1, tn), lambda j: (0, j))],
        out_specs=pl.BlockSpec((b, tn), lambda j: (0, j)),
        out_shape=jax.ShapeDtypeStruct((b, n), jnp.float32),
        compiler_params=pltpu.CompilerParams(vmem_limit_bytes=VMEM_LIMIT),
        name="adaln",
    )(c, w_ada, b_ada.reshape(1, n))


def _rms_mod(x, g, sc, sh):
    ms = jnp.mean(x * x, axis=-1, keepdims=True)
    y = x * lax.rsqrt(ms + NORM_EPS)
    return (y * g) * (1.0 + sc) + sh


def _head_norm(p, gain, bd):
    sq = _bf16(p * p)
    half = 2 * LANES
    ss = jnp.concatenate([_dot(sq[:, :half], bd), _dot(sq[:, half:], bd)], axis=1)
    return (p * lax.rsqrt(ss * (1.0 / HEAD_DIM) + NORM_EPS)) * gain


def _rope(y, c, sa, sb):
    half = ROPE_DIM // 2
    outs = []
    for i in range(y.shape[1] // LANES):
        yc = y[:, i * LANES:(i + 1) * LANES]
        up = pltpu.roll(yc, LANES - half, 1)
        dn = pltpu.roll(yc, half, 1)
        outs.append(yc * c + up * sa + dn * sb)
    return jnp.concatenate(outs, axis=1)


def _inproj_kernel(x_ref, sc_ref, sh_ref, g_ref, wint_ref, gains_ref, rc_ref, rsa_ref, rsb_ref,
                   bd_ref, *refs):
    n_cast = (len(refs) - 8) // 2
    cast_in, refs = refs[:n_cast], refs[n_cast:]
    mq_ref, mk_ref, fq_ref, fk_ref, fl_ref, mvt_ref, fvt_ref = refs[:7]
    cast_out, w_ref = refs[7:7 + n_cast], refs[-1]
    for src, dst in zip(cast_in, cast_out):
        dst[...] = _bf16(src[...])
    w = MOBA_WIDTH
    n_groups = IN_QKV_COLS // w

    @pl.when((pl.program_id(0) == 0) & (pl.program_id(1) == 0))
    def _():
        for i in range(n_groups):
            w_ref[i * w:(i + 1) * w, :] = _bf16(wint_ref[i * w:(i + 1) * w, :])
        logit_rows = wint_ref[IN_QKV_COLS:IN_QKV_COLS + N_FOX_HEADS, :]
        r = lax.broadcasted_iota(jnp.int32, (LANES, 1), 0)
        r_head = jnp.where(r % AUX_GROUP < 2 * N_SPLIT, _group_head(r), -1)
        wlog = jnp.zeros((LANES, logit_rows.shape[1]), jnp.float32)
        for head in range(N_FOX_HEADS):
            wlog = jnp.where(r_head == head, logit_rows[head:head + 1, :], wlog)
        w_ref[IN_QKV_COLS:IN_QKV_COLS + LANES, :] = _bf16(wlog)

    half = x_ref.shape[1] // 2
    hn_halves = [_bf16(_rms_mod(x_ref[0, r * half:(r + 1) * half, :], g_ref[...], sc_ref[0], sh_ref[0]))
                 for r in range(2)]
    p_mq = jnp.concatenate([_dot_nt(h, w_ref[0:w, :]) for h in hn_halves], axis=0)
    hn = jnp.concatenate(hn_halves, axis=0)
    bd = bd_ref[...]
    c, sa, sb = rc_ref[...], rsa_ref[...], rsb_ref[...]
    proj = lambda i: _dot_nt(hn, w_ref[i * w:(i + 1) * w, :])

    p_mk = proj(1)
    mq = _rope(_head_norm(p_mq, gains_ref[0:1, :], bd), c, sa, sb)
    mq_ref[0] = _bf16(mq * Q_SCALE)
    p_fq = proj(3)
    mk = _rope(_head_norm(p_mk, gains_ref[1:2, :], bd), c, sa, sb)
    mk_ref[0] = _bf16(mk)
    p_fk = proj(4)
    fq = _head_norm(p_fq, gains_ref[2:3, :], bd)
    fq_ref[0] = _bf16(fq * Q_SCALE)
    fl_ref[0] = _dot_nt(hn, w_ref[IN_QKV_COLS:IN_QKV_COLS + LANES, :])
    mvt = _dot_nt(w_ref[2 * w:3 * w, :], hn)
    fk = _head_norm(p_fk, gains_ref[3:4, :], bd)
    fk_ref[0] = _bf16(fk)
    mvt_ref[0] = _bf16(mvt)
    fvt_ref[0] = _bf16(_dot_nt(w_ref[5 * w:6 * w, :], hn))


def _rope_tables(t):
    half = ROPE_DIM // 2
    inv_freq = np.power(ROPE_THETA, -2.0 * np.arange(half, dtype=np.float64) / ROPE_DIM)
    ang = np.arange(t, dtype=np.float64)[:, None] * inv_freq[None, :]
    cos, sin = np.cos(ang), np.sin(ang)
    d = np.arange(LANES) % HEAD_DIM
    first = (d < half)[None, :]
    second = ((d >= half) & (d < ROPE_DIM))[None, :]
    idx = np.where(d < ROPE_DIM, d % half, 0)
    cos_l, sin_l = cos[:, idx], sin[:, idx]
    c = np.where(first | second, cos_l, 1.0)
    sa = np.where(first, -sin_l, 0.0)
    sb = np.where(second, sin_l, 0.0)
    return tuple(jnp.asarray(a, jnp.float32) for a in (c, sa, sb))


def _slab_rows(n_rows, n_steps):
    tiles = n_rows // BF16_ROWS
    assert tiles * BF16_ROWS == n_rows
    n_slabs = max(g for g in range(1, tiles + 1) if tiles % g == 0 and g <= n_steps)
    return n_rows // n_slabs


def _inproj(x, sc1, sh1, g_mix, w_in_t, gains, tables, bd, to_cast):
    b, t, d = x.shape
    tm = min(ROW_TILE, t)
    w = MOBA_WIDTH
    nt = t // tm
    cast_specs, cast_shapes = [], []
    for a in to_cast:
        rows = _slab_rows(a.shape[0], b * nt)
        last = a.shape[0] // rows - 1
        slab = lambda i, j, last=last: (jnp.minimum(i * nt + j, last), 0)
        cast_specs.append(pl.BlockSpec((rows, a.shape[1]), slab))
        cast_shapes.append(jax.ShapeDtypeStruct(a.shape, jnp.bfloat16))
    row = lambda i, j: (i, 0, 0)
    tile = lambda i, j: (i, j, 0)
    tile_t = lambda i, j: (i, 0, j)
    const = lambda i, j: (0, 0)
    wide = jax.ShapeDtypeStruct((b, t, w), jnp.bfloat16)
    wide_t = jax.ShapeDtypeStruct((b, w, t), jnp.bfloat16)
    out_specs = ([pl.BlockSpec((1, tm, w), tile)] * 4 + [pl.BlockSpec((1, tm, LANES), tile)]
                 + [pl.BlockSpec((1, w, tm), tile_t)] * 2)
    return pl.pallas_call(
        _inproj_kernel,
        grid=(b, t // tm),
        in_specs=[pl.BlockSpec((1, tm, d), tile),
                  pl.BlockSpec((1, 1, d), row),
                  pl.BlockSpec((1, 1, d), row),
                  pl.BlockSpec((1, d), const),
                  pl.BlockSpec(w_in_t.shape, const, pipeline_mode=pl.Buffered(1)),
                  pl.BlockSpec((4, w), const),
                  pl.BlockSpec((tm, LANES), lambda i, j: (j, 0)),
                  pl.BlockSpec((tm, LANES), lambda i, j: (j, 0)),
                  pl.BlockSpec((tm, LANES), lambda i, j: (j, 0)),
                  pl.BlockSpec((2 * LANES, 2 * LANES), const)] + cast_specs,
        out_specs=out_specs + cast_specs,
        out_shape=([wide] * 4 + [jax.ShapeDtypeStruct((b, t, LANES), jnp.float32)] + [wide_t] * 2
                   + cast_shapes),
        scratch_shapes=[pltpu.VMEM((IN_QKV_COLS + LANES, d), jnp.bfloat16)],
        compiler_params=pltpu.CompilerParams(vmem_limit_bytes=VMEM_LIMIT),
        name="inproj",
    )(x, sc1, sh1, g_mix, w_in_t, gains, *tables, bd, *to_cast)


def _split3(a):
    hi = _bf16(a)
    r1 = a - hi.astype(jnp.float32)
    mid = _bf16(r1)
    lo = _bf16(r1 - mid.astype(jnp.float32))
    return hi, mid, lo


def _group_head(lane_idx):
    return 2 * ((lane_idx % HEAD_DIM) // AUX_GROUP) + jnp.where(lane_idx < HEAD_DIM, 1, 0)


def _routing_kernel(mq_ref, mk_ref, fl_ref, bf_ref, mqa_ref, fqa_ref, fka_ref):
    t = mq_ref.shape[1]
    nb = t // MOBA_BLOCK
    q = mq_ref[0]
    k = mk_ref[0]
    blk_of_col = lax.broadcasted_iota(jnp.int32, (nb, t), 1) // MOBA_BLOCK
    blk_row = lax.broadcasted_iota(jnp.int32, (nb, t), 0)
    ind = _bf16(jnp.where(blk_of_col == blk_row, 1.0, 0.0))
    kmean = _dot(ind, k) * (1.0 / MOBA_BLOCK)
    kmt = jnp.concatenate([kmean] * (LANES // nb), axis=0)
    r_head = _group_head(lax.broadcasted_iota(jnp.int32, kmt.shape, 0))
    c_head = lax.broadcasted_iota(jnp.int32, kmt.shape, 1) // HEAD_DIM
    kmt = jnp.where(r_head == c_head, kmt, 0.0)
    k_hi = _bf16(kmt)
    k_lo = _bf16(kmt - k_hi.astype(jnp.float32))
    gate = (_dot_nt(q, k_hi) + _dot_nt(q, k_lo)) * (1.0 / Q_SCALE)

    lane = lax.broadcasted_iota(jnp.int32, (1, LANES), 1)
    j = lane % nb
    own = lax.broadcasted_iota(jnp.int32, (t, 1), 0) // MOBA_BLOCK
    past = j < own
    g = jnp.where(past, gate, -jnp.inf)
    rank = jnp.zeros(gate.shape, jnp.int32)
    for d in range(1, nb):
        other = pltpu.roll(g, LANES - d, 1)
        first = ((j + d) % nb) < j
        beats = (other > g) | (first & (other == g))
        rank = rank + jnp.where(beats, 1, 0)
    attend = (past & (rank < MOBA_TOPK)) | (j == own)
    mqa_ref[0] = jnp.where(attend, 0.0, NEG_INF)

    lane_head = _group_head(lane)
    bias = jnp.zeros((1, LANES), jnp.float32)
    for head in range(N_FOX_HEADS):
        bias = jnp.where(lane_head == head, bf_ref[:, head:head + 1], bias)
    z = fl_ref[0] + bias
    logf = -(jnp.maximum(-z, 0.0) + jnp.log1p(jnp.exp(-jnp.abs(z))))
    rr = lax.broadcasted_iota(jnp.int32, (MOBA_BLOCK, MOBA_BLOCK), 0)
    cc = lax.broadcasted_iota(jnp.int32, (MOBA_BLOCK, MOBA_BLOCK), 1)
    tri = _bf16(jnp.where(cc <= rr, 1.0, 0.0))
    slot = lane % AUX_GROUP
    carry = jnp.zeros((1, LANES), jnp.float32)
    for i in range(nb):
        rows = slice(i * MOBA_BLOCK, (i + 1) * MOBA_BLOCK)
        hi, mid, lo = _split3(logf[rows, :])
        cum = (_dot(tri, hi) + _dot(tri, mid)) + _dot(tri, lo) + carry
        carry = cum[MOBA_BLOCK - 1:MOBA_BLOCK, :]
        parts = [p.astype(jnp.float32) for p in _split3(cum * LOG2E)]
        fq = jnp.where(slot < 2 * N_SPLIT, 1.0, 0.0)
        fk = jnp.where(slot < N_SPLIT, 1.0, 0.0)
        for s in range(N_SPLIT):
            fq = jnp.where(slot == s, parts[s], fq)
            fk = jnp.where(slot == N_SPLIT + s, -parts[s], fk)
        fqa_ref[0, rows, :] = fq
        fka_ref[0, rows, :] = fk


def _routing(mq, mk, fl, b_forget):
    b, t, w = mq.shape
    full = lambda i: (i, 0, 0)
    aux = jax.ShapeDtypeStruct((b, t, LANES), jnp.float32)
    return pl.pallas_call(
        _routing_kernel,
        grid=(b,),
        in_specs=[pl.BlockSpec((1, t, w), full),
                  pl.BlockSpec((1, t, w), full),
                  pl.BlockSpec((1, t, LANES), full),
                  pl.BlockSpec((1, N_FOX_HEADS), lambda i: (0, 0))],
        out_specs=[pl.BlockSpec((1, t, LANES), full)] * 3,
        out_shape=[aux] * 3,
        compiler_params=pltpu.CompilerParams(vmem_limit_bytes=VMEM_LIMIT),
        name="routing",
    )(mq, mk, fl, b_forget)


def _attn_kernel(*refs, moba):
    if moba:
        q_ref, qa_ref, k_ref, vt_ref, o_ref, kx_ref, vx_ref = refs
    else:
        q_ref, qa_ref, k_ref, vt_ref, ka_ref, o_ref, kx_ref, vx_ref = refs
    tq = SEQ_TILE
    t = k_ref.shape[1]
    nb = t // tq
    n_local = q_ref.shape[2] // LANES
    lane = lax.broadcasted_iota(jnp.int32, (1, LANES), 1)
    low = lane < HEAD_DIM
    in_group = (lane % HEAD_DIM) < AUX_GROUP // 2

    def aux_shift(pi):
        return (LANES - (pl.program_id(1) * n_local + pi) * AUX_GROUP) % LANES

    for pi in range(n_local):
        cols = slice(pi * LANES, (pi + 1) * LANES)
        k2 = k_ref[0, :, cols]
        if moba:
            blk = lax.broadcasted_iota(jnp.int32, (t, LANES), 0) // MOBA_BLOCK
            ln = lax.broadcasted_iota(jnp.int32, (t, LANES), 1) % HEAD_DIM
            ka = _bf16(jnp.where(ln == blk, 1.0, 0.0))
        else:
            ka = _bf16(pltpu.roll(ka_ref[0], aux_shift(pi), 1))
        kx_ref[2 * pi] = jnp.where(low, k2, ka)
        kx_ref[2 * pi + 1] = jnp.where(low, ka, k2)
        for h in range(2):
            first = pi * LANES + h * HEAD_DIM
            vx_ref[2 * pi + h, 0:HEAD_DIM, :] = vt_ref[0, first:first + HEAD_DIM, :]
            vx_ref[2 * pi + h, HEAD_DIM:V_ROWS, :] = jnp.ones((V_ROWS - HEAD_DIM, t), jnp.bfloat16)

    key = lax.broadcasted_iota(jnp.int32, (tq, tq), 0)
    qry = lax.broadcasted_iota(jnp.int32, (tq, tq), 1)
    causal = key <= qry

    def scores(n, pi, h):
        rows = slice(n * tq, (n + 1) * tq)
        q2 = q_ref[0, rows, pi * LANES:(pi + 1) * LANES]
        qa = pltpu.roll(qa_ref[0, rows, :], aux_shift(pi), 1)
        qa = _bf16(jnp.where(in_group, qa, 0.0))
        qx = jnp.where(low, q2, qa) if h == 0 else jnp.where(low, qa, q2)
        hd = 2 * pi + h
        sd = jnp.where(causal, _dot_nt(kx_ref[hd, rows, :], qx), NEG_INF)
        m = jnp.max(sd, axis=0, keepdims=True)
        sp = None
        if n > 0:
            sp = _dot_nt(kx_ref[hd, 0:n * tq, :], qx)
            m = jnp.maximum(m, jnp.max(sp, axis=0, keepdims=True))
        return (n, pi, h), sd, sp, m

    def weighted_values(chain, sd, sp, m):
        n, pi, h = chain
        hd = 2 * pi + h
        rows = slice(n * tq, (n + 1) * tq)
        acc = _dot(vx_ref[hd, :, rows], _bf16(jnp.exp2(sd - m)))
        if n > 0:
            acc = acc + _dot(vx_ref[hd, :, 0:n * tq], _bf16(jnp.exp2(sp - m)))
        return acc[0:HEAD_DIM, :] / acc[HEAD_DIM:HEAD_DIM + 1, :]

    chains = [(n, pi, h) for n in reversed(range(nb)) for pi in range(n_local) for h in range(2)]
    outs = {}
    queue = [scores(*c) for c in chains[:ATTN_LOOKAHEAD]]
    for nxt in chains[ATTN_LOOKAHEAD:] + [None] * ATTN_LOOKAHEAD:
        if nxt is not None:
            queue.append(scores(*nxt))
        done = queue.pop(0)
        outs[done[0]] = weighted_values(*done)
        n, pi, h = done[0]
        if h == 1:
            o_t = jnp.concatenate([outs.pop((n, pi, 0)), outs.pop((n, pi, 1))], axis=0)
            o_ref[0, n * tq:(n + 1) * tq, pi * LANES:(pi + 1) * LANES] = _bf16(o_t.T)


def _attention(moba, q, qa, k, vt, ka=None):
    b, t, w = q.shape
    wl = ATTN_PAIRS_PER_STEP * LANES
    assert w % wl == 0 and t % SEQ_TILE == 0
    spec = pl.BlockSpec((1, t, wl), lambda i, p: (i, 0, p))
    spec_t = pl.BlockSpec((1, wl, t), lambda i, p: (i, p, 0))
    spec_aux = pl.BlockSpec((1, t, LANES), lambda i, p: (i, 0, 0))
    args = [q, qa, k, vt] if moba else [q, qa, k, vt, ka]
    specs = [spec, spec_aux, spec, spec_t] + ([] if moba else [spec_aux])
    kern = lambda *refs: _attn_kernel(*refs, moba=moba)
    return pl.pallas_call(
        kern,
        grid=(b, w // wl),
        in_specs=specs,
        out_specs=spec,
        out_shape=jax.ShapeDtypeStruct((b, t, w), jnp.bfloat16),
        scratch_shapes=[pltpu.VMEM((2 * ATTN_PAIRS_PER_STEP, t, LANES), jnp.bfloat16),
                        pltpu.VMEM((2 * ATTN_PAIRS_PER_STEP, V_ROWS, t), jnp.bfloat16)],
        compiler_params=pltpu.CompilerParams(vmem_limit_bytes=VMEM_LIMIT),
        name="moba" if moba else "fox",
    )(*args)


def _ffn_kernel(x_ref, om_ref, of_ref, wo_ref, gt1_ref, g_ref, sc_ref, sh_ref, wup_ref, cv_ref,
                wdn_ref, gt2_ref, out_ref, x1_ref, halo_ref):
    tm = x_ref.shape[1]

    @pl.when(pl.program_id(1) == 0)
    def _():
        halo_ref[...] = jnp.zeros(halo_ref.shape, jnp.float32)

    pm = tm // FFN_PARTS
    slabs = [slice(p * pm, (p + 1) * pm) for p in range(FFN_PARTS)]
    attn = [_dot(jnp.concatenate([om_ref[0, r, :], of_ref[0, r, :]], axis=1), wo_ref[...])
            for r in slabs]

    def normed(p):
        x1 = x_ref[0, slabs[p], :] + gt1_ref[0] * attn[p]
        x1_ref[slabs[p], :] = x1
        return _bf16(_rms_mod(x1, g_ref[...], sc_ref[0], sh_ref[0]))

    def up_proj(hn, c):
        gate_cols = slice(c * FF_CHUNK, (c + 1) * FF_CHUNK)
        val_cols = slice(D_FF + c * FF_CHUNK, D_FF + (c + 1) * FF_CHUNK)
        u = jnp.concatenate([_dot(hn, wup_ref[:, gate_cols]), _dot(hn, wup_ref[:, val_cols])],
                            axis=1)
        return u, gate_cols, val_cols

    hn = normed(0)
    tails = [halo_ref[c] for c in range(N_FF_CHUNKS)]
    for p in range(FFN_PARTS):
        hn_next = None
        hmid = []
        ahead = [up_proj(hn, c) for c in range(FFN_LOOKAHEAD)]
        for c in range(N_FF_CHUNKS):
            if c + FFN_LOOKAHEAD < N_FF_CHUNKS:
                ahead.append(up_proj(hn, c + FFN_LOOKAHEAD))
            u, gate_cols, val_cols = ahead.pop(0)
            ext = jnp.concatenate([tails[c], u], axis=0)
            tails[c] = u[pm - HALO:pm, :]
            cv = jnp.concatenate([cv_ref[:, gate_cols], cv_ref[:, val_cols]], axis=1)
            u1 = pltpu.roll(ext, 1, 0)[HALO:, :]
            u2 = pltpu.roll(ext, 2, 0)[HALO:, :]
            uc = cv[0:1, :] * u2 + cv[1:2, :] * u1 + cv[2:3, :] * u + cv[3:4, :]
            a = uc[:, :FF_CHUNK]
            val = uc[:, FF_CHUNK:]
            half = 0.5 * a
            hmid.append(_bf16((half + half * jnp.tanh(half)) * val))
            if c == FFN_LOOKAHEAD and p + 1 < FFN_PARTS:
                hn_next = normed(p + 1)
        ffn = _dot(jnp.concatenate(hmid, axis=1), wdn_ref[...])
        out_ref[0, slabs[p], :] = x1_ref[slabs[p], :] + gt2_ref[0] * ffn
        hn = hn_next
    for c in range(N_FF_CHUNKS):
        halo_ref[c] = tails[c]


def _ffn(x, om, of, wo, gt1, g_ffn, sc2, sh2, wup, cv, wdn, gt2):
    b, t, d = x.shape
    tm = min(ROW_TILE, t)
    row = lambda i, j: (i, 0, 0)
    tile = lambda i, j: (i, j, 0)
    const2 = lambda i, j: (0, 0)
    once = dict(pipeline_mode=pl.Buffered(1))
    return pl.pallas_call(
        _ffn_kernel,
        grid=(b, t // tm),
        in_specs=[pl.BlockSpec((1, tm, d), tile),
                  pl.BlockSpec((1, tm, MOBA_WIDTH), tile),
                  pl.BlockSpec((1, tm, FOX_WIDTH), tile),
                  pl.BlockSpec((d, d), const2, **once),
                  pl.BlockSpec((1, 1, d), row),
                  pl.BlockSpec((1, d), const2),
                  pl.BlockSpec((1, 1, d), row),
                  pl.BlockSpec((1, 1, d), row),
                  pl.BlockSpec((d, 2 * D_FF), const2, **once),
                  pl.BlockSpec((HALO, 2 * D_FF), const2, **once),
                  pl.BlockSpec((D_FF, d), const2, **once),
                  pl.BlockSpec((1, 1, d), row)],
        out_specs=pl.BlockSpec((1, tm, d), tile),
        out_shape=jax.ShapeDtypeStruct((b, t, d), jnp.float32),
        scratch_shapes=[pltpu.VMEM((tm, d), jnp.float32),
                        pltpu.VMEM((N_FF_CHUNKS, HALO, 2 * FF_CHUNK), jnp.float32)],
        compiler_params=pltpu.CompilerParams(vmem_limit_bytes=VMEM_LIMIT),
        name="ffn",
    )(x, om, of, wo, gt1, g_ffn, sc2, sh2, wup, cv, wdn, gt2)


def _layer(x, c, w_ada, b_ada, g_mix, w_in, b_forget, moba_q_gain, moba_k_gain, fox_q_gain,
           fox_k_gain, w_out, g_ffn, w_up, conv_w, conv_b, w_down):
    b, t, d = x.shape
    nb = t // MOBA_BLOCK
    assert d == D_MODEL and t % SEQ_TILE == 0
    assert AUX_GROUP % nb == 0 and 2 * nb <= AUX_GROUP and 2 * N_SPLIT <= AUX_GROUP // 2

    mod = _adaln(c, w_ada, b_ada).reshape(b, 6, 1, d)
    sh1, sc1, gt1, sh2, sc2, gt2 = [mod[:, i] for i in range(6)]

    assert w_in.shape == (d, IN_QKV_COLS + N_FOX_HEADS)
    gains = jnp.stack([jnp.tile(g, N_MOBA_HEADS) for g in
                       (moba_q_gain, moba_k_gain, fox_q_gain, fox_k_gain)])
    r = np.arange(2 * LANES) // HEAD_DIM
    bd = jnp.asarray(r[:, None] == r[None, :], jnp.bfloat16)
    mq, mk, fq, fk, fl, mvt, fvt, wo16, wu16, wd16 = _inproj(
        x, sc1, sh1, g_mix.reshape(1, d), w_in.T, gains, _rope_tables(t), bd, [w_out, w_up, w_down])

    mqa, fqa, fka = _routing(mq, mk, fl, b_forget.reshape(1, N_FOX_HEADS))
    o_moba = _attention(True, mq, mqa, mk, mvt)
    o_fox = _attention(False, fq, fqa, fk, fvt, fka)

    cv = jnp.concatenate([conv_w, conv_b[None, :],
                          jnp.zeros((HALO - CONV_WIDTH - 1, 2 * D_FF), jnp.float32)], axis=0)
    return _ffn(x, o_moba, o_fox, wo16, gt1, g_ffn.reshape(1, d), sc2, sh2, wu16, cv, wd16, gt2)


def kernel(x, c, w_ada, b_ada, g_mix, w_in, b_forget, moba_q_gain, moba_k_gain, fox_q_gain,
           fox_k_gain, w_out, g_ffn, w_up, conv_w, conv_b, w_down):
    for l in range(w_ada.shape[0]):
        x = _layer(x, c, w_ada[l], b_ada[l], g_mix[l], w_in[l], b_forget[l], moba_q_gain[l],
                   moba_k_gain[l], fox_q_gain[l], fox_k_gain[l], w_out[l], g_ffn[l], w_up[l],
                   conv_w[l], conv_b[l], w_down[l])
    return x
```

```python
import math

import jax
import jax.numpy as jnp
import numpy as np
from jax import lax
from jax.experimental import pallas as pl
from jax.experimental.pallas import tpu as pltpu

D_MODEL = 1024
HEAD_DIM = 64
N_MOBA_HEADS = 8
N_FOX_HEADS = 8
MOBA_WIDTH = N_MOBA_HEADS * HEAD_DIM
FOX_WIDTH = N_FOX_HEADS * HEAD_DIM
MOBA_BLOCK = 256
MOBA_TOPK = 3
ROPE_THETA = 500000.0
ROPE_DIM = HEAD_DIM // 4
D_FF = 2816
CONV_WIDTH = 3
NORM_EPS = 1e-6
NEG_INF = -1e30
LOG2E = math.log2(math.e)
Q_SCALE = HEAD_DIM ** -0.5 * LOG2E
LANES = 128
BF16_ROWS = 16
IN_QKV_COLS = 3 * MOBA_WIDTH + 3 * FOX_WIDTH
V_ROWS = HEAD_DIM + BF16_ROWS
SEQ_TILE = 256
ROW_TILE = 512
ADALN_COLS = 1536
ATTN_PAIRS_PER_STEP = 2
ATTN_LOOKAHEAD = 5
FFN_PARTS = 2
FFN_LOOKAHEAD = 2
FF_CHUNK = 256
N_FF_CHUNKS = D_FF // FF_CHUNK
HALO = 8
N_SPLIT = 3
AUX_GROUP = 16
VMEM_LIMIT = 56 * 1024 * 1024

_NT = (((1,), (1,)), ((), ()))


def _bf16(a):
    return a.astype(jnp.bfloat16)


def _dot(a, b):
    return jnp.dot(a, b, preferred_element_type=jnp.float32)


def _dot_nt(a, b):
    return lax.dot_general(a, b, _NT, preferred_element_type=jnp.float32)


def _sigmoid(a):
    return 1.0 / (1.0 + jnp.exp(-a))


def _adaln_kernel(c_ref, w_ref, b_ref, o_ref):
    c = c_ref[...]
    s = c * _sigmoid(c)
    o_ref[...] = _dot(_bf16(s), _bf16(w_ref[...])) + b_ref[...]


def _adaln(c, w_ada, b_ada):
    b, d = c.shape
    n = w_ada.shape[1]
    tn = ADALN_COLS
    return pl.pallas_call(
        _adaln_kernel,
        grid=(n // tn,),
        in_specs=[pl.BlockSpec((b, d), lambda j: (0, 0)),
                  pl.BlockSpec((d, tn), lambda j: (0, j)),
                  pl.BlockSpec((1, tn), lambda j: (0, j))],
        out_specs=pl.BlockSpec((b, tn), lambda j: (0, j)),
        out_shape=jax.ShapeDtypeStruct((b, n), jnp.float32),
        compiler_params=pltpu.CompilerParams(vmem_limit_bytes=VMEM_LIMIT),
        name="adaln",
    )(c, w_ada, b_ada.reshape(1, n))


def _rms_mod(x, g, sc, sh):
    ms = jnp.mean(x * x, axis=-1, keepdims=True)
    y = x * lax.rsqrt(ms + NORM_EPS)
    return (y * g) * (1.0 + sc) + sh


def _head_norm(p, gain, bd):
    sq = _bf16(p * p)
    half = 2 * LANES
    ss = jnp.concatenate([_dot(sq[:, :half], bd), _dot(sq[:, half:], bd)], axis=1)
    return (p * lax.rsqrt(ss * (1.0 / HEAD_DIM) + NORM_EPS)) * gain


def _rope(y, c, sa, sb):
    half = ROPE_DIM // 2
    outs = []
    for i in range(y.shape[1] // LANES):
        yc = y[:, i * LANES:(i + 1) * LANES]
        up = pltpu.roll(yc, LANES - half, 1)
        dn = pltpu.roll(yc, half, 1)
        outs.append(yc * c + up * sa + dn * sb)
    return jnp.concatenate(outs, axis=1)


def _inproj_kernel(x_ref, sc_ref, sh_ref, g_ref, wint_ref, gains_ref, rc_ref, rsa_ref, rsb_ref,
                   bd_ref, *refs):
    n_cast = (len(refs) - 8) // 2
    cast_in, refs = refs[:n_cast], refs[n_cast:]
    mq_ref, mk_ref, fq_ref, fk_ref, fl_ref, mvt_ref, fvt_ref = refs[:7]
    cast_out, w_ref = refs[7:7 + n_cast], refs[-1]
    for src, dst in zip(cast_in, cast_out):
        dst[...] = _bf16(src[...])
    w = MOBA_WIDTH
    n_groups = IN_QKV_COLS // w

    @pl.when((pl.program_id(0) == 0) & (pl.program_id(1) == 0))
    def _():
        for i in range(n_groups):
            w_ref[i * w:(i + 1) * w, :] = _bf16(wint_ref[i * w:(i + 1) * w, :])
        logit_rows = wint_ref[IN_QKV_COLS:IN_QKV_COLS + N_FOX_HEADS, :]
        r = lax.broadcasted_iota(jnp.int32, (LANES, 1), 0)
        r_head = jnp.where(r % AUX_GROUP < 2 * N_SPLIT, _group_head(r), -1)
        wlog = jnp.zeros((LANES, logit_rows.shape[1]), jnp.float32)
        for head in range(N_FOX_HEADS):
            wlog = jnp.where(r_head == head, logit_rows[head:head + 1, :], wlog)
        w_ref[IN_QKV_COLS:IN_QKV_COLS + LANES, :] = _bf16(wlog)

    half = x_ref.shape[1] // 2
    hn_halves = [_bf16(_rms_mod(x_ref[0, r * half:(r + 1) * half, :], g_ref[...], sc_ref[0], sh_ref[0]))
                 for r in range(2)]
    p_mq = jnp.concatenate([_dot_nt(h, w_ref[0:w, :]) for h in hn_halves], axis=0)
    hn = jnp.concatenate(hn_halves, axis=0)
    bd = bd_ref[...]
    c, sa, sb = rc_ref[...], rsa_ref[...], rsb_ref[...]
    proj = lambda i: _dot_nt(hn, w_ref[i * w:(i + 1) * w, :])

    p_mk = proj(1)
    mq = _rope(_head_norm(p_mq, gains_ref[0:1, :], bd), c, sa, sb)
    mq_ref[0] = _bf16(mq * Q_SCALE)
    p_fq = proj(3)
    mk = _rope(_head_norm(p_mk, gains_ref[1:2, :], bd), c, sa, sb)
    mk_ref[0] = _bf16(mk)
    p_fk = proj(4)
    fq = _head_norm(p_fq, gains_ref[2:3, :], bd)
    fq_ref[0] = _bf16(fq * Q_SCALE)
    fl_ref[0] = _dot_nt(hn, w_ref[IN_QKV_COLS:IN_QKV_COLS + LANES, :])
    mvt = _dot_nt(w_ref[2 * w:3 * w, :], hn)
    fk = _head_norm(p_fk, gains_ref[3:4, :], bd)
    fk_ref[0] = _bf16(fk)
    mvt_ref[0] = _bf16(mvt)
    fvt_ref[0] = _bf16(_dot_nt(w_ref[5 * w:6 * w, :], hn))


def _rope_tables(t):
    half = ROPE_DIM // 2
    inv_freq = np.power(ROPE_THETA, -2.0 * np.arange(half, dtype=np.float64) / ROPE_DIM)
    ang = np.arange(t, dtype=np.float64)[:, None] * inv_freq[None, :]
    cos, sin = np.cos(ang), np.sin(ang)
    d = np.arange(LANES) % HEAD_DIM
    first = (d < half)[None, :]
    second = ((d >= half) & (d < ROPE_DIM))[None, :]
    idx = np.where(d < ROPE_DIM, d % half, 0)
    cos_l, sin_l = cos[:, idx], sin[:, idx]
    c = np.where(first | second, cos_l, 1.0)
    sa = np.where(first, -sin_l, 0.0)
    sb = np.where(second, sin_l, 0.0)
    return tuple(jnp.asarray(a, jnp.float32) for a in (c, sa, sb))


def _slab_rows(n_rows, n_steps):
    tiles = n_rows // BF16_ROWS
    assert tiles * BF16_ROWS == n_rows
    n_slabs = max(g for g in range(1, tiles + 1) if tiles % g == 0 and g <= n_steps)
    return n_rows // n_slabs


def _inproj(x, sc1, sh1, g_mix, w_in_t, gains, tables, bd, to_cast):
    b, t, d = x.shape
    tm = min(ROW_TILE, t)
    w = MOBA_WIDTH
    nt = t // tm
    cast_specs, cast_shapes = [], []
    for a in to_cast:
        rows = _slab_rows(a.shape[0], b * nt)
        last = a.shape[0] // rows - 1
        slab = lambda i, j, last=last: (jnp.minimum(i * nt + j, last), 0)
        cast_specs.append(pl.BlockSpec((rows, a.shape[1]), slab))
        cast_shapes.append(jax.ShapeDtypeStruct(a.shape, jnp.bfloat16))
    row = lambda i, j: (i, 0, 0)
    tile = lambda i, j: (i, j, 0)
    tile_t = lambda i, j: (i, 0, j)
    const = lambda i, j: (0, 0)
    wide = jax.ShapeDtypeStruct((b, t, w), jnp.bfloat16)
    wide_t = jax.ShapeDtypeStruct((b, w, t), jnp.bfloat16)
    out_specs = ([pl.BlockSpec((1, tm, w), tile)] * 4 + [pl.BlockSpec((1, tm, LANES), tile)]
                 + [pl.BlockSpec((1, w, tm), tile_t)] * 2)
    return pl.pallas_call(
        _inproj_kernel,
        grid=(b, t // tm),
        in_specs=[pl.BlockSpec((1, tm, d), tile),
                  pl.BlockSpec((1, 1, d), row),
                  pl.BlockSpec((1, 1, d), row),
                  pl.BlockSpec((1, d), const),
                  pl.BlockSpec(w_in_t.shape, const, pipeline_mode=pl.Buffered(1)),
                  pl.BlockSpec((4, w), const),
                  pl.BlockSpec((tm, LANES), lambda i, j: (j, 0)),
                  pl.BlockSpec((tm, LANES), lambda i, j: (j, 0)),
                  pl.BlockSpec((tm, LANES), lambda i, j: (j, 0)),
                  pl.BlockSpec((2 * LANES, 2 * LANES), const)] + cast_specs,
        out_specs=out_specs + cast_specs,
        out_shape=([wide] * 4 + [jax.ShapeDtypeStruct((b, t, LANES), jnp.float32)] + [wide_t] * 2
                   + cast_shapes),
        scratch_shapes=[pltpu.VMEM((IN_QKV_COLS + LANES, d), jnp.bfloat16)],
        compiler_params=pltpu.CompilerParams(vmem_limit_bytes=VMEM_LIMIT),
        name="inproj",
    )(x, sc1, sh1, g_mix, w_in_t, gains, *tables, bd, *to_cast)


def _split3(a):
    hi = _bf16(a)
    r1 = a - hi.astype(jnp.float32)
    mid = _bf16(r1)
    lo = _bf16(r1 - mid.astype(jnp.float32))
    return hi, mid, lo


def _group_head(lane_idx):
    return 2 * ((lane_idx % HEAD_DIM) // AUX_GROUP) + jnp.where(lane_idx < HEAD_DIM, 1, 0)


def _routing_kernel(mq_ref, mk_ref, fl_ref, bf_ref, mqa_ref, fqa_ref, fka_ref):
    t = mq_ref.shape[1]
    nb = t // MOBA_BLOCK
    q = mq_ref[0]
    k = mk_ref[0]
    blk_of_col = lax.broadcasted_iota(jnp.int32, (nb, t), 1) // MOBA_BLOCK
    blk_row = lax.broadcasted_iota(jnp.int32, (nb, t), 0)
    ind = _bf16(jnp.where(blk_of_col == blk_row, 1.0, 0.0))
    kmean = _dot(ind, k) * (1.0 / MOBA_BLOCK)
    kmt = jnp.concatenate([kmean] * (LANES // nb), axis=0)
    r_head = _group_head(lax.broadcasted_iota(jnp.int32, kmt.shape, 0))
    c_head = lax.broadcasted_iota(jnp.int32, kmt.shape, 1) // HEAD_DIM
    kmt = jnp.where(r_head == c_head, kmt, 0.0)
    k_hi = _bf16(kmt)
    k_lo = _bf16(kmt - k_hi.astype(jnp.float32))
    gate = (_dot_nt(q, k_hi) + _dot_nt(q, k_lo)) * (1.0 / Q_SCALE)

    lane = lax.broadcasted_iota(jnp.int32, (1, LANES), 1)
    j = lane % nb
    own = lax.broadcasted_iota(jnp.int32, (t, 1), 0) // MOBA_BLOCK
    past = j < own
    g = jnp.where(past, gate, -jnp.inf)
    rank = jnp.zeros(gate.shape, jnp.int32)
    for d in range(1, nb):
        other = pltpu.roll(g, LANES - d, 1)
        first = ((j + d) % nb) < j
        beats = (other > g) | (first & (other == g))
        rank = rank + jnp.where(beats, 1, 0)
    attend = (past & (rank < MOBA_TOPK)) | (j == own)
    mqa_ref[0] = jnp.where(attend, 0.0, NEG_INF)

    lane_head = _group_head(lane)
    bias = jnp.zeros((1, LANES), jnp.float32)
    for head in range(N_FOX_HEADS):
        bias = jnp.where(lane_head == head, bf_ref[:, head:head + 1], bias)
    z = fl_ref[0] + bias
    logf = -(jnp.maximum(-z, 0.0) + jnp.log1p(jnp.exp(-jnp.abs(z))))
    rr = lax.broadcasted_iota(jnp.int32, (MOBA_BLOCK, MOBA_BLOCK), 0)
    cc = lax.broadcasted_iota(jnp.int32, (MOBA_BLOCK, MOBA_BLOCK), 1)
    tri = _bf16(jnp.where(cc <= rr, 1.0, 0.0))
    slot = lane % AUX_GROUP
    carry = jnp.zeros((1, LANES), jnp.float32)
    for i in range(nb):
        rows = slice(i * MOBA_BLOCK, (i + 1) * MOBA_BLOCK)
        hi, mid, lo = _split3(logf[rows, :])
        cum = (_dot(tri, hi) + _dot(tri, mid)) + _dot(tri, lo) + carry
        carry = cum[MOBA_BLOCK - 1:MOBA_BLOCK, :]
        parts = [p.astype(jnp.float32) for p in _split3(cum * LOG2E)]
        fq = jnp.where(slot < 2 * N_SPLIT, 1.0, 0.0)
        fk = jnp.where(slot < N_SPLIT, 1.0, 0.0)
        for s in range(N_SPLIT):
            fq = jnp.where(slot == s, parts[s], fq)
            fk = jnp.where(slot == N_SPLIT + s, -parts[s], fk)
        fqa_ref[0, rows, :] = fq
        fka_ref[0, rows, :] = fk


def _routing(mq, mk, fl, b_forget):
    b, t, w = mq.shape
    full = lambda i: (i, 0, 0)
    aux = jax.ShapeDtypeStruct((b, t, LANES), jnp.float32)
    return pl.pallas_call(
        _routing_kernel,
        grid=(b,),
        in_specs=[pl.BlockSpec((1, t, w), full),
                  pl.BlockSpec((1, t, w), full),
                  pl.BlockSpec((1, t, LANES), full),
                  pl.BlockSpec((1, N_FOX_HEADS), lambda i: (0, 0))],
        out_specs=[pl.BlockSpec((1, t, LANES), full)] * 3,
        out_shape=[aux] * 3,
        compiler_params=pltpu.CompilerParams(vmem_limit_bytes=VMEM_LIMIT),
        name="routing",
    )(mq, mk, fl, b_forget)


def _attn_kernel(*refs, moba):
    if moba:
        q_ref, qa_ref, k_ref, vt_ref, o_ref, kx_ref, vx_ref = refs
    else:
        q_ref, qa_ref, k_ref, vt_ref, ka_ref, o_ref, kx_ref, vx_ref = refs
    tq = SEQ_TILE
    t = k_ref.shape[1]
    nb = t // tq
    n_local = q_ref.shape[2] // LANES
    lane = lax.broadcasted_iota(jnp.int32, (1, LANES), 1)
    low = lane < HEAD_DIM
    in_group = (lane % HEAD_DIM) < AUX_GROUP // 2

    def aux_shift(pi):
        return (LANES - (pl.program_id(1) * n_local + pi) * AUX_GROUP) % LANES

    for pi in range(n_local):
        cols = slice(pi * LANES, (pi + 1) * LANES)
        k2 = k_ref[0, :, cols]
        if moba:
            blk = lax.broadcasted_iota(jnp.int32, (t, LANES), 0) // MOBA_BLOCK
            ln = lax.broadcasted_iota(jnp.int32, (t, LANES), 1) % HEAD_DIM
            ka = _bf16(jnp.where(ln == blk, 1.0, 0.0))
        else:
            ka = _bf16(pltpu.roll(ka_ref[0], aux_shift(pi), 1))
        kx_ref[2 * pi] = jnp.where(low, k2, ka)
        kx_ref[2 * pi + 1] = jnp.where(low, ka, k2)
        for h in range(2):
            first = pi * LANES + h * HEAD_DIM
            vx_ref[2 * pi + h, 0:HEAD_DIM, :] = vt_ref[0, first:first + HEAD_DIM, :]
            vx_ref[2 * pi + h, HEAD_DIM:V_ROWS, :] = jnp.ones((V_ROWS - HEAD_DIM, t), jnp.bfloat16)

    key = lax.broadcasted_iota(jnp.int32, (tq, tq), 0)
    qry = lax.broadcasted_iota(jnp.int32, (tq, tq), 1)
    causal = key <= qry

    def scores(n, pi, h):
        rows = slice(n * tq, (n + 1) * tq)
        q2 = q_ref[0, rows, pi * LANES:(pi + 1) * LANES]
        qa = pltpu.roll(qa_ref[0, rows, :], aux_shift(pi), 1)
        qa = _bf16(jnp.where(in_group, qa, 0.0))
        qx = jnp.where(low, q2, qa) if h == 0 else jnp.where(low, qa, q2)
        hd = 2 * pi + h
        s = _dot_nt(kx_ref[hd, 0:(n + 1) * tq, :], qx)
        sd = jnp.where(causal, s[n * tq:, :], NEG_INF)
        s = sd if n == 0 else jnp.concatenate([s[:n * tq, :], sd], axis=0)
        return (n, pi, h), s, jnp.max(s, axis=0, keepdims=True)

    def weighted_values(chain, s, m):
        n, pi, h = chain
        acc = _dot(vx_ref[2 * pi + h, :, 0:(n + 1) * tq], _bf16(jnp.exp2(s - m)))
        return acc[0:HEAD_DIM, :] / acc[HEAD_DIM:HEAD_DIM + 1, :]

    chains = [(n, pi, h) for n in reversed(range(nb)) for pi in range(n_local) for h in range(2)]
    outs = {}
    queue = [scores(*c) for c in chains[:ATTN_LOOKAHEAD]]
    for nxt in chains[ATTN_LOOKAHEAD:] + [None] * ATTN_LOOKAHEAD:
        if nxt is not None:
            queue.append(scores(*nxt))
        done = queue.pop(0)
        outs[done[0]] = weighted_values(*done)
        n, pi, h = done[0]
        if h == 1:
            o_t = jnp.concatenate([outs.pop((n, pi, 0)), outs.pop((n, pi, 1))], axis=0)
            o_ref[0, n * tq:(n + 1) * tq, pi * LANES:(pi + 1) * LANES] = _bf16(o_t.T)


def _attention(moba, q, qa, k, vt, ka=None):
    b, t, w = q.shape
    wl = ATTN_PAIRS_PER_STEP * LANES
    assert w % wl == 0 and t % SEQ_TILE == 0
    spec = pl.BlockSpec((1, t, wl), lambda i, p: (i, 0, p))
    spec_t = pl.BlockSpec((1, wl, t), lambda i, p: (i, p, 0))
    spec_aux = pl.BlockSpec((1, t, LANES), lambda i, p: (i, 0, 0))
    args = [q, qa, k, vt] if moba else [q, qa, k, vt, ka]
    specs = [spec, spec_aux, spec, spec_t] + ([] if moba else [spec_aux])
    kern = lambda *refs: _attn_kernel(*refs, moba=moba)
    return pl.pallas_call(
        kern,
        grid=(b, w // wl),
        in_specs=specs,
        out_specs=spec,
        out_shape=jax.ShapeDtypeStruct((b, t, w), jnp.bfloat16),
        scratch_shapes=[pltpu.VMEM((2 * ATTN_PAIRS_PER_STEP, t, LANES), jnp.bfloat16),
                        pltpu.VMEM((2 * ATTN_PAIRS_PER_STEP, V_ROWS, t), jnp.bfloat16)],
        compiler_params=pltpu.CompilerParams(vmem_limit_bytes=VMEM_LIMIT),
        name="moba" if moba else "fox",
    )(*args)


def _ffn_kernel(x_ref, om_ref, of_ref, wo_ref, gt1_ref, g_ref, sc_ref, sh_ref, wup_ref, cv_ref,
                wdn_ref, gt2_ref, out_ref, x1_ref, halo_ref):
    tm = x_ref.shape[1]

    @pl.when(pl.program_id(1) == 0)
    def _():
        halo_ref[...] = jnp.zeros(halo_ref.shape, jnp.float32)

    pm = tm // FFN_PARTS
    slabs = [slice(p * pm, (p + 1) * pm) for p in range(FFN_PARTS)]
    attn = [_dot(jnp.concatenate([om_ref[0, r, :], of_ref[0, r, :]], axis=1), wo_ref[...])
            for r in slabs]

    def normed(p):
        x1 = x_ref[0, slabs[p], :] + gt1_ref[0] * attn[p]
        x1_ref[slabs[p], :] = x1
        return _bf16(_rms_mod(x1, g_ref[...], sc_ref[0], sh_ref[0]))

    def up_proj(hn, c):
        gate_cols = slice(c * FF_CHUNK, (c + 1) * FF_CHUNK)
        val_cols = slice(D_FF + c * FF_CHUNK, D_FF + (c + 1) * FF_CHUNK)
        u = jnp.concatenate([_dot(hn, wup_ref[:, gate_cols]), _dot(hn, wup_ref[:, val_cols])],
                            axis=1)
        return u, gate_cols, val_cols

    hn = normed(0)
    tails = [halo_ref[c] for c in range(N_FF_CHUNKS)]
    for p in range(FFN_PARTS):
        hn_next = None
        hmid = []
        ahead = [up_proj(hn, c) for c in range(FFN_LOOKAHEAD)]
        for c in range(N_FF_CHUNKS):
            if c + FFN_LOOKAHEAD < N_FF_CHUNKS:
                ahead.append(up_proj(hn, c + FFN_LOOKAHEAD))
            u, gate_cols, val_cols = ahead.pop(0)
            ext = jnp.concatenate([tails[c], u], axis=0)
            tails[c] = u[pm - HALO:pm, :]
            cv = jnp.concatenate([cv_ref[:, gate_cols], cv_ref[:, val_cols]], axis=1)
            u1 = pltpu.roll(ext, 1, 0)[HALO:, :]
            u2 = pltpu.roll(ext, 2, 0)[HALO:, :]
            uc = cv[0:1, :] * u2 + cv[1:2, :] * u1 + cv[2:3, :] * u + cv[3:4, :]
            a = uc[:, :FF_CHUNK]
            val = uc[:, FF_CHUNK:]
            half = 0.5 * a
            hmid.append(_bf16((half + half * jnp.tanh(half)) * val))
            if c == FFN_LOOKAHEAD and p + 1 < FFN_PARTS:
                hn_next = normed(p + 1)
        ffn = _dot(jnp.concatenate(hmid, axis=1), wdn_ref[...])
        out_ref[0, slabs[p], :] = x1_ref[slabs[p], :] + gt2_ref[0] * ffn
        hn = hn_next
    for c in range(N_FF_CHUNKS):
        halo_ref[c] = tails[c]


def _ffn(x, om, of, wo, gt1, g_ffn, sc2, sh2, wup, cv, wdn, gt2):
    b, t, d = x.shape
    tm = min(ROW_TILE, t)
    row = lambda i, j: (i, 0, 0)
    tile = lambda i, j: (i, j, 0)
    const2 = lambda i, j: (0, 0)
    once = dict(pipeline_mode=pl.Buffered(1))
    return pl.pallas_call(
        _ffn_kernel,
        grid=(b, t // tm),
        in_specs=[pl.BlockSpec((1, tm, d), tile),
                  pl.BlockSpec((1, tm, MOBA_WIDTH), tile),
                  pl.BlockSpec((1, tm, FOX_WIDTH), tile),
                  pl.BlockSpec((d, d), const2, **once),
                  pl.BlockSpec((1, 1, d), row),
                  pl.BlockSpec((1, d), const2),
                  pl.BlockSpec((1, 1, d), row),
                  pl.BlockSpec((1, 1, d), row),
                  pl.BlockSpec((d, 2 * D_FF), const2, **once),
                  pl.BlockSpec((HALO, 2 * D_FF), const2, **once),
                  pl.BlockSpec((D_FF, d), const2, **once),
                  pl.BlockSpec((1, 1, d), row)],
        out_specs=pl.BlockSpec((1, tm, d), tile),
        out_shape=jax.ShapeDtypeStruct((b, t, d), jnp.float32),
        scratch_shapes=[pltpu.VMEM((tm, d), jnp.float32),
                        pltpu.VMEM((N_FF_CHUNKS, HALO, 2 * FF_CHUNK), jnp.float32)],
        compiler_params=pltpu.CompilerParams(vmem_limit_bytes=VMEM_LIMIT),
        name="ffn",
    )(x, om, of, wo, gt1, g_ffn, sc2, sh2, wup, cv, wdn, gt2)


def _layer(x, c, w_ada, b_ada, g_mix, w_in, b_forget, moba_q_gain, moba_k_gain, fox_q_gain,
           fox_k_gain, w_out, g_ffn, w_up, conv_w, conv_b, w_down):
    b, t, d = x.shape
    nb = t // MOBA_BLOCK
    assert d == D_MODEL and t % SEQ_TILE == 0
    assert AUX_GROUP % nb == 0 and 2 * nb <= AUX_GROUP and 2 * N_SPLIT <= AUX_GROUP // 2

    mod = _adaln(c, w_ada, b_ada).reshape(b, 6, 1, d)
    sh1, sc1, gt1, sh2, sc2, gt2 = [mod[:, i] for i in range(6)]

    assert w_in.shape == (d, IN_QKV_COLS + N_FOX_HEADS)
    gains = jnp.stack([jnp.tile(g, N_MOBA_HEADS) for g in
                       (moba_q_gain, moba_k_gain, fox_q_gain, fox_k_gain)])
    r = np.arange(2 * LANES) // HEAD_DIM
    bd = jnp.asarray(r[:, None] == r[None, :], jnp.bfloat16)
    mq, mk, fq, fk, fl, mvt, fvt, wo16, wu16, wd16 = _inproj(
        x, sc1, sh1, g_mix.reshape(1, d), w_in.T, gains, _rope_tables(t), bd, [w_out, w_up, w_down])

    mqa, fqa, fka = _routing(mq, mk, fl, b_forget.reshape(1, N_FOX_HEADS))
    o_moba = _attention(True, mq, mqa, mk, mvt)
    o_fox = _attention(False, fq, fqa, fk, fvt, fka)

    cv = jnp.concatenate([conv_w, conv_b[None, :],
                          jnp.zeros((HALO - CONV_WIDTH - 1, 2 * D_FF), jnp.float32)], axis=0)
    return _ffn(x, o_moba, o_fox, wo16, gt1, g_ffn.reshape(1, d), sc2, sh2, wu16, cv, wd16, gt2)


def kernel(x, c, w_ada, b_ada, g_mix, w_in, b_forget, moba_q_gain, moba_k_gain, fox_q_gain,
           fox_k_gain, w_out, g_ffn, w_up, conv_w, conv_b, w_down):
    for l in range(w_ada.shape[0]):
        x = _layer(x, c, w_ada[l], b_ada[l], g_mix[l], w_in[l], b_forget[l], moba_q_gain[l],
                   moba_k_gain[l], fox_q_gain[l], fox_k_gain[l], w_out[l], g_ffn[l], w_up[l],
                   conv_w[l], conv_b[l], w_down[l])
    return x
```

```python
import math

import jax
import jax.numpy as jnp
import numpy as np
from jax import lax
from jax.experimental import pallas as pl
from jax.experimental.pallas import tpu as pltpu

D_MODEL = 1024
HEAD_DIM = 64
N_MOBA_HEADS = 8
N_FOX_HEADS = 8
MOBA_WIDTH = N_MOBA_HEADS * HEAD_DIM
FOX_WIDTH = N_FOX_HEADS * HEAD_DIM
MOBA_BLOCK = 256
MOBA_TOPK = 3
ROPE_THETA = 500000.0
ROPE_DIM = HEAD_DIM // 4
D_FF = 2816
CONV_WIDTH = 3
NORM_EPS = 1e-6
NEG_INF = -1e30
LOG2E = math.log2(math.e)
Q_SCALE = HEAD_DIM ** -0.5 * LOG2E
LANES = 128
BF16_ROWS = 16
IN_QKV_COLS = 3 * MOBA_WIDTH + 3 * FOX_WIDTH
V_ROWS = HEAD_DIM + BF16_ROWS
SEQ_TILE = 256
ROW_TILE = 512
ADALN_COLS = 1536
ATTN_PAIRS_PER_STEP = 2
ATTN_LOOKAHEAD = 5
FFN_PARTS = 2
FFN_LOOKAHEAD = 2
FF_CHUNK = 256
N_FF_CHUNKS = D_FF // FF_CHUNK
HALO = 8
N_SPLIT = 3
AUX_GROUP = 16
VMEM_LIMIT = 56 * 1024 * 1024

_NT = (((1,), (1,)), ((), ()))


def _bf16(a):
    return a.astype(jnp.bfloat16)


def _dot(a, b):
    return jnp.dot(a, b, preferred_element_type=jnp.float32)


def _dot_nt(a, b):
    return lax.dot_general(a, b, _NT, preferred_element_type=jnp.float32)


def _sigmoid(a):
    return 1.0 / (1.0 + jnp.exp(-a))


def _adaln_kernel(c_ref, w_ref, b_ref, o_ref):
    c = c_ref[...]
    s = c * _sigmoid(c)
    o_ref[...] = _dot(_bf16(s), _bf16(w_ref[...])) + b_ref[...]


def _adaln(c, w_ada, b_ada):
    b, d = c.shape
    n = w_ada.shape[1]
    tn = ADALN_COLS
    return pl.pallas_call(
        _adaln_kernel,
        grid=(n // tn,),
        in_specs=[pl.BlockSpec((b, d), lambda j: (0, 0)),
                  pl.BlockSpec((d, tn), lambda j: (0, j)),
                  pl.BlockSpec((1, tn), lambda j: (0, j))],
        out_specs=pl.BlockSpec((b, tn), lambda j: (0, j)),
        out_shape=jax.ShapeDtypeStruct((b, n), jnp.float32),
        compiler_params=pltpu.CompilerParams(vmem_limit_bytes=VMEM_LIMIT),
        name="adaln",
    )(c, w_ada, b_ada.reshape(1, n))


def _rms_mod(x, g, sc, sh):
    ms = jnp.mean(x * x, axis=-1, keepdims=True)
    y = x * lax.rsqrt(ms + NORM_EPS)
    return (y * g) * (1.0 + sc) + sh


def _head_norm(p, gain, bd):
    sq = _bf16(p * p)
    half = 2 * LANES
    ss = jnp.concatenate([_dot(sq[:, :half], bd), _dot(sq[:, half:], bd)], axis=1)
    return (p * lax.rsqrt(ss * (1.0 / HEAD_DIM) + NORM_EPS)) * gain


def _rope(y, c, sa, sb):
    half = ROPE_DIM // 2
    outs = []
    for i in range(y.shape[1] // LANES):
        yc = y[:, i * LANES:(i + 1) * LANES]
        up = pltpu.roll(yc, LANES - half, 1)
        dn = pltpu.roll(yc, half, 1)
        outs.append(yc * c + up * sa + dn * sb)
    return jnp.concatenate(outs, axis=1)


def _inproj_kernel(x_ref, sc_ref, sh_ref, g_ref, wint_ref, gains_ref, rc_ref, rsa_ref, rsb_ref,
                   bd_ref, *refs):
    n_cast = (len(refs) - 8) // 2
    cast_in, refs = refs[:n_cast], refs[n_cast:]
    mq_ref, mk_ref, fq_ref, fk_ref, fl_ref, mvt_ref, fvt_ref = refs[:7]
    cast_out, w_ref = refs[7:7 + n_cast], refs[-1]
    for src, dst in zip(cast_in, cast_out):
        dst[...] = _bf16(src[...])
    w = MOBA_WIDTH
    n_groups = IN_QKV_COLS // w

    @pl.when((pl.program_id(0) == 0) & (pl.program_id(1) == 0))
    def _():
        for i in range(n_groups):
            w_ref[i * w:(i + 1) * w, :] = _bf16(wint_ref[i * w:(i + 1) * w, :])
        logit_rows = wint_ref[IN_QKV_COLS:IN_QKV_COLS + N_FOX_HEADS, :]
        r = lax.broadcasted_iota(jnp.int32, (LANES, 1), 0)
        r_head = jnp.where(r % AUX_GROUP < 2 * N_SPLIT, _group_head(r), -1)
        wlog = jnp.zeros((LANES, logit_rows.shape[1]), jnp.float32)
        for head in range(N_FOX_HEADS):
            wlog = jnp.where(r_head == head, logit_rows[head:head + 1, :], wlog)
        w_ref[IN_QKV_COLS:IN_QKV_COLS + LANES, :] = _bf16(wlog)

    half = x_ref.shape[1] // 2
    hn_halves = [_bf16(_rms_mod(x_ref[0, r * half:(r + 1) * half, :], g_ref[...], sc_ref[0], sh_ref[0]))
                 for r in range(2)]
    p_mq = jnp.concatenate([_dot_nt(h, w_ref[0:w, :]) for h in hn_halves], axis=0)
    hn = jnp.concatenate(hn_halves, axis=0)
    bd = bd_ref[...]
    c, sa, sb = rc_ref[...], rsa_ref[...], rsb_ref[...]
    proj = lambda i: _dot_nt(hn, w_ref[i * w:(i + 1) * w, :])

    p_mk = proj(1)
    mq = _rope(_head_norm(p_mq, gains_ref[0:1, :], bd), c, sa, sb)
    mq_ref[0] = _bf16(mq * Q_SCALE)
    p_fq = proj(3)
    mk = _rope(_head_norm(p_mk, gains_ref[1:2, :], bd), c, sa, sb)
    mk_ref[0] = _bf16(mk)
    p_fk = proj(4)
    fq = _head_norm(p_fq, gains_ref[2:3, :], bd)
    fq_ref[0] = _bf16(fq * Q_SCALE)
    fl_ref[0] = _dot_nt(hn, w_ref[IN_QKV_COLS:IN_QKV_COLS + LANES, :])
    mvt = _dot_nt(w_ref[2 * w:3 * w, :], hn)
    fk = _head_norm(p_fk, gains_ref[3:4, :], bd)
    fk_ref[0] = _bf16(fk)
    mvt_ref[0] = _bf16(mvt)
    fvt_ref[0] = _bf16(_dot_nt(w_ref[5 * w:6 * w, :], hn))


def _rope_tables(t):
    half = ROPE_DIM // 2
    inv_freq = np.power(ROPE_THETA, -2.0 * np.arange(half, dtype=np.float64) / ROPE_DIM)
    ang = np.arange(t, dtype=np.float64)[:, None] * inv_freq[None, :]
    cos, sin = np.cos(ang), np.sin(ang)
    d = np.arange(LANES) % HEAD_DIM
    first = (d < half)[None, :]
    second = ((d >= half) & (d < ROPE_DIM))[None, :]
    idx = np.where(d < ROPE_DIM, d % half, 0)
    cos_l, sin_l = cos[:, idx], sin[:, idx]
    c = np.where(first | second, cos_l, 1.0)
    sa = np.where(first, -sin_l, 0.0)
    sb = np.where(second, sin_l, 0.0)
    return tuple(jnp.asarray(a, jnp.float32) for a in (c, sa, sb))


def _slab_rows(n_rows, n_steps):
    tiles = n_rows // BF16_ROWS
    assert tiles * BF16_ROWS == n_rows
    n_slabs = max(g for g in range(1, tiles + 1) if tiles % g == 0 and g <= n_steps)
    return n_rows // n_slabs


def _inproj(x, sc1, sh1, g_mix, w_in_t, gains, tables, bd, to_cast):
    b, t, d = x.shape
    tm = min(ROW_TILE, t)
    w = MOBA_WIDTH
    nt = t // tm
    cast_specs, cast_shapes = [], []
    for a in to_cast:
        rows = _slab_rows(a.shape[0], b * nt)
        last = a.shape[0] // rows - 1
        slab = lambda i, j, last=last: (jnp.minimum(i * nt + j, last), 0)
        cast_specs.append(pl.BlockSpec((rows, a.shape[1]), slab))
        cast_shapes.append(jax.ShapeDtypeStruct(a.shape, jnp.bfloat16))
    row = lambda i, j: (i, 0, 0)
    tile = lambda i, j: (i, j, 0)
    tile_t = lambda i, j: (i, 0, j)
    const = lambda i, j: (0, 0)
    wide = jax.ShapeDtypeStruct((b, t, w), jnp.bfloat16)
    wide_t = jax.ShapeDtypeStruct((b, w, t), jnp.bfloat16)
    out_specs = ([pl.BlockSpec((1, tm, w), tile)] * 4 + [pl.BlockSpec((1, tm, LANES), tile)]
                 + [pl.BlockSpec((1, w, tm), tile_t)] * 2)
    return pl.pallas_call(
        _inproj_kernel,
        grid=(b, t // tm),
        in_specs=[pl.BlockSpec((1, tm, d), tile),
                  pl.BlockSpec((1, 1, d), row),
                  pl.BlockSpec((1, 1, d), row),
                  pl.BlockSpec((1, d), const),
                  pl.BlockSpec(w_in_t.shape, const, pipeline_mode=pl.Buffered(1)),
                  pl.BlockSpec((4, w), const),
                  pl.BlockSpec((tm, LANES), lambda i, j: (j, 0)),
                  pl.BlockSpec((tm, LANES), lambda i, j: (j, 0)),
                  pl.BlockSpec((tm, LANES), lambda i, j: (j, 0)),
                  pl.BlockSpec((2 * LANES, 2 * LANES), const)] + cast_specs,
        out_specs=out_specs + cast_specs,
        out_shape=([wide] * 4 + [jax.ShapeDtypeStruct((b, t, LANES), jnp.float32)] + [wide_t] * 2
                   + cast_shapes),
        scratch_shapes=[pltpu.VMEM((IN_QKV_COLS + LANES, d), jnp.bfloat16)],
        compiler_params=pltpu.CompilerParams(vmem_limit_bytes=VMEM_LIMIT),
        name="inproj",
    )(x, sc1, sh1, g_mix, w_in_t, gains, *tables, bd, *to_cast)


def _split3(a):
    hi = _bf16(a)
    r1 = a - hi.astype(jnp.float32)
    mid = _bf16(r1)
    lo = _bf16(r1 - mid.astype(jnp.float32))
    return hi, mid, lo


def _group_head(lane_idx):
    return 2 * ((lane_idx % HEAD_DIM) // AUX_GROUP) + jnp.where(lane_idx < HEAD_DIM, 1, 0)


def _routing_kernel(mq_ref, mk_ref, fl_ref, bf_ref, mqa_ref, fqa_ref, fka_ref):
    t = mq_ref.shape[1]
    nb = t // MOBA_BLOCK
    k = mk_ref[0]
    blk_of_col = lax.broadcasted_iota(jnp.int32, (nb, t), 1) // MOBA_BLOCK
    blk_row = lax.broadcasted_iota(jnp.int32, (nb, t), 0)
    ind = _bf16(jnp.where(blk_of_col == blk_row, 1.0, 0.0))
    kmean = _dot(ind, k) * (1.0 / MOBA_BLOCK)
    kmt = jnp.concatenate([kmean] * (LANES // nb), axis=0)
    r_head = _group_head(lax.broadcasted_iota(jnp.int32, kmt.shape, 0))
    c_head = lax.broadcasted_iota(jnp.int32, kmt.shape, 1) // HEAD_DIM
    kmt = jnp.where(r_head == c_head, kmt, 0.0)
    k_hi = _bf16(kmt)
    k_lo = _bf16(kmt - k_hi.astype(jnp.float32))

    lane = lax.broadcasted_iota(jnp.int32, (1, LANES), 1)
    j = lane % nb
    ranked = min(MOBA_TOPK + 1, nb) * MOBA_BLOCK
    own = lax.broadcasted_iota(jnp.int32, (ranked, 1), 0) // MOBA_BLOCK
    mqa_ref[0, 0:ranked, :] = jnp.where(j <= own, 0.0, NEG_INF)
    if ranked < t:
        q = mq_ref[0, ranked:, :]
        gate = (_dot_nt(q, k_hi) + _dot_nt(q, k_lo)) * (1.0 / Q_SCALE)
        own = (lax.broadcasted_iota(jnp.int32, (t - ranked, 1), 0) + ranked) // MOBA_BLOCK
        past = j < own
        g = jnp.where(past, gate, -jnp.inf)
        rank = jnp.zeros(g.shape, jnp.int32)
        for d in range(1, nb):
            other = pltpu.roll(g, LANES - d, 1)
            first = ((j + d) % nb) < j
            beats = (other > g) | (first & (other == g))
            rank = rank + jnp.where(beats, 1, 0)
        attend = (past & (rank < MOBA_TOPK)) | (j == own)
        mqa_ref[0, ranked:, :] = jnp.where(attend, 0.0, NEG_INF)

    lane_head = _group_head(lane)
    bias = jnp.zeros((1, LANES), jnp.float32)
    for head in range(N_FOX_HEADS):
        bias = jnp.where(lane_head == head, bf_ref[:, head:head + 1], bias)
    z = fl_ref[0] + bias
    logf = -(jnp.maximum(-z, 0.0) + jnp.log1p(jnp.exp(-jnp.abs(z))))
    rr = lax.broadcasted_iota(jnp.int32, (MOBA_BLOCK, MOBA_BLOCK), 0)
    cc = lax.broadcasted_iota(jnp.int32, (MOBA_BLOCK, MOBA_BLOCK), 1)
    tri = _bf16(jnp.where(cc <= rr, 1.0, 0.0))
    slot = lane % AUX_GROUP
    carry = jnp.zeros((1, LANES), jnp.float32)
    for i in range(nb):
        rows = slice(i * MOBA_BLOCK, (i + 1) * MOBA_BLOCK)
        hi, mid, lo = _split3(logf[rows, :])
        cum = (_dot(tri, hi) + _dot(tri, mid)) + _dot(tri, lo) + carry
        carry = cum[MOBA_BLOCK - 1:MOBA_BLOCK, :]
        parts = [p.astype(jnp.float32) for p in _split3(cum * LOG2E)]
        fq = jnp.where(slot < 2 * N_SPLIT, 1.0, 0.0)
        fk = jnp.where(slot < N_SPLIT, 1.0, 0.0)
        for s in range(N_SPLIT):
            fq = jnp.where(slot == s, parts[s], fq)
            fk = jnp.where(slot == N_SPLIT + s, -parts[s], fk)
        fqa_ref[0, rows, :] = fq
        fka_ref[0, rows, :] = fk


def _routing(mq, mk, fl, b_forget):
    b, t, w = mq.shape
    full = lambda i: (i, 0, 0)
    aux = jax.ShapeDtypeStruct((b, t, LANES), jnp.float32)
    return pl.pallas_call(
        _routing_kernel,
        grid=(b,),
        in_specs=[pl.BlockSpec((1, t, w), full),
                  pl.BlockSpec((1, t, w), full),
                  pl.BlockSpec((1, t, LANES), full),
                  pl.BlockSpec((1, N_FOX_HEADS), lambda i: (0, 0))],
        out_specs=[pl.BlockSpec((1, t, LANES), full)] * 3,
        out_shape=[aux] * 3,
        compiler_params=pltpu.CompilerParams(vmem_limit_bytes=VMEM_LIMIT),
        name="routing",
    )(mq, mk, fl, b_forget)


def _attn_kernel(*refs, moba):
    if moba:
        q_ref, qa_ref, k_ref, vt_ref, o_ref, kx_ref, vx_ref = refs
    else:
        q_ref, qa_ref, k_ref, vt_ref, ka_ref, o_ref, kx_ref, vx_ref = refs
    tq = SEQ_TILE
    t = k_ref.shape[1]
    nb = t // tq
    n_local = q_ref.shape[2] // LANES
    lane = lax.broadcasted_iota(jnp.int32, (1, LANES), 1)
    low = lane < HEAD_DIM
    in_group = (lane % HEAD_DIM) < AUX_GROUP // 2

    def aux_shift(pi):
        return (LANES - (pl.program_id(1) * n_local + pi) * AUX_GROUP) % LANES

    for pi in range(n_local):
        cols = slice(pi * LANES, (pi + 1) * LANES)
        k2 = k_ref[0, :, cols]
        if moba:
            blk = lax.broadcasted_iota(jnp.int32, (t, LANES), 0) // MOBA_BLOCK
            ln = lax.broadcasted_iota(jnp.int32, (t, LANES), 1) % HEAD_DIM
            ka = _bf16(jnp.where(ln == blk, 1.0, 0.0))
        else:
            ka = _bf16(pltpu.roll(ka_ref[0], aux_shift(pi), 1))
        kx_ref[2 * pi] = jnp.where(low, k2, ka)
        kx_ref[2 * pi + 1] = jnp.where(low, ka, k2)
        for h in range(2):
            first = pi * LANES + h * HEAD_DIM
            vx_ref[2 * pi + h, 0:HEAD_DIM, :] = vt_ref[0, first:first + HEAD_DIM, :]
            vx_ref[2 * pi + h, HEAD_DIM:V_ROWS, :] = jnp.ones((V_ROWS - HEAD_DIM, t), jnp.bfloat16)

    key = lax.broadcasted_iota(jnp.int32, (tq, tq), 0)
    qry = lax.broadcasted_iota(jnp.int32, (tq, tq), 1)
    causal = key <= qry

    def scores(n, pi, h):
        rows = slice(n * tq, (n + 1) * tq)
        q2 = q_ref[0, rows, pi * LANES:(pi + 1) * LANES]
        qa = pltpu.roll(qa_ref[0, rows, :], aux_shift(pi), 1)
        qa = _bf16(jnp.where(in_group, qa, 0.0))
        qx = jnp.where(low, q2, qa) if h == 0 else jnp.where(low, qa, q2)
        hd = 2 * pi + h
        s = _dot_nt(kx_ref[hd, 0:(n + 1) * tq, :], qx)
        sd = jnp.where(causal, s[n * tq:, :], NEG_INF)
        s = sd if n == 0 else jnp.concatenate([s[:n * tq, :], sd], axis=0)
        return (n, pi, h), s, jnp.max(s, axis=0, keepdims=True)

    def weighted_values(chain, s, m):
        n, pi, h = chain
        acc = _dot(vx_ref[2 * pi + h, :, 0:(n + 1) * tq], _bf16(jnp.exp2(s - m)))
        return acc[0:HEAD_DIM, :] / acc[HEAD_DIM:HEAD_DIM + 1, :]

    chains = [(n, pi, h) for n in reversed(range(nb)) for pi in range(n_local) for h in range(2)]
    outs = {}
    queue = [scores(*c) for c in chains[:ATTN_LOOKAHEAD]]
    for nxt in chains[ATTN_LOOKAHEAD:] + [None] * ATTN_LOOKAHEAD:
        if nxt is not None:
            queue.append(scores(*nxt))
        done = queue.pop(0)
        outs[done[0]] = weighted_values(*done)
        n, pi, h = done[0]
        if h == 1:
            o_t = jnp.concatenate([outs.pop((n, pi, 0)), outs.pop((n, pi, 1))], axis=0)
            o_ref[0, n * tq:(n + 1) * tq, pi * LANES:(pi + 1) * LANES] = _bf16(o_t.T)


def _attention(moba, q, qa, k, vt, ka=None):
    b, t, w = q.shape
    wl = ATTN_PAIRS_PER_STEP * LANES
    assert w % wl == 0 and t % SEQ_TILE == 0
    spec = pl.BlockSpec((1, t, wl), lambda i, p: (i, 0, p))
    spec_t = pl.BlockSpec((1, wl, t), lambda i, p: (i, p, 0))
    spec_aux = pl.BlockSpec((1, t, LANES), lambda i, p: (i, 0, 0))
    args = [q, qa, k, vt] if moba else [q, qa, k, vt, ka]
    specs = [spec, spec_aux, spec, spec_t] + ([] if moba else [spec_aux])
    kern = lambda *refs: _attn_kernel(*refs, moba=moba)
    return pl.pallas_call(
        kern,
        grid=(b, w // wl),
        in_specs=specs,
        out_specs=spec,
        out_shape=jax.ShapeDtypeStruct((b, t, w), jnp.bfloat16),
        scratch_shapes=[pltpu.VMEM((2 * ATTN_PAIRS_PER_STEP, t, LANES), jnp.bfloat16),
                        pltpu.VMEM((2 * ATTN_PAIRS_PER_STEP, V_ROWS, t), jnp.bfloat16)],
        compiler_params=pltpu.CompilerParams(vmem_limit_bytes=VMEM_LIMIT),
        name="moba" if moba else "fox",
    )(*args)


def _ffn_kernel(x_ref, om_ref, of_ref, wo_ref, gt1_ref, g_ref, sc_ref, sh_ref, wup_ref, cv_ref,
                wdn_ref, gt2_ref, out_ref, x1_ref, halo_ref):
    tm = x_ref.shape[1]

    @pl.when(pl.program_id(1) == 0)
    def _():
        halo_ref[...] = jnp.zeros(halo_ref.shape, jnp.float32)

    pm = tm // FFN_PARTS
    slabs = [slice(p * pm, (p + 1) * pm) for p in range(FFN_PARTS)]
    attn = [_dot(jnp.concatenate([om_ref[0, r, :], of_ref[0, r, :]], axis=1), wo_ref[...])
            for r in slabs]

    def normed(p):
        x1 = x_ref[0, slabs[p], :] + gt1_ref[0] * attn[p]
        x1_ref[slabs[p], :] = x1
        return _bf16(_rms_mod(x1, g_ref[...], sc_ref[0], sh_ref[0]))

    def up_proj(hn, c):
        gate_cols = slice(c * FF_CHUNK, (c + 1) * FF_CHUNK)
        val_cols = slice(D_FF + c * FF_CHUNK, D_FF + (c + 1) * FF_CHUNK)
        u = jnp.concatenate([_dot(hn, wup_ref[:, gate_cols]), _dot(hn, wup_ref[:, val_cols])],
                            axis=1)
        return u, gate_cols, val_cols

    hn = normed(0)
    tails = [halo_ref[c] for c in range(N_FF_CHUNKS)]
    for p in range(FFN_PARTS):
        hn_next = None
        hmid = []
        ahead = [up_proj(hn, c) for c in range(FFN_LOOKAHEAD)]
        for c in range(N_FF_CHUNKS):
            if c + FFN_LOOKAHEAD < N_FF_CHUNKS:
                ahead.append(up_proj(hn, c + FFN_LOOKAHEAD))
            u, gate_cols, val_cols = ahead.pop(0)
            ext = jnp.concatenate([tails[c], u], axis=0)
            tails[c] = u[pm - HALO:pm, :]
            cv = jnp.concatenate([cv_ref[:, gate_cols], cv_ref[:, val_cols]], axis=1)
            u1 = pltpu.roll(ext, 1, 0)[HALO:, :]
            u2 = pltpu.roll(ext, 2, 0)[HALO:, :]
            uc = cv[0:1, :] * u2 + cv[1:2, :] * u1 + cv[2:3, :] * u + cv[3:4, :]
            a = uc[:, :FF_CHUNK]
            val = uc[:, FF_CHUNK:]
            half = 0.5 * a
            hmid.append(_bf16((half + half * jnp.tanh(half)) * val))
            if c == FFN_LOOKAHEAD and p + 1 < FFN_PARTS:
                hn_next = normed(p + 1)
        ffn = _dot(jnp.concatenate(hmid, axis=1), wdn_ref[...])
        out_ref[0, slabs[p], :] = x1_ref[slabs[p], :] + gt2_ref[0] * ffn
        hn = hn_next
    for c in range(N_FF_CHUNKS):
        halo_ref[c] = tails[c]


def _ffn(x, om, of, wo, gt1, g_ffn, sc2, sh2, wup, cv, wdn, gt2):
    b, t, d = x.shape
    tm = min(ROW_TILE, t)
    row = lambda i, j: (i, 0, 0)
    tile = lambda i, j: (i, j, 0)
    const2 = lambda i, j: (0, 0)
    once = dict(pipeline_mode=pl.Buffered(1))
    return pl.pallas_call(
        _ffn_kernel,
        grid=(b, t // tm),
        in_specs=[pl.BlockSpec((1, tm, d), tile),
                  pl.BlockSpec((1, tm, MOBA_WIDTH), tile),
                  pl.BlockSpec((1, tm, FOX_WIDTH), tile),
                  pl.BlockSpec((d, d), const2, **once),
                  pl.BlockSpec((1, 1, d), row),
                  pl.BlockSpec((1, d), const2),
                  pl.BlockSpec((1, 1, d), row),
                  pl.BlockSpec((1, 1, d), row),
                  pl.BlockSpec((d, 2 * D_FF), const2, **once),
                  pl.BlockSpec((HALO, 2 * D_FF), const2, **once),
                  pl.BlockSpec((D_FF, d), const2, **once),
                  pl.BlockSpec((1, 1, d), row)],
        out_specs=pl.BlockSpec((1, tm, d), tile),
        out_shape=jax.ShapeDtypeStruct((b, t, d), jnp.float32),
        scratch_shapes=[pltpu.VMEM((tm, d), jnp.float32),
                        pltpu.VMEM((N_FF_CHUNKS, HALO, 2 * FF_CHUNK), jnp.float32)],
        compiler_params=pltpu.CompilerParams(vmem_limit_bytes=VMEM_LIMIT),
        name="ffn",
    )(x, om, of, wo, gt1, g_ffn, sc2, sh2, wup, cv, wdn, gt2)


def _layer(x, c, w_ada, b_ada, g_mix, w_in, b_forget, moba_q_gain, moba_k_gain, fox_q_gain,
           fox_k_gain, w_out, g_ffn, w_up, conv_w, conv_b, w_down):
    b, t, d = x.shape
    nb = t // MOBA_BLOCK
    assert d == D_MODEL and t % SEQ_TILE == 0
    assert AUX_GROUP % nb == 0 and 2 * nb <= AUX_GROUP and 2 * N_SPLIT <= AUX_GROUP // 2

    mod = _adaln(c, w_ada, b_ada).reshape(b, 6, 1, d)
    sh1, sc1, gt1, sh2, sc2, gt2 = [mod[:, i] for i in range(6)]

    assert w_in.shape == (d, IN_QKV_COLS + N_FOX_HEADS)
    gains = jnp.stack([jnp.tile(g, N_MOBA_HEADS) for g in
                       (moba_q_gain, moba_k_gain, fox_q_gain, fox_k_gain)])
    r = np.arange(2 * LANES) // HEAD_DIM
    bd = jnp.asarray(r[:, None] == r[None, :], jnp.bfloat16)
    mq, mk, fq, fk, fl, mvt, fvt, wo16, wu16, wd16 = _inproj(
        x, sc1, sh1, g_mix.reshape(1, d), w_in.T, gains, _rope_tables(t), bd, [w_out, w_up, w_down])

    mqa, fqa, fka = _routing(mq, mk, fl, b_forget.reshape(1, N_FOX_HEADS))
    o_moba = _attention(True, mq, mqa, mk, mvt)
    o_fox = _attention(False, fq, fqa, fk, fvt, fka)

    cv = jnp.concatenate([conv_w, conv_b[None, :],
                          jnp.zeros((HALO - CONV_WIDTH - 1, 2 * D_FF), jnp.float32)], axis=0)
    return _ffn(x, o_moba, o_fox, wo16, gt1, g_ffn.reshape(1, d), sc2, sh2, wu16, cv, wd16, gt2)


def kernel(x, c, w_ada, b_ada, g_mix, w_in, b_forget, moba_q_gain, moba_k_gain, fox_q_gain,
           fox_k_gain, w_out, g_ffn, w_up, conv_w, conv_b, w_down):
    for l in range(w_ada.shape[0]):
        x = _layer(x, c, w_ada[l], b_ada[l], g_mix[l], w_in[l], b_forget[l], moba_q_gain[l],
                   moba_k_gain[l], fox_q_gain[l], fox_k_gain[l], w_out[l], g_ffn[l], w_up[l],
                   conv_w[l], conv_b[l], w_down[l])
    return x
```

```python
import math

import jax
import jax.numpy as jnp
import numpy as np
from jax import lax
from jax.experimental import pallas as pl
from jax.experimental.pallas import tpu as pltpu

D_MODEL = 1024
HEAD_DIM = 64
N_MOBA_HEADS = 8
N_FOX_HEADS = 8
MOBA_WIDTH = N_MOBA_HEADS * HEAD_DIM
FOX_WIDTH = N_FOX_HEADS * HEAD_DIM
MOBA_BLOCK = 256
MOBA_TOPK = 3
ROPE_THETA = 500000.0
ROPE_DIM = HEAD_DIM // 4
D_FF = 2816
CONV_WIDTH = 3
NORM_EPS = 1e-6
NEG_INF = -1e30
LOG2E = math.log2(math.e)
Q_SCALE = HEAD_DIM ** -0.5 * LOG2E
LANES = 128
BF16_ROWS = 16
IN_QKV_COLS = 3 * MOBA_WIDTH + 3 * FOX_WIDTH
V_ROWS = HEAD_DIM + BF16_ROWS
SEQ_TILE = 256
ROW_TILE = 512
ADALN_COLS = 1536
ATTN_PAIRS_PER_STEP = 2
ATTN_LOOKAHEAD = 5
FFN_PARTS = 2
FFN_LOOKAHEAD = 2
FF_CHUNK = 256
N_FF_CHUNKS = D_FF // FF_CHUNK
HALO = 8
N_SPLIT = 3
AUX_GROUP = 16
VMEM_LIMIT = 56 * 1024 * 1024

_NT = (((1,), (1,)), ((), ()))


def _bf16(a):
    return a.astype(jnp.bfloat16)


def _dot(a, b):
    return jnp.dot(a, b, preferred_element_type=jnp.float32)


def _dot_nt(a, b):
    return lax.dot_general(a, b, _NT, preferred_element_type=jnp.float32)


def _sigmoid(a):
    return 1.0 / (1.0 + jnp.exp(-a))


def _adaln_kernel(c_ref, w_ref, b_ref, o_ref):
    c = c_ref[...]
    s = c * _sigmoid(c)
    o_ref[...] = _dot(_bf16(s), _bf16(w_ref[...])) + b_ref[...]


def _adaln(c, w_ada, b_ada):
    b, d = c.shape
    n = w_ada.shape[1]
    tn = ADALN_COLS
    return pl.pallas_call(
        _adaln_kernel,
        grid=(n // tn,),
        in_specs=[pl.BlockSpec((b, d), lambda j: (0, 0)),
                  pl.BlockSpec((d, tn), lambda j: (0, j)),
                  pl.BlockSpec((1, tn), lambda j: (0, j))],
        out_specs=pl.BlockSpec((b, tn), lambda j: (0, j)),
        out_shape=jax.ShapeDtypeStruct((b, n), jnp.float32),
        compiler_params=pltpu.CompilerParams(vmem_limit_bytes=VMEM_LIMIT),
        name="adaln",
    )(c, w_ada, b_ada.reshape(1, n))


def _rms_mod(x, g, sc, sh):
    ms = jnp.mean(x * x, axis=-1, keepdims=True)
    y = x * lax.rsqrt(ms + NORM_EPS)
    return (y * g) * (1.0 + sc) + sh


def _head_norm(p, gain, bd):
    sq = _bf16(p * p)
    half = 2 * LANES
    ss = jnp.concatenate([_dot(sq[:, :half], bd), _dot(sq[:, half:], bd)], axis=1)
    return (p * lax.rsqrt(ss * (1.0 / HEAD_DIM) + NORM_EPS)) * gain


def _rope(y, c, sa, sb):
    half = ROPE_DIM // 2
    outs = []
    for i in range(y.shape[1] // LANES):
        yc = y[:, i * LANES:(i + 1) * LANES]
        up = pltpu.roll(yc, LANES - half, 1)
        dn = pltpu.roll(yc, half, 1)
        outs.append(yc * c + up * sa + dn * sb)
    return jnp.concatenate(outs, axis=1)


def _inproj_kernel(x_ref, sc_ref, sh_ref, g_ref, wint_ref, gains_ref, rc_ref, rsa_ref, rsb_ref,
                   bd_ref, *refs):
    n_cast = (len(refs) - 8) // 2
    cast_in, refs = refs[:n_cast], refs[n_cast:]
    mq_ref, mk_ref, fq_ref, fk_ref, fl_ref, mvt_ref, fvt_ref = refs[:7]
    cast_out, w_ref = refs[7:7 + n_cast], refs[-1]
    for src, dst in zip(cast_in, cast_out):
        dst[...] = _bf16(src[...])
    w = MOBA_WIDTH
    n_groups = IN_QKV_COLS // w

    @pl.when((pl.program_id(0) == 0) & (pl.program_id(1) == 0))
    def _():
        for i in range(n_groups):
            w_ref[i * w:(i + 1) * w, :] = _bf16(wint_ref[i * w:(i + 1) * w, :])
        logit_rows = wint_ref[IN_QKV_COLS:IN_QKV_COLS + N_FOX_HEADS, :]
        r = lax.broadcasted_iota(jnp.int32, (LANES, 1), 0)
        r_head = jnp.where(r % AUX_GROUP < 2 * N_SPLIT, _group_head(r), -1)
        wlog = jnp.zeros((LANES, logit_rows.shape[1]), jnp.float32)
        for head in range(N_FOX_HEADS):
            wlog = jnp.where(r_head == head, logit_rows[head:head + 1, :], wlog)
        w_ref[IN_QKV_COLS:IN_QKV_COLS + LANES, :] = _bf16(wlog)

    half = x_ref.shape[1] // 2
    hn_halves = [_bf16(_rms_mod(x_ref[0, r * half:(r + 1) * half, :], g_ref[...], sc_ref[0], sh_ref[0]))
                 for r in range(2)]
    p_mq = jnp.concatenate([_dot_nt(h, w_ref[0:w, :]) for h in hn_halves], axis=0)
    hn = jnp.concatenate(hn_halves, axis=0)
    bd = bd_ref[...]
    c, sa, sb = rc_ref[...], rsa_ref[...], rsb_ref[...]
    proj = lambda i: _dot_nt(hn, w_ref[i * w:(i + 1) * w, :])

    p_mk = proj(1)
    mq = _rope(_head_norm(p_mq, gains_ref[0:1, :], bd), c, sa, sb)
    mq_ref[0] = _bf16(mq * Q_SCALE)
    p_fq = proj(3)
    mk = _rope(_head_norm(p_mk, gains_ref[1:2, :], bd), c, sa, sb)
    mk_ref[0] = _bf16(mk)
    p_fk = proj(4)
    fq = _head_norm(p_fq, gains_ref[2:3, :], bd)
    fq_ref[0] = _bf16(fq * Q_SCALE)
    fl_ref[0] = _dot_nt(hn, w_ref[IN_QKV_COLS:IN_QKV_COLS + LANES, :])
    mvt = _dot_nt(w_ref[2 * w:3 * w, :], hn)
    fk = _head_norm(p_fk, gains_ref[3:4, :], bd)
    fk_ref[0] = _bf16(fk)
    mvt_ref[0] = _bf16(mvt)
    fvt_ref[0] = _bf16(_dot_nt(w_ref[5 * w:6 * w, :], hn))


def _rope_tables(t):
    half = ROPE_DIM // 2
    inv_freq = np.power(ROPE_THETA, -2.0 * np.arange(half, dtype=np.float64) / ROPE_DIM)
    ang = np.arange(t, dtype=np.float64)[:, None] * inv_freq[None, :]
    cos, sin = np.cos(ang), np.sin(ang)
    d = np.arange(LANES) % HEAD_DIM
    first = (d < half)[None, :]
    second = ((d >= half) & (d < ROPE_DIM))[None, :]
    idx = np.where(d < ROPE_DIM, d % half, 0)
    cos_l, sin_l = cos[:, idx], sin[:, idx]
    c = np.where(first | second, cos_l, 1.0)
    sa = np.where(first, -sin_l, 0.0)
    sb = np.where(second, sin_l, 0.0)
    return tuple(jnp.asarray(a, jnp.float32) for a in (c, sa, sb))


def _slab_rows(n_rows, n_steps):
    tiles = n_rows // BF16_ROWS
    assert tiles * BF16_ROWS == n_rows
    n_slabs = max(g for g in range(1, tiles + 1) if tiles % g == 0 and g <= n_steps)
    return n_rows // n_slabs


def _inproj(x, sc1, sh1, g_mix, w_in_t, gains, tables, bd, to_cast):
    b, t, d = x.shape
    tm = min(ROW_TILE, t)
    w = MOBA_WIDTH
    nt = t // tm
    cast_specs, cast_shapes = [], []
    for a in to_cast:
        rows = _slab_rows(a.shape[0], b * nt)
        last = a.shape[0] // rows - 1
        slab = lambda i, j, last=last: (jnp.minimum(i * nt + j, last), 0)
        cast_specs.append(pl.BlockSpec((rows, a.shape[1]), slab))
        cast_shapes.append(jax.ShapeDtypeStruct(a.shape, jnp.bfloat16))
    row = lambda i, j: (i, 0, 0)
    tile = lambda i, j: (i, j, 0)
    tile_t = lambda i, j: (i, 0, j)
    const = lambda i, j: (0, 0)
    wide = jax.ShapeDtypeStruct((b, t, w), jnp.bfloat16)
    wide_t = jax.ShapeDtypeStruct((b, w, t), jnp.bfloat16)
    out_specs = ([pl.BlockSpec((1, tm, w), tile)] * 4 + [pl.BlockSpec((1, tm, LANES), tile)]
                 + [pl.BlockSpec((1, w, tm), tile_t)] * 2)
    return pl.pallas_call(
        _inproj_kernel,
        grid=(b, t // tm),
        in_specs=[pl.BlockSpec((1, tm, d), tile),
                  pl.BlockSpec((1, 1, d), row),
                  pl.BlockSpec((1, 1, d), row),
                  pl.BlockSpec((1, d), const),
                  pl.BlockSpec(w_in_t.shape, const, pipeline_mode=pl.Buffered(1)),
                  pl.BlockSpec((4, w), const),
                  pl.BlockSpec((tm, LANES), lambda i, j: (j, 0)),
                  pl.BlockSpec((tm, LANES), lambda i, j: (j, 0)),
                  pl.BlockSpec((tm, LANES), lambda i, j: (j, 0)),
                  pl.BlockSpec((2 * LANES, 2 * LANES), const)] + cast_specs,
        out_specs=out_specs + cast_specs,
        out_shape=([wide] * 4 + [jax.ShapeDtypeStruct((b, t, LANES), jnp.float32)] + [wide_t] * 2
                   + cast_shapes),
        scratch_shapes=[pltpu.VMEM((IN_QKV_COLS + LANES, d), jnp.bfloat16)],
        compiler_params=pltpu.CompilerParams(vmem_limit_bytes=VMEM_LIMIT),
        name="inproj",
    )(x, sc1, sh1, g_mix, w_in_t, gains, *tables, bd, *to_cast)


def _split3(a):
    hi = _bf16(a)
    r1 = a - hi.astype(jnp.float32)
    mid = _bf16(r1)
    lo = _bf16(r1 - mid.astype(jnp.float32))
    return hi, mid, lo


def _group_head(lane_idx):
    return 2 * ((lane_idx % HEAD_DIM) // AUX_GROUP) + jnp.where(lane_idx < HEAD_DIM, 1, 0)


def _routing_kernel(mq_ref, mk_ref, fl_ref, bf_ref, mqa_ref, fqa_ref, fka_ref):
    t = mq_ref.shape[1]
    nb = t // MOBA_BLOCK
    k = mk_ref[0]
    blk_of_col = lax.broadcasted_iota(jnp.int32, (nb, t), 1) // MOBA_BLOCK
    blk_row = lax.broadcasted_iota(jnp.int32, (nb, t), 0)
    ind = _bf16(jnp.where(blk_of_col == blk_row, 1.0, 0.0))
    kmean = _dot(ind, k) * (1.0 / MOBA_BLOCK)
    kmt = jnp.concatenate([kmean] * (LANES // nb), axis=0)
    r_head = _group_head(lax.broadcasted_iota(jnp.int32, kmt.shape, 0))
    c_head = lax.broadcasted_iota(jnp.int32, kmt.shape, 1) // HEAD_DIM
    kmt = jnp.where(r_head == c_head, kmt, 0.0)
    k_hi = _bf16(kmt)
    k_lo = _bf16(kmt - k_hi.astype(jnp.float32))

    lane = lax.broadcasted_iota(jnp.int32, (1, LANES), 1)
    j = lane % nb
    ranked = min(MOBA_TOPK + 1, nb) * MOBA_BLOCK
    own = lax.broadcasted_iota(jnp.int32, (ranked, 1), 0) // MOBA_BLOCK
    mqa_ref[0, 0:ranked, :] = jnp.where(j <= own, 0.0, NEG_INF)
    if ranked < t:
        q = mq_ref[0, ranked:, :]
        gate = (_dot_nt(q, k_hi) + _dot_nt(q, k_lo)) * (1.0 / Q_SCALE)
        own = (lax.broadcasted_iota(jnp.int32, (t - ranked, 1), 0) + ranked) // MOBA_BLOCK
        past = j < own
        g = jnp.where(past, gate, -jnp.inf)
        rank = jnp.zeros(g.shape, jnp.int32)
        for d in range(1, nb):
            other = pltpu.roll(g, LANES - d, 1)
            first = ((j + d) % nb) < j
            beats = (other > g) | (first & (other == g))
            rank = rank + jnp.where(beats, 1, 0)
        attend = (past & (rank < MOBA_TOPK)) | (j == own)
        mqa_ref[0, ranked:, :] = jnp.where(attend, 0.0, NEG_INF)

    lane_head = _group_head(lane)
    bias = jnp.zeros((1, LANES), jnp.float32)
    for head in range(N_FOX_HEADS):
        bias = jnp.where(lane_head == head, bf_ref[:, head:head + 1], bias)
    z = fl_ref[0] + bias
    logf = -(jnp.maximum(-z, 0.0) + jnp.log1p(jnp.exp(-jnp.abs(z))))
    rr = lax.broadcasted_iota(jnp.int32, (MOBA_BLOCK, MOBA_BLOCK), 0)
    cc = lax.broadcasted_iota(jnp.int32, (MOBA_BLOCK, MOBA_BLOCK), 1)
    tri = _bf16(jnp.where(cc <= rr, 1.0, 0.0))
    slot = lane % AUX_GROUP
    carry = jnp.zeros((1, LANES), jnp.float32)
    for i in range(nb):
        rows = slice(i * MOBA_BLOCK, (i + 1) * MOBA_BLOCK)
        hi, mid, lo = _split3(logf[rows, :])
        cum = (_dot(tri, hi) + _dot(tri, mid)) + _dot(tri, lo) + carry
        carry = cum[MOBA_BLOCK - 1:MOBA_BLOCK, :]
        parts = [p.astype(jnp.float32) for p in _split3(cum * LOG2E)]
        fq = jnp.where(slot < 2 * N_SPLIT, 1.0, 0.0)
        fk = jnp.where(slot < N_SPLIT, 1.0, 0.0)
        for s in range(N_SPLIT):
            fq = jnp.where(slot == s, parts[s], fq)
            fk = jnp.where(slot == N_SPLIT + s, -parts[s], fk)
        fqa_ref[0, rows, :] = fq
        fka_ref[0, rows, :] = fk


def _routing(mq, mk, fl, b_forget):
    b, t, w = mq.shape
    full = lambda i: (i, 0, 0)
    aux = jax.ShapeDtypeStruct((b, t, LANES), jnp.float32)
    return pl.pallas_call(
        _routing_kernel,
        grid=(b,),
        in_specs=[pl.BlockSpec((1, t, w), full),
                  pl.BlockSpec((1, t, w), full),
                  pl.BlockSpec((1, t, LANES), full),
                  pl.BlockSpec((1, N_FOX_HEADS), lambda i: (0, 0))],
        out_specs=[pl.BlockSpec((1, t, LANES), full)] * 3,
        out_shape=[aux] * 3,
        compiler_params=pltpu.CompilerParams(vmem_limit_bytes=VMEM_LIMIT),
        name="routing",
    )(mq, mk, fl, b_forget)


def _attn_kernel(q_ref, qa_ref, k_ref, vt_ref, ka_ref, o_ref, kx_ref, vx_ref):
    tq = SEQ_TILE
    t = k_ref.shape[1]
    nb = t // tq
    n_local = q_ref.shape[2] // LANES
    lane = lax.broadcasted_iota(jnp.int32, (1, LANES), 1)
    low = lane < HEAD_DIM
    in_group = (lane % HEAD_DIM) < AUX_GROUP // 2

    def aux_shift(pi):
        return (LANES - (pl.program_id(1) * n_local + pi) * AUX_GROUP) % LANES

    for pi in range(n_local):
        cols = slice(pi * LANES, (pi + 1) * LANES)
        k2 = k_ref[0, :, cols]
        ka = _bf16(pltpu.roll(ka_ref[0], aux_shift(pi), 1))
        kx_ref[2 * pi] = jnp.where(low, k2, ka)
        kx_ref[2 * pi + 1] = jnp.where(low, ka, k2)
        for h in range(2):
            first = pi * LANES + h * HEAD_DIM
            vx_ref[2 * pi + h, 0:HEAD_DIM, :] = vt_ref[0, first:first + HEAD_DIM, :]
            vx_ref[2 * pi + h, HEAD_DIM:V_ROWS, :] = jnp.ones((V_ROWS - HEAD_DIM, t), jnp.bfloat16)

    key = lax.broadcasted_iota(jnp.int32, (tq, tq), 0)
    qry = lax.broadcasted_iota(jnp.int32, (tq, tq), 1)
    causal = key <= qry

    def scores(n, pi, h):
        rows = slice(n * tq, (n + 1) * tq)
        q2 = q_ref[0, rows, pi * LANES:(pi + 1) * LANES]
        qa = pltpu.roll(qa_ref[0, rows, :], aux_shift(pi), 1)
        qa = _bf16(jnp.where(in_group, qa, 0.0))
        qx = jnp.where(low, q2, qa) if h == 0 else jnp.where(low, qa, q2)
        hd = 2 * pi + h
        s = _dot_nt(kx_ref[hd, 0:(n + 1) * tq, :], qx)
        sd = jnp.where(causal, s[n * tq:, :], NEG_INF)
        s = sd if n == 0 else jnp.concatenate([s[:n * tq, :], sd], axis=0)
        return (n, pi, h), s, jnp.max(s, axis=0, keepdims=True)

    def weighted_values(chain, s, m):
        n, pi, h = chain
        acc = _dot(vx_ref[2 * pi + h, :, 0:(n + 1) * tq], _bf16(jnp.exp2(s - m)))
        return acc[0:HEAD_DIM, :] / acc[HEAD_DIM:HEAD_DIM + 1, :]

    chains = [(n, pi, h) for n in reversed(range(nb)) for pi in range(n_local) for h in range(2)]
    outs = {}
    queue = [scores(*c) for c in chains[:ATTN_LOOKAHEAD]]
    for nxt in chains[ATTN_LOOKAHEAD:] + [None] * ATTN_LOOKAHEAD:
        if nxt is not None:
            queue.append(scores(*nxt))
        done = queue.pop(0)
        outs[done[0]] = weighted_values(*done)
        n, pi, h = done[0]
        if h == 1:
            o_t = jnp.concatenate([outs.pop((n, pi, 0)), outs.pop((n, pi, 1))], axis=0)
            o_ref[0, n * tq:(n + 1) * tq, pi * LANES:(pi + 1) * LANES] = _bf16(o_t.T)


def _block_one_hot(t):
    blk = np.arange(t)[:, None] // MOBA_BLOCK
    slot = np.arange(LANES)[None, :] % AUX_GROUP
    return jnp.asarray(slot == blk, jnp.float32)[None]


def _attention(name, q, qa, k, vt, ka):
    b, t, w = q.shape
    wl = ATTN_PAIRS_PER_STEP * LANES
    assert w % wl == 0 and t % SEQ_TILE == 0
    spec = pl.BlockSpec((1, t, wl), lambda i, p: (i, 0, p))
    spec_t = pl.BlockSpec((1, wl, t), lambda i, p: (i, p, 0))
    spec_aux = pl.BlockSpec((1, t, LANES), lambda i, p: (i, 0, 0))
    ka_per_batch = ka.shape[0] == b
    spec_ka = pl.BlockSpec((1, t, LANES), lambda i, p: (i if ka_per_batch else 0, 0, 0))
    return pl.pallas_call(
        _attn_kernel,
        grid=(b, w // wl),
        in_specs=[spec, spec_aux, spec, spec_t, spec_ka],
        out_specs=spec,
        out_shape=jax.ShapeDtypeStruct((b, t, w), jnp.bfloat16),
        scratch_shapes=[pltpu.VMEM((2 * ATTN_PAIRS_PER_STEP, t, LANES), jnp.bfloat16),
                        pltpu.VMEM((2 * ATTN_PAIRS_PER_STEP, V_ROWS, t), jnp.bfloat16)],
        compiler_params=pltpu.CompilerParams(vmem_limit_bytes=VMEM_LIMIT),
        name=name,
    )(q, qa, k, vt, ka)


def _ffn_kernel(x_ref, om_ref, of_ref, wo_ref, gt1_ref, g_ref, sc_ref, sh_ref, wup_ref, cv_ref,
                wdn_ref, gt2_ref, out_ref, x1_ref, halo_ref):
    tm = x_ref.shape[1]

    @pl.when(pl.program_id(1) == 0)
    def _():
        halo_ref[...] = jnp.zeros(halo_ref.shape, jnp.float32)

    pm = tm // FFN_PARTS
    slabs = [slice(p * pm, (p + 1) * pm) for p in range(FFN_PARTS)]
    attn = [_dot(jnp.concatenate([om_ref[0, r, :], of_ref[0, r, :]], axis=1), wo_ref[...])
            for r in slabs]

    def normed(p):
        x1 = x_ref[0, slabs[p], :] + gt1_ref[0] * attn[p]
        x1_ref[slabs[p], :] = x1
        return _bf16(_rms_mod(x1, g_ref[...], sc_ref[0], sh_ref[0]))

    def up_proj(hn, c):
        gate_cols = slice(c * FF_CHUNK, (c + 1) * FF_CHUNK)
        val_cols = slice(D_FF + c * FF_CHUNK, D_FF + (c + 1) * FF_CHUNK)
        u = jnp.concatenate([_dot(hn, wup_ref[:, gate_cols]), _dot(hn, wup_ref[:, val_cols])],
                            axis=1)
        return u, gate_cols, val_cols

    hn = normed(0)
    tails = [halo_ref[c] for c in range(N_FF_CHUNKS)]
    for p in range(FFN_PARTS):
        hn_next = None
        hmid = []
        ahead = [up_proj(hn, c) for c in range(FFN_LOOKAHEAD)]
        for c in range(N_FF_CHUNKS):
            if c + FFN_LOOKAHEAD < N_FF_CHUNKS:
                ahead.append(up_proj(hn, c + FFN_LOOKAHEAD))
            u, gate_cols, val_cols = ahead.pop(0)
            ext = jnp.concatenate([tails[c], u], axis=0)
            tails[c] = u[pm - HALO:pm, :]
            cv = jnp.concatenate([cv_ref[:, gate_cols], cv_ref[:, val_cols]], axis=1)
            u1 = pltpu.roll(ext, 1, 0)[HALO:, :]
            u2 = pltpu.roll(ext, 2, 0)[HALO:, :]
            uc = cv[0:1, :] * u2 + cv[1:2, :] * u1 + cv[2:3, :] * u + cv[3:4, :]
            a = uc[:, :FF_CHUNK]
            val = uc[:, FF_CHUNK:]
            half = 0.5 * a
            hmid.append(_bf16((half + half * jnp.tanh(half)) * val))
            if c == FFN_LOOKAHEAD and p + 1 < FFN_PARTS:
                hn_next = normed(p + 1)
        ffn = _dot(jnp.concatenate(hmid, axis=1), wdn_ref[...])
        out_ref[0, slabs[p], :] = x1_ref[slabs[p], :] + gt2_ref[0] * ffn
        hn = hn_next
    for c in range(N_FF_CHUNKS):
        halo_ref[c] = tails[c]


def _ffn(x, om, of, wo, gt1, g_ffn, sc2, sh2, wup, cv, wdn, gt2):
    b, t, d = x.shape
    tm = min(ROW_TILE, t)
    row = lambda i, j: (i, 0, 0)
    tile = lambda i, j: (i, j, 0)
    const2 = lambda i, j: (0, 0)
    once = dict(pipeline_mode=pl.Buffered(1))
    return pl.pallas_call(
        _ffn_kernel,
        grid=(b, t // tm),
        in_specs=[pl.BlockSpec((1, tm, d), tile),
                  pl.BlockSpec((1, tm, MOBA_WIDTH), tile),
                  pl.BlockSpec((1, tm, FOX_WIDTH), tile),
                  pl.BlockSpec((d, d), const2, **once),
                  pl.BlockSpec((1, 1, d), row),
                  pl.BlockSpec((1, d), const2),
                  pl.BlockSpec((1, 1, d), row),
                  pl.BlockSpec((1, 1, d), row),
                  pl.BlockSpec((d, 2 * D_FF), const2, **once),
                  pl.BlockSpec((HALO, 2 * D_FF), const2, **once),
                  pl.BlockSpec((D_FF, d), const2, **once),
                  pl.BlockSpec((1, 1, d), row)],
        out_specs=pl.BlockSpec((1, tm, d), tile),
        out_shape=jax.ShapeDtypeStruct((b, t, d), jnp.float32),
        scratch_shapes=[pltpu.VMEM((tm, d), jnp.float32),
                        pltpu.VMEM((N_FF_CHUNKS, HALO, 2 * FF_CHUNK), jnp.float32)],
        compiler_params=pltpu.CompilerParams(vmem_limit_bytes=VMEM_LIMIT),
        name="ffn",
    )(x, om, of, wo, gt1, g_ffn, sc2, sh2, wup, cv, wdn, gt2)


def _layer(x, c, w_ada, b_ada, g_mix, w_in, b_forget, moba_q_gain, moba_k_gain, fox_q_gain,
           fox_k_gain, w_out, g_ffn, w_up, conv_w, conv_b, w_down):
    b, t, d = x.shape
    nb = t // MOBA_BLOCK
    assert d == D_MODEL and t % SEQ_TILE == 0
    assert AUX_GROUP % nb == 0 and 2 * nb <= AUX_GROUP and 2 * N_SPLIT <= AUX_GROUP // 2

    mod = _adaln(c, w_ada, b_ada).reshape(b, 6, 1, d)
    sh1, sc1, gt1, sh2, sc2, gt2 = [mod[:, i] for i in range(6)]

    assert w_in.shape == (d, IN_QKV_COLS + N_FOX_HEADS)
    gains = jnp.stack([jnp.tile(g, N_MOBA_HEADS) for g in
                       (moba_q_gain, moba_k_gain, fox_q_gain, fox_k_gain)])
    r = np.arange(2 * LANES) // HEAD_DIM
    bd = jnp.asarray(r[:, None] == r[None, :], jnp.bfloat16)
    mq, mk, fq, fk, fl, mvt, fvt, wo16, wu16, wd16 = _inproj(
        x, sc1, sh1, g_mix.reshape(1, d), w_in.T, gains, _rope_tables(t), bd, [w_out, w_up, w_down])

    mqa, fqa, fka = _routing(mq, mk, fl, b_forget.reshape(1, N_FOX_HEADS))
    o_moba = _attention("moba", mq, mqa, mk, mvt, _block_one_hot(t))
    o_fox = _attention("fox", fq, fqa, fk, fvt, fka)

    cv = jnp.concatenate([conv_w, conv_b[None, :],
                          jnp.zeros((HALO - CONV_WIDTH - 1, 2 * D_FF), jnp.float32)], axis=0)
    return _ffn(x, o_moba, o_fox, wo16, gt1, g_ffn.reshape(1, d), sc2, sh2, wu16, cv, wd16, gt2)


def kernel(x, c, w_ada, b_ada, g_mix, w_in, b_forget, moba_q_gain, moba_k_gain, fox_q_gain,
           fox_k_gain, w_out, g_ffn, w_up, conv_w, conv_b, w_down):
    for l in range(w_ada.shape[0]):
        x = _layer(x, c, w_ada[l], b_ada[l], g_mix[l], w_in[l], b_forget[l], moba_q_gain[l],
                   moba_k_gain[l], fox_q_gain[l], fox_k_gain[l], w_out[l], g_ffn[l], w_up[l],
                   conv_w[l], conv_b[l], w_down[l])
    return x
```

```python
import math

import jax
import jax.numpy as jnp
import numpy as np
from jax import lax
from jax.experimental import pallas as pl
from jax.experimental.pallas import tpu as pltpu

D_MODEL = 1024
HEAD_DIM = 64
N_MOBA_HEADS = 8
N_FOX_HEADS = 8
MOBA_WIDTH = N_MOBA_HEADS * HEAD_DIM
FOX_WIDTH = N_FOX_HEADS * HEAD_DIM
MOBA_BLOCK = 256
MOBA_TOPK = 3
ROPE_THETA = 500000.0
ROPE_DIM = HEAD_DIM // 4
D_FF = 2816
CONV_WIDTH = 3
NORM_EPS = 1e-6
NEG_INF = -1e30
LOG2E = math.log2(math.e)
Q_SCALE = HEAD_DIM ** -0.5 * LOG2E
LANES = 128
BF16_ROWS = 16
IN_QKV_COLS = 3 * MOBA_WIDTH + 3 * FOX_WIDTH
V_ROWS = HEAD_DIM + BF16_ROWS
SEQ_TILE = 256
ROW_TILE = 512
ADALN_COLS = 1536
ATTN_PAIRS_PER_STEP = 2
ATTN_LOOKAHEAD = 5
FFN_PARTS = 2
FFN_LOOKAHEAD = 2
FF_CHUNK = 256
N_FF_CHUNKS = D_FF // FF_CHUNK
HALO = 8
N_SPLIT = 3
AUX_GROUP = 16
VMEM_LIMIT = 56 * 1024 * 1024

_NT = (((1,), (1,)), ((), ()))


def _bf16(a):
    return a.astype(jnp.bfloat16)


def _dot(a, b):
    return jnp.dot(a, b, preferred_element_type=jnp.float32)


def _dot_nt(a, b):
    return lax.dot_general(a, b, _NT, preferred_element_type=jnp.float32)


def _sigmoid(a):
    return 1.0 / (1.0 + jnp.exp(-a))


def _adaln_kernel(c_ref, w_ref, b_ref, o_ref):
    c = c_ref[...]
    s = c * _sigmoid(c)
    o_ref[...] = _dot(_bf16(s), _bf16(w_ref[...])) + b_ref[...]


def _adaln(c, w_ada, b_ada):
    b, d = c.shape
    n = w_ada.shape[1]
    tn = ADALN_COLS
    return pl.pallas_call(
        _adaln_kernel,
        grid=(n // tn,),
        in_specs=[pl.BlockSpec((b, d), lambda j: (0, 0)),
                  pl.BlockSpec((d, tn), lambda j: (0, j)),
                  pl.BlockSpec((1, tn), lambda j: (0, j))],
        out_specs=pl.BlockSpec((b, tn), lambda j: (0, j)),
        out_shape=jax.ShapeDtypeStruct((b, n), jnp.float32),
        compiler_params=pltpu.CompilerParams(vmem_limit_bytes=VMEM_LIMIT),
        name="adaln",
    )(c, w_ada, b_ada.reshape(1, n))


def _rms_mod(x, g, sc, sh):
    ms = jnp.mean(x * x, axis=-1, keepdims=True)
    y = x * lax.rsqrt(ms + NORM_EPS)
    return (y * g) * (1.0 + sc) + sh


def _head_norm(p, gain, bd):
    sq = _bf16(p * p)
    half = 2 * LANES
    ss = jnp.concatenate([_dot(sq[:, :half], bd), _dot(sq[:, half:], bd)], axis=1)
    return (p * lax.rsqrt(ss * (1.0 / HEAD_DIM) + NORM_EPS)) * gain


def _rope(y, c, sa, sb):
    half = ROPE_DIM // 2
    outs = []
    for i in range(y.shape[1] // LANES):
        yc = y[:, i * LANES:(i + 1) * LANES]
        up = pltpu.roll(yc, LANES - half, 1)
        dn = pltpu.roll(yc, half, 1)
        outs.append(yc * c + up * sa + dn * sb)
    return jnp.concatenate(outs, axis=1)


def _inproj_kernel(x_ref, sc_ref, sh_ref, g_ref, wint_ref, gains_ref, rc_ref, rsa_ref, rsb_ref,
                   bd_ref, *refs):
    n_cast = (len(refs) - 8) // 2
    cast_in, refs = refs[:n_cast], refs[n_cast:]
    mq_ref, mk_ref, fq_ref, fk_ref, fl_ref, mvt_ref, fvt_ref = refs[:7]
    cast_out, w_ref = refs[7:7 + n_cast], refs[-1]
    w = MOBA_WIDTH
    n_groups = IN_QKV_COLS // w

    @pl.when((pl.program_id(0) == 0) & (pl.program_id(1) == 0))
    def _():
        for i in range(n_groups):
            w_ref[i * w:(i + 1) * w, :] = _bf16(wint_ref[i * w:(i + 1) * w, :])
        logit_rows = wint_ref[IN_QKV_COLS:IN_QKV_COLS + N_FOX_HEADS, :]
        r = lax.broadcasted_iota(jnp.int32, (LANES, 1), 0)
        r_head = jnp.where(r % AUX_GROUP < 2 * N_SPLIT, _group_head(r), -1)
        wlog = jnp.zeros((LANES, logit_rows.shape[1]), jnp.float32)
        for head in range(N_FOX_HEADS):
            wlog = jnp.where(r_head == head, logit_rows[head:head + 1, :], wlog)
        w_ref[IN_QKV_COLS:IN_QKV_COLS + LANES, :] = _bf16(wlog)

    half = x_ref.shape[1] // 2
    hn_halves = [_bf16(_rms_mod(x_ref[0, r * half:(r + 1) * half, :], g_ref[...], sc_ref[0], sh_ref[0]))
                 for r in range(2)]
    p_mq = jnp.concatenate([_dot_nt(h, w_ref[0:w, :]) for h in hn_halves], axis=0)
    hn = jnp.concatenate(hn_halves, axis=0)
    bd = bd_ref[...]
    c, sa, sb = rc_ref[...], rsa_ref[...], rsb_ref[...]
    proj = lambda i: _dot_nt(hn, w_ref[i * w:(i + 1) * w, :])

    p_mk = proj(1)
    mq = _rope(_head_norm(p_mq, gains_ref[0:1, :], bd), c, sa, sb)
    mq_ref[0] = _bf16(mq * Q_SCALE)
    p_fq = proj(3)
    mk = _rope(_head_norm(p_mk, gains_ref[1:2, :], bd), c, sa, sb)
    mk_ref[0] = _bf16(mk)
    p_fk = proj(4)
    fq = _head_norm(p_fq, gains_ref[2:3, :], bd)
    fq_ref[0] = _bf16(fq * Q_SCALE)
    fl_ref[0] = _dot_nt(hn, w_ref[IN_QKV_COLS:IN_QKV_COLS + LANES, :])
    mvt = _dot_nt(w_ref[2 * w:3 * w, :], hn)
    fk = _head_norm(p_fk, gains_ref[3:4, :], bd)
    fk_ref[0] = _bf16(fk)
    mvt_ref[0] = _bf16(mvt)
    fvt_ref[0] = _bf16(_dot_nt(w_ref[5 * w:6 * w, :], hn))
    for src, dst in zip(cast_in, cast_out):
        dst[...] = _bf16(src[...])


def _rope_tables(t):
    half = ROPE_DIM // 2
    inv_freq = np.power(ROPE_THETA, -2.0 * np.arange(half, dtype=np.float64) / ROPE_DIM)
    ang = np.arange(t, dtype=np.float64)[:, None] * inv_freq[None, :]
    cos, sin = np.cos(ang), np.sin(ang)
    d = np.arange(LANES) % HEAD_DIM
    first = (d < half)[None, :]
    second = ((d >= half) & (d < ROPE_DIM))[None, :]
    idx = np.where(d < ROPE_DIM, d % half, 0)
    cos_l, sin_l = cos[:, idx], sin[:, idx]
    c = np.where(first | second, cos_l, 1.0)
    sa = np.where(first, -sin_l, 0.0)
    sb = np.where(second, sin_l, 0.0)
    return tuple(jnp.asarray(a, jnp.float32) for a in (c, sa, sb))


def _slab_rows(n_rows, n_steps):
    tiles = n_rows // BF16_ROWS
    assert tiles * BF16_ROWS == n_rows
    n_slabs = max(g for g in range(1, tiles + 1) if tiles % g == 0 and g <= n_steps)
    return n_rows // n_slabs


def _inproj(x, sc1, sh1, g_mix, w_in_t, gains, tables, bd, to_cast):
    b, t, d = x.shape
    tm = min(ROW_TILE, t)
    w = MOBA_WIDTH
    nt = t // tm
    cast_specs, cast_shapes = [], []
    for a in to_cast:
        rows = _slab_rows(a.shape[0], b * nt)
        last = a.shape[0] // rows - 1
        slab = lambda i, j, last=last: (jnp.minimum(i * nt + j, last), 0)
        cast_specs.append(pl.BlockSpec((rows, a.shape[1]), slab))
        cast_shapes.append(jax.ShapeDtypeStruct(a.shape, jnp.bfloat16))
    row = lambda i, j: (i, 0, 0)
    tile = lambda i, j: (i, j, 0)
    tile_t = lambda i, j: (i, 0, j)
    const = lambda i, j: (0, 0)
    wide = jax.ShapeDtypeStruct((b, t, w), jnp.bfloat16)
    wide_t = jax.ShapeDtypeStruct((b, w, t), jnp.bfloat16)
    out_specs = ([pl.BlockSpec((1, tm, w), tile)] * 4 + [pl.BlockSpec((1, tm, LANES), tile)]
                 + [pl.BlockSpec((1, w, tm), tile_t)] * 2)
    return pl.pallas_call(
        _inproj_kernel,
        grid=(b, t // tm),
        in_specs=[pl.BlockSpec((1, tm, d), tile),
                  pl.BlockSpec((1, 1, d), row),
                  pl.BlockSpec((1, 1, d), row),
                  pl.BlockSpec((1, d), const),
                  pl.BlockSpec(w_in_t.shape, const, pipeline_mode=pl.Buffered(1)),
                  pl.BlockSpec((4, w), const),
                  pl.BlockSpec((tm, LANES), lambda i, j: (j, 0)),
                  pl.BlockSpec((tm, LANES), lambda i, j: (j, 0)),
                  pl.BlockSpec((tm, LANES), lambda i, j: (j, 0)),
                  pl.BlockSpec((2 * LANES, 2 * LANES), const)] + cast_specs,
        out_specs=out_specs + cast_specs,
        out_shape=([wide] * 4 + [jax.ShapeDtypeStruct((b, t, LANES), jnp.float32)] + [wide_t] * 2
                   + cast_shapes),
        scratch_shapes=[pltpu.VMEM((IN_QKV_COLS + LANES, d), jnp.bfloat16)],
        compiler_params=pltpu.CompilerParams(vmem_limit_bytes=VMEM_LIMIT),
        name="inproj",
    )(x, sc1, sh1, g_mix, w_in_t, gains, *tables, bd, *to_cast)


def _split3(a):
    hi = _bf16(a)
    r1 = a - hi.astype(jnp.float32)
    mid = _bf16(r1)
    lo = _bf16(r1 - mid.astype(jnp.float32))
    return hi, mid, lo


def _group_head(lane_idx):
    return 2 * ((lane_idx % HEAD_DIM) // AUX_GROUP) + jnp.where(lane_idx < HEAD_DIM, 1, 0)


def _routing_kernel(mq_ref, mk_ref, fl_ref, bf_ref, mqa_ref, fqa_ref, fka_ref):
    t = mq_ref.shape[1]
    nb = t // MOBA_BLOCK
    k = mk_ref[0]
    blk_of_col = lax.broadcasted_iota(jnp.int32, (nb, t), 1) // MOBA_BLOCK
    blk_row = lax.broadcasted_iota(jnp.int32, (nb, t), 0)
    ind = _bf16(jnp.where(blk_of_col == blk_row, 1.0, 0.0))
    kmean = _dot(ind, k) * (1.0 / MOBA_BLOCK)
    kmt = jnp.concatenate([kmean] * (LANES // nb), axis=0)
    r_head = _group_head(lax.broadcasted_iota(jnp.int32, kmt.shape, 0))
    c_head = lax.broadcasted_iota(jnp.int32, kmt.shape, 1) // HEAD_DIM
    kmt = jnp.where(r_head == c_head, kmt, 0.0)
    k_hi = _bf16(kmt)
    k_lo = _bf16(kmt - k_hi.astype(jnp.float32))

    lane = lax.broadcasted_iota(jnp.int32, (1, LANES), 1)
    j = lane % nb
    ranked = min(MOBA_TOPK + 1, nb) * MOBA_BLOCK
    own = lax.broadcasted_iota(jnp.int32, (ranked, 1), 0) // MOBA_BLOCK
    mqa_ref[0, 0:ranked, :] = jnp.where(j <= own, 0.0, NEG_INF)
    if ranked < t:
        q = mq_ref[0, ranked:, :]
        gate = (_dot_nt(q, k_hi) + _dot_nt(q, k_lo)) * (1.0 / Q_SCALE)
        own = (lax.broadcasted_iota(jnp.int32, (t - ranked, 1), 0) + ranked) // MOBA_BLOCK
        past = j < own
        g = jnp.where(past, gate, -jnp.inf)
        rank = jnp.zeros(g.shape, jnp.int32)
        for d in range(1, nb):
            other = pltpu.roll(g, LANES - d, 1)
            first = ((j + d) % nb) < j
            beats = (other > g) | (first & (other == g))
            rank = rank + jnp.where(beats, 1, 0)
        attend = (past & (rank < MOBA_TOPK)) | (j == own)
        mqa_ref[0, ranked:, :] = jnp.where(attend, 0.0, NEG_INF)

    lane_head = _group_head(lane)
    bias = jnp.zeros((1, LANES), jnp.float32)
    for head in range(N_FOX_HEADS):
        bias = jnp.where(lane_head == head, bf_ref[:, head:head + 1], bias)
    z = fl_ref[0] + bias
    logf = -(jnp.maximum(-z, 0.0) + jnp.log1p(jnp.exp(-jnp.abs(z))))
    rr = lax.broadcasted_iota(jnp.int32, (MOBA_BLOCK, MOBA_BLOCK), 0)
    cc = lax.broadcasted_iota(jnp.int32, (MOBA_BLOCK, MOBA_BLOCK), 1)
    tri = _bf16(jnp.where(cc <= rr, 1.0, 0.0))
    slot = lane % AUX_GROUP
    carry = jnp.zeros((1, LANES), jnp.float32)
    for i in range(nb):
        rows = slice(i * MOBA_BLOCK, (i + 1) * MOBA_BLOCK)
        hi, mid, lo = _split3(logf[rows, :])
        cum = (_dot(tri, hi) + _dot(tri, mid)) + _dot(tri, lo) + carry
        carry = cum[MOBA_BLOCK - 1:MOBA_BLOCK, :]
        parts = [p.astype(jnp.float32) for p in _split3(cum * LOG2E)]
        fq = jnp.where(slot < 2 * N_SPLIT, 1.0, 0.0)
        fk = jnp.where(slot < N_SPLIT, 1.0, 0.0)
        for s in range(N_SPLIT):
            fq = jnp.where(slot == s, parts[s], fq)
            fk = jnp.where(slot == N_SPLIT + s, -parts[s], fk)
        fqa_ref[0, rows, :] = fq
        fka_ref[0, rows, :] = fk


def _routing(mq, mk, fl, b_forget):
    b, t, w = mq.shape
    full = lambda i: (i, 0, 0)
    aux = jax.ShapeDtypeStruct((b, t, LANES), jnp.float32)
    return pl.pallas_call(
        _routing_kernel,
        grid=(b,),
        in_specs=[pl.BlockSpec((1, t, w), full),
                  pl.BlockSpec((1, t, w), full),
                  pl.BlockSpec((1, t, LANES), full),
                  pl.BlockSpec((1, N_FOX_HEADS), lambda i: (0, 0))],
        out_specs=[pl.BlockSpec((1, t, LANES), full)] * 3,
        out_shape=[aux] * 3,
        compiler_params=pltpu.CompilerParams(vmem_limit_bytes=VMEM_LIMIT),
        name="routing",
    )(mq, mk, fl, b_forget)


def _attn_kernel(q_ref, qa_ref, k_ref, vt_ref, ka_ref, o_ref, kx_ref, vx_ref):
    tq = SEQ_TILE
    t = k_ref.shape[1]
    nb = t // tq
    n_local = q_ref.shape[2] // LANES
    lane = lax.broadcasted_iota(jnp.int32, (1, LANES), 1)
    low = lane < HEAD_DIM
    in_group = (lane % HEAD_DIM) < AUX_GROUP // 2

    def aux_shift(pi):
        return (LANES - (pl.program_id(1) * n_local + pi) * AUX_GROUP) % LANES

    for pi in range(n_local):
        cols = slice(pi * LANES, (pi + 1) * LANES)
        k2 = k_ref[0, :, cols]
        ka = _bf16(pltpu.roll(ka_ref[0], aux_shift(pi), 1))
        kx_ref[2 * pi] = jnp.where(low, k2, ka)
        kx_ref[2 * pi + 1] = jnp.where(low, ka, k2)
        for h in range(2):
            first = pi * LANES + h * HEAD_DIM
            vx_ref[2 * pi + h, 0:HEAD_DIM, :] = vt_ref[0, first:first + HEAD_DIM, :]
            vx_ref[2 * pi + h, HEAD_DIM:V_ROWS, :] = jnp.ones((V_ROWS - HEAD_DIM, t), jnp.bfloat16)

    key = lax.broadcasted_iota(jnp.int32, (tq, tq), 0)
    qry = lax.broadcasted_iota(jnp.int32, (tq, tq), 1)
    causal = key <= qry

    def scores(n, pi, h):
        rows = slice(n * tq, (n + 1) * tq)
        q2 = q_ref[0, rows, pi * LANES:(pi + 1) * LANES]
        qa = pltpu.roll(qa_ref[0, rows, :], aux_shift(pi), 1)
        qa = _bf16(jnp.where(in_group, qa, 0.0))
        qx = jnp.where(low, q2, qa) if h == 0 else jnp.where(low, qa, q2)
        hd = 2 * pi + h
        s = _dot_nt(kx_ref[hd, 0:(n + 1) * tq, :], qx)
        sd = jnp.where(causal, s[n * tq:, :], NEG_INF)
        s = sd if n == 0 else jnp.concatenate([s[:n * tq, :], sd], axis=0)
        return (n, pi, h), s, jnp.max(s, axis=0, keepdims=True)

    def weighted_values(chain, s, m):
        n, pi, h = chain
        acc = _dot(vx_ref[2 * pi + h, :, 0:(n + 1) * tq], _bf16(jnp.exp2(s - m)))
        return acc[0:HEAD_DIM, :] / acc[HEAD_DIM:HEAD_DIM + 1, :]

    chains = [(n, pi, h) for n in reversed(range(nb)) for pi in range(n_local) for h in range(2)]
    outs = {}
    queue = [scores(*c) for c in chains[:ATTN_LOOKAHEAD]]
    for nxt in chains[ATTN_LOOKAHEAD:] + [None] * ATTN_LOOKAHEAD:
        if nxt is not None:
            queue.append(scores(*nxt))
        done = queue.pop(0)
        outs[done[0]] = weighted_values(*done)
        n, pi, h = done[0]
        if h == 1:
            o_t = jnp.concatenate([outs.pop((n, pi, 0)), outs.pop((n, pi, 1))], axis=0)
            o_ref[0, n * tq:(n + 1) * tq, pi * LANES:(pi + 1) * LANES] = _bf16(o_t.T)


def _block_one_hot(t):
    blk = np.arange(t)[:, None] // MOBA_BLOCK
    slot = np.arange(LANES)[None, :] % AUX_GROUP
    return jnp.asarray(slot == blk, jnp.float32)[None]


def _attention(name, q, qa, k, vt, ka):
    b, t, w = q.shape
    wl = ATTN_PAIRS_PER_STEP * LANES
    assert w % wl == 0 and t % SEQ_TILE == 0
    spec = pl.BlockSpec((1, t, wl), lambda i, p: (i, 0, p))
    spec_t = pl.BlockSpec((1, wl, t), lambda i, p: (i, p, 0))
    spec_aux = pl.BlockSpec((1, t, LANES), lambda i, p: (i, 0, 0))
    ka_per_batch = ka.shape[0] == b
    spec_ka = pl.BlockSpec((1, t, LANES), lambda i, p: (i if ka_per_batch else 0, 0, 0))
    return pl.pallas_call(
        _attn_kernel,
        grid=(b, w // wl),
        in_specs=[spec, spec_aux, spec, spec_t, spec_ka],
        out_specs=spec,
        out_shape=jax.ShapeDtypeStruct((b, t, w), jnp.bfloat16),
        scratch_shapes=[pltpu.VMEM((2 * ATTN_PAIRS_PER_STEP, t, LANES), jnp.bfloat16),
                        pltpu.VMEM((2 * ATTN_PAIRS_PER_STEP, V_ROWS, t), jnp.bfloat16)],
        compiler_params=pltpu.CompilerParams(vmem_limit_bytes=VMEM_LIMIT),
        name=name,
    )(q, qa, k, vt, ka)


def _ffn_kernel(x_ref, om_ref, of_ref, wo_ref, gt1_ref, g_ref, sc_ref, sh_ref, wup_ref, cv_ref,
                wdn_ref, gt2_ref, out_ref, x1_ref, halo_ref):
    tm = x_ref.shape[1]

    @pl.when(pl.program_id(1) == 0)
    def _():
        halo_ref[...] = jnp.zeros(halo_ref.shape, jnp.float32)

    pm = tm // FFN_PARTS
    slabs = [slice(p * pm, (p + 1) * pm) for p in range(FFN_PARTS)]
    attn = [_dot(jnp.concatenate([om_ref[0, r, :], of_ref[0, r, :]], axis=1), wo_ref[...])
            for r in slabs]

    def normed(p):
        x1 = x_ref[0, slabs[p], :] + gt1_ref[0] * attn[p]
        x1_ref[slabs[p], :] = x1
        return _bf16(_rms_mod(x1, g_ref[...], sc_ref[0], sh_ref[0]))

    def up_proj(hn, c):
        gate_cols = slice(c * FF_CHUNK, (c + 1) * FF_CHUNK)
        val_cols = slice(D_FF + c * FF_CHUNK, D_FF + (c + 1) * FF_CHUNK)
        u = jnp.concatenate([_dot(hn, wup_ref[:, gate_cols]), _dot(hn, wup_ref[:, val_cols])],
                            axis=1)
        return u, gate_cols, val_cols

    hn = normed(0)
    tails = [halo_ref[c] for c in range(N_FF_CHUNKS)]
    for p in range(FFN_PARTS):
        hn_next = None
        hmid = []
        ahead = [up_proj(hn, c) for c in range(FFN_LOOKAHEAD)]
        for c in range(N_FF_CHUNKS):
            if c + FFN_LOOKAHEAD < N_FF_CHUNKS:
                ahead.append(up_proj(hn, c + FFN_LOOKAHEAD))
            u, gate_cols, val_cols = ahead.pop(0)
            ext = jnp.concatenate([tails[c], u], axis=0)
            tails[c] = u[pm - HALO:pm, :]
            cv = jnp.concatenate([cv_ref[:, gate_cols], cv_ref[:, val_cols]], axis=1)
            u1 = pltpu.roll(ext, 1, 0)[HALO:, :]
            u2 = pltpu.roll(ext, 2, 0)[HALO:, :]
            uc = cv[0:1, :] * u2 + cv[1:2, :] * u1 + cv[2:3, :] * u + cv[3:4, :]
            a = uc[:, :FF_CHUNK]
            val = uc[:, FF_CHUNK:]
            half = 0.5 * a
            hmid.append(_bf16((half + half * jnp.tanh(half)) * val))
            if c == FFN_LOOKAHEAD and p + 1 < FFN_PARTS:
                hn_next = normed(p + 1)
        ffn = _dot(jnp.concatenate(hmid, axis=1), wdn_ref[...])
        out_ref[0, slabs[p], :] = x1_ref[slabs[p], :] + gt2_ref[0] * ffn
        hn = hn_next
    for c in range(N_FF_CHUNKS):
        halo_ref[c] = tails[c]


def _ffn(x, om, of, wo, gt1, g_ffn, sc2, sh2, wup, cv, wdn, gt2):
    b, t, d = x.shape
    tm = min(ROW_TILE, t)
    row = lambda i, j: (i, 0, 0)
    tile = lambda i, j: (i, j, 0)
    const2 = lambda i, j: (0, 0)
    once = dict(pipeline_mode=pl.Buffered(1))
    return pl.pallas_call(
        _ffn_kernel,
        grid=(b, t // tm),
        in_specs=[pl.BlockSpec((1, tm, d), tile),
                  pl.BlockSpec((1, tm, MOBA_WIDTH), tile),
                  pl.BlockSpec((1, tm, FOX_WIDTH), tile),
                  pl.BlockSpec((d, d), const2, **once),
                  pl.BlockSpec((1, 1, d), row),
                  pl.BlockSpec((1, d), const2),
                  pl.BlockSpec((1, 1, d), row),
                  pl.BlockSpec((1, 1, d), row),
                  pl.BlockSpec((d, 2 * D_FF), const2, **once),
                  pl.BlockSpec((HALO, 2 * D_FF), const2, **once),
                  pl.BlockSpec((D_FF, d), const2, **once),
                  pl.BlockSpec((1, 1, d), row)],
        out_specs=pl.BlockSpec((1, tm, d), tile),
        out_shape=jax.ShapeDtypeStruct((b, t, d), jnp.float32),
        scratch_shapes=[pltpu.VMEM((tm, d), jnp.float32),
                        pltpu.VMEM((N_FF_CHUNKS, HALO, 2 * FF_CHUNK), jnp.float32)],
        compiler_params=pltpu.CompilerParams(vmem_limit_bytes=VMEM_LIMIT),
        name="ffn",
    )(x, om, of, wo, gt1, g_ffn, sc2, sh2, wup, cv, wdn, gt2)


def _layer(x, c, w_ada, b_ada, g_mix, w_in, b_forget, moba_q_gain, moba_k_gain, fox_q_gain,
           fox_k_gain, w_out, g_ffn, w_up, conv_w, conv_b, w_down):
    b, t, d = x.shape
    nb = t // MOBA_BLOCK
    assert d == D_MODEL and t % SEQ_TILE == 0
    assert AUX_GROUP % nb == 0 and 2 * nb <= AUX_GROUP and 2 * N_SPLIT <= AUX_GROUP // 2

    mod = _adaln(c, w_ada, b_ada).reshape(b, 6, 1, d)
    sh1, sc1, gt1, sh2, sc2, gt2 = [mod[:, i] for i in range(6)]

    assert w_in.shape == (d, IN_QKV_COLS + N_FOX_HEADS)
    gains = jnp.stack([jnp.tile(g, N_MOBA_HEADS) for g in
                       (moba_q_gain, moba_k_gain, fox_q_gain, fox_k_gain)])
    r = np.arange(2 * LANES) // HEAD_DIM
    bd = jnp.asarray(r[:, None] == r[None, :], jnp.bfloat16)
    mq, mk, fq, fk, fl, mvt, fvt, wo16, wu16, wd16 = _inproj(
        x, sc1, sh1, g_mix.reshape(1, d), w_in.T, gains, _rope_tables(t), bd, [w_out, w_up, w_down])

    mqa, fqa, fka = _routing(mq, mk, fl, b_forget.reshape(1, N_FOX_HEADS))
    o_moba = _attention("moba", mq, mqa, mk, mvt, _block_one_hot(t))
    o_fox = _attention("fox", fq, fqa, fk, fvt, fka)

    cv = jnp.concatenate([conv_w, conv_b[None, :],
                          jnp.zeros((HALO - CONV_WIDTH - 1, 2 * D_FF), jnp.float32)], axis=0)
    return _ffn(x, o_moba, o_fox, wo16, gt1, g_ffn.reshape(1, d), sc2, sh2, wu16, cv, wd16, gt2)


def kernel(x, c, w_ada, b_ada, g_mix, w_in, b_forget, moba_q_gain, moba_k_gain, fox_q_gain,
           fox_k_gain, w_out, g_ffn, w_up, conv_w, conv_b, w_down):
    for l in range(w_ada.shape[0]):
        x = _layer(x, c, w_ada[l], b_ada[l], g_mix[l], w_in[l], b_forget[l], moba_q_gain[l],
                   moba_k_gain[l], fox_q_gain[l], fox_k_gain[l], w_out[l], g_ffn[l], w_up[l],
                   conv_w[l], conv_b[l], w_down[l])
    return x
```

```python
import math

import jax
import jax.numpy as jnp
import numpy as np
from jax import lax
from jax.experimental import pallas as pl
from jax.experimental.pallas import tpu as pltpu

D_MODEL = 1024
HEAD_DIM = 64
N_MOBA_HEADS = 8
N_FOX_HEADS = 8
MOBA_WIDTH = N_MOBA_HEADS * HEAD_DIM
FOX_WIDTH = N_FOX_HEADS * HEAD_DIM
MOBA_BLOCK = 256
MOBA_TOPK = 3
ROPE_THETA = 500000.0
ROPE_DIM = HEAD_DIM // 4
D_FF = 2816
CONV_WIDTH = 3
NORM_EPS = 1e-6
NEG_INF = -1e30
LOG2E = math.log2(math.e)
Q_SCALE = HEAD_DIM ** -0.5 * LOG2E
LANES = 128
BF16_ROWS = 16
IN_QKV_COLS = 3 * MOBA_WIDTH + 3 * FOX_WIDTH
V_ROWS = HEAD_DIM + BF16_ROWS
SEQ_TILE = 256
ROW_TILE = 512
ADALN_COLS = 1536
ATTN_PAIRS_PER_STEP = 2
ATTN_LOOKAHEAD = 5
FFN_PARTS = 2
FFN_LOOKAHEAD = 2
FF_CHUNK = 256
N_FF_CHUNKS = D_FF // FF_CHUNK
HALO = 8
N_SPLIT = 3
AUX_GROUP = 16
VMEM_LIMIT = 56 * 1024 * 1024

_NT = (((1,), (1,)), ((), ()))


def _bf16(a):
    return a.astype(jnp.bfloat16)


def _dot(a, b):
    return jnp.dot(a, b, preferred_element_type=jnp.float32)


def _dot_nt(a, b):
    return lax.dot_general(a, b, _NT, preferred_element_type=jnp.float32)


def _sigmoid(a):
    return 1.0 / (1.0 + jnp.exp(-a))


def _adaln_kernel(c_ref, w_ref, b_ref, o_ref):
    c = c_ref[...]
    s = c * _sigmoid(c)
    o_ref[...] = _dot(_bf16(s), _bf16(w_ref[...])) + b_ref[...]


def _adaln(c, w_ada, b_ada):
    b, d = c.shape
    n = w_ada.shape[1]
    tn = ADALN_COLS
    return pl.pallas_call(
        _adaln_kernel,
        grid=(n // tn,),
        in_specs=[pl.BlockSpec((b, d), lambda j: (0, 0)),
                  pl.BlockSpec((d, tn), lambda j: (0, j)),
                  pl.BlockSpec((1, tn), lambda j: (0, j))],
        out_specs=pl.BlockSpec((b, tn), lambda j: (0, j)),
        out_shape=jax.ShapeDtypeStruct((b, n), jnp.float32),
        compiler_params=pltpu.CompilerParams(vmem_limit_bytes=VMEM_LIMIT),
        name="adaln",
    )(c, w_ada, b_ada.reshape(1, n))


def _rms_mod(x, g, sc, sh):
    ms = jnp.mean(x * x, axis=-1, keepdims=True)
    y = x * lax.rsqrt(ms + NORM_EPS)
    return (y * g) * (1.0 + sc) + sh


def _head_norm(p, gain, bd):
    sq = _bf16(p * p)
    half = 2 * LANES
    ss = jnp.concatenate([_dot(sq[:, :half], bd), _dot(sq[:, half:], bd)], axis=1)
    return (p * lax.rsqrt(ss * (1.0 / HEAD_DIM) + NORM_EPS)) * gain


def _rope(y, c, sa, sb):
    half = ROPE_DIM // 2
    outs = []
    for i in range(y.shape[1] // LANES):
        yc = y[:, i * LANES:(i + 1) * LANES]
        up = pltpu.roll(yc, LANES - half, 1)
        dn = pltpu.roll(yc, half, 1)
        outs.append(yc * c + up * sa + dn * sb)
    return jnp.concatenate(outs, axis=1)


def _inproj_kernel(x_ref, sc_ref, sh_ref, g_ref, wint_ref, gains_ref, rc_ref, rsa_ref, rsb_ref,
                   bd_ref, *refs):
    n_cast = (len(refs) - 5) // 2
    cast_in, refs = refs[:n_cast], refs[n_cast:]
    q_ref, k_ref, fl_ref, vt_ref = refs[:4]
    cast_out, w_ref = refs[4:4 + n_cast], refs[-1]
    w = MOBA_WIDTH
    n_groups = IN_QKV_COLS // w

    @pl.when((pl.program_id(0) == 0) & (pl.program_id(1) == 0))
    def _():
        for i in range(n_groups):
            w_ref[i * w:(i + 1) * w, :] = _bf16(wint_ref[i * w:(i + 1) * w, :])
        logit_rows = wint_ref[IN_QKV_COLS:IN_QKV_COLS + N_FOX_HEADS, :]
        r = lax.broadcasted_iota(jnp.int32, (LANES, 1), 0)
        r_head = jnp.where(r % AUX_GROUP < 2 * N_SPLIT, _group_head(r), -1)
        wlog = jnp.zeros((LANES, logit_rows.shape[1]), jnp.float32)
        for head in range(N_FOX_HEADS):
            wlog = jnp.where(r_head == head, logit_rows[head:head + 1, :], wlog)
        w_ref[IN_QKV_COLS:IN_QKV_COLS + LANES, :] = _bf16(wlog)

    half = x_ref.shape[1] // 2
    hn_halves = [_bf16(_rms_mod(x_ref[0, r * half:(r + 1) * half, :], g_ref[...], sc_ref[0], sh_ref[0]))
                 for r in range(2)]
    p_mq = jnp.concatenate([_dot_nt(h, w_ref[0:w, :]) for h in hn_halves], axis=0)
    hn = jnp.concatenate(hn_halves, axis=0)
    bd = bd_ref[...]
    c, sa, sb = rc_ref[...], rsa_ref[...], rsb_ref[...]
    proj = lambda i: _dot_nt(hn, w_ref[i * w:(i + 1) * w, :])

    p_mk = proj(1)
    mq = _rope(_head_norm(p_mq, gains_ref[0:1, :], bd), c, sa, sb)
    q_ref[0, :, 0:w] = _bf16(mq * Q_SCALE)
    p_fq = proj(3)
    mk = _rope(_head_norm(p_mk, gains_ref[1:2, :], bd), c, sa, sb)
    k_ref[0, :, 0:w] = _bf16(mk)
    p_fk = proj(4)
    fq = _head_norm(p_fq, gains_ref[2:3, :], bd)
    q_ref[0, :, w:2 * w] = _bf16(fq * Q_SCALE)
    fl_ref[0] = _dot_nt(hn, w_ref[IN_QKV_COLS:IN_QKV_COLS + LANES, :])
    mvt = _dot_nt(w_ref[2 * w:3 * w, :], hn)
    fk = _head_norm(p_fk, gains_ref[3:4, :], bd)
    k_ref[0, :, w:2 * w] = _bf16(fk)
    vt_ref[0, 0:w, :] = _bf16(mvt)
    vt_ref[0, w:2 * w, :] = _bf16(_dot_nt(w_ref[5 * w:6 * w, :], hn))
    for src, dst in zip(cast_in, cast_out):
        dst[...] = _bf16(src[...])


def _rope_tables(t):
    half = ROPE_DIM // 2
    inv_freq = np.power(ROPE_THETA, -2.0 * np.arange(half, dtype=np.float64) / ROPE_DIM)
    ang = np.arange(t, dtype=np.float64)[:, None] * inv_freq[None, :]
    cos, sin = np.cos(ang), np.sin(ang)
    d = np.arange(LANES) % HEAD_DIM
    first = (d < half)[None, :]
    second = ((d >= half) & (d < ROPE_DIM))[None, :]
    idx = np.where(d < ROPE_DIM, d % half, 0)
    cos_l, sin_l = cos[:, idx], sin[:, idx]
    c = np.where(first | second, cos_l, 1.0)
    sa = np.where(first, -sin_l, 0.0)
    sb = np.where(second, sin_l, 0.0)
    return tuple(jnp.asarray(a, jnp.float32) for a in (c, sa, sb))


def _slab_rows(n_rows, n_steps):
    tiles = n_rows // BF16_ROWS
    assert tiles * BF16_ROWS == n_rows
    n_slabs = max(g for g in range(1, tiles + 1) if tiles % g == 0 and g <= n_steps)
    return n_rows // n_slabs


def _inproj(x, sc1, sh1, g_mix, w_in_t, gains, tables, bd, to_cast):
    b, t, d = x.shape
    tm = min(ROW_TILE, t)
    w = MOBA_WIDTH
    nt = t // tm
    cast_specs, cast_shapes = [], []
    for a in to_cast:
        rows = _slab_rows(a.shape[0], b * nt)
        last = a.shape[0] // rows - 1
        slab = lambda i, j, last=last: (jnp.minimum(i * nt + j, last), 0)
        cast_specs.append(pl.BlockSpec((rows, a.shape[1]), slab))
        cast_shapes.append(jax.ShapeDtypeStruct(a.shape, jnp.bfloat16))
    row = lambda i, j: (i, 0, 0)
    tile = lambda i, j: (i, j, 0)
    tile_t = lambda i, j: (i, 0, j)
    const = lambda i, j: (0, 0)
    wide = jax.ShapeDtypeStruct((b, t, 2 * w), jnp.bfloat16)
    wide_t = jax.ShapeDtypeStruct((b, 2 * w, t), jnp.bfloat16)
    out_specs = ([pl.BlockSpec((1, tm, 2 * w), tile)] * 2 + [pl.BlockSpec((1, tm, LANES), tile)]
                 + [pl.BlockSpec((1, 2 * w, tm), tile_t)])
    return pl.pallas_call(
        _inproj_kernel,
        grid=(b, t // tm),
        in_specs=[pl.BlockSpec((1, tm, d), tile),
                  pl.BlockSpec((1, 1, d), row),
                  pl.BlockSpec((1, 1, d), row),
                  pl.BlockSpec((1, d), const),
                  pl.BlockSpec(w_in_t.shape, const, pipeline_mode=pl.Buffered(1)),
                  pl.BlockSpec((4, w), const),
                  pl.BlockSpec((tm, LANES), lambda i, j: (j, 0)),
                  pl.BlockSpec((tm, LANES), lambda i, j: (j, 0)),
                  pl.BlockSpec((tm, LANES), lambda i, j: (j, 0)),
                  pl.BlockSpec((2 * LANES, 2 * LANES), const)] + cast_specs,
        out_specs=out_specs + cast_specs,
        out_shape=([wide] * 2 + [jax.ShapeDtypeStruct((b, t, LANES), jnp.float32), wide_t]
                   + cast_shapes),
        scratch_shapes=[pltpu.VMEM((IN_QKV_COLS + LANES, d), jnp.bfloat16)],
        compiler_params=pltpu.CompilerParams(vmem_limit_bytes=VMEM_LIMIT),
        name="inproj",
    )(x, sc1, sh1, g_mix, w_in_t, gains, *tables, bd, *to_cast)


def _split3(a):
    hi = _bf16(a)
    r1 = a - hi.astype(jnp.float32)
    mid = _bf16(r1)
    lo = _bf16(r1 - mid.astype(jnp.float32))
    return hi, mid, lo


def _group_head(lane_idx):
    return 2 * ((lane_idx % HEAD_DIM) // AUX_GROUP) + jnp.where(lane_idx < HEAD_DIM, 1, 0)


def _routing_kernel(mq_ref, mk_ref, fl_ref, bf_ref, oh_ref, qa_ref, ka_ref):
    t = mq_ref.shape[1]
    nb = t // MOBA_BLOCK
    k = mk_ref[0]
    blk_of_col = lax.broadcasted_iota(jnp.int32, (nb, t), 1) // MOBA_BLOCK
    blk_row = lax.broadcasted_iota(jnp.int32, (nb, t), 0)
    ind = _bf16(jnp.where(blk_of_col == blk_row, 1.0, 0.0))
    kmean = _dot(ind, k) * (1.0 / MOBA_BLOCK)
    kmt = jnp.concatenate([kmean] * (LANES // nb), axis=0)
    r_head = _group_head(lax.broadcasted_iota(jnp.int32, kmt.shape, 0))
    c_head = lax.broadcasted_iota(jnp.int32, kmt.shape, 1) // HEAD_DIM
    kmt = jnp.where(r_head == c_head, kmt, 0.0)
    k_hi = _bf16(kmt)
    k_lo = _bf16(kmt - k_hi.astype(jnp.float32))

    lane = lax.broadcasted_iota(jnp.int32, (1, LANES), 1)
    j = lane % nb
    ranked = min(MOBA_TOPK + 1, nb) * MOBA_BLOCK
    own = lax.broadcasted_iota(jnp.int32, (ranked, 1), 0) // MOBA_BLOCK
    qa_ref[0, 0:ranked, 0:LANES] = jnp.where(j <= own, 0.0, NEG_INF)
    if ranked < t:
        q = mq_ref[0, ranked:, :]
        gate = (_dot_nt(q, k_hi) + _dot_nt(q, k_lo)) * (1.0 / Q_SCALE)
        own = (lax.broadcasted_iota(jnp.int32, (t - ranked, 1), 0) + ranked) // MOBA_BLOCK
        past = j < own
        g = jnp.where(past, gate, -jnp.inf)
        rank = jnp.zeros(g.shape, jnp.int32)
        for d in range(1, nb):
            other = pltpu.roll(g, LANES - d, 1)
            first = ((j + d) % nb) < j
            beats = (other > g) | (first & (other == g))
            rank = rank + jnp.where(beats, 1, 0)
        attend = (past & (rank < MOBA_TOPK)) | (j == own)
        qa_ref[0, ranked:, 0:LANES] = jnp.where(attend, 0.0, NEG_INF)
    ka_ref[0, :, 0:LANES] = oh_ref[0]

    lane_head = _group_head(lane)
    bias = jnp.zeros((1, LANES), jnp.float32)
    for head in range(N_FOX_HEADS):
        bias = jnp.where(lane_head == head, bf_ref[:, head:head + 1], bias)
    z = fl_ref[0] + bias
    logf = -(jnp.maximum(-z, 0.0) + jnp.log1p(jnp.exp(-jnp.abs(z))))
    rr = lax.broadcasted_iota(jnp.int32, (MOBA_BLOCK, MOBA_BLOCK), 0)
    cc = lax.broadcasted_iota(jnp.int32, (MOBA_BLOCK, MOBA_BLOCK), 1)
    tri = _bf16(jnp.where(cc <= rr, 1.0, 0.0))
    slot = lane % AUX_GROUP
    carry = jnp.zeros((1, LANES), jnp.float32)
    for i in range(nb):
        rows = slice(i * MOBA_BLOCK, (i + 1) * MOBA_BLOCK)
        hi, mid, lo = _split3(logf[rows, :])
        cum = (_dot(tri, hi) + _dot(tri, mid)) + _dot(tri, lo) + carry
        carry = cum[MOBA_BLOCK - 1:MOBA_BLOCK, :]
        parts = [p.astype(jnp.float32) for p in _split3(cum * LOG2E)]
        fq = jnp.where(slot < 2 * N_SPLIT, 1.0, 0.0)
        fk = jnp.where(slot < N_SPLIT, 1.0, 0.0)
        for s in range(N_SPLIT):
            fq = jnp.where(slot == s, parts[s], fq)
            fk = jnp.where(slot == N_SPLIT + s, -parts[s], fk)
        qa_ref[0, rows, LANES:2 * LANES] = fq
        ka_ref[0, rows, LANES:2 * LANES] = fk


def _routing(q_all, k_all, fl, b_forget):
    b, t, _ = q_all.shape
    w = MOBA_WIDTH
    full = lambda i: (i, 0, 0)
    aux = jax.ShapeDtypeStruct((b, t, 2 * LANES), jnp.float32)
    return pl.pallas_call(
        _routing_kernel,
        grid=(b,),
        in_specs=[pl.BlockSpec((1, t, w), full),
                  pl.BlockSpec((1, t, w), full),
                  pl.BlockSpec((1, t, LANES), full),
                  pl.BlockSpec((1, N_FOX_HEADS), lambda i: (0, 0)),
                  pl.BlockSpec((1, t, LANES), lambda i: (0, 0, 0))],
        out_specs=[pl.BlockSpec((1, t, 2 * LANES), full)] * 2,
        out_shape=[aux] * 2,
        compiler_params=pltpu.CompilerParams(vmem_limit_bytes=VMEM_LIMIT),
        name="routing",
    )(q_all, k_all, fl, b_forget, _block_one_hot(t))


def _attn_kernel(q_ref, qa_ref, k_ref, vt_ref, ka_ref, o_ref, kx_ref, vx_ref):
    tq = SEQ_TILE
    t = k_ref.shape[1]
    nb = t // tq
    n_local = q_ref.shape[2] // LANES
    lane = lax.broadcasted_iota(jnp.int32, (1, LANES), 1)
    low = lane < HEAD_DIM
    in_group = (lane % HEAD_DIM) < AUX_GROUP // 2

    def aux_shift(pi):
        pair = (pl.program_id(1) * n_local + pi) % (MOBA_WIDTH // LANES)
        return (LANES - pair * AUX_GROUP) % LANES

    for pi in range(n_local):
        cols = slice(pi * LANES, (pi + 1) * LANES)
        k2 = k_ref[0, :, cols]
        ka = _bf16(pltpu.roll(ka_ref[0], aux_shift(pi), 1))
        kx_ref[2 * pi] = jnp.where(low, k2, ka)
        kx_ref[2 * pi + 1] = jnp.where(low, ka, k2)
        for h in range(2):
            first = pi * LANES + h * HEAD_DIM
            vx_ref[2 * pi + h, 0:HEAD_DIM, :] = vt_ref[0, first:first + HEAD_DIM, :]
            vx_ref[2 * pi + h, HEAD_DIM:V_ROWS, :] = jnp.ones((V_ROWS - HEAD_DIM, t), jnp.bfloat16)

    key = lax.broadcasted_iota(jnp.int32, (tq, tq), 0)
    qry = lax.broadcasted_iota(jnp.int32, (tq, tq), 1)
    causal = key <= qry

    def scores(n, pi, h):
        rows = slice(n * tq, (n + 1) * tq)
        q2 = q_ref[0, rows, pi * LANES:(pi + 1) * LANES]
        qa = pltpu.roll(qa_ref[0, rows, :], aux_shift(pi), 1)
        qa = _bf16(jnp.where(in_group, qa, 0.0))
        qx = jnp.where(low, q2, qa) if h == 0 else jnp.where(low, qa, q2)
        hd = 2 * pi + h
        s = _dot_nt(kx_ref[hd, 0:(n + 1) * tq, :], qx)
        sd = jnp.where(causal, s[n * tq:, :], NEG_INF)
        s = sd if n == 0 else jnp.concatenate([s[:n * tq, :], sd], axis=0)
        return (n, pi, h), s, jnp.max(s, axis=0, keepdims=True)

    def weighted_values(chain, s, m):
        n, pi, h = chain
        acc = _dot(vx_ref[2 * pi + h, :, 0:(n + 1) * tq], _bf16(jnp.exp2(s - m)))
        return acc[0:HEAD_DIM, :] / acc[HEAD_DIM:HEAD_DIM + 1, :]

    chains = [(n, pi, h) for n in reversed(range(nb)) for pi in range(n_local) for h in range(2)]
    outs = {}
    queue = [scores(*c) for c in chains[:ATTN_LOOKAHEAD]]
    for nxt in chains[ATTN_LOOKAHEAD:] + [None] * ATTN_LOOKAHEAD:
        if nxt is not None:
            queue.append(scores(*nxt))
        done = queue.pop(0)
        outs[done[0]] = weighted_values(*done)
        n, pi, h = done[0]
        if h == 1:
            o_t = jnp.concatenate([outs.pop((n, pi, 0)), outs.pop((n, pi, 1))], axis=0)
            o_ref[0, n * tq:(n + 1) * tq, pi * LANES:(pi + 1) * LANES] = _bf16(o_t.T)


def _block_one_hot(t):
    blk = np.arange(t)[:, None] // MOBA_BLOCK
    slot = np.arange(LANES)[None, :] % AUX_GROUP
    return jnp.asarray(slot == blk, jnp.float32)[None]


def _attention(q, qa, k, vt, ka):
    b, t, w = q.shape
    wl = ATTN_PAIRS_PER_STEP * LANES
    steps_per_mixer = MOBA_WIDTH // wl
    assert MOBA_WIDTH == FOX_WIDTH and MOBA_WIDTH % wl == 0 and t % SEQ_TILE == 0
    spec = pl.BlockSpec((1, t, wl), lambda i, p: (i, 0, p))
    spec_t = pl.BlockSpec((1, wl, t), lambda i, p: (i, p, 0))
    spec_aux = pl.BlockSpec((1, t, LANES), lambda i, p: (i, 0, p // steps_per_mixer))
    return pl.pallas_call(
        _attn_kernel,
        grid=(b, w // wl),
        in_specs=[spec, spec_aux, spec, spec_t, spec_aux],
        out_specs=spec,
        out_shape=jax.ShapeDtypeStruct((b, t, w), jnp.bfloat16),
        scratch_shapes=[pltpu.VMEM((2 * ATTN_PAIRS_PER_STEP, t, LANES), jnp.bfloat16),
                        pltpu.VMEM((2 * ATTN_PAIRS_PER_STEP, V_ROWS, t), jnp.bfloat16)],
        compiler_params=pltpu.CompilerParams(vmem_limit_bytes=VMEM_LIMIT),
        name="attention",
    )(q, qa, k, vt, ka)


def _ffn_kernel(x_ref, o_ref, wo_ref, gt1_ref, g_ref, sc_ref, sh_ref, wup_ref, cv_ref,
                wdn_ref, gt2_ref, out_ref, x1_ref, halo_ref):
    tm = x_ref.shape[1]

    @pl.when(pl.program_id(1) == 0)
    def _():
        halo_ref[...] = jnp.zeros(halo_ref.shape, jnp.float32)

    pm = tm // FFN_PARTS
    slabs = [slice(p * pm, (p + 1) * pm) for p in range(FFN_PARTS)]
    attn = [_dot(o_ref[0, r, :], wo_ref[...]) for r in slabs]

    def normed(p):
        x1 = x_ref[0, slabs[p], :] + gt1_ref[0] * attn[p]
        x1_ref[slabs[p], :] = x1
        return _bf16(_rms_mod(x1, g_ref[...], sc_ref[0], sh_ref[0]))

    def up_proj(hn, c):
        gate_cols = slice(c * FF_CHUNK, (c + 1) * FF_CHUNK)
        val_cols = slice(D_FF + c * FF_CHUNK, D_FF + (c + 1) * FF_CHUNK)
        u = jnp.concatenate([_dot(hn, wup_ref[:, gate_cols]), _dot(hn, wup_ref[:, val_cols])],
                            axis=1)
        return u, gate_cols, val_cols

    hn = normed(0)
    tails = [halo_ref[c] for c in range(N_FF_CHUNKS)]
    for p in range(FFN_PARTS):
        hn_next = None
        hmid = []
        ahead = [up_proj(hn, c) for c in range(FFN_LOOKAHEAD)]
        for c in range(N_FF_CHUNKS):
            if c + FFN_LOOKAHEAD < N_FF_CHUNKS:
                ahead.append(up_proj(hn, c + FFN_LOOKAHEAD))
            u, gate_cols, val_cols = ahead.pop(0)
            ext = jnp.concatenate([tails[c], u], axis=0)
            tails[c] = u[pm - HALO:pm, :]
            cv = jnp.concatenate([cv_ref[:, gate_cols], cv_ref[:, val_cols]], axis=1)
            u1 = pltpu.roll(ext, 1, 0)[HALO:, :]
            u2 = pltpu.roll(ext, 2, 0)[HALO:, :]
            uc = cv[0:1, :] * u2 + cv[1:2, :] * u1 + cv[2:3, :] * u + cv[3:4, :]
            a = uc[:, :FF_CHUNK]
            val = uc[:, FF_CHUNK:]
            half = 0.5 * a
            hmid.append(_bf16((half + half * jnp.tanh(half)) * val))
            if c == FFN_LOOKAHEAD and p + 1 < FFN_PARTS:
                hn_next = normed(p + 1)
        ffn = _dot(jnp.concatenate(hmid, axis=1), wdn_ref[...])
        out_ref[0, slabs[p], :] = x1_ref[slabs[p], :] + gt2_ref[0] * ffn
        hn = hn_next
    for c in range(N_FF_CHUNKS):
        halo_ref[c] = tails[c]


def _ffn(x, o, wo, gt1, g_ffn, sc2, sh2, wup, cv, wdn, gt2):
    b, t, d = x.shape
    tm = min(ROW_TILE, t)
    row = lambda i, j: (i, 0, 0)
    tile = lambda i, j: (i, j, 0)
    const2 = lambda i, j: (0, 0)
    once = dict(pipeline_mode=pl.Buffered(1))
    return pl.pallas_call(
        _ffn_kernel,
        grid=(b, t // tm),
        in_specs=[pl.BlockSpec((1, tm, d), tile),
                  pl.BlockSpec((1, tm, d), tile),
                  pl.BlockSpec((d, d), const2, **once),
                  pl.BlockSpec((1, 1, d), row),
                  pl.BlockSpec((1, d), const2),
                  pl.BlockSpec((1, 1, d), row),
                  pl.BlockSpec((1, 1, d), row),
                  pl.BlockSpec((d, 2 * D_FF), const2, **once),
                  pl.BlockSpec((HALO, 2 * D_FF), const2, **once),
                  pl.BlockSpec((D_FF, d), const2, **once),
                  pl.BlockSpec((1, 1, d), row)],
        out_specs=pl.BlockSpec((1, tm, d), tile),
        out_shape=jax.ShapeDtypeStruct((b, t, d), jnp.float32),
        scratch_shapes=[pltpu.VMEM((tm, d), jnp.float32),
                        pltpu.VMEM((N_FF_CHUNKS, HALO, 2 * FF_CHUNK), jnp.float32)],
        compiler_params=pltpu.CompilerParams(vmem_limit_bytes=VMEM_LIMIT),
        name="ffn",
    )(x, o, wo, gt1, g_ffn, sc2, sh2, wup, cv, wdn, gt2)


def _layer(x, c, w_ada, b_ada, g_mix, w_in, b_forget, moba_q_gain, moba_k_gain, fox_q_gain,
           fox_k_gain, w_out, g_ffn, w_up, conv_w, conv_b, w_down):
    b, t, d = x.shape
    nb = t // MOBA_BLOCK
    assert d == D_MODEL and t % SEQ_TILE == 0
    assert AUX_GROUP % nb == 0 and 2 * nb <= AUX_GROUP and 2 * N_SPLIT <= AUX_GROUP // 2

    mod = _adaln(c, w_ada, b_ada).reshape(b, 6, 1, d)
    sh1, sc1, gt1, sh2, sc2, gt2 = [mod[:, i] for i in range(6)]

    assert w_in.shape == (d, IN_QKV_COLS + N_FOX_HEADS)
    gains = jnp.stack([jnp.tile(g, N_MOBA_HEADS) for g in
                       (moba_q_gain, moba_k_gain, fox_q_gain, fox_k_gain)])
    r = np.arange(2 * LANES) // HEAD_DIM
    bd = jnp.asarray(r[:, None] == r[None, :], jnp.bfloat16)
    q_all, k_all, fl, vt_all, wo16, wu16, wd16 = _inproj(
        x, sc1, sh1, g_mix.reshape(1, d), w_in.T, gains, _rope_tables(t), bd, [w_out, w_up, w_down])

    qa_all, ka_all = _routing(q_all, k_all, fl, b_forget.reshape(1, N_FOX_HEADS))
    o_all = _attention(q_all, qa_all, k_all, vt_all, ka_all)

    cv = jnp.concatenate([conv_w, conv_b[None, :],
                          jnp.zeros((HALO - CONV_WIDTH - 1, 2 * D_FF), jnp.float32)], axis=0)
    return _ffn(x, o_all, wo16, gt1, g_ffn.reshape(1, d), sc2, sh2, wu16, cv, wd16, gt2)


def kernel(x, c, w_ada, b_ada, g_mix, w_in, b_forget, moba_q_gain, moba_k_gain, fox_q_gain,
           fox_k_gain, w_out, g_ffn, w_up, conv_w, conv_b, w_down):
    for l in range(w_ada.shape[0]):
        x = _layer(x, c, w_ada[l], b_ada[l], g_mix[l], w_in[l], b_forget[l], moba_q_gain[l],
                   moba_k_gain[l], fox_q_gain[l], fox_k_gain[l], w_out[l], g_ffn[l], w_up[l],
                   conv_w[l], conv_b[l], w_down[l])
    return x
```
